```python
import math
import jax, jax.numpy as jnp
from jax import lax
import numpy as np

D_MODEL = 2048
BATCH = 8
SEQ = 8192
DEPTH = 4

GDN_DK = 128
GDN_DV = 128
GDN_W = 3 * D_MODEL // 8
GDN_HEADS = GDN_W // GDN_DV
GDN_QK = GDN_HEADS * GDN_DK
ATT_HD = 64
ATT_W = 3 * D_MODEL // 8
ATT_HEADS = ATT_W // ATT_HD
CONV_CH = D_MODEL - GDN_W - ATT_W
MIX_W = CONV_CH + GDN_W + ATT_W

CONV_WIDTH = 31
SHORT_CONV = 4
GDN_CHUNK = 64

ROPE_THETA = 500000.0
ROPE_DIM = ATT_HD // 4
DIL_PATTERNS = ((128, 1), (512, 4), (2048, 16))
ATT_BLOCK = 128
NEG_INF = -1e30

IN_SPLITS = (
    2 * CONV_CH, CONV_CH,
    GDN_QK, GDN_QK, GDN_W, GDN_W, GDN_HEADS, GDN_HEADS,
    ATT_W, ATT_W, ATT_W, ATT_W,
)
IN_W = sum(IN_SPLITS)

kernel_name = "hymba_style_conformer_gdn_dilated_hybrid"


def rms_norm(x, w, eps=1e-6):
    xf = x.astype(jnp.float32)
    y = xf * lax.rsqrt(jnp.mean(xf * xf, axis=-1, keepdims=True) + eps)
    return (y * w.astype(jnp.float32)).astype(x.dtype)


def layer_norm(x, w, b, eps=1e-5):
    xf = x.astype(jnp.float32)
    mu = jnp.mean(xf, axis=-1, keepdims=True)
    var = jnp.mean(jnp.square(xf - mu), axis=-1, keepdims=True)
    y = (xf - mu) * lax.rsqrt(var + eps) * w.astype(jnp.float32) + b.astype(jnp.float32)
    return y.astype(x.dtype)


def l2_normalize(x, eps=1e-6):
    xf = x.astype(jnp.float32)
    return xf * lax.rsqrt(jnp.sum(xf * xf, axis=-1, keepdims=True) + eps)


def causal_dwconv(x, w):
    K, C = w.shape
    return lax.conv_general_dilated(
        x, w[:, None, :].astype(x.dtype), window_strides=(1,), padding=[(K - 1, 0)],
        dimension_numbers=("NWC", "WIO", "NWC"), feature_group_count=C)


def rope_tables(S):
    half = ROPE_DIM // 2
    inv = ROPE_THETA ** (-jnp.arange(half, dtype=jnp.float32) / half)
    ang = jnp.arange(S, dtype=jnp.float32)[:, None] * inv[None, :]
    return jnp.cos(ang), jnp.sin(ang)


def apply_partial_rope(x, cos, sin):
    half = ROPE_DIM // 2
    c, s = cos[None, :, None, :], sin[None, :, None, :]
    x1, x2, rest = x[..., :half], x[..., half:ROPE_DIM], x[..., ROPE_DIM:]
    return jnp.concatenate([x1 * c - x2 * s, x2 * c + x1 * s, rest], axis=-1)


def conformer_conv(u, dw_w, dw_b, ln_w, ln_b, pw_w):
    a, b = jnp.split(u, 2, axis=-1)
    h = a * jax.nn.sigmoid(b)
    h = causal_dwconv(h, dw_w) + dw_b.astype(h.dtype)
    h = layer_norm(h, ln_w, ln_b)
    h = jax.nn.silu(h)
    return h @ pw_w.astype(h.dtype)


def chunk_gated_delta_rule(q, k, v, g, beta):
    B, S, H, DK = q.shape
    DV = v.shape[-1]
    C = GDN_CHUNK
    N = S // C
    f32 = jnp.float32

    def chunks(t):
        return t.astype(f32).reshape(B, N, C, H, -1).transpose(0, 3, 1, 2, 4)

    def chunks_h(t):
        return t.astype(f32).reshape(B, N, C, H).transpose(0, 3, 1, 2)

    q, k, v = chunks(q), chunks(k), chunks(v)
    beta = chunks_h(beta)
    g = jnp.cumsum(chunks_h(g), axis=-1)

    causal = jnp.tril(jnp.ones((C, C), dtype=bool))
    strict = jnp.tril(jnp.ones((C, C), dtype=bool), -1)
    diff = g[..., :, None] - g[..., None, :]
    decay = jnp.where(causal, jnp.exp(jnp.where(causal, diff, 0.0)), 0.0)

    kk = jnp.einsum("bhncd,bhnmd->bhncm", k, k)
    lower = jnp.where(strict, beta[..., :, None] * kk * decay, 0.0)
    eye = jnp.eye(C, dtype=f32)
    T = lax.linalg.triangular_solve(eye + lower, jnp.broadcast_to(eye, lower.shape),
                                    left_side=True, lower=True, unit_diagonal=True)
    w_v = jnp.einsum("bhncm,bhnmd->bhncd", T, v * beta[..., None])
    w_k = jnp.einsum("bhncm,bhnmd->bhncd", T, k * (beta * jnp.exp(g))[..., None])
    qk = jnp.where(causal, jnp.einsum("bhncd,bhnmd->bhncm", q, k) * decay, 0.0)
    q_dec = q * jnp.exp(g)[..., None]
    k_dec = k * jnp.exp(g[..., -1:] - g)[..., None]
    g_last = jnp.exp(g[..., -1])

    def step(state, xs):
        qk_i, qd_i, wv_i, wk_i, kd_i, gl_i = xs
        v_new = wv_i - jnp.einsum("bhcd,bhde->bhce", wk_i, state)
        o_i = jnp.einsum("bhcd,bhde->bhce", qd_i, state) + jnp.einsum("bhcm,bhme->bhce", qk_i, v_new)
        state = state * gl_i[..., None, None] + jnp.einsum("bhcd,bhce->bhde", kd_i, v_new)
        return state, o_i

    xs = (jnp.moveaxis(qk, 2, 0), jnp.moveaxis(q_dec, 2, 0), jnp.moveaxis(w_v, 2, 0),
          jnp.moveaxis(w_k, 2, 0), jnp.moveaxis(k_dec, 2, 0), jnp.moveaxis(g_last, 2, 0))
    state0 = jnp.zeros((B, H, DK, DV), f32)
    _, o = lax.scan(step, state0, xs)
    return o.transpose(1, 0, 3, 2, 4).reshape(B, S, H, DV)


def gated_deltanet(q, k, v, z, beta_in, alpha_in, conv_w, a_log, dt_bias, norm_w):
    B, S, _ = q.shape
    qkv = jax.nn.silu(causal_dwconv(jnp.concatenate([q, k, v], axis=-1), conv_w))
    q, k, v = jnp.split(qkv, [GDN_QK, 2 * GDN_QK], axis=-1)
    q = l2_normalize(q.reshape(B, S, GDN_HEADS, GDN_DK)) * (GDN_DK ** -0.5)
    k = l2_normalize(k.reshape(B, S, GDN_HEADS, GDN_DK))
    v = v.reshape(B, S, GDN_HEADS, GDN_DV)
    beta = jax.nn.sigmoid(beta_in.astype(jnp.float32))
    g = -jnp.exp(a_log.astype(jnp.float32)) * jax.nn.softplus(
        alpha_in.astype(jnp.float32) + dt_bias.astype(jnp.float32))
    o = chunk_gated_delta_rule(q, k, v, g, beta).astype(z.dtype)
    o = rms_norm(o, norm_w) * jax.nn.silu(z.reshape(B, S, GDN_HEADS, GDN_DV))
    return o.reshape(B, S, GDN_W)


def strided_window_attention(q, k, v, span, dil):
    B, S, H, E = q.shape
    unit = dil * ATT_BLOCK
    Sp = -(-S // unit) * unit
    Lr = Sp // dil
    nb = Lr // ATT_BLOCK

    def to_strided(t):
        t = jnp.pad(t, ((0, 0), (0, Sp - S), (0, 0), (0, 0)))
        return t.reshape(B, Lr, dil, H, E).transpose(0, 2, 1, 3, 4).reshape(B, dil, nb, ATT_BLOCK, H, E)

    def with_prev(t):
        prev = jnp.pad(t, ((0, 0), (0, 0), (1, 0), (0, 0), (0, 0), (0, 0)))[:, :, :-1]
        return jnp.concatenate([prev, t], axis=3)

    qs = to_strided(q)
    kb, vb = with_prev(to_strided(k)), with_prev(to_strided(v))
    s = jnp.einsum("bdnqhe,bdnkhe->bdnhqk", qs, kb) * (E ** -0.5)
    qi = jnp.arange(ATT_BLOCK)[:, None]
    ki = jnp.arange(2 * ATT_BLOCK)[None, :]
    dist = qi + ATT_BLOCK - ki
    blk = jnp.arange(nb)[:, None, None]
    valid = (dist >= 0) & (dist <= span) & ((blk - 1) * ATT_BLOCK + ki >= 0)
    s = jnp.where(valid[:, None], s, NEG_INF)
    m = jnp.max(s, axis=-1, keepdims=True)
    p = jnp.exp(s - m)
    l = jnp.sum(p, axis=-1, keepdims=True)
    o = jnp.einsum("bdnhqk,bdnkhe->bdnqhe", p / l, vb)
    lse = (m + jnp.log(l))[..., 0]
    o = o.reshape(B, dil, Lr, H, E).transpose(0, 2, 1, 3, 4).reshape(B, Sp, H, E)[:, :S]
    lse = lse.transpose(0, 1, 2, 4, 3).reshape(B, dil, Lr, H).transpose(0, 2, 1, 3).reshape(B, Sp, H)[:, :S]
    return o, lse


def dilated_attention(q, k, v, cos, sin):
    B, S, _ = q.shape
    dt = q.dtype
    q = apply_partial_rope(q.astype(jnp.float32).reshape(B, S, ATT_HEADS, ATT_HD), cos, sin)
    k = apply_partial_rope(k.astype(jnp.float32).reshape(B, S, ATT_HEADS, ATT_HD), cos, sin)
    v = v.astype(jnp.float32).reshape(B, S, ATT_HEADS, ATT_HD)
    outs, lses = [], []
    for window, dil in DIL_PATTERNS:
        o_g, lse_g = strided_window_attention(q, k, v, window // dil, dil)
        outs.append(o_g)
        lses.append(lse_g)
    wts = jax.nn.softmax(jnp.stack(lses, axis=0), axis=0)
    o = jnp.einsum("pbsh,pbshe->bshe", wts, jnp.stack(outs, axis=0))
    return o.reshape(B, S, ATT_W).astype(dt)


def _fwd_setup_inputs(seed: int = 0) -> dict:
    key = jax.random.key(seed)
    ks = jax.random.split(key, 16)
    f32 = jnp.float32

    def nrm(k, shape, scale):
        return jax.random.normal(k, shape, f32) * scale

    x = nrm(ks[0], (BATCH, SEQ, D_MODEL), 1.0)
    norm_w = 1.0 + nrm(ks[1], (DEPTH, D_MODEL), 0.01)
    w_in = nrm(ks[2], (DEPTH, D_MODEL, IN_W), D_MODEL ** -0.5)
    conv_qkv_w = nrm(ks[3], (DEPTH, SHORT_CONV, 2 * GDN_QK + GDN_W), SHORT_CONV ** -0.5)
    a_log = jnp.log(jax.random.uniform(ks[4], (DEPTH, GDN_HEADS), f32, 1.0, 16.0))
    dt = jnp.exp(jax.random.uniform(ks[5], (DEPTH, GDN_HEADS), f32, math.log(1e-3), math.log(1e-1)))
    dt_bias = dt + jnp.log(-jnp.expm1(-dt))
    gdn_norm_w = 1.0 + nrm(ks[6], (DEPTH, GDN_DV), 0.01)
    conf_dw_w = nrm(ks[7], (DEPTH, CONV_WIDTH, CONV_CH), CONV_WIDTH ** -0.5)
    conf_dw_b = nrm(ks[8], (DEPTH, CONV_CH), 0.01)
    conf_ln_w = 1.0 + nrm(ks[9], (DEPTH, CONV_CH), 0.01)
    conf_ln_b = nrm(ks[10], (DEPTH, CONV_CH), 0.01)
    conf_pw_w = nrm(ks[11], (DEPTH, CONV_CH, CONV_CH), CONV_CH ** -0.5)
    w_out = nrm(ks[12], (DEPTH, MIX_W, D_MODEL), MIX_W ** -0.5)
    final_norm_w = 1.0 + nrm(ks[13], (D_MODEL,), 0.01)
    return {"x": x, "norm_w": norm_w, "w_in": w_in, "conv_qkv_w": conv_qkv_w, "a_log": a_log,
            "dt_bias": dt_bias, "gdn_norm_w": gdn_norm_w, "conf_dw_w": conf_dw_w,
            "conf_dw_b": conf_dw_b, "conf_ln_w": conf_ln_w, "conf_ln_b": conf_ln_b,
            "conf_pw_w": conf_pw_w, "w_out": w_out, "final_norm_w": final_norm_w}


def _fwd_reference(x, norm_w, w_in, conv_qkv_w, a_log, dt_bias, gdn_norm_w, conf_dw_w, conf_dw_b,
              conf_ln_w, conf_ln_b, conf_pw_w, w_out, final_norm_w):
    B, S, _ = x.shape
    cos, sin = rope_tables(S)
    cuts, acc = [], 0
    for n in IN_SPLITS[:-1]:
        acc += n
        cuts.append(acc)
    for l in range(DEPTH):
        h = rms_norm(x, norm_w[l])
        u = h @ w_in[l].astype(h.dtype)
        (c_in, c_gate, g_q, g_k, g_v, g_z, g_b, g_a,
         a_q, a_k, a_v, a_gate) = jnp.split(u, cuts, axis=-1)
        y_conv = conformer_conv(c_in, conf_dw_w[l], conf_dw_b[l], conf_ln_w[l], conf_ln_b[l],
                                conf_pw_w[l]) * jax.nn.silu(c_gate)
        y_gdn = gated_deltanet(g_q, g_k, g_v, g_z, g_b, g_a, conv_qkv_w[l], a_log[l], dt_bias[l],
                               gdn_norm_w[l])
        y_att = dilated_attention(a_q, a_k, a_v, cos, sin) * jax.nn.silu(a_gate)
        y = jnp.concatenate([y_conv, y_gdn, y_att], axis=-1)
        x = x + y @ w_out[l].astype(y.dtype)
    return rms_norm(x, final_norm_w)


import jax as _jax
import jax.numpy as _jnp

TWIN_FORMAT = 'train_step'
FWD_PARAMS = ['x', 'norm_w', 'w_in', 'conv_qkv_w', 'a_log', 'dt_bias', 'gdn_norm_w', 'conf_dw_w', 'conf_dw_b', 'conf_ln_w', 'conf_ln_b', 'conf_pw_w', 'w_out', 'final_norm_w']
TWIN_WEIGHTS = ['norm_w', 'w_in', 'conv_qkv_w', 'a_log', 'dt_bias', 'gdn_norm_w', 'conf_dw_w', 'conf_dw_b', 'conf_ln_w', 'conf_ln_b', 'conf_pw_w', 'w_out', 'final_norm_w']
TWIN_DIFF_INPUT = 'x'
TWIN_INPUTS = ['x', 'norm_w', 'w_in', 'conv_qkv_w', 'a_log', 'dt_bias', 'gdn_norm_w', 'conf_dw_w', 'conf_dw_b', 'conf_ln_w', 'conf_ln_b', 'conf_pw_w', 'w_out', 'final_norm_w', 'loss_target', 'm_norm_w', 'm_w_in', 'm_conv_qkv_w', 'm_a_log', 'm_dt_bias', 'm_gdn_norm_w', 'm_conf_dw_w', 'm_conf_dw_b', 'm_conf_ln_w', 'm_conf_ln_b', 'm_conf_pw_w', 'm_w_out', 'm_final_norm_w', 'v_norm_w', 'v_w_in', 'v_conv_qkv_w', 'v_a_log', 'v_dt_bias', 'v_gdn_norm_w', 'v_conf_dw_w', 'v_conf_dw_b', 'v_conf_ln_w', 'v_conf_ln_b', 'v_conf_pw_w', 'v_w_out', 'v_final_norm_w']
TWIN_OUTPUTS = ['loss', 'grad_x', 'grad_norm_w', 'grad_w_in', 'grad_conv_qkv_w', 'grad_a_log', 'grad_dt_bias', 'grad_gdn_norm_w', 'grad_conf_dw_w', 'grad_conf_dw_b', 'grad_conf_ln_w', 'grad_conf_ln_b', 'grad_conf_pw_w', 'grad_w_out', 'grad_final_norm_w', 'delta_norm_w', 'delta_w_in', 'delta_conv_qkv_w', 'delta_a_log', 'delta_dt_bias', 'delta_gdn_norm_w', 'delta_conf_dw_w', 'delta_conf_dw_b', 'delta_conf_ln_w', 'delta_conf_ln_b', 'delta_conf_pw_w', 'delta_w_out', 'delta_final_norm_w', 'new_m_norm_w', 'new_m_w_in', 'new_m_conv_qkv_w', 'new_m_a_log', 'new_m_dt_bias', 'new_m_gdn_norm_w', 'new_m_conf_dw_w', 'new_m_conf_dw_b', 'new_m_conf_ln_w', 'new_m_conf_ln_b', 'new_m_conf_pw_w', 'new_m_w_out', 'new_m_final_norm_w', 'new_v_norm_w', 'new_v_w_in', 'new_v_conv_qkv_w', 'new_v_a_log', 'new_v_dt_bias', 'new_v_gdn_norm_w', 'new_v_conf_dw_w', 'new_v_conf_dw_b', 'new_v_conf_ln_w', 'new_v_conf_ln_b', 'new_v_conf_pw_w', 'new_v_w_out', 'new_v_final_norm_w']
TWIN_LEAF_KINDS = {'loss': 'loss', 'grad_x': 'grad_x', 'grad_norm_w': 'grad_w', 'grad_w_in': 'grad_w', 'grad_conv_qkv_w': 'grad_w', 'grad_a_log': 'grad_w', 'grad_dt_bias': 'grad_w', 'grad_gdn_norm_w': 'grad_w', 'grad_conf_dw_w': 'grad_w', 'grad_conf_dw_b': 'grad_w', 'grad_conf_ln_w': 'grad_w', 'grad_conf_ln_b': 'grad_w', 'grad_conf_pw_w': 'grad_w', 'grad_w_out': 'grad_w', 'grad_final_norm_w': 'grad_w', 'delta_norm_w': 'delta_w', 'delta_w_in': 'delta_w', 'delta_conv_qkv_w': 'delta_w', 'delta_a_log': 'delta_w', 'delta_dt_bias': 'delta_w', 'delta_gdn_norm_w': 'delta_w', 'delta_conf_dw_w': 'delta_w', 'delta_conf_dw_b': 'delta_w', 'delta_conf_ln_w': 'delta_w', 'delta_conf_ln_b': 'delta_w', 'delta_conf_pw_w': 'delta_w', 'delta_w_out': 'delta_w', 'delta_final_norm_w': 'delta_w', 'new_m_norm_w': 'new_m', 'new_m_w_in': 'new_m', 'new_m_conv_qkv_w': 'new_m', 'new_m_a_log': 'new_m', 'new_m_dt_bias': 'new_m', 'new_m_gdn_norm_w': 'new_m', 'new_m_conf_dw_w': 'new_m', 'new_m_conf_dw_b': 'new_m', 'new_m_conf_ln_w': 'new_m', 'new_m_conf_ln_b': 'new_m', 'new_m_conf_pw_w': 'new_m', 'new_m_w_out': 'new_m', 'new_m_final_norm_w': 'new_m', 'new_v_norm_w': 'new_v', 'new_v_w_in': 'new_v', 'new_v_conv_qkv_w': 'new_v', 'new_v_a_log': 'new_v', 'new_v_dt_bias': 'new_v', 'new_v_gdn_norm_w': 'new_v', 'new_v_conf_dw_w': 'new_v', 'new_v_conf_dw_b': 'new_v', 'new_v_conf_ln_w': 'new_v', 'new_v_conf_ln_b': 'new_v', 'new_v_conf_pw_w': 'new_v', 'new_v_w_out': 'new_v', 'new_v_final_norm_w': 'new_v'}


def _forward(args):
    return _fwd_reference(*[args[k] for k in FWD_PARAMS])


def _output_shape():
    def fwd():
        inp = _fwd_setup_inputs(0)
        return _fwd_reference(*[inp[k] for k in FWD_PARAMS])
    out = _jax.eval_shape(fwd)
    return out.shape, out.dtype

N_MICROBATCH = 1
ADAM_LR = 0.001
ADAM_B1 = 0.9
ADAM_B2 = 0.999
ADAM_EPS = 1e-08
ADAM_WD = 0.01
ADAM_STEP = 10
PER_EXAMPLE_BATCH_AXIS = {'x': 0, 'loss_target': 0}
SHARED_INPUTS = []
_WEIGHT_DTYPES = {'norm_w': _jnp.float32, 'w_in': _jnp.float32, 'conv_qkv_w': _jnp.float32, 'a_log': _jnp.float32, 'dt_bias': _jnp.float32, 'gdn_norm_w': _jnp.float32, 'conf_dw_w': _jnp.float32, 'conf_dw_b': _jnp.float32, 'conf_ln_w': _jnp.float32, 'conf_ln_b': _jnp.float32, 'conf_pw_w': _jnp.float32, 'w_out': _jnp.float32, 'final_norm_w': _jnp.float32}
MOMENT_SCALE = {'norm_w': 8.501785e-02, 'w_in': 4.386297e-02, 'conv_qkv_w': 5.663370e-02, 'a_log': 5.935492e-01, 'dt_bias': 5.891260e-01, 'gdn_norm_w': 2.091875e-01, 'conf_dw_w': 4.642329e-02, 'conf_dw_b': 1.044706e-01, 'conf_ln_w': 5.412238e-02, 'conf_ln_b': 4.757950e-02, 'conf_pw_w': 4.491822e-02, 'w_out': 5.074166e-02, 'final_norm_w': 3.195415e+01}


def _to_microbatches(a, axis):
    t = _jnp.moveaxis(a, axis, 0)
    t = t.reshape((N_MICROBATCH, t.shape[0] // N_MICROBATCH) + t.shape[1:])
    return _jnp.moveaxis(t, 1, axis + 1)


def setup_inputs(seed: int = 0) -> dict:
    inp = _fwd_setup_inputs(seed)
    key = _jax.random.fold_in(_jax.random.key(seed), 7919)
    shape, _ = _output_shape()
    out = dict(inp)
    out["loss_target"] = _jax.random.normal(_jax.random.fold_in(key, 0), shape, _jnp.float32)
    for i, name in enumerate(TWIN_WEIGHTS):
        w = inp[name].astype(_jnp.float32)
        if MOMENT_SCALE is None:
            s = _jnp.sqrt(_jnp.mean(_jnp.square(w)) + 1e-30)
        else:
            s = MOMENT_SCALE[name]
        km, kv = _jax.random.split(_jax.random.fold_in(key, i + 1))
        out[name] = w
        out["m_" + name] = s * _jax.random.normal(km, w.shape, _jnp.float32)
        out["v_" + name] = (s * s) * _jax.random.uniform(kv, w.shape, _jnp.float32, 0.5, 1.5)
    if N_MICROBATCH > 1:
        for name, axis in PER_EXAMPLE_BATCH_AXIS.items():
            out[name] = _to_microbatches(out[name], axis)
    return {'x': out['x'], 'norm_w': out['norm_w'], 'w_in': out['w_in'], 'conv_qkv_w': out['conv_qkv_w'], 'a_log': out['a_log'], 'dt_bias': out['dt_bias'], 'gdn_norm_w': out['gdn_norm_w'], 'conf_dw_w': out['conf_dw_w'], 'conf_dw_b': out['conf_dw_b'], 'conf_ln_w': out['conf_ln_w'], 'conf_ln_b': out['conf_ln_b'], 'conf_pw_w': out['conf_pw_w'], 'w_out': out['w_out'], 'final_norm_w': out['final_norm_w'], 'loss_target': out['loss_target'], 'm_norm_w': out['m_norm_w'], 'm_w_in': out['m_w_in'], 'm_conv_qkv_w': out['m_conv_qkv_w'], 'm_a_log': out['m_a_log'], 'm_dt_bias': out['m_dt_bias'], 'm_gdn_norm_w': out['m_gdn_norm_w'], 'm_conf_dw_w': out['m_conf_dw_w'], 'm_conf_dw_b': out['m_conf_dw_b'], 'm_conf_ln_w': out['m_conf_ln_w'], 'm_conf_ln_b': out['m_conf_ln_b'], 'm_conf_pw_w': out['m_conf_pw_w'], 'm_w_out': out['m_w_out'], 'm_final_norm_w': out['m_final_norm_w'], 'v_norm_w': out['v_norm_w'], 'v_w_in': out['v_w_in'], 'v_conv_qkv_w': out['v_conv_qkv_w'], 'v_a_log': out['v_a_log'], 'v_dt_bias': out['v_dt_bias'], 'v_gdn_norm_w': out['v_gdn_norm_w'], 'v_conf_dw_w': out['v_conf_dw_w'], 'v_conf_dw_b': out['v_conf_dw_b'], 'v_conf_ln_w': out['v_conf_ln_w'], 'v_conf_ln_b': out['v_conf_ln_b'], 'v_conf_pw_w': out['v_conf_pw_w'], 'v_w_out': out['v_w_out'], 'v_final_norm_w': out['v_final_norm_w']}


def _loss(weights, diff, rest, loss_target):
    with _jax.named_scope("forward"):
        args = {**rest, TWIN_DIFF_INPUT: diff, **{k: w.astype(_WEIGHT_DTYPES[k]) for k, w in weights.items()}}
        y = _forward(args)
    with _jax.named_scope("loss_head"):
        err = _jnp.square(y.astype(_jnp.float32) - loss_target)
        return 0.5 * _jnp.sum(_jnp.mean(err, axis=-1)) if err.ndim else 0.5 * err


def _adamw(w, g, m, v):
    m = ADAM_B1 * m + (1.0 - ADAM_B1) * g
    v = ADAM_B2 * v + (1.0 - ADAM_B2) * _jnp.square(g)
    m_hat = m / (1.0 - ADAM_B1 ** ADAM_STEP)
    v_hat = v / (1.0 - ADAM_B2 ** ADAM_STEP)
    delta = -ADAM_LR * (m_hat / (_jnp.sqrt(v_hat) + ADAM_EPS) + ADAM_WD * w)
    return delta, m, v


def reference(x, norm_w, w_in, conv_qkv_w, a_log, dt_bias, gdn_norm_w, conf_dw_w, conf_dw_b, conf_ln_w, conf_ln_b, conf_pw_w, w_out, final_norm_w, loss_target, m_norm_w, m_w_in, m_conv_qkv_w, m_a_log, m_dt_bias, m_gdn_norm_w, m_conf_dw_w, m_conf_dw_b, m_conf_ln_w, m_conf_ln_b, m_conf_pw_w, m_w_out, m_final_norm_w, v_norm_w, v_w_in, v_conv_qkv_w, v_a_log, v_dt_bias, v_gdn_norm_w, v_conf_dw_w, v_conf_dw_b, v_conf_ln_w, v_conf_ln_b, v_conf_pw_w, v_w_out, v_final_norm_w):
    given = dict(x=x, norm_w=norm_w, w_in=w_in, conv_qkv_w=conv_qkv_w, a_log=a_log, dt_bias=dt_bias, gdn_norm_w=gdn_norm_w, conf_dw_w=conf_dw_w, conf_dw_b=conf_dw_b, conf_ln_w=conf_ln_w, conf_ln_b=conf_ln_b, conf_pw_w=conf_pw_w, w_out=w_out, final_norm_w=final_norm_w, loss_target=loss_target, m_norm_w=m_norm_w, m_w_in=m_w_in, m_conv_qkv_w=m_conv_qkv_w, m_a_log=m_a_log, m_dt_bias=m_dt_bias, m_gdn_norm_w=m_gdn_norm_w, m_conf_dw_w=m_conf_dw_w, m_conf_dw_b=m_conf_dw_b, m_conf_ln_w=m_conf_ln_w, m_conf_ln_b=m_conf_ln_b, m_conf_pw_w=m_conf_pw_w, m_w_out=m_w_out, m_final_norm_w=m_final_norm_w, v_norm_w=v_norm_w, v_w_in=v_w_in, v_conv_qkv_w=v_conv_qkv_w, v_a_log=v_a_log, v_dt_bias=v_dt_bias, v_gdn_norm_w=v_gdn_norm_w, v_conf_dw_w=v_conf_dw_w, v_conf_dw_b=v_conf_dw_b, v_conf_ln_w=v_conf_ln_w, v_conf_ln_b=v_conf_ln_b, v_conf_pw_w=v_conf_pw_w, v_w_out=v_w_out, v_final_norm_w=v_final_norm_w)
    weights = {n: given[n] for n in TWIN_WEIGHTS}
    shared = {n: given[n] for n in SHARED_INPUTS}
    per_example = {n: given[n] for n in ['x']}
    grad_fn = _jax.value_and_grad(_loss, argnums=(0, 1))

    def one_microbatch(ex, loss_target):
        ex = dict(ex)
        diff = ex.pop(TWIN_DIFF_INPUT)
        return grad_fn(weights, diff, {**shared, **ex}, loss_target)

    if N_MICROBATCH == 1:
        loss, (grad_w, grad_x) = one_microbatch(per_example, given["loss_target"])
    else:
        def body(carry, xs):
            loss_sum, grad_sum = carry
            l_k, (gw_k, gx_k) = one_microbatch(xs[0], xs[1])
            with _jax.named_scope("update"):
                return (loss_sum + l_k, _jax.tree.map(_jnp.add, grad_sum, gw_k)), gx_k

        init = (_jnp.zeros((), _jnp.float32), _jax.tree.map(_jnp.zeros_like, weights))
        (loss, grad_w), grad_x = _jax.lax.scan(body, init, (per_example, given["loss_target"]))
    with _jax.named_scope("update"):
        delta_w, new_m, new_v = {}, {}, {}
        for n in TWIN_WEIGHTS:
            delta_w[n], new_m[n], new_v[n] = _adamw(weights[n], grad_w[n], given["m_" + n], given["v_" + n])
    return (loss, grad_x, *[grad_w[n] for n in TWIN_WEIGHTS], *[delta_w[n] for n in TWIN_WEIGHTS],
            *[new_m[n] for n in TWIN_WEIGHTS], *[new_v[n] for n in TWIN_WEIGHTS])
```

```python
import functools
import math

import jax
import jax.numpy as jnp
from jax import lax
from jax.experimental import pallas as pl
from jax.experimental.pallas import tpu as pltpu

D_MODEL = 2048
DEPTH = 4
N_DEV = 8
GDN_DK = 128
GDN_HEADS = 6
GDN_W = 768
ATT_HD = 64
ATT_HEADS = 12
ATT_W = 768
CONV_CH = 512
CONV_WIDTH = 31
SHORT_CONV = 4
GDN_CHUNK = 64
ROPE_THETA = 500000.0
ROPE_DIM = 16
DIL_PATTERNS = ((128, 1), (512, 4), (2048, 16))
ATT_BLOCK = 128
NEG_INF = -1e30
IN_W = 7692

ADAM_LR = 0.001
ADAM_B1 = 0.9
ADAM_B2 = 0.999
ADAM_EPS = 1e-08
ADAM_WD = 0.01
ADAM_STEP = 10

C_CA, C_CB, C_CG = 0, 512, 1024
C_GQ, C_GK, C_GV, C_GZ = 1536, 2304, 3072, 3840
C_AQ, C_AK, C_AV, C_AG = 4608, 5376, 6144, 6912
C_BETA, C_ALPHA = 7680, 7808
IN_PAD = 8192

VMEM_LIMIT = 56 * 1024 * 1024
CONF_HALO = 32
GDN_HALO = 8
GDN_GROUP = 4

F32 = jnp.float32
BF16 = jnp.bfloat16
HI = lax.Precision.HIGHEST


def _cparams(sem, vmem=VMEM_LIMIT):
    return pltpu.CompilerParams(dimension_semantics=sem, vmem_limit_bytes=vmem)


def _dg(a, b, ca, cb, prec):
    nb = a.ndim - 2
    batch = tuple(range(nb))
    dn = (((ca + nb,), (cb + nb,)), (batch, batch))
    if prec == "bf16":
        return lax.dot_general(a.astype(BF16), b.astype(BF16), dn, preferred_element_type=F32)
    return lax.dot_general(a.astype(F32), b.astype(F32), dn, precision=HI, preferred_element_type=F32)


def _nn_raw(a, b, prec):
    return _dg(a, b, 1, 0, prec)


def _nt_raw(a, b, prec):
    return _dg(a, b, 1, 1, prec)


def _tn_raw(a, b, prec):
    return _dg(a, b, 0, 0, prec)


@functools.partial(jax.custom_vjp, nondiff_argnums=(2,))
def mm_nn(a, b, prec="bf16"):
    return _nn_raw(a, b, prec)


def _mm_nn_f(a, b, prec):
    return _nn_raw(a, b, prec), (a, b)


def _mm_nn_b(prec, res, g):
    a, b = res
    return _nt_raw(g, b, prec).astype(a.dtype), _tn_raw(a, g, prec).astype(b.dtype)


mm_nn.defvjp(_mm_nn_f, _mm_nn_b)


@functools.partial(jax.custom_vjp, nondiff_argnums=(2,))
def mm_nt(a, b, prec="bf16"):
    return _nt_raw(a, b, prec)


def _mm_nt_f(a, b, prec):
    return _nt_raw(a, b, prec), (a, b)


def _mm_nt_b(prec, res, g):
    a, b = res
    return _nn_raw(g, b, prec).astype(a.dtype), _tn_raw(g, a, prec).astype(b.dtype)


mm_nt.defvjp(_mm_nt_f, _mm_nt_b)


@functools.partial(jax.custom_vjp, nondiff_argnums=(2,))
def mm_tn(a, b, prec="bf16"):
    return _tn_raw(a, b, prec)


def _mm_tn_f(a, b, prec):
    return _tn_raw(a, b, prec), (a, b)


def _mm_tn_b(prec, res, g):
    a, b = res
    return _nt_raw(b, g, prec).astype(a.dtype), _nn_raw(a, g, prec).astype(b.dtype)


mm_tn.defvjp(_mm_tn_f, _mm_tn_b)


def _sigmoid(x):
    return 1.0 / (1.0 + jnp.exp(-x))


def _silu(x):
    return x * _sigmoid(x)


def _softplus(x):
    return jnp.maximum(x, 0.0) + jnp.log(1.0 + jnp.exp(-jnp.abs(x)))


def matmul(a, b, *, mode, tm, tn, tk, out_dtype=F32, residual=None, name):
    if mode == "tn":
        K, M = a.shape
    else:
        M, K = a.shape
    N = b.shape[0] if mode == "nt" else b.shape[1]
    assert M % tm == 0 and N % tn == 0 and K % tk == 0, (a.shape, b.shape, tm, tn, tk)
    nk = K // tk
    a_spec = pl.BlockSpec((tk, tm), lambda i, j, k: (k, i)) if mode == "tn" else pl.BlockSpec((tm, tk), lambda i, j, k: (i, k))
    b_spec = pl.BlockSpec((tn, tk), lambda i, j, k: (j, k)) if mode == "nt" else pl.BlockSpec((tk, tn), lambda i, j, k: (k, j))
    o_spec = pl.BlockSpec((tm, tn), lambda i, j, k: (i, j))
    raw = {"nn": _nn_raw, "nt": _nt_raw, "tn": _tn_raw}[mode]
    has_res = residual is not None

    def body(*refs):
        if has_res:
            a_ref, b_ref, r_ref, o_ref, acc_ref = refs
        else:
            a_ref, b_ref, o_ref, acc_ref = refs
        k = pl.program_id(2)
        part = raw(a_ref[...], b_ref[...], "bf16")

        @pl.when(k == 0)
        def _():
            acc_ref[...] = part

        @pl.when(k > 0)
        def _():
            acc_ref[...] += part

        @pl.when(k == nk - 1)
        def _():
            r = acc_ref[...]
            if has_res:
                r = r + r_ref[...].astype(F32)
            o_ref[...] = r.astype(out_dtype)

    in_specs = [a_spec, b_spec] + ([o_spec] if has_res else [])
    args = (a, b) + ((residual,) if has_res else ())
    return pl.pallas_call(
        body, name=name, grid=(M // tm, N // tn, nk), in_specs=in_specs, out_specs=o_spec,
        out_shape=jax.ShapeDtypeStruct((M, N), out_dtype),
        scratch_shapes=[pltpu.VMEM((tm, tn), F32)],
        compiler_params=_cparams(("parallel", "parallel", "arbitrary")),
    )(*args)


def _rms_fn(x, w, eps=1e-6):
    return x * lax.rsqrt(jnp.mean(x * x, axis=-1, keepdims=True) + eps) * w


def rms_fwd(x, w, *, tm, name):
    S, D = x.shape

    def body(x_ref, w_ref, o_ref):
        o_ref[...] = _rms_fn(x_ref[...], w_ref[...]).astype(BF16)

    return pl.pallas_call(
        body, name=name, grid=(S // tm,),
        in_specs=[pl.BlockSpec((tm, D), lambda i: (i, 0)), pl.BlockSpec((1, D), lambda i: (0, 0))],
        out_specs=pl.BlockSpec((tm, D), lambda i: (i, 0)),
        out_shape=jax.ShapeDtypeStruct((S, D), BF16),
        compiler_params=_cparams(("parallel",)),
    )(x, w)


def rms_bwd(x, w, dh, dres, *, tm, name):
    S, D = x.shape

    def body(x_ref, w_ref, dh_ref, dr_ref, dx_ref, dw_ref):
        _, vjp = jax.vjp(_rms_fn, x_ref[...], w_ref[...])
        dx, dw = vjp(dh_ref[...].astype(F32))
        dx_ref[...] = dx + dr_ref[...]

        @pl.when(pl.program_id(0) == 0)
        def _():
            dw_ref[...] = jnp.zeros_like(dw_ref)

        dw_ref[...] += dw

    row = pl.BlockSpec((tm, D), lambda i: (i, 0))
    vec = pl.BlockSpec((1, D), lambda i: (0, 0))
    return pl.pallas_call(
        body, name=name, grid=(S // tm,), in_specs=[row, vec, row, row], out_specs=[row, vec],
        out_shape=[jax.ShapeDtypeStruct((S, D), F32), jax.ShapeDtypeStruct((1, D), F32)],
        compiler_params=_cparams(("arbitrary",)),
    )(x, w, dh, dres)


def _fill_ext(ext_ref, halo, tile, first, H):
    ext_ref[pl.ds(0, H), :] = jnp.where(first, 0.0, halo)
    ext_ref[pl.ds(H, tile.shape[0]), :] = tile


def _conv_taps(ext_ref, w_ref, K, H, tm):
    acc = ext_ref[pl.ds(H - (K - 1), tm), :] * w_ref[pl.ds(0, 1), :]
    for k in range(1, K):
        acc = acc + ext_ref[pl.ds(H - (K - 1) + k, tm), :] * w_ref[pl.ds(k, 1), :]
    return acc


def _halo_spec(H, tm, cw, col):
    return pl.BlockSpec((H, cw), lambda *g, _c=col: (jnp.maximum(g[-1] * (tm // H) - 1, 0), _c))


def conv_bwd(dc, srcs, w, *, K, H, tm, cw, glu, name):
    S, C = dc.shape
    nc, nt = C // cw, S // tm
    last_halo = S // H - 1
    n_src = 2 if glu else 1
    bases = [c0 // cw for _, c0 in srcs]

    def body(*refs):
        dc_ref, dcn_ref = refs[0], refs[1]
        src_refs = refs[2:2 + 2 * n_src]
        w_ref = refs[2 + 2 * n_src]
        outs = refs[3 + 2 * n_src:3 + 3 * n_src]
        dw_ref = refs[3 + 3 * n_src]
        ext_ref, dext_ref = refs[4 + 3 * n_src:]
        i = pl.program_id(1)
        first, last = i == 0, i == nt - 1
        if glu:
            a_ref, ah_ref, b_ref, bh_ref = src_refs
            sg = _sigmoid(b_ref[...])
            _fill_ext(ext_ref, ah_ref[...] * _sigmoid(bh_ref[...]), a_ref[...] * sg, first, H)
        else:
            x_ref, xh_ref = src_refs
            _fill_ext(ext_ref, xh_ref[...], x_ref[...], first, H)
        dc_t = dc_ref[...]
        dext_ref[pl.ds(0, tm), :] = dc_t
        dext_ref[pl.ds(tm, H), :] = jnp.where(last, 0.0, dcn_ref[...])
        dx = dext_ref[pl.ds(K - 1, tm), :] * w_ref[pl.ds(0, 1), :]
        for k in range(1, K):
            dx = dx + dext_ref[pl.ds(K - 1 - k, tm), :] * w_ref[pl.ds(k, 1), :]
        if glu:
            a = a_ref[...]
            outs[0][...] = (dx * sg).astype(BF16)
            outs[1][...] = (dx * a * sg * (1.0 - sg)).astype(BF16)
        else:
            outs[0][...] = dx.astype(BF16)

        @pl.when(first)
        def _():
            dw_ref[...] = jnp.zeros_like(dw_ref)

        for k in range(K):
            dw_ref[pl.ds(k, 1), :] += jnp.sum(dc_t * ext_ref[pl.ds(H - (K - 1) + k, tm), :], axis=0, keepdims=True)

    tile = lambda base: pl.BlockSpec((tm, cw), lambda j, i, _b=base: (i, _b + j))
    halo = lambda base: pl.BlockSpec((H, cw), lambda j, i, _b=base: (jnp.maximum(i * (tm // H) - 1, 0), _b + j))
    in_specs = [tile(0), pl.BlockSpec((H, cw), lambda j, i: (jnp.minimum((i + 1) * (tm // H), last_halo), j))]
    args = [dc, dc]
    for (arr, _), base in zip(srcs, bases):
        in_specs += [tile(base), halo(base)]
        args += [arr, arr]
    in_specs.append(pl.BlockSpec((K, cw), lambda j, i: (0, j)))
    args.append(w)
    out_specs = [tile(0)] * n_src + [pl.BlockSpec((K, cw), lambda j, i: (0, j))]
    out_shape = [jax.ShapeDtypeStruct((S, C), BF16)] * n_src + [jax.ShapeDtypeStruct((K, C), F32)]
    return pl.pallas_call(
        body, name=name, grid=(nc, nt), in_specs=in_specs, out_specs=out_specs, out_shape=out_shape,
        scratch_shapes=[pltpu.VMEM((tm + H, cw), F32), pltpu.VMEM((tm + H, cw), F32)],
        compiler_params=_cparams(("parallel", "arbitrary")),
    )(*args)


def _conf_post(c, gate, ln_w, ln_b, pw):
    mu = jnp.mean(c, axis=-1, keepdims=True)
    cc = c - mu
    var = jnp.mean(cc * cc, axis=-1, keepdims=True)
    hn = cc * lax.rsqrt(var + 1e-5) * ln_w + ln_b
    return mm_nn(_silu(hn), pw) * _silu(gate)


def _conf_specs(tm):
    H = CONF_HALO
    blk = lambda col: pl.BlockSpec((tm, CONV_CH), lambda i, _c=col: (i, _c))
    vec = pl.BlockSpec((1, CONV_CH), lambda i: (0, 0))
    specs = [blk(0), blk(1), blk(2), _halo_spec(H, tm, CONV_CH, 0), _halo_spec(H, tm, CONV_CH, 1),
             pl.BlockSpec((CONV_WIDTH, CONV_CH), lambda i: (0, 0)), vec, vec, vec,
             pl.BlockSpec((CONV_CH, CONV_CH), lambda i: (0, 0))]
    return specs, blk, vec


def _conf_conv(a_ref, b_ref, ah_ref, bh_ref, dww_ref, dwb_ref, ext_ref, tm):
    first = pl.program_id(0) == 0
    _fill_ext(ext_ref, ah_ref[...] * _sigmoid(bh_ref[...]), a_ref[...] * _sigmoid(b_ref[...]), first, CONF_HALO)
    return _conv_taps(ext_ref, dww_ref, CONV_WIDTH, CONF_HALO, tm) + dwb_ref[...]


def conf_fwd(u, dw_w, dw_b, ln_w, ln_b, pw, *, tm, name):
    S = u.shape[0]
    specs, blk, vec = _conf_specs(tm)

    def body(a_ref, b_ref, g_ref, ah_ref, bh_ref, dww_ref, dwb_ref, lnw_ref, lnb_ref, pw_ref, y_ref, ext_ref):
        c = _conf_conv(a_ref, b_ref, ah_ref, bh_ref, dww_ref, dwb_ref, ext_ref, tm)
        y_ref[...] = _conf_post(c, g_ref[...], lnw_ref[...], lnb_ref[...], pw_ref[...]).astype(BF16)

    return pl.pallas_call(
        body, name=name, grid=(S // tm,), in_specs=specs, out_specs=blk(0),
        out_shape=jax.ShapeDtypeStruct((S, CONV_CH), BF16),
        scratch_shapes=[pltpu.VMEM((tm + CONF_HALO, CONV_CH), F32)],
        compiler_params=_cparams(("parallel",)),
    )(u, u, u, u, u, dw_w, dw_b, ln_w, ln_b, pw)


def conf_bwd_post(u, dy, dw_w, dw_b, ln_w, ln_b, pw, *, tm, name):
    S = u.shape[0]
    specs, blk, vec = _conf_specs(tm)
    mat = pl.BlockSpec((CONV_CH, CONV_CH), lambda i: (0, 0))

    def body(a_ref, b_ref, g_ref, ah_ref, bh_ref, dww_ref, dwb_ref, lnw_ref, lnb_ref, pw_ref, dy_ref,
             dc_ref, dg_ref, dlnw_ref, dlnb_ref, dpw_ref, ddwb_ref, ext_ref):
        c = _conf_conv(a_ref, b_ref, ah_ref, bh_ref, dww_ref, dwb_ref, ext_ref, tm)
        _, vjp = jax.vjp(_conf_post, c, g_ref[...], lnw_ref[...], lnb_ref[...], pw_ref[...])
        dc, dg, dlnw, dlnb, dpw = vjp(dy_ref[...])
        dc_ref[...] = dc
        dg_ref[...] = dg.astype(BF16)

        @pl.when(pl.program_id(0) == 0)
        def _():
            dlnw_ref[...] = jnp.zeros_like(dlnw_ref)
            dlnb_ref[...] = jnp.zeros_like(dlnb_ref)
            dpw_ref[...] = jnp.zeros_like(dpw_ref)

            ddwb_ref[...] = jnp.zeros_like(ddwb_ref)

        dlnw_ref[...] += dlnw
        dlnb_ref[...] += dlnb
        dpw_ref[...] += dpw
        ddwb_ref[...] += jnp.sum(dc, axis=0, keepdims=True)

    return pl.pallas_call(
        body, name=name, grid=(S // tm,), in_specs=specs + [blk(0)], out_specs=[blk(0), blk(0), vec, vec, mat, vec],
        out_shape=[jax.ShapeDtypeStruct((S, CONV_CH), F32), jax.ShapeDtypeStruct((S, CONV_CH), BF16),
                   jax.ShapeDtypeStruct((1, CONV_CH), F32), jax.ShapeDtypeStruct((1, CONV_CH), F32),
                   jax.ShapeDtypeStruct((CONV_CH, CONV_CH), F32), jax.ShapeDtypeStruct((1, CONV_CH), F32)],
        scratch_shapes=[pltpu.VMEM((tm + CONF_HALO, CONV_CH), F32)],
        compiler_params=_cparams(("arbitrary",)),
    )(u, u, u, u, u, dw_w, dw_b, ln_w, ln_b, pw, dy)


def _iota2(shape, dim):
    return lax.broadcasted_iota(jnp.int32, shape, dim)


def _gdn_post(pre_q, pre_k, pre_v, b_in, a_in, a_log, dt_bias):
    tm = pre_q.shape[0]
    q, k, v = _silu(pre_q), _silu(pre_k), _silu(pre_v)
    qs, ks = [], []
    for h in range(GDN_HEADS):
        sl = slice(h * GDN_DK, (h + 1) * GDN_DK)
        qh, kh = q[:, sl], k[:, sl]
        qs.append(qh * lax.rsqrt(jnp.sum(qh * qh, axis=-1, keepdims=True) + 1e-6) * (GDN_DK ** -0.5))
        ks.append(kh * lax.rsqrt(jnp.sum(kh * kh, axis=-1, keepdims=True) + 1e-6))
    beta = _sigmoid(b_in)
    g = -jnp.exp(a_log) * _softplus(a_in + dt_bias)
    nb = tm // GDN_CHUNK
    tril = (_iota2((nb, GDN_CHUNK, GDN_CHUNK), 1) >= _iota2((nb, GDN_CHUNK, GDN_CHUNK), 2)).astype(F32)
    gc = mm_nn(tril, g.reshape(nb, GDN_CHUNK, 128), "f32").reshape(tm, 128)
    return jnp.concatenate(qs, axis=1), jnp.concatenate(ks, axis=1), v, beta, gc


def _gdn_prep_specs(tm):
    H = GDN_HALO
    blk = lambda col: pl.BlockSpec((tm, GDN_W), lambda i, _c=col: (i, _c))
    lane = lambda col: pl.BlockSpec((tm, 128), lambda i, _c=col: (i, _c))
    vec = pl.BlockSpec((1, 128), lambda i: (0, 0))
    q0 = C_GQ // GDN_W
    specs = [blk(q0), blk(q0 + 1), blk(q0 + 2),
             _halo_spec(H, tm, GDN_W, q0), _halo_spec(H, tm, GDN_W, q0 + 1), _halo_spec(H, tm, GDN_W, q0 + 2),
             lane(C_BETA // 128), lane(C_ALPHA // 128),
             pl.BlockSpec((SHORT_CONV, GDN_W), lambda i: (0, 0)), pl.BlockSpec((SHORT_CONV, GDN_W), lambda i: (0, 1)),
             pl.BlockSpec((SHORT_CONV, GDN_W), lambda i: (0, 2)), vec, vec]
    return specs, blk, lane, vec


def _gdn_pre(x_refs, h_refs, w_refs, ext_ref, tm):
    first = pl.program_id(0) == 0
    pres = []
    for x_ref, h_ref, w_ref in zip(x_refs, h_refs, w_refs):
        _fill_ext(ext_ref, h_ref[...], x_ref[...], first, GDN_HALO)
        pres.append(_conv_taps(ext_ref, w_ref, SHORT_CONV, GDN_HALO, tm))
    return pres


def gdn_prep_fwd(u, conv_w, a_log, dt_bias, *, tm, name):
    S = u.shape[0]
    specs, blk, lane, vec = _gdn_prep_specs(tm)

    def body(xq, xk, xv, hq, hk, hv, bi, ai, wq, wk, wv, al, db, q_ref, k_ref, v_ref, beta_ref, gc_ref, ext_ref):
        pres = _gdn_pre((xq, xk, xv), (hq, hk, hv), (wq, wk, wv), ext_ref, tm)
        q, k, v, beta, gc = _gdn_post(*pres, bi[...], ai[...], al[...], db[...])
        q_ref[...] = q
        k_ref[...] = k
        v_ref[...] = v
        beta_ref[...] = beta
        gc_ref[...] = gc

    wide = jax.ShapeDtypeStruct((S, GDN_W), F32)
    narrow = jax.ShapeDtypeStruct((S, 128), F32)
    return pl.pallas_call(
        body, name=name, grid=(S // tm,), in_specs=specs,
        out_specs=[blk(0), blk(0), blk(0), lane(0), lane(0)], out_shape=[wide, wide, wide, narrow, narrow],
        scratch_shapes=[pltpu.VMEM((tm + GDN_HALO, GDN_W), F32)],
        compiler_params=_cparams(("parallel",)),
    )(u, u, u, u, u, u, u, u, conv_w, conv_w, conv_w, a_log, dt_bias)


def gdn_prep_bwd(u, conv_w, a_log, dt_bias, dq, dk, dv, dbeta, dgc, *, tm, name):
    S = u.shape[0]
    specs, blk, lane, vec = _gdn_prep_specs(tm)

    def body(xq, xk, xv, hq, hk, hv, bi, ai, wq, wk, wv, al, db, dq_ref, dk_ref, dv_ref, dbe_ref, dgc_ref,
             dpq_ref, dpk_ref, dpv_ref, dbi_ref, dai_ref, dal_ref, ddb_ref, ext_ref):
        pres = _gdn_pre((xq, xk, xv), (hq, hk, hv), (wq, wk, wv), ext_ref, tm)
        _, vjp = jax.vjp(_gdn_post, *pres, bi[...], ai[...], al[...], db[...])
        dpq, dpk, dpv, dbi, dai, dal, ddb = vjp((dq_ref[...], dk_ref[...], dv_ref[...], dbe_ref[...], dgc_ref[...]))
        dpq_ref[...] = dpq
        dpk_ref[...] = dpk
        dpv_ref[...] = dpv
        dbi_ref[...] = dbi.astype(BF16)
        dai_ref[...] = dai.astype(BF16)

        @pl.when(pl.program_id(0) == 0)
        def _():
            dal_ref[...] = jnp.zeros_like(dal_ref)
            ddb_ref[...] = jnp.zeros_like(ddb_ref)

        dal_ref[...] += dal
        ddb_ref[...] += ddb

    wide = jax.ShapeDtypeStruct((S, GDN_W), F32)
    outs = pl.pallas_call(
        body, name=name, grid=(S // tm,), in_specs=specs + [blk(0), blk(0), blk(0), lane(0), lane(0)],
        out_specs=[blk(0), blk(0), blk(0), lane(0), lane(0), vec, vec],
        out_shape=[wide, wide, wide, jax.ShapeDtypeStruct((S, 128), BF16), jax.ShapeDtypeStruct((S, 128), BF16),
                   jax.ShapeDtypeStruct((1, 128), F32), jax.ShapeDtypeStruct((1, 128), F32)],
        scratch_shapes=[pltpu.VMEM((tm + GDN_HALO, GDN_W), F32)],
        compiler_params=_cparams(("arbitrary",)),
    )(u, u, u, u, u, u, u, u, conv_w, conv_w, conv_w, a_log, dt_bias, dq, dk, dv, dbeta, dgc)
    return outs


@jax.custom_vjp
def _tri_inv(low):
    n = low.shape[-1]
    r, c = _iota2(low.shape, low.ndim - 2), _iota2(low.shape, low.ndim - 1)
    eye = (r == c).astype(F32)
    t = eye - jnp.where((r // 2 == c // 2) & (r > c), low, 0.0)
    s = 2
    while s < n:
        off = jnp.where((r // (2 * s) == c // (2 * s)) & (r // s > c // s), low, 0.0)
        t = t - _nn_raw(t, _nn_raw(off, t, "f32"), "f32")
        s *= 2
    return t


def _tri_inv_f(low):
    t = _tri_inv(low)
    return t, t


def _tri_inv_b(t, dt):
    d = -_nt_raw(_tn_raw(t, dt, "f32"), t, "f32")
    r, c = _iota2(d.shape, d.ndim - 2), _iota2(d.shape, d.ndim - 1)
    return (jnp.where(r > c, d, 0.0),)


_tri_inv.defvjp(_tri_inv_f, _tri_inv_b)


def _gdn_group(s0, q, k, v, z, beta, gc, nw):
    C = GDN_CHUNK
    nb = q.shape[0] // C
    q3, k3, v3 = (t.reshape(nb, C, GDN_DK) for t in (q, k, v))
    b3, g3 = beta.reshape(nb, C, 1), gc.reshape(nb, C, 1)
    r, c = _iota2((nb, C, C), 1), _iota2((nb, C, C), 2)
    causal, strict = r >= c, r > c
    g_row = mm_nn(jnp.ones((nb, C, C), F32), jnp.where(r == c, g3, 0.0), "f32")
    decay = jnp.where(causal, jnp.exp(jnp.where(causal, g3 - g_row, 0.0)), 0.0)
    low = jnp.where(strict, b3 * mm_nt(k3, k3) * decay, 0.0)
    t = _tri_inv(low)
    eg = jnp.exp(g3)
    w_v = mm_nn(t, v3 * b3)
    w_k = mm_nn(t, k3 * (b3 * eg))
    qk = jnp.where(causal, mm_nt(q3, k3) * decay, 0.0)
    q_dec = q3 * eg
    g_last = jnp.sum(jnp.where(_iota2((nb, C, 1), 1) == C - 1, g3, 0.0), axis=1, keepdims=True)
    k_dec = k3 * jnp.exp(g_last - g3)
    e_last = jnp.exp(g_last)
    s, outs = s0, []
    for i in range(nb):
        v_new = w_v[i] - mm_nn(w_k[i], s)
        outs.append(mm_nn(q_dec[i], s) + mm_nn(qk[i], v_new))
        s = s * e_last[i] + mm_tn(k_dec[i], v_new)
    o = jnp.concatenate(outs, axis=0)
    y = o * lax.rsqrt(jnp.mean(o * o, axis=-1, keepdims=True) + 1e-6) * nw * _silu(z)
    return s, y


def _lane_col(blk, h):
    return jnp.sum(jnp.where(_iota2(blk.shape, 1) == h, blk, 0.0), axis=1, keepdims=True)


def gdn_core_fwd(q, k, v, u, beta, gc, nw, *, name):
    S = q.shape[0]
    R = GDN_CHUNK * GDN_GROUP
    G = S // R
    head = lambda g, h: (g, h)
    blk = pl.BlockSpec((R, GDN_DK), head)
    lane = pl.BlockSpec((R, 128), lambda g, h: (g, 0))
    st = pl.BlockSpec((1, 1, GDN_DK, GDN_DK), lambda g, h: (g, h, 0, 0))

    def body(q_ref, k_ref, v_ref, z_ref, be_ref, gc_ref, nw_ref, y_ref, st_ref, s_ref):
        g, h = pl.program_id(0), pl.program_id(1)
        s0 = jnp.where(g == 0, 0.0, s_ref[h])
        st_ref[0, 0] = s0
        s1, y = _gdn_group(s0, q_ref[...], k_ref[...], v_ref[...], z_ref[...],
                           _lane_col(be_ref[...], h), _lane_col(gc_ref[...], h), nw_ref[...])
        s_ref[h] = s1
        y_ref[...] = y.astype(BF16)

    return pl.pallas_call(
        body, name=name, grid=(G, GDN_HEADS),
        in_specs=[blk, blk, blk, pl.BlockSpec((R, GDN_DK), lambda g, h: (g, C_GZ // GDN_DK + h)), lane, lane,
                  pl.BlockSpec((1, 128), lambda g, h: (0, 0))],
        out_specs=[blk, st],
        out_shape=[jax.ShapeDtypeStruct((S, GDN_W), BF16), jax.ShapeDtypeStruct((G, GDN_HEADS, GDN_DK, GDN_DK), F32)],
        scratch_shapes=[pltpu.VMEM((GDN_HEADS, GDN_DK, GDN_DK), F32)],
        compiler_params=_cparams(("arbitrary", "arbitrary")),
    )(q, k, v, u, beta, gc, nw)


def gdn_core_bwd(q, k, v, u, beta, gc, nw, states, dy, *, name):
    S = q.shape[0]
    R = GDN_CHUNK * GDN_GROUP
    G = S // R
    blk = pl.BlockSpec((R, GDN_DK), lambda g, h: (G - 1 - g, h))
    lane = pl.BlockSpec((R, 128), lambda g, h: (G - 1 - g, 0))
    vec = pl.BlockSpec((1, 128), lambda g, h: (0, 0))

    def body(q_ref, k_ref, v_ref, z_ref, be_ref, gc_ref, nw_ref, st_ref, dy_ref,
             dq_ref, dk_ref, dv_ref, dz_ref, dbe_ref, dgc_ref, dnw_ref, ds_ref):
        g, h = pl.program_id(0), pl.program_id(1)
        _, vjp = jax.vjp(_gdn_group, st_ref[0, 0], q_ref[...], k_ref[...], v_ref[...], z_ref[...],
                         _lane_col(be_ref[...], h), _lane_col(gc_ref[...], h), nw_ref[...])
        ds_in = jnp.where(g == 0, 0.0, ds_ref[h])
        ds0, dq, dk, dv, dz, dbe, dgc, dnw = vjp((ds_in, dy_ref[...]))
        ds_ref[h] = ds0
        dq_ref[...] = dq
        dk_ref[...] = dk
        dv_ref[...] = dv
        dz_ref[...] = dz.astype(BF16)

        @pl.when(h == 0)
        def _():
            dbe_ref[...] = jnp.zeros_like(dbe_ref)
            dgc_ref[...] = jnp.zeros_like(dgc_ref)

        @pl.when((h == 0) & (g == 0))
        def _():
            dnw_ref[...] = jnp.zeros_like(dnw_ref)

        lane_h = _iota2((R, 128), 1) == h
        dbe_ref[...] += jnp.where(lane_h, dbe, 0.0)
        dgc_ref[...] += jnp.where(lane_h, dgc, 0.0)
        dnw_ref[...] += dnw

    wide = jax.ShapeDtypeStruct((S, GDN_W), F32)
    narrow = jax.ShapeDtypeStruct((S, 128), F32)
    return pl.pallas_call(
        body, name=name, grid=(G, GDN_HEADS),
        in_specs=[blk, blk, blk, pl.BlockSpec((R, GDN_DK), lambda g, h: (G - 1 - g, C_GZ // GDN_DK + h)), lane, lane, vec,
                  pl.BlockSpec((1, 1, GDN_DK, GDN_DK), lambda g, h: (G - 1 - g, h, 0, 0)),
                  pl.BlockSpec((R, GDN_DK), lambda g, h: (G - 1 - g, CONV_CH // GDN_DK + h))],
        out_specs=[blk, blk, blk, blk, lane, lane, vec],
        out_shape=[wide, wide, wide, jax.ShapeDtypeStruct((S, GDN_W), BF16), narrow, narrow,
                   jax.ShapeDtypeStruct((1, 128), F32)],
        scratch_shapes=[pltpu.VMEM((GDN_HEADS, GDN_DK, GDN_DK), F32)],
        compiler_params=_cparams(("arbitrary", "arbitrary")),
    )(q, k, v, u, beta, gc, nw, states, dy)


def rope_tables(S):
    half = ROPE_DIM // 2
    inv = ROPE_THETA ** (-jnp.arange(half, dtype=F32) / half)
    ang = jnp.arange(S, dtype=F32)[:, None] * inv[None, :]
    cos, sin = jnp.cos(ang), jnp.sin(ang)
    rest = ATT_HD - ROPE_DIM
    c = jnp.concatenate([cos, cos, jnp.ones((S, rest), F32)], axis=1)
    s1 = jnp.concatenate([-sin, jnp.zeros((S, ATT_HD - half), F32)], axis=1)
    s2 = jnp.concatenate([jnp.zeros((S, half), F32), sin, jnp.zeros((S, rest), F32)], axis=1)
    return tuple(jnp.tile(t, (1, 2)) for t in (c, s1, s2))


def _rope(x, c, s1, s2):
    half = ROPE_DIM // 2
    return x * c + pltpu.roll(x, ATT_W - half, 1) * s1 + pltpu.roll(x, half, 1) * s2


def _unrope(dy, c, s1, s2):
    half = ROPE_DIM // 2
    return dy * c + pltpu.roll(dy * s1, half, 1) + pltpu.roll(dy * s2, ATT_W - half, 1)


def att_prep_fwd(u, tables, *, tm, name):
    S = u.shape[0]
    blk = lambda col: pl.BlockSpec((tm, ATT_W), lambda i, _c=col: (i, _c))
    tab = pl.BlockSpec((tm, 128), lambda i: (i, 0))

    def body(q_ref, k_ref, v_ref, c_ref, s1_ref, s2_ref, qo_ref, ko_ref, vo_ref):
        reps = ATT_W // 128
        c, s1, s2 = (jnp.tile(t[...], (1, reps)) for t in (c_ref, s1_ref, s2_ref))
        qo_ref[...] = (_rope(q_ref[...], c, s1, s2) * (ATT_HD ** -0.5)).astype(BF16)
        ko_ref[...] = _rope(k_ref[...], c, s1, s2).astype(BF16)
        vo_ref[...] = v_ref[...].astype(BF16)

    out = jax.ShapeDtypeStruct((S, ATT_W), BF16)
    return pl.pallas_call(
        body, name=name, grid=(S // tm,),
        in_specs=[blk(C_AQ // ATT_W), blk(C_AK // ATT_W), blk(C_AV // ATT_W), tab, tab, tab],
        out_specs=[blk(0)] * 3, out_shape=[out] * 3, compiler_params=_cparams(("parallel",)),
    )(u, u, u, *tables)


def att_prep_bwd(dqs, dks, dvs, tables, *, tm, name):
    S = dqs[0].shape[0]
    blk = pl.BlockSpec((tm, ATT_W), lambda i: (i, 0))
    tab = pl.BlockSpec((tm, 128), lambda i: (i, 0))

    def body(*refs):
        dq, dk, dv = (refs[3 * j][...] + refs[3 * j + 1][...] + refs[3 * j + 2][...] for j in range(3))
        c_ref, s1_ref, s2_ref, qo_ref, ko_ref, vo_ref = refs[9:]
        reps = ATT_W // 128
        c, s1, s2 = (jnp.tile(t[...], (1, reps)) for t in (c_ref, s1_ref, s2_ref))
        qo_ref[...] = (_unrope(dq, c, s1, s2) * (ATT_HD ** -0.5)).astype(BF16)
        ko_ref[...] = _unrope(dk, c, s1, s2).astype(BF16)
        vo_ref[...] = dv.astype(BF16)

    out = jax.ShapeDtypeStruct((S, ATT_W), BF16)
    return pl.pallas_call(
        body, name=name, grid=(S // tm,), in_specs=[blk] * 9 + [tab] * 3, out_specs=[blk] * 3, out_shape=[out] * 3,
        compiler_params=_cparams(("parallel",)),
    )(*dqs, *dks, *dvs, *tables)


def _band_masks():
    qi, ki = _iota2((ATT_BLOCK, ATT_BLOCK), 0), _iota2((ATT_BLOCK, ATT_BLOCK), 1)
    return qi <= ki, ki <= qi


def att_pattern_fwd(qr, kr, vb, dil, *, name):
    S = qr.shape[0]
    L = S // dil
    nb = L // ATT_BLOCK
    view = lambda t: t.reshape(L, dil * ATT_W)
    cur = pl.BlockSpec((ATT_BLOCK, ATT_W), lambda r, n: (n, r))
    prev = pl.BlockSpec((ATT_BLOCK, ATT_W), lambda r, n: (jnp.maximum(n - 1, 0), r))

    def body(q_ref, kc_ref, kp_ref, vc_ref, vp_ref, o_ref, l_ref):
        has_prev = pl.program_id(1) > 0
        m_prev, m_cur = _band_masks()
        m_prev = m_prev & has_prev
        for h in range(ATT_HEADS):
            sl = slice(h * ATT_HD, (h + 1) * ATT_HD)
            q = q_ref[:, sl]
            sp = jnp.where(m_prev, _nt_raw(q, kp_ref[:, sl], "bf16"), NEG_INF)
            sc = jnp.where(m_cur, _nt_raw(q, kc_ref[:, sl], "bf16"), NEG_INF)
            m = jnp.maximum(jnp.max(sp, axis=1, keepdims=True), jnp.max(sc, axis=1, keepdims=True))
            pp, pc = jnp.exp(sp - m), jnp.exp(sc - m)
            den = jnp.sum(pp, axis=1, keepdims=True) + jnp.sum(pc, axis=1, keepdims=True)
            o = _nn_raw(pp, vp_ref[:, sl], "bf16") + _nn_raw(pc, vc_ref[:, sl], "bf16")
            o_ref[:, sl] = o / den
            l_ref[:, sl] = jnp.broadcast_to(m + jnp.log(den), (ATT_BLOCK, ATT_HD))

    out = jax.ShapeDtypeStruct((L, dil * ATT_W), F32)
    o, l = pl.pallas_call(
        body, name=name, grid=(dil, nb), in_specs=[cur, cur, prev, cur, prev], out_specs=[cur, cur],
        out_shape=[out, out], compiler_params=_cparams(("parallel", "arbitrary")),
    )(view(qr), view(kr), view(kr), view(vb), view(vb))
    return o.reshape(S, ATT_W), l.reshape(S, ATT_W)


def att_combine_fwd(os_, ls, u, *, tm, name):
    S = u.shape[0]
    blk = lambda col: pl.BlockSpec((tm, ATT_W), lambda i, _c=col: (i, _c))

    def body(o1, o2, o3, l1, l2, l3, g_ref, y_ref, o_ref, lse_ref):
        a, b, c = l1[...], l2[...], l3[...]
        m = jnp.maximum(jnp.maximum(a, b), c)
        ea, eb, ec = jnp.exp(a - m), jnp.exp(b - m), jnp.exp(c - m)
        den = ea + eb + ec
        o = (ea * o1[...] + eb * o2[...] + ec * o3[...]) / den
        o_ref[...] = o
        lse_ref[...] = m + jnp.log(den)
        y_ref[...] = (o * _silu(g_ref[...])).astype(BF16)

    wide = jax.ShapeDtypeStruct((S, ATT_W), F32)
    return pl.pallas_call(
        body, name=name, grid=(S // tm,), in_specs=[blk(0)] * 6 + [blk(C_AG // ATT_W)],
        out_specs=[blk(0)] * 3, out_shape=[jax.ShapeDtypeStruct((S, ATT_W), BF16), wide, wide],
        compiler_params=_cparams(("parallel",)),
    )(*os_, *ls, u)


def att_combine_bwd(dy, o, u, *, tm, name):
    S = u.shape[0]
    cw = 256
    blk = lambda base: pl.BlockSpec((tm, cw), lambda i, j, _b=base: (i, _b + j))

    def body(dy_ref, o_ref, g_ref, do_ref, dg_ref):
        g, d = g_ref[...], dy_ref[...]
        sg = _sigmoid(g)
        do_ref[...] = d * (g * sg)
        dg_ref[...] = (d * o_ref[...] * (sg * (1.0 + g * (1.0 - sg)))).astype(BF16)

    return pl.pallas_call(
        body, name=name, grid=(S // tm, ATT_W // cw),
        in_specs=[blk((CONV_CH + GDN_W) // cw), blk(0), blk(C_AG // cw)], out_specs=[blk(0), blk(0)],
        out_shape=[jax.ShapeDtypeStruct((S, ATT_W), F32), jax.ShapeDtypeStruct((S, ATT_W), BF16)],
        compiler_params=_cparams(("parallel", "parallel")),
    )(dy, o, u)


def att_pattern_bwd(qr, kr, vb, do, o, lse, dil, *, name):
    S = qr.shape[0]
    L = S // dil
    nb = L // ATT_BLOCK
    view = lambda t: t.reshape(L, dil * ATT_W)
    cur = pl.BlockSpec((ATT_BLOCK, ATT_W), lambda r, n: (jnp.minimum(n, nb - 1), r))
    prev = pl.BlockSpec((ATT_BLOCK, ATT_W), lambda r, n: (jnp.maximum(n - 1, 0), r))

    def body(q_ref, kc_ref, kp_ref, vc_ref, vp_ref, do_ref, o_ref, l_ref, dq_ref, dk_ref, dv_ref, ck_ref, cv_ref):
        n = pl.program_id(1)

        @pl.when(n < nb)
        def _():
            m_prev, m_cur = _band_masks()
            m_prev = m_prev & (n > 0)
            for h in range(ATT_HEADS):
                sl = slice(h * ATT_HD, (h + 1) * ATT_HD)
                q, kp, kc, vp, vc = q_ref[:, sl], kp_ref[:, sl], kc_ref[:, sl], vp_ref[:, sl], vc_ref[:, sl]
                d_o = do_ref[:, sl]
                lse_h = l_ref[:, h * ATT_HD:h * ATT_HD + 1]
                delta = jnp.sum(d_o * o_ref[:, sl], axis=1, keepdims=True)
                pp = jnp.where(m_prev, jnp.exp(_nt_raw(q, kp, "bf16") - lse_h), 0.0)
                pc = jnp.where(m_cur, jnp.exp(_nt_raw(q, kc, "bf16") - lse_h), 0.0)
                dsp = pp * (_nt_raw(d_o, vp, "bf16") - delta)
                dsc = pc * (_nt_raw(d_o, vc, "bf16") - delta)
                dq_ref[:, sl] = _nn_raw(dsp, kp, "bf16") + _nn_raw(dsc, kc, "bf16")
                dk_prev, dv_prev = _tn_raw(dsp, q, "bf16"), _tn_raw(pp, d_o, "bf16")

                @pl.when(n > 0)
                def _():
                    dk_ref[:, sl] = ck_ref[:, sl] + dk_prev
                    dv_ref[:, sl] = cv_ref[:, sl] + dv_prev

                ck_ref[:, sl] = _tn_raw(dsc, q, "bf16")
                cv_ref[:, sl] = _tn_raw(pc, d_o, "bf16")

        @pl.when(n == nb)
        def _():
            dk_ref[...] = ck_ref[...]
            dv_ref[...] = cv_ref[...]

    out = jax.ShapeDtypeStruct((L, dil * ATT_W), F32)
    dq, dk, dv = pl.pallas_call(
        body, name=name, grid=(dil, nb + 1), in_specs=[cur, cur, prev, cur, prev, cur, cur, cur],
        out_specs=[cur, prev, prev], out_shape=[out, out, out],
        scratch_shapes=[pltpu.VMEM((ATT_BLOCK, ATT_W), F32), pltpu.VMEM((ATT_BLOCK, ATT_W), F32)],
        compiler_params=_cparams(("arbitrary", "arbitrary")),
    )(view(qr), view(kr), view(kr), view(vb), view(vb), view(do), view(o), view(lse))
    return dq.reshape(S, ATT_W), dk.reshape(S, ATT_W), dv.reshape(S, ATT_W)


def _loss_rows(x, w, tgt):
    err = _rms_fn(x, w) - tgt
    return jnp.sum(0.5 * jnp.mean(err * err, axis=-1, keepdims=True), axis=0, keepdims=True)


def loss_head(x, w, tgt, *, tm, name):
    S, D = x.shape

    def body(x_ref, w_ref, t_ref, l_ref, dx_ref, dw_ref):
        val, vjp = jax.vjp(_loss_rows, x_ref[...], w_ref[...], t_ref[...])
        dx, dw, _ = vjp(jnp.ones((1, 1), F32))
        dx_ref[...] = dx

        @pl.when(pl.program_id(0) == 0)
        def _():
            l_ref[...] = jnp.zeros_like(l_ref)
            dw_ref[...] = jnp.zeros_like(dw_ref)

        l_ref[...] += val
        dw_ref[...] += dw

    row = pl.BlockSpec((tm, D), lambda i: (i, 0))
    vec = pl.BlockSpec((1, D), lambda i: (0, 0))
    one = pl.BlockSpec((1, 1), lambda i: (0, 0))
    return pl.pallas_call(
        body, name=name, grid=(S // tm,), in_specs=[row, vec, row], out_specs=[one, row, vec],
        out_shape=[jax.ShapeDtypeStruct((1, 1), F32), jax.ShapeDtypeStruct((S, D), F32), jax.ShapeDtypeStruct((1, D), F32)],
        compiler_params=_cparams(("arbitrary",)),
    )(x, w, tgt)


def adam(w, g, m, v, *, name):
    shape = w.shape
    C = shape[-1]
    R = w.size // C
    br = R
    while br * C * 4 > (1 << 21) and br % 16 == 0:
        br //= 2
    two = lambda t: t.reshape(R, C)

    def body(w_ref, g_ref, m_ref, v_ref, d_ref, mo_ref, vo_ref):
        gg = g_ref[...]
        m_new = ADAM_B1 * m_ref[...] + (1.0 - ADAM_B1) * gg
        v_new = ADAM_B2 * v_ref[...] + (1.0 - ADAM_B2) * jnp.square(gg)
        m_hat = m_new / (1.0 - ADAM_B1 ** ADAM_STEP)
        v_hat = v_new / (1.0 - ADAM_B2 ** ADAM_STEP)
        d_ref[...] = -ADAM_LR * (m_hat / (jnp.sqrt(v_hat) + ADAM_EPS) + ADAM_WD * w_ref[...])
        mo_ref[...] = m_new
        vo_ref[...] = v_new

    blk = pl.BlockSpec((br, C), lambda i: (i, 0))
    out = jax.ShapeDtypeStruct((R, C), F32)
    d, mo, vo = pl.pallas_call(
        body, name=name, grid=(R // br,), in_specs=[blk] * 4, out_specs=[blk] * 3, out_shape=[out] * 3,
        compiler_params=_cparams(("parallel",)),
    )(two(w), two(g), two(m), two(v))
    return d.reshape(shape), mo.reshape(shape), vo.reshape(shape)


MESH_IDS = pl.DeviceIdType.MESH
ANY = pl.BlockSpec(memory_space=pl.ANY)


def _my_id():
    return 4 * lax.axis_index("x") + 2 * lax.axis_index("y") + lax.axis_index("c")


def _peer(k):
    x, y, c = lax.axis_index("x"), lax.axis_index("y"), lax.axis_index("c")
    flip = lambda v, bit: 1 - v if bit else v
    return (flip(x, k & 4), flip(y, k & 2), flip(c, k & 1))


def all_gather(arrs, *, name):
    n = len(arrs)

    def body(*refs):
        ins, outs = refs[:n], refs[n:2 * n]
        send, recv, local = refs[2 * n:]
        me = _my_id()
        started = []
        for a in range(n):
            lc = pltpu.make_async_copy(ins[a], outs[a].at[me], local.at[a])
            lc.start()
            started.append(lc)
            for k in range(1, N_DEV):
                cp = pltpu.make_async_remote_copy(src_ref=ins[a], dst_ref=outs[a].at[me], send_sem=send.at[a, k - 1],
                                                  recv_sem=recv.at[a, k - 1], device_id=_peer(k), device_id_type=MESH_IDS)
                cp.start()
                started.append(cp)
        for cp in started:
            cp.wait()

    return pl.pallas_call(
        body, name=name, in_specs=[ANY] * n, out_specs=[ANY] * n,
        out_shape=[jax.ShapeDtypeStruct((N_DEV,) + a.shape, a.dtype) for a in arrs],
        scratch_shapes=[pltpu.SemaphoreType.DMA((n, N_DEV - 1)), pltpu.SemaphoreType.DMA((n, N_DEV - 1)),
                        pltpu.SemaphoreType.DMA((n,))],
        compiler_params=pltpu.CompilerParams(has_side_effects=True),
    )(*arrs)


def scatter_exchange(groups, pack, *, name):
    flat = [a for grp in groups for a in grp]
    n = len(flat) + 1
    shapes = [jax.ShapeDtypeStruct((N_DEV, len(grp), grp[0].shape[0] // N_DEV, grp[0].shape[1]), grp[0].dtype) for grp in groups]
    shapes.append(jax.ShapeDtypeStruct((N_DEV,) + pack.shape, pack.dtype))
    index = [(gi, li) for gi, grp in enumerate(groups) for li in range(len(grp))]

    def body(*refs):
        ins, outs = refs[:n], refs[n:n + len(shapes)]
        send, recv, local = refs[n + len(shapes):]
        me = _my_id()
        started = []
        for a in range(n):
            if a < n - 1:
                gi, li = index[a]
                r = ins[a].shape[0] // N_DEV
                src = lambda j, _a=a, _r=r: ins[_a].at[pl.ds(pl.multiple_of(j * _r, 8), _r), :]
                dst = outs[gi].at[me, li]
            else:
                src = lambda j, _a=a: ins[_a]
                dst = outs[-1].at[me]
            lc = pltpu.make_async_copy(src(me), dst, local.at[a])
            lc.start()
            started.append(lc)
            for k in range(1, N_DEV):
                cp = pltpu.make_async_remote_copy(src_ref=src(me ^ k), dst_ref=dst, send_sem=send.at[a, k - 1],
                                                  recv_sem=recv.at[a, k - 1], device_id=_peer(k), device_id_type=MESH_IDS)
                cp.start()
                started.append(cp)
        for cp in started:
            cp.wait()

    return pl.pallas_call(
        body, name=name, in_specs=[ANY] * n, out_specs=[ANY] * len(shapes), out_shape=shapes,
        scratch_shapes=[pltpu.SemaphoreType.DMA((n, N_DEV - 1)), pltpu.SemaphoreType.DMA((n, N_DEV - 1)),
                        pltpu.SemaphoreType.DMA((n,))],
        compiler_params=pltpu.CompilerParams(has_side_effects=True),
    )(*flat, pack)


def slot_sum(x, *, name):
    _, A, R, C = x.shape
    br = R
    while br * C * 4 * N_DEV > (1 << 23) and br % 16 == 0:
        br //= 2

    def body(x_ref, o_ref):
        acc = x_ref[0, 0]
        for s in range(1, N_DEV):
            acc = acc + x_ref[s, 0]
        o_ref[0] = acc

    return pl.pallas_call(
        body, name=name, grid=(A, R // br),
        in_specs=[pl.BlockSpec((N_DEV, 1, br, C), lambda a, i: (0, a, i, 0))],
        out_specs=pl.BlockSpec((1, br, C), lambda a, i: (a, i, 0)),
        out_shape=jax.ShapeDtypeStruct((A, R, C), x.dtype),
        compiler_params=_cparams(("parallel", "parallel")),
    )(x)


def _pack(arrs):
    flat = []
    for a in arrs:
        f = a.reshape(-1).astype(F32)
        flat.append(jnp.pad(f, (0, (-f.size) % 128)))
    f = jnp.concatenate(flat)
    return jnp.pad(f, (0, (-f.size) % 1024)).reshape(-1, 128)


def _unpack(p, shapes):
    f = p.reshape(-1)
    out, off = [], 0
    for s in shapes:
        n = math.prod(s)
        out.append(f[off:off + n].reshape(s))
        off += n + (-n) % 128
    return out


def _to_padded_cols(w):
    z = lambda n: jnp.zeros(w.shape[:-1] + (n,), w.dtype)
    return jnp.concatenate([w[..., 0:4608], w[..., 4620:7692], w[..., 4608:4614], z(122), w[..., 4614:4620], z(378)], axis=-1)


def _from_padded_cols(w):
    return jnp.concatenate([w[..., 0:4608], w[..., C_BETA:C_BETA + 6], w[..., C_ALPHA:C_ALPHA + 6], w[..., 4608:7680]], axis=-1)


def _lane_pad(v):
    return jnp.pad(v, (0, 128 - v.shape[0]))[None, :]


TM_MM, TN_MM, TK_MM = 1024, 1024, 2048
TM_ROW = 512


def layer_fwd(x, p, tabs, l):
    h = rms_fwd(x, p["norm_w"], tm=TM_ROW, name=f"rms_fwd_{l}")
    u = matmul(h, p["w_in"], mode="nn", tm=TM_MM, tn=TN_MM, tk=TK_MM, name=f"in_proj_{l}")
    y_conv = conf_fwd(u, p["dw_w"], p["dw_b"], p["ln_w"], p["ln_b"], p["pw"], tm=TM_ROW, name=f"conf_fwd_{l}")
    q, k, v, beta, gc = gdn_prep_fwd(u, p["conv_w"], p["a_log"], p["dt_bias"], tm=TM_ROW, name=f"gdn_prep_fwd_{l}")
    y_gdn, states = gdn_core_fwd(q, k, v, u, beta, gc, p["gdn_nw"], name=f"gdn_core_fwd_{l}")
    qr, kr, vb = att_prep_fwd(u, tabs, tm=TM_ROW, name=f"att_prep_fwd_{l}")
    os_, ls = [], []
    for _, dil in DIL_PATTERNS:
        o_p, l_p = att_pattern_fwd(qr, kr, vb, dil, name=f"att_fwd_d{dil}_{l}")
        os_.append(o_p)
        ls.append(l_p)
    y_att, o, lse = att_combine_fwd(os_, ls, u, tm=TM_ROW, name=f"att_combine_fwd_{l}")
    y = jnp.concatenate([y_conv, y_gdn, y_att], axis=1)
    x_new = matmul(y, p["w_out"], mode="nn", tm=TM_MM, tn=TN_MM, tk=TK_MM, residual=x, name=f"out_proj_{l}")
    saved = dict(x=x, h=h, u=u, y=y, q=q, k=k, v=v, beta=beta, gc=gc, states=states, qr=qr, kr=kr, vb=vb, o=o, lse=lse)
    return x_new, saved


def layer_bwd(dx_out, s, p, tabs, l):
    S = dx_out.shape[0]
    u = s["u"]
    dy = matmul(dx_out, p["w_out"], mode="nt", tm=TM_MM, tn=TN_MM, tk=TK_MM, name=f"out_proj_dy_{l}")
    g_w_out = matmul(s["y"], dx_out, mode="tn", tm=TM_MM, tn=TN_MM, tk=TK_MM, name=f"out_proj_dw_{l}")
    dc, d_cg, g_ln_w, g_ln_b, g_pw, g_dw_b = conf_bwd_post(u, dy, p["dw_w"], p["dw_b"], p["ln_w"], p["ln_b"], p["pw"],
                                                         tm=TM_ROW, name=f"conf_bwd_post_{l}")
    d_ca, d_cb, g_dw_w = conv_bwd(dc, [(u, C_CA), (u, C_CB)], p["dw_w"], K=CONV_WIDTH, H=CONF_HALO, tm=TM_ROW, cw=CONV_CH,
                                  glu=True, name=f"conf_bwd_conv_{l}")
    dq, dk, dv, d_gz, dbeta, dgc, g_gdn_nw = gdn_core_bwd(s["q"], s["k"], s["v"], u, s["beta"], s["gc"], p["gdn_nw"],
                                                          s["states"], dy, name=f"gdn_core_bwd_{l}")
    dpq, dpk, dpv, d_bi, d_ai, g_a_log, g_dt_bias = gdn_prep_bwd(u, p["conv_w"], p["a_log"], p["dt_bias"], dq, dk, dv, dbeta, dgc,
                                                                 tm=TM_ROW, name=f"gdn_prep_bwd_{l}")
    d_gqkv, g_conv_w = conv_bwd(jnp.concatenate([dpq, dpk, dpv], axis=1), [(u, C_GQ)], p["conv_w"], K=SHORT_CONV, H=GDN_HALO,
                                tm=TM_ROW, cw=GDN_W, glu=False, name=f"gdn_bwd_conv_{l}")
    do, d_ag = att_combine_bwd(dy, s["o"], u, tm=TM_ROW, name=f"att_combine_bwd_{l}")
    dqs, dks, dvs = [], [], []
    for _, dil in DIL_PATTERNS:
        a, b, c = att_pattern_bwd(s["qr"], s["kr"], s["vb"], do, s["o"], s["lse"], dil, name=f"att_bwd_d{dil}_{l}")
        dqs.append(a)
        dks.append(b)
        dvs.append(c)
    d_aq, d_ak, d_av = att_prep_bwd(dqs, dks, dvs, tabs, tm=TM_ROW, name=f"att_prep_bwd_{l}")
    zeros = jnp.zeros((S, IN_PAD - C_ALPHA - 128), BF16)
    du = jnp.concatenate([d_ca, d_cb, d_cg, d_gqkv, d_gz, d_aq, d_ak, d_av, d_ag, d_bi, d_ai, zeros], axis=1)
    dh = matmul(du, p["w_in"], mode="nt", tm=TM_MM, tn=TN_MM, tk=TK_MM, name=f"in_proj_dh_{l}")
    g_w_in = matmul(s["h"], du, mode="tn", tm=TM_MM, tn=TN_MM, tk=TK_MM, name=f"in_proj_dw_{l}")
    dx, g_norm_w = rms_bwd(s["x"], p["norm_w"], dh, dx_out, tm=TM_ROW // 2, name=f"rms_bwd_{l}")
    grads = dict(norm_w=g_norm_w[0], w_in=g_w_in, conv_qkv_w=g_conv_w, a_log=g_a_log[0, :GDN_HEADS], dt_bias=g_dt_bias[0, :GDN_HEADS],
                 gdn_norm_w=g_gdn_nw[0], conf_dw_w=g_dw_w, conf_dw_b=g_dw_b[0], conf_ln_w=g_ln_w[0], conf_ln_b=g_ln_b[0],
                 conf_pw_w=g_pw, w_out=g_w_out)
    return dx, grads


WEIGHTS = ("norm_w", "w_in", "conv_qkv_w", "a_log", "dt_bias", "gdn_norm_w", "conf_dw_w", "conf_dw_b", "conf_ln_w",
           "conf_ln_b", "conf_pw_w", "w_out", "final_norm_w")
SMALL_REPLICATED = ("norm_w", "a_log", "dt_bias", "gdn_norm_w", "conf_dw_b", "conf_ln_w", "conf_ln_b")


def kernel(x, norm_w, w_in, conv_qkv_w, a_log, dt_bias, gdn_norm_w, conf_dw_w, conf_dw_b, conf_ln_w, conf_ln_b, conf_pw_w, w_out, final_norm_w, loss_target, m_norm_w, m_w_in, m_conv_qkv_w, m_a_log, m_dt_bias, m_gdn_norm_w, m_conf_dw_w, m_conf_dw_b, m_conf_ln_w, m_conf_ln_b, m_conf_pw_w, m_w_out, m_final_norm_w, v_norm_w, v_w_in, v_conv_qkv_w, v_a_log, v_dt_bias, v_gdn_norm_w, v_conf_dw_w, v_conf_dw_b, v_conf_ln_w, v_conf_ln_b, v_conf_pw_w, v_w_out, v_final_norm_w):
    w = dict(norm_w=norm_w, w_in=w_in, conv_qkv_w=conv_qkv_w, a_log=a_log, dt_bias=dt_bias, gdn_norm_w=gdn_norm_w,
             conf_dw_w=conf_dw_w, conf_dw_b=conf_dw_b, conf_ln_w=conf_ln_w, conf_ln_b=conf_ln_b, conf_pw_w=conf_pw_w,
             w_out=w_out, final_norm_w=final_norm_w)
    m = dict(zip(WEIGHTS, (m_norm_w, m_w_in, m_conv_qkv_w, m_a_log, m_dt_bias, m_gdn_norm_w, m_conf_dw_w, m_conf_dw_b,
                           m_conf_ln_w, m_conf_ln_b, m_conf_pw_w, m_w_out, m_final_norm_w)))
    v = dict(zip(WEIGHTS, (v_norm_w, v_w_in, v_conv_qkv_w, v_a_log, v_dt_bias, v_gdn_norm_w, v_conf_dw_w, v_conf_dw_b,
                           v_conf_ln_w, v_conf_ln_b, v_conf_pw_w, v_w_out, v_final_norm_w)))
    S = x.shape[1]
    L = norm_w.shape[0]
    me = _my_id()

    small_shapes = [conv_qkv_w.shape, conf_dw_w.shape, conf_pw_w.shape]
    g_in, g_out, g_small = all_gather(
        [_to_padded_cols(w_in).astype(BF16), w_out.astype(BF16), _pack([conv_qkv_w, conf_dw_w, conf_pw_w])], name="gather_weights")
    parts = [_unpack(g_small[s], small_shapes) for s in range(N_DEV)]
    conv_full = jnp.concatenate([pt[0] for pt in parts], axis=2)
    dw_full = jnp.concatenate([pt[1] for pt in parts], axis=2)
    pw_full = jnp.concatenate([pt[2] for pt in parts], axis=1)
    tabs = rope_tables(S)

    params = []
    for l in range(L):
        params.append(dict(
            norm_w=norm_w[l][None], w_in=g_in[:, l].reshape(D_MODEL, IN_PAD), w_out=g_out[:, l].reshape(D_MODEL, D_MODEL),
            conv_w=conv_full[l], a_log=_lane_pad(a_log[l]), dt_bias=_lane_pad(dt_bias[l]), gdn_nw=gdn_norm_w[l][None],
            dw_w=dw_full[l], dw_b=conf_dw_b[l][None], ln_w=conf_ln_w[l][None], ln_b=conf_ln_b[l][None], pw=pw_full[l]))

    xs = x[0]
    saved = []
    for l in range(L):
        xs, sv = layer_fwd(xs, params[l], tabs, l)
        saved.append(sv)
    loss_part, dx, g_final = loss_head(xs, final_norm_w[None], loss_target[0], tm=TM_ROW // 2, name="loss_head")
    layer_grads = [None] * L
    for l in reversed(range(L)):
        dx, layer_grads[l] = layer_bwd(dx, saved[l], params[l], tabs, l)

    stack = lambda name: jnp.stack([layer_grads[l][name] for l in range(L)])
    small = [loss_part] + [stack(n) for n in SMALL_REPLICATED] + [g_final[0], stack("conv_qkv_w"), stack("conf_dw_w")]
    small_shapes = [a.shape for a in small]
    r_in, r_out, r_pw, r_small = scatter_exchange(
        [[layer_grads[l]["w_in"] for l in range(L)], [layer_grads[l]["w_out"] for l in range(L)],
         [layer_grads[l]["conf_pw_w"] for l in range(L)]], _pack(small), name="scatter_grads")
    g = {}
    g["w_in"] = _from_padded_cols(slot_sum(r_in, name="sum_w_in"))
    g["w_out"] = slot_sum(r_out, name="sum_w_out")
    g["conf_pw_w"] = slot_sum(r_pw, name="sum_pw")
    summed = _unpack(slot_sum(r_small[:, None], name="sum_small")[0], small_shapes)
    loss = summed[0].reshape(())
    for n, a in zip(SMALL_REPLICATED, summed[1:1 + len(SMALL_REPLICATED)]):
        g[n] = a
    g["final_norm_w"] = summed[-3]
    g["conv_qkv_w"] = lax.dynamic_slice_in_dim(summed[-2], me * conv_qkv_w.shape[2], conv_qkv_w.shape[2], axis=2)
    g["conf_dw_w"] = lax.dynamic_slice_in_dim(summed[-1], me * conf_dw_w.shape[2], conf_dw_w.shape[2], axis=2)

    deltas, new_m, new_v = {}, {}, {}
    for n in WEIGHTS:
        deltas[n], new_m[n], new_v[n] = adam(w[n], g[n], m[n], v[n], name=f"adam_{n}")
    return (loss, dx[None], *[g[n] for n in WEIGHTS], *[deltas[n] for n in WEIGHTS],
            *[new_m[n] for n in WEIGHTS], *[new_v[n] for n in WEIGHTS])
```

```python
import functools
import math

import jax
import jax.numpy as jnp
from jax import lax
from jax.experimental import pallas as pl
from jax.experimental.pallas import tpu as pltpu

D_MODEL = 2048
DEPTH = 4
N_DEV = 8
GDN_DK = 128
GDN_HEADS = 6
GDN_W = 768
ATT_HD = 64
ATT_HEADS = 12
ATT_W = 768
CONV_CH = 512
CONV_WIDTH = 31
SHORT_CONV = 4
GDN_CHUNK = 64
ROPE_THETA = 500000.0
ROPE_DIM = 16
DIL_PATTERNS = ((128, 1), (512, 4), (2048, 16))
ATT_BLOCK = 128
NEG_INF = -1e30
IN_W = 7692

ADAM_LR = 0.001
ADAM_B1 = 0.9
ADAM_B2 = 0.999
ADAM_EPS = 1e-08
ADAM_WD = 0.01
ADAM_STEP = 10

C_CA, C_CB, C_CG = 0, 512, 1024
C_GQ, C_GK, C_GV, C_GZ = 1536, 2304, 3072, 3840
C_AQ, C_AK, C_AV, C_AG = 4608, 5376, 6144, 6912
C_BETA, C_ALPHA = 7680, 7808
IN_PAD = 8192

VMEM_LIMIT = 56 * 1024 * 1024
CONF_HALO = 32
GDN_HALO = 8
GDN_GROUP = 4
GDN_HEADS_PER_STEP = 6

F32 = jnp.float32
BF16 = jnp.bfloat16
HI = lax.Precision.HIGHEST


def _cparams(sem, vmem=VMEM_LIMIT):
    return pltpu.CompilerParams(dimension_semantics=sem, vmem_limit_bytes=vmem)


def _dg(a, b, ca, cb, prec):
    nb = a.ndim - 2
    batch = tuple(range(nb))
    dn = (((ca + nb,), (cb + nb,)), (batch, batch))
    if prec == "bf16":
        return lax.dot_general(a.astype(BF16), b.astype(BF16), dn, preferred_element_type=F32)
    if prec == "bf16x3":
        ah, bh = a.astype(BF16), b.astype(BF16)
        al, bl = (a - ah.astype(F32)).astype(BF16), (b - bh.astype(F32)).astype(BF16)
        dot = lambda x, y: lax.dot_general(x, y, dn, preferred_element_type=F32)
        return dot(ah, bh) + (dot(ah, bl) + dot(al, bh))
    return lax.dot_general(a.astype(F32), b.astype(F32), dn, precision=HI, preferred_element_type=F32)


def _nn_raw(a, b, prec):
    return _dg(a, b, 1, 0, prec)


def _nt_raw(a, b, prec):
    return _dg(a, b, 1, 1, prec)


def _tn_raw(a, b, prec):
    return _dg(a, b, 0, 0, prec)


@functools.partial(jax.custom_vjp, nondiff_argnums=(2,))
def mm_nn(a, b, prec="bf16"):
    return _nn_raw(a, b, prec)


def _mm_nn_f(a, b, prec):
    return _nn_raw(a, b, prec), (a, b)


def _mm_nn_b(prec, res, g):
    a, b = res
    return _nt_raw(g, b, prec).astype(a.dtype), _tn_raw(a, g, prec).astype(b.dtype)


mm_nn.defvjp(_mm_nn_f, _mm_nn_b)


@functools.partial(jax.custom_vjp, nondiff_argnums=(2,))
def mm_nt(a, b, prec="bf16"):
    return _nt_raw(a, b, prec)


def _mm_nt_f(a, b, prec):
    return _nt_raw(a, b, prec), (a, b)


def _mm_nt_b(prec, res, g):
    a, b = res
    return _nn_raw(g, b, prec).astype(a.dtype), _tn_raw(g, a, prec).astype(b.dtype)


mm_nt.defvjp(_mm_nt_f, _mm_nt_b)


@functools.partial(jax.custom_vjp, nondiff_argnums=(2,))
def mm_tn(a, b, prec="bf16"):
    return _tn_raw(a, b, prec)


def _mm_tn_f(a, b, prec):
    return _tn_raw(a, b, prec), (a, b)


def _mm_tn_b(prec, res, g):
    a, b = res
    return _nt_raw(b, g, prec).astype(a.dtype), _nn_raw(a, g, prec).astype(b.dtype)


mm_tn.defvjp(_mm_tn_f, _mm_tn_b)


def _sigmoid(x):
    return 1.0 / (1.0 + jnp.exp(-x))


def _silu(x):
    return x * _sigmoid(x)


def _softplus(x):
    return jnp.maximum(x, 0.0) + jnp.log(1.0 + jnp.exp(-jnp.abs(x)))


def matmul(a, b, *, mode, tm, tn, tk, out_dtype=F32, residual=None, name):
    if mode == "tn":
        K, M = a.shape
    else:
        M, K = a.shape
    N = b.shape[0] if mode == "nt" else b.shape[1]
    assert M % tm == 0 and N % tn == 0 and K % tk == 0, (a.shape, b.shape, tm, tn, tk)
    nk = K // tk
    a_spec = pl.BlockSpec((tk, tm), lambda i, j, k: (k, i)) if mode == "tn" else pl.BlockSpec((tm, tk), lambda i, j, k: (i, k))
    b_spec = pl.BlockSpec((tn, tk), lambda i, j, k: (j, k)) if mode == "nt" else pl.BlockSpec((tk, tn), lambda i, j, k: (k, j))
    o_spec = pl.BlockSpec((tm, tn), lambda i, j, k: (i, j))
    raw = {"nn": _nn_raw, "nt": _nt_raw, "tn": _tn_raw}[mode]
    has_res = residual is not None

    def body(*refs):
        if has_res:
            a_ref, b_ref, r_ref, o_ref, acc_ref = refs
        else:
            a_ref, b_ref, o_ref, acc_ref = refs
        k = pl.program_id(2)
        part = raw(a_ref[...], b_ref[...], "bf16")

        @pl.when(k == 0)
        def _():
            acc_ref[...] = part

        @pl.when(k > 0)
        def _():
            acc_ref[...] += part

        @pl.when(k == nk - 1)
        def _():
            r = acc_ref[...]
            if has_res:
                r = r + r_ref[...].astype(F32)
            o_ref[...] = r.astype(out_dtype)

    in_specs = [a_spec, b_spec] + ([o_spec] if has_res else [])
    args = (a, b) + ((residual,) if has_res else ())
    return pl.pallas_call(
        body, name=name, grid=(M // tm, N // tn, nk), in_specs=in_specs, out_specs=o_spec,
        out_shape=jax.ShapeDtypeStruct((M, N), out_dtype),
        scratch_shapes=[pltpu.VMEM((tm, tn), F32)],
        compiler_params=_cparams(("parallel", "parallel", "arbitrary")),
    )(*args)


def _rms_fn(x, w, eps=1e-6):
    return x * lax.rsqrt(jnp.mean(x * x, axis=-1, keepdims=True) + eps) * w


def rms_fwd(x, w, *, tm, name):
    S, D = x.shape

    def body(x_ref, w_ref, o_ref):
        o_ref[...] = _rms_fn(x_ref[...], w_ref[...]).astype(BF16)

    return pl.pallas_call(
        body, name=name, grid=(S // tm,),
        in_specs=[pl.BlockSpec((tm, D), lambda i: (i, 0)), pl.BlockSpec((1, D), lambda i: (0, 0))],
        out_specs=pl.BlockSpec((tm, D), lambda i: (i, 0)),
        out_shape=jax.ShapeDtypeStruct((S, D), BF16),
        compiler_params=_cparams(("parallel",)),
    )(x, w)


def rms_bwd(x, w, dh, dres, *, tm, name):
    S, D = x.shape

    def body(x_ref, w_ref, dh_ref, dr_ref, dx_ref, dw_ref):
        _, vjp = jax.vjp(_rms_fn, x_ref[...], w_ref[...])
        dx, dw = vjp(dh_ref[...].astype(F32))
        dx_ref[...] = dx + dr_ref[...]

        @pl.when(pl.program_id(0) == 0)
        def _():
            dw_ref[...] = jnp.zeros_like(dw_ref)

        dw_ref[...] += dw

    row = pl.BlockSpec((tm, D), lambda i: (i, 0))
    vec = pl.BlockSpec((1, D), lambda i: (0, 0))
    return pl.pallas_call(
        body, name=name, grid=(S // tm,), in_specs=[row, vec, row, row], out_specs=[row, vec],
        out_shape=[jax.ShapeDtypeStruct((S, D), F32), jax.ShapeDtypeStruct((1, D), F32)],
        compiler_params=_cparams(("arbitrary",)),
    )(x, w, dh, dres)


def _fill_ext(ext_ref, halo, tile, first, H):
    ext_ref[pl.ds(0, H), :] = jnp.where(first, 0.0, halo)
    ext_ref[pl.ds(H, tile.shape[0]), :] = tile


def _conv_taps(ext_ref, w_ref, K, H, tm):
    acc = ext_ref[pl.ds(H - (K - 1), tm), :] * w_ref[pl.ds(0, 1), :]
    for k in range(1, K):
        acc = acc + ext_ref[pl.ds(H - (K - 1) + k, tm), :] * w_ref[pl.ds(k, 1), :]
    return acc


def _halo_spec(H, tm, cw, col):
    return pl.BlockSpec((H, cw), lambda *g, _c=col: (jnp.maximum(g[-1] * (tm // H) - 1, 0), _c))


def conv_bwd(dc, srcs, w, *, K, H, tm, cw, glu, name):
    S, C = dc.shape
    nc, nt = C // cw, S // tm
    last_halo = S // H - 1
    n_src = 2 if glu else 1
    bases = [c0 // cw for _, c0 in srcs]

    def body(*refs):
        dc_ref, dcn_ref = refs[0], refs[1]
        src_refs = refs[2:2 + 2 * n_src]
        w_ref = refs[2 + 2 * n_src]
        outs = refs[3 + 2 * n_src:3 + 3 * n_src]
        dw_ref = refs[3 + 3 * n_src]
        ext_ref, dext_ref = refs[4 + 3 * n_src:]
        i = pl.program_id(1)
        first, last = i == 0, i == nt - 1
        if glu:
            a_ref, ah_ref, b_ref, bh_ref = src_refs
            sg = _sigmoid(b_ref[...])
            _fill_ext(ext_ref, ah_ref[...] * _sigmoid(bh_ref[...]), a_ref[...] * sg, first, H)
        else:
            x_ref, xh_ref = src_refs
            _fill_ext(ext_ref, xh_ref[...], x_ref[...], first, H)
        dc_t = dc_ref[...]
        dext_ref[pl.ds(0, tm), :] = dc_t
        dext_ref[pl.ds(tm, H), :] = jnp.where(last, 0.0, dcn_ref[...])
        dx = dext_ref[pl.ds(K - 1, tm), :] * w_ref[pl.ds(0, 1), :]
        for k in range(1, K):
            dx = dx + dext_ref[pl.ds(K - 1 - k, tm), :] * w_ref[pl.ds(k, 1), :]
        if glu:
            a = a_ref[...]
            outs[0][...] = (dx * sg).astype(BF16)
            outs[1][...] = (dx * a * sg * (1.0 - sg)).astype(BF16)
        else:
            outs[0][...] = dx.astype(BF16)

        @pl.when(first)
        def _():
            dw_ref[...] = jnp.zeros_like(dw_ref)

        for k in range(K):
            dw_ref[pl.ds(k, 1), :] += jnp.sum(dc_t * ext_ref[pl.ds(H - (K - 1) + k, tm), :], axis=0, keepdims=True)

    tile = lambda base: pl.BlockSpec((tm, cw), lambda j, i, _b=base: (i, _b + j))
    halo = lambda base: pl.BlockSpec((H, cw), lambda j, i, _b=base: (jnp.maximum(i * (tm // H) - 1, 0), _b + j))
    in_specs = [tile(0), pl.BlockSpec((H, cw), lambda j, i: (jnp.minimum((i + 1) * (tm // H), last_halo), j))]
    args = [dc, dc]
    for (arr, _), base in zip(srcs, bases):
        in_specs += [tile(base), halo(base)]
        args += [arr, arr]
    in_specs.append(pl.BlockSpec((K, cw), lambda j, i: (0, j)))
    args.append(w)
    out_specs = [tile(0)] * n_src + [pl.BlockSpec((K, cw), lambda j, i: (0, j))]
    out_shape = [jax.ShapeDtypeStruct((S, C), BF16)] * n_src + [jax.ShapeDtypeStruct((K, C), F32)]
    return pl.pallas_call(
        body, name=name, grid=(nc, nt), in_specs=in_specs, out_specs=out_specs, out_shape=out_shape,
        scratch_shapes=[pltpu.VMEM((tm + H, cw), F32), pltpu.VMEM((tm + H, cw), F32)],
        compiler_params=_cparams(("parallel", "arbitrary")),
    )(*args)


def _conf_post(c, gate, ln_w, ln_b, pw):
    mu = jnp.mean(c, axis=-1, keepdims=True)
    cc = c - mu
    var = jnp.mean(cc * cc, axis=-1, keepdims=True)
    hn = cc * lax.rsqrt(var + 1e-5) * ln_w + ln_b
    return mm_nn(_silu(hn), pw) * _silu(gate)


def _conf_specs(tm):
    H = CONF_HALO
    blk = lambda col: pl.BlockSpec((tm, CONV_CH), lambda i, _c=col: (i, _c))
    vec = pl.BlockSpec((1, CONV_CH), lambda i: (0, 0))
    specs = [blk(0), blk(1), blk(2), _halo_spec(H, tm, CONV_CH, 0), _halo_spec(H, tm, CONV_CH, 1),
             pl.BlockSpec((CONV_WIDTH, CONV_CH), lambda i: (0, 0)), vec, vec, vec,
             pl.BlockSpec((CONV_CH, CONV_CH), lambda i: (0, 0))]
    return specs, blk, vec


def _conf_conv(a_ref, b_ref, ah_ref, bh_ref, dww_ref, dwb_ref, ext_ref, tm):
    first = pl.program_id(0) == 0
    _fill_ext(ext_ref, ah_ref[...] * _sigmoid(bh_ref[...]), a_ref[...] * _sigmoid(b_ref[...]), first, CONF_HALO)
    return _conv_taps(ext_ref, dww_ref, CONV_WIDTH, CONF_HALO, tm) + dwb_ref[...]


def conf_fwd(u, dw_w, dw_b, ln_w, ln_b, pw, *, tm, name):
    S = u.shape[0]
    specs, blk, vec = _conf_specs(tm)

    def body(a_ref, b_ref, g_ref, ah_ref, bh_ref, dww_ref, dwb_ref, lnw_ref, lnb_ref, pw_ref, y_ref, ext_ref):
        c = _conf_conv(a_ref, b_ref, ah_ref, bh_ref, dww_ref, dwb_ref, ext_ref, tm)
        y_ref[...] = _conf_post(c, g_ref[...], lnw_ref[...], lnb_ref[...], pw_ref[...]).astype(BF16)

    return pl.pallas_call(
        body, name=name, grid=(S // tm,), in_specs=specs, out_specs=blk(0),
        out_shape=jax.ShapeDtypeStruct((S, CONV_CH), BF16),
        scratch_shapes=[pltpu.VMEM((tm + CONF_HALO, CONV_CH), F32)],
        compiler_params=_cparams(("parallel",)),
    )(u, u, u, u, u, dw_w, dw_b, ln_w, ln_b, pw)


def conf_bwd_post(u, dy, dw_w, dw_b, ln_w, ln_b, pw, *, tm, name):
    S = u.shape[0]
    specs, blk, vec = _conf_specs(tm)
    mat = pl.BlockSpec((CONV_CH, CONV_CH), lambda i: (0, 0))

    def body(a_ref, b_ref, g_ref, ah_ref, bh_ref, dww_ref, dwb_ref, lnw_ref, lnb_ref, pw_ref, dy_ref,
             dc_ref, dg_ref, dlnw_ref, dlnb_ref, dpw_ref, ddwb_ref, ext_ref):
        c = _conf_conv(a_ref, b_ref, ah_ref, bh_ref, dww_ref, dwb_ref, ext_ref, tm)
        _, vjp = jax.vjp(_conf_post, c, g_ref[...], lnw_ref[...], lnb_ref[...], pw_ref[...])
        dc, dg, dlnw, dlnb, dpw = vjp(dy_ref[...])
        dc_ref[...] = dc
        dg_ref[...] = dg.astype(BF16)

        @pl.when(pl.program_id(0) == 0)
        def _():
            dlnw_ref[...] = jnp.zeros_like(dlnw_ref)
            dlnb_ref[...] = jnp.zeros_like(dlnb_ref)
            dpw_ref[...] = jnp.zeros_like(dpw_ref)

            ddwb_ref[...] = jnp.zeros_like(ddwb_ref)

        dlnw_ref[...] += dlnw
        dlnb_ref[...] += dlnb
        dpw_ref[...] += dpw
        ddwb_ref[...] += jnp.sum(dc, axis=0, keepdims=True)

    return pl.pallas_call(
        body, name=name, grid=(S // tm,), in_specs=specs + [blk(0)], out_specs=[blk(0), blk(0), vec, vec, mat, vec],
        out_shape=[jax.ShapeDtypeStruct((S, CONV_CH), F32), jax.ShapeDtypeStruct((S, CONV_CH), BF16),
                   jax.ShapeDtypeStruct((1, CONV_CH), F32), jax.ShapeDtypeStruct((1, CONV_CH), F32),
                   jax.ShapeDtypeStruct((CONV_CH, CONV_CH), F32), jax.ShapeDtypeStruct((1, CONV_CH), F32)],
        scratch_shapes=[pltpu.VMEM((tm + CONF_HALO, CONV_CH), F32)],
        compiler_params=_cparams(("arbitrary",)),
    )(u, u, u, u, u, dw_w, dw_b, ln_w, ln_b, pw, dy)


def _iota2(shape, dim):
    return lax.broadcasted_iota(jnp.int32, shape, dim)


def _gdn_post(pre_q, pre_k, pre_v, b_in, a_in, a_log, dt_bias):
    tm = pre_q.shape[0]
    q, k, v = _silu(pre_q), _silu(pre_k), _silu(pre_v)
    qs, ks = [], []
    for h in range(GDN_HEADS):
        sl = slice(h * GDN_DK, (h + 1) * GDN_DK)
        qh, kh = q[:, sl], k[:, sl]
        qs.append(qh * lax.rsqrt(jnp.sum(qh * qh, axis=-1, keepdims=True) + 1e-6) * (GDN_DK ** -0.5))
        ks.append(kh * lax.rsqrt(jnp.sum(kh * kh, axis=-1, keepdims=True) + 1e-6))
    beta = _sigmoid(b_in)
    g = -jnp.exp(a_log) * _softplus(a_in + dt_bias)
    nb = tm // GDN_CHUNK
    tril = (_iota2((nb, GDN_CHUNK, GDN_CHUNK), 1) >= _iota2((nb, GDN_CHUNK, GDN_CHUNK), 2)).astype(F32)
    gc = mm_nn(tril, g.reshape(nb, GDN_CHUNK, 128), "f32").reshape(tm, 128)
    return jnp.concatenate(qs, axis=1), jnp.concatenate(ks, axis=1), v, beta, gc


def _gdn_prep_specs(tm):
    H = GDN_HALO
    blk = lambda col: pl.BlockSpec((tm, GDN_W), lambda i, _c=col: (i, _c))
    lane = lambda col: pl.BlockSpec((tm, 128), lambda i, _c=col: (i, _c))
    vec = pl.BlockSpec((1, 128), lambda i: (0, 0))
    q0 = C_GQ // GDN_W
    specs = [blk(q0), blk(q0 + 1), blk(q0 + 2),
             _halo_spec(H, tm, GDN_W, q0), _halo_spec(H, tm, GDN_W, q0 + 1), _halo_spec(H, tm, GDN_W, q0 + 2),
             lane(C_BETA // 128), lane(C_ALPHA // 128),
             pl.BlockSpec((SHORT_CONV, GDN_W), lambda i: (0, 0)), pl.BlockSpec((SHORT_CONV, GDN_W), lambda i: (0, 1)),
             pl.BlockSpec((SHORT_CONV, GDN_W), lambda i: (0, 2)), vec, vec]
    return specs, blk, lane, vec


def _gdn_pre(x_refs, h_refs, w_refs, ext_ref, tm):
    first = pl.program_id(0) == 0
    pres = []
    for x_ref, h_ref, w_ref in zip(x_refs, h_refs, w_refs):
        _fill_ext(ext_ref, h_ref[...], x_ref[...], first, GDN_HALO)
        pres.append(_conv_taps(ext_ref, w_ref, SHORT_CONV, GDN_HALO, tm))
    return pres


def gdn_prep_fwd(u, conv_w, a_log, dt_bias, *, tm, name):
    S = u.shape[0]
    specs, blk, lane, vec = _gdn_prep_specs(tm)

    def body(xq, xk, xv, hq, hk, hv, bi, ai, wq, wk, wv, al, db, q_ref, k_ref, v_ref, beta_ref, gc_ref, ext_ref):
        pres = _gdn_pre((xq, xk, xv), (hq, hk, hv), (wq, wk, wv), ext_ref, tm)
        q, k, v, beta, gc = _gdn_post(*pres, bi[...], ai[...], al[...], db[...])
        q_ref[...] = q
        k_ref[...] = k
        v_ref[...] = v
        beta_ref[...] = beta
        gc_ref[...] = gc

    wide = jax.ShapeDtypeStruct((S, GDN_W), F32)
    narrow = jax.ShapeDtypeStruct((S, 128), F32)
    return pl.pallas_call(
        body, name=name, grid=(S // tm,), in_specs=specs,
        out_specs=[blk(0), blk(0), blk(0), lane(0), lane(0)], out_shape=[wide, wide, wide, narrow, narrow],
        scratch_shapes=[pltpu.VMEM((tm + GDN_HALO, GDN_W), F32)],
        compiler_params=_cparams(("parallel",)),
    )(u, u, u, u, u, u, u, u, conv_w, conv_w, conv_w, a_log, dt_bias)


def gdn_prep_bwd(u, conv_w, a_log, dt_bias, dq, dk, dv, dbeta, dgc, *, tm, name):
    S = u.shape[0]
    specs, blk, lane, vec = _gdn_prep_specs(tm)

    def body(xq, xk, xv, hq, hk, hv, bi, ai, wq, wk, wv, al, db, dq_ref, dk_ref, dv_ref, dbe_ref, dgc_ref,
             dpq_ref, dpk_ref, dpv_ref, dbi_ref, dai_ref, dal_ref, ddb_ref, ext_ref):
        pres = _gdn_pre((xq, xk, xv), (hq, hk, hv), (wq, wk, wv), ext_ref, tm)
        _, vjp = jax.vjp(_gdn_post, *pres, bi[...], ai[...], al[...], db[...])
        dpq, dpk, dpv, dbi, dai, dal, ddb = vjp((dq_ref[...], dk_ref[...], dv_ref[...], dbe_ref[...], dgc_ref[...]))
        dpq_ref[...] = dpq
        dpk_ref[...] = dpk
        dpv_ref[...] = dpv
        dbi_ref[...] = dbi.astype(BF16)
        dai_ref[...] = dai.astype(BF16)

        @pl.when(pl.program_id(0) == 0)
        def _():
            dal_ref[...] = jnp.zeros_like(dal_ref)
            ddb_ref[...] = jnp.zeros_like(ddb_ref)

        dal_ref[...] += dal
        ddb_ref[...] += ddb

    wide = jax.ShapeDtypeStruct((S, GDN_W), F32)
    outs = pl.pallas_call(
        body, name=name, grid=(S // tm,), in_specs=specs + [blk(0), blk(0), blk(0), lane(0), lane(0)],
        out_specs=[blk(0), blk(0), blk(0), lane(0), lane(0), vec, vec],
        out_shape=[wide, wide, wide, jax.ShapeDtypeStruct((S, 128), BF16), jax.ShapeDtypeStruct((S, 128), BF16),
                   jax.ShapeDtypeStruct((1, 128), F32), jax.ShapeDtypeStruct((1, 128), F32)],
        scratch_shapes=[pltpu.VMEM((tm + GDN_HALO, GDN_W), F32)],
        compiler_params=_cparams(("arbitrary",)),
    )(u, u, u, u, u, u, u, u, conv_w, conv_w, conv_w, a_log, dt_bias, dq, dk, dv, dbeta, dgc)
    return outs


def _lane_col(blk, h):
    return jnp.sum(jnp.where(_iota2(blk.shape, 1) == h, blk, 0.0), axis=1, keepdims=True)


@jax.custom_vjp
def _tri_inv(low):
    n = low.shape[-1]
    r, c = _iota2(low.shape, low.ndim - 2), _iota2(low.shape, low.ndim - 1)
    eye = (r == c).astype(F32)
    t = eye - jnp.where((r // 2 == c // 2) & (r > c), low, 0.0)
    s = 2
    while s < n:
        off = jnp.where((r // (2 * s) == c // (2 * s)) & (r // s > c // s), low, 0.0)
        t = t - _nn_raw(t, _nn_raw(off, t, "bf16x3"), "bf16x3")
        s *= 2
    return t


def _tri_inv_f(low):
    t = _tri_inv(low)
    return t, t


def _tri_inv_b(t, dt):
    d = -_nt_raw(_tn_raw(t, dt, "bf16x3"), t, "bf16x3")
    r, c = _iota2(d.shape, d.ndim - 2), _iota2(d.shape, d.ndim - 1)
    return (jnp.where(r > c, d, 0.0),)


_tri_inv.defvjp(_tri_inv_f, _tri_inv_b)


def _gdn_group(s0, q, k, v, z, beta_blk, gc_blk, nw, h0):
    C = GDN_CHUNK
    HP, R, _ = q.shape
    nb = R // C
    B = HP * nb
    q3, k3, v3 = (t.reshape(B, C, GDN_DK) for t in (q, k, v))
    b3 = jnp.stack([_lane_col(beta_blk, h0 + j) for j in range(HP)]).reshape(B, C, 1)
    g3 = jnp.stack([_lane_col(gc_blk, h0 + j) for j in range(HP)]).reshape(B, C, 1)
    r, c = _iota2((B, C, C), 1), _iota2((B, C, C), 2)
    causal, strict = r >= c, r > c
    g_t = gc_blk.T
    rows = [jnp.sum(jnp.where(_iota2((128, R), 0) == h0 + j, g_t, 0.0), axis=0, keepdims=True) for j in range(HP)]
    g_row = jnp.stack([rows[j][:, i * C:(i + 1) * C] for j in range(HP) for i in range(nb)])
    decay = jnp.where(causal, jnp.exp(jnp.where(causal, g3 - g_row, 0.0)), 0.0)
    low = jnp.where(strict, b3 * mm_nt(k3, k3) * decay, 0.0)
    t = _tri_inv(low)
    eg = jnp.exp(g3)
    four = lambda x: x.reshape((HP, nb) + x.shape[1:])
    w_v = four(mm_nn(t, v3 * b3))
    w_k = four(mm_nn(t, k3 * (b3 * eg)))
    qk = four(jnp.where(causal, mm_nt(q3, k3) * decay, 0.0))
    q_dec = four(q3 * eg)
    g_last = jnp.sum(jnp.where(_iota2((B, C, 1), 1) == C - 1, g3, 0.0), axis=1, keepdims=True)
    k_dec = four(k3 * jnp.exp(g_last - g3))
    e_last = four(jnp.exp(g_last))
    s, outs = s0, []
    for i in range(nb):
        v_new = w_v[:, i] - mm_nn(w_k[:, i], s)
        outs.append(mm_nn(q_dec[:, i], s) + mm_nn(qk[:, i], v_new))
        s = s * e_last[:, i] + mm_tn(k_dec[:, i], v_new)
    o = jnp.concatenate(outs, axis=1)
    y = o * lax.rsqrt(jnp.mean(o * o, axis=-1, keepdims=True) + 1e-6) * nw * _silu(z)
    return s, y


def _heads(ref, HP):
    return jnp.stack([ref[:, j * GDN_DK:(j + 1) * GDN_DK] for j in range(HP)])


def gdn_core_fwd(q, k, v, u, beta, gc, nw, *, name):
    S = q.shape[0]
    R = GDN_CHUNK * GDN_GROUP
    G = S // R
    HP = GDN_HEADS_PER_STEP
    W = HP * GDN_DK
    blk = pl.BlockSpec((R, W), lambda g, h: (g, h))
    lane = pl.BlockSpec((R, 128), lambda g, h: (g, 0))
    st = pl.BlockSpec((1, HP, GDN_DK, GDN_DK), lambda g, h: (g, h, 0, 0))

    def body(q_ref, k_ref, v_ref, z_ref, be_ref, gc_ref, nw_ref, y_ref, st_ref, s_ref):
        g, hs = pl.program_id(0), pl.program_id(1)
        s0 = jnp.where(g == 0, 0.0, s_ref[hs])
        st_ref[0] = s0
        s1, y = _gdn_group(s0, _heads(q_ref, HP), _heads(k_ref, HP), _heads(v_ref, HP), _heads(z_ref, HP), be_ref[...],
                           gc_ref[...], nw_ref[...], hs * HP)
        s_ref[hs] = s1
        for j in range(HP):
            y_ref[:, j * GDN_DK:(j + 1) * GDN_DK] = y[j].astype(BF16)

    return pl.pallas_call(
        body, name=name, grid=(G, GDN_HEADS // HP),
        in_specs=[blk, blk, blk, pl.BlockSpec((R, W), lambda g, h: (g, C_GZ // W + h)), lane, lane,
                  pl.BlockSpec((1, 128), lambda g, h: (0, 0))],
        out_specs=[blk, st],
        out_shape=[jax.ShapeDtypeStruct((S, GDN_W), BF16), jax.ShapeDtypeStruct((G, GDN_HEADS, GDN_DK, GDN_DK), F32)],
        scratch_shapes=[pltpu.VMEM((GDN_HEADS // HP, HP, GDN_DK, GDN_DK), F32)],
        compiler_params=_cparams(("arbitrary", "arbitrary")),
    )(q, k, v, u, beta, gc, nw)


def gdn_core_bwd(q, k, v, u, beta, gc, nw, states, dy, *, name):
    S = q.shape[0]
    R = GDN_CHUNK * GDN_GROUP
    G = S // R
    HP = GDN_HEADS_PER_STEP
    W = HP * GDN_DK
    blk = pl.BlockSpec((R, W), lambda g, h: (G - 1 - g, h))
    lane = pl.BlockSpec((R, 128), lambda g, h: (G - 1 - g, 0))
    vec = pl.BlockSpec((1, 128), lambda g, h: (0, 0))

    def body(q_ref, k_ref, v_ref, z_ref, be_ref, gc_ref, nw_ref, st_ref, *rest):
        dy_refs = rest[:HP]
        dq_ref, dk_ref, dv_ref, dz_ref, dbe_ref, dgc_ref, dnw_ref, ds_ref = rest[HP:]
        g, hs = pl.program_id(0), pl.program_id(1)

        @pl.when(hs == 0)
        def _():
            dbe_ref[...] = jnp.zeros_like(dbe_ref)
            dgc_ref[...] = jnp.zeros_like(dgc_ref)

        @pl.when((hs == 0) & (g == 0))
        def _():
            dnw_ref[...] = jnp.zeros_like(dnw_ref)

        _, vjp = jax.vjp(functools.partial(_gdn_group, h0=hs * HP), st_ref[0], _heads(q_ref, HP), _heads(k_ref, HP),
                         _heads(v_ref, HP), _heads(z_ref, HP), be_ref[...], gc_ref[...], nw_ref[...])
        ds_in = jnp.where(g == 0, 0.0, ds_ref[hs])
        dy = jnp.stack([r[...] for r in dy_refs])
        ds0, dq, dk, dv, dz, dbe, dgc, dnw = vjp((ds_in, dy))
        ds_ref[hs] = ds0
        for j in range(HP):
            sl = slice(j * GDN_DK, (j + 1) * GDN_DK)
            dq_ref[:, sl] = dq[j]
            dk_ref[:, sl] = dk[j]
            dv_ref[:, sl] = dv[j]
            dz_ref[:, sl] = dz[j].astype(BF16)
        dbe_ref[...] += dbe
        dgc_ref[...] += dgc
        dnw_ref[...] += dnw

    wide = jax.ShapeDtypeStruct((S, GDN_W), F32)
    narrow = jax.ShapeDtypeStruct((S, 128), F32)
    return pl.pallas_call(
        body, name=name, grid=(G, GDN_HEADS // HP),
        in_specs=[blk, blk, blk, pl.BlockSpec((R, W), lambda g, h: (G - 1 - g, C_GZ // W + h)), lane, lane, vec,
                  pl.BlockSpec((1, HP, GDN_DK, GDN_DK), lambda g, h: (G - 1 - g, h, 0, 0))]
        + [pl.BlockSpec((R, GDN_DK), lambda g, h, _j=j: (G - 1 - g, CONV_CH // GDN_DK + h * HP + _j)) for j in range(HP)],
        out_specs=[blk, blk, blk, blk, lane, lane, vec],
        out_shape=[wide, wide, wide, jax.ShapeDtypeStruct((S, GDN_W), BF16), narrow, narrow,
                   jax.ShapeDtypeStruct((1, 128), F32)],
        scratch_shapes=[pltpu.VMEM((GDN_HEADS // HP, HP, GDN_DK, GDN_DK), F32)],
        compiler_params=_cparams(("arbitrary", "arbitrary")),
    )(q, k, v, u, beta, gc, nw, states, *([dy] * HP))


def rope_tables(S):
    half = ROPE_DIM // 2
    inv = ROPE_THETA ** (-jnp.arange(half, dtype=F32) / half)
    ang = jnp.arange(S, dtype=F32)[:, None] * inv[None, :]
    cos, sin = jnp.cos(ang), jnp.sin(ang)
    rest = ATT_HD - ROPE_DIM
    c = jnp.concatenate([cos, cos, jnp.ones((S, rest), F32)], axis=1)
    s1 = jnp.concatenate([-sin, jnp.zeros((S, ATT_HD - half), F32)], axis=1)
    s2 = jnp.concatenate([jnp.zeros((S, half), F32), sin, jnp.zeros((S, rest), F32)], axis=1)
    return tuple(jnp.tile(t, (1, 2)) for t in (c, s1, s2))


def _rope(x, c, s1, s2):
    half = ROPE_DIM // 2
    return x * c + pltpu.roll(x, ATT_W - half, 1) * s1 + pltpu.roll(x, half, 1) * s2


def _unrope(dy, c, s1, s2):
    half = ROPE_DIM // 2
    return dy * c + pltpu.roll(dy * s1, half, 1) + pltpu.roll(dy * s2, ATT_W - half, 1)


def att_prep_fwd(u, tables, *, tm, name):
    S = u.shape[0]
    blk = lambda col: pl.BlockSpec((tm, ATT_W), lambda i, _c=col: (i, _c))
    tab = pl.BlockSpec((tm, 128), lambda i: (i, 0))

    def body(q_ref, k_ref, v_ref, c_ref, s1_ref, s2_ref, qo_ref, ko_ref, vo_ref):
        reps = ATT_W // 128
        c, s1, s2 = (jnp.tile(t[...], (1, reps)) for t in (c_ref, s1_ref, s2_ref))
        qo_ref[...] = (_rope(q_ref[...], c, s1, s2) * (ATT_HD ** -0.5)).astype(BF16)
        ko_ref[...] = _rope(k_ref[...], c, s1, s2).astype(BF16)
        vo_ref[...] = v_ref[...].astype(BF16)

    out = jax.ShapeDtypeStruct((S, ATT_W), BF16)
    return pl.pallas_call(
        body, name=name, grid=(S // tm,),
        in_specs=[blk(C_AQ // ATT_W), blk(C_AK // ATT_W), blk(C_AV // ATT_W), tab, tab, tab],
        out_specs=[blk(0)] * 3, out_shape=[out] * 3, compiler_params=_cparams(("parallel",)),
    )(u, u, u, *tables)


def att_prep_bwd(dqs, dks, dvs, tables, *, tm, name):
    S = dqs[0].shape[0]
    blk = pl.BlockSpec((tm, ATT_W), lambda i: (i, 0))
    tab = pl.BlockSpec((tm, 128), lambda i: (i, 0))

    def body(*refs):
        dq, dk, dv = (refs[3 * j][...] + refs[3 * j + 1][...] + refs[3 * j + 2][...] for j in range(3))
        c_ref, s1_ref, s2_ref, qo_ref, ko_ref, vo_ref = refs[9:]
        reps = ATT_W // 128
        c, s1, s2 = (jnp.tile(t[...], (1, reps)) for t in (c_ref, s1_ref, s2_ref))
        qo_ref[...] = (_unrope(dq, c, s1, s2) * (ATT_HD ** -0.5)).astype(BF16)
        ko_ref[...] = _unrope(dk, c, s1, s2).astype(BF16)
        vo_ref[...] = dv.astype(BF16)

    out = jax.ShapeDtypeStruct((S, ATT_W), BF16)
    return pl.pallas_call(
        body, name=name, grid=(S // tm,), in_specs=[blk] * 9 + [tab] * 3, out_specs=[blk] * 3, out_shape=[out] * 3,
        compiler_params=_cparams(("parallel",)),
    )(*dqs, *dks, *dvs, *tables)


def _band_masks():
    qi, ki = _iota2((ATT_BLOCK, ATT_BLOCK), 0), _iota2((ATT_BLOCK, ATT_BLOCK), 1)
    return qi <= ki, ki <= qi


def _pair_diag(x):
    first = _iota2(x.shape, 1) < ATT_HD
    zero = jnp.zeros_like(x)
    return jnp.concatenate([jnp.where(first, x, zero), jnp.where(first, zero, x)], axis=0)


def att_pattern_fwd(qr, kr, vb, dil, *, name):
    S = qr.shape[0]
    L = S // dil
    nb = L // ATT_BLOCK
    view = lambda t: t.reshape(L, dil * ATT_W)
    cur = pl.BlockSpec((ATT_BLOCK, ATT_W), lambda r, n: (n, r))
    prev = pl.BlockSpec((ATT_BLOCK, ATT_W), lambda r, n: (jnp.maximum(n - 1, 0), r))

    def body(q_ref, kc_ref, kp_ref, vc_ref, vp_ref, o_ref, l_ref):
        has_prev = pl.program_id(1) > 0
        m_prev, m_cur = _band_masks()
        m_prev = m_prev & has_prev
        first = _iota2((ATT_BLOCK, 128), 1) < ATT_HD
        for p in range(ATT_HEADS // 2):
            sl = slice(p * 128, (p + 1) * 128)
            q = q_ref[:, sl]
            sp = _nt_raw(q, _pair_diag(kp_ref[:, sl]), "bf16")
            sc = _nt_raw(q, _pair_diag(kc_ref[:, sl]), "bf16")
            pps, pcs, dens, lses = [], [], [], []
            for half in range(2):
                hs = slice(half * 128, (half + 1) * 128)
                sp_h, sc_h = jnp.where(m_prev, sp[:, hs], NEG_INF), jnp.where(m_cur, sc[:, hs], NEG_INF)
                m = jnp.maximum(jnp.max(sp_h, axis=1, keepdims=True), jnp.max(sc_h, axis=1, keepdims=True))
                pp, pc = jnp.exp(sp_h - m), jnp.exp(sc_h - m)
                den = jnp.sum(pp, axis=1, keepdims=True) + jnp.sum(pc, axis=1, keepdims=True)
                pps.append(pp)
                pcs.append(pc)
                dens.append(den)
                lses.append(m + jnp.log(den))
            o = (_nn_raw(jnp.concatenate(pps, axis=1), _pair_diag(vp_ref[:, sl]), "bf16")
                 + _nn_raw(jnp.concatenate(pcs, axis=1), _pair_diag(vc_ref[:, sl]), "bf16"))
            o_ref[:, sl] = o / jnp.where(first, dens[0], dens[1])
            l_ref[:, sl] = jnp.where(first, lses[0], lses[1])

    out = jax.ShapeDtypeStruct((L, dil * ATT_W), F32)
    o, l = pl.pallas_call(
        body, name=name, grid=(dil, nb), in_specs=[cur, cur, prev, cur, prev], out_specs=[cur, cur],
        out_shape=[out, out], compiler_params=_cparams(("parallel", "arbitrary")),
    )(view(qr), view(kr), view(kr), view(vb), view(vb))
    return o.reshape(S, ATT_W), l.reshape(S, ATT_W)


def att_combine_fwd(os_, ls, u, *, tm, name):
    S = u.shape[0]
    blk = lambda col: pl.BlockSpec((tm, ATT_W), lambda i, _c=col: (i, _c))

    def body(o1, o2, o3, l1, l2, l3, g_ref, y_ref, o_ref, lse_ref):
        a, b, c = l1[...], l2[...], l3[...]
        m = jnp.maximum(jnp.maximum(a, b), c)
        ea, eb, ec = jnp.exp(a - m), jnp.exp(b - m), jnp.exp(c - m)
        den = ea + eb + ec
        o = (ea * o1[...] + eb * o2[...] + ec * o3[...]) / den
        o_ref[...] = o
        lse_ref[...] = m + jnp.log(den)
        y_ref[...] = (o * _silu(g_ref[...])).astype(BF16)

    wide = jax.ShapeDtypeStruct((S, ATT_W), F32)
    return pl.pallas_call(
        body, name=name, grid=(S // tm,), in_specs=[blk(0)] * 6 + [blk(C_AG // ATT_W)],
        out_specs=[blk(0)] * 3, out_shape=[jax.ShapeDtypeStruct((S, ATT_W), BF16), wide, wide],
        compiler_params=_cparams(("parallel",)),
    )(*os_, *ls, u)


def att_combine_bwd(dy, o, u, *, tm, name):
    S = u.shape[0]
    cw = 256
    blk = lambda base: pl.BlockSpec((tm, cw), lambda i, j, _b=base: (i, _b + j))

    def body(dy_ref, o_ref, g_ref, do_ref, dg_ref):
        g, d = g_ref[...], dy_ref[...]
        sg = _sigmoid(g)
        do_ref[...] = d * (g * sg)
        dg_ref[...] = (d * o_ref[...] * (sg * (1.0 + g * (1.0 - sg)))).astype(BF16)

    return pl.pallas_call(
        body, name=name, grid=(S // tm, ATT_W // cw),
        in_specs=[blk((CONV_CH + GDN_W) // cw), blk(0), blk(C_AG // cw)], out_specs=[blk(0), blk(0)],
        out_shape=[jax.ShapeDtypeStruct((S, ATT_W), F32), jax.ShapeDtypeStruct((S, ATT_W), BF16)],
        compiler_params=_cparams(("parallel", "parallel")),
    )(dy, o, u)


def att_pattern_bwd(qr, kr, vb, do, o, lse, dil, *, name):
    S = qr.shape[0]
    L = S // dil
    nb = L // ATT_BLOCK
    view = lambda t: t.reshape(L, dil * ATT_W)
    cur = pl.BlockSpec((ATT_BLOCK, ATT_W), lambda r, n: (jnp.minimum(n, nb - 1), r))
    prev = pl.BlockSpec((ATT_BLOCK, ATT_W), lambda r, n: (jnp.maximum(n - 1, 0), r))

    def body(q_ref, kc_ref, kp_ref, vc_ref, vp_ref, do_ref, o_ref, l_ref, dq_ref, dk_ref, dv_ref, ck_ref, cv_ref):
        n = pl.program_id(1)

        @pl.when(n < nb)
        def _():
            m_prev, m_cur = _band_masks()
            m_prev = m_prev & (n > 0)
            m_prev2, m_cur2 = jnp.concatenate([m_prev, m_prev], axis=1), jnp.concatenate([m_cur, m_cur], axis=1)
            first = _iota2((ATT_BLOCK, 128), 1) < ATT_HD
            wide = (ATT_BLOCK, 128)
            for p in range(ATT_HEADS // 2):
                sl = slice(p * 128, (p + 1) * 128)
                q, d_o = q_ref[:, sl], do_ref[:, sl]
                kp, kc, vp, vc = (_pair_diag(r[:, sl]) for r in (kp_ref, kc_ref, vp_ref, vc_ref))
                prod = d_o * o_ref[:, sl]
                halves = lambda a, b: jnp.concatenate([jnp.broadcast_to(a, wide), jnp.broadcast_to(b, wide)], axis=1)
                delta = halves(jnp.sum(jnp.where(first, prod, 0.0), axis=1, keepdims=True),
                               jnp.sum(jnp.where(first, 0.0, prod), axis=1, keepdims=True))
                lse2 = halves(l_ref[:, p * 128:p * 128 + 1], l_ref[:, p * 128 + ATT_HD:p * 128 + ATT_HD + 1])
                pp = jnp.where(m_prev2, jnp.exp(_nt_raw(q, kp, "bf16") - lse2), 0.0)
                pc = jnp.where(m_cur2, jnp.exp(_nt_raw(q, kc, "bf16") - lse2), 0.0)
                dsp = pp * (_nt_raw(d_o, vp, "bf16") - delta)
                dsc = pc * (_nt_raw(d_o, vc, "bf16") - delta)
                dq_ref[:, sl] = _nn_raw(dsp, kp, "bf16") + _nn_raw(dsc, kc, "bf16")
                fold = lambda t: jnp.where(first, t[:ATT_BLOCK], t[ATT_BLOCK:])
                dk_prev, dv_prev = fold(_tn_raw(dsp, q, "bf16")), fold(_tn_raw(pp, d_o, "bf16"))

                @pl.when(n > 0)
                def _():
                    dk_ref[:, sl] = ck_ref[:, sl] + dk_prev
                    dv_ref[:, sl] = cv_ref[:, sl] + dv_prev

                ck_ref[:, sl] = fold(_tn_raw(dsc, q, "bf16"))
                cv_ref[:, sl] = fold(_tn_raw(pc, d_o, "bf16"))

        @pl.when(n == nb)
        def _():
            dk_ref[...] = ck_ref[...]
            dv_ref[...] = cv_ref[...]

    out = jax.ShapeDtypeStruct((L, dil * ATT_W), F32)
    dq, dk, dv = pl.pallas_call(
        body, name=name, grid=(dil, nb + 1), in_specs=[cur, cur, prev, cur, prev, cur, cur, cur],
        out_specs=[cur, prev, prev], out_shape=[out, out, out],
        scratch_shapes=[pltpu.VMEM((ATT_BLOCK, ATT_W), F32), pltpu.VMEM((ATT_BLOCK, ATT_W), F32)],
        compiler_params=_cparams(("arbitrary", "arbitrary")),
    )(view(qr), view(kr), view(kr), view(vb), view(vb), view(do), view(o), view(lse))
    return dq.reshape(S, ATT_W), dk.reshape(S, ATT_W), dv.reshape(S, ATT_W)


def _loss_rows(x, w, tgt):
    err = _rms_fn(x, w) - tgt
    return jnp.sum(0.5 * jnp.mean(err * err, axis=-1, keepdims=True), axis=0, keepdims=True)


def loss_head(x, w, tgt, *, tm, name):
    S, D = x.shape

    def body(x_ref, w_ref, t_ref, l_ref, dx_ref, dw_ref):
        val, vjp = jax.vjp(_loss_rows, x_ref[...], w_ref[...], t_ref[...])
        dx, dw, _ = vjp(jnp.ones((1, 1), F32))
        dx_ref[...] = dx

        @pl.when(pl.program_id(0) == 0)
        def _():
            l_ref[...] = jnp.zeros_like(l_ref)
            dw_ref[...] = jnp.zeros_like(dw_ref)

        l_ref[...] += val
        dw_ref[...] += dw

    row = pl.BlockSpec((tm, D), lambda i: (i, 0))
    vec = pl.BlockSpec((1, D), lambda i: (0, 0))
    one = pl.BlockSpec((1, 1), lambda i: (0, 0))
    return pl.pallas_call(
        body, name=name, grid=(S // tm,), in_specs=[row, vec, row], out_specs=[one, row, vec],
        out_shape=[jax.ShapeDtypeStruct((1, 1), F32), jax.ShapeDtypeStruct((S, D), F32), jax.ShapeDtypeStruct((1, D), F32)],
        compiler_params=_cparams(("arbitrary",)),
    )(x, w, tgt)


def adam(w, g, m, v, *, name):
    shape = w.shape
    C = shape[-1]
    R = w.size // C
    br = R
    while br * C * 4 > (1 << 21) and br % 16 == 0:
        br //= 2
    two = lambda t: t.reshape(R, C)

    def body(w_ref, g_ref, m_ref, v_ref, d_ref, mo_ref, vo_ref):
        gg = g_ref[...]
        m_new = ADAM_B1 * m_ref[...] + (1.0 - ADAM_B1) * gg
        v_new = ADAM_B2 * v_ref[...] + (1.0 - ADAM_B2) * jnp.square(gg)
        m_hat = m_new / (1.0 - ADAM_B1 ** ADAM_STEP)
        v_hat = v_new / (1.0 - ADAM_B2 ** ADAM_STEP)
        d_ref[...] = -ADAM_LR * (m_hat / (jnp.sqrt(v_hat) + ADAM_EPS) + ADAM_WD * w_ref[...])
        mo_ref[...] = m_new
        vo_ref[...] = v_new

    blk = pl.BlockSpec((br, C), lambda i: (i, 0))
    out = jax.ShapeDtypeStruct((R, C), F32)
    d, mo, vo = pl.pallas_call(
        body, name=name, grid=(R // br,), in_specs=[blk] * 4, out_specs=[blk] * 3, out_shape=[out] * 3,
        compiler_params=_cparams(("parallel",)),
    )(two(w), two(g), two(m), two(v))
    return d.reshape(shape), mo.reshape(shape), vo.reshape(shape)


MESH_IDS = pl.DeviceIdType.MESH
ANY = pl.BlockSpec(memory_space=pl.ANY)


def _my_id():
    return 4 * lax.axis_index("x") + 2 * lax.axis_index("y") + lax.axis_index("c")


def _peer(k):
    x, y, c = lax.axis_index("x"), lax.axis_index("y"), lax.axis_index("c")
    flip = lambda v, bit: 1 - v if bit else v
    return (flip(x, k & 4), flip(y, k & 2), flip(c, k & 1))


def all_gather(arrs, *, name):
    n = len(arrs)

    def body(*refs):
        ins, outs = refs[:n], refs[n:2 * n]
        send, recv, local = refs[2 * n:]
        me = _my_id()
        started = []
        for a in range(n):
            lc = pltpu.make_async_copy(ins[a], outs[a].at[me], local.at[a])
            lc.start()
            started.append(lc)
            for k in range(1, N_DEV):
                cp = pltpu.make_async_remote_copy(src_ref=ins[a], dst_ref=outs[a].at[me], send_sem=send.at[a, k - 1],
                                                  recv_sem=recv.at[a, k - 1], device_id=_peer(k), device_id_type=MESH_IDS)
                cp.start()
                started.append(cp)
        for cp in started:
            cp.wait()

    return pl.pallas_call(
        body, name=name, in_specs=[ANY] * n, out_specs=[ANY] * n,
        out_shape=[jax.ShapeDtypeStruct((N_DEV,) + a.shape, a.dtype) for a in arrs],
        scratch_shapes=[pltpu.SemaphoreType.DMA((n, N_DEV - 1)), pltpu.SemaphoreType.DMA((n, N_DEV - 1)),
                        pltpu.SemaphoreType.DMA((n,))],
        compiler_params=pltpu.CompilerParams(has_side_effects=True),
    )(*arrs)


def scatter_exchange(groups, pack, *, name):
    flat = [a for grp in groups for a in grp]
    n = len(flat) + 1
    shapes = [jax.ShapeDtypeStruct((N_DEV, len(grp), grp[0].shape[0] // N_DEV, grp[0].shape[1]), grp[0].dtype) for grp in groups]
    shapes.append(jax.ShapeDtypeStruct((N_DEV,) + pack.shape, pack.dtype))
    index = [(gi, li) for gi, grp in enumerate(groups) for li in range(len(grp))]

    def body(*refs):
        ins, outs = refs[:n], refs[n:n + len(shapes)]
        send, recv, local = refs[n + len(shapes):]
        me = _my_id()
        started = []
        for a in range(n):
            if a < n - 1:
                gi, li = index[a]
                r = ins[a].shape[0] // N_DEV
                src = lambda j, _a=a, _r=r: ins[_a].at[pl.ds(pl.multiple_of(j * _r, 8), _r), :]
                dst = outs[gi].at[me, li]
            else:
                src = lambda j, _a=a: ins[_a]
                dst = outs[-1].at[me]
            lc = pltpu.make_async_copy(src(me), dst, local.at[a])
            lc.start()
            started.append(lc)
            for k in range(1, N_DEV):
                cp = pltpu.make_async_remote_copy(src_ref=src(me ^ k), dst_ref=dst, send_sem=send.at[a, k - 1],
                                                  recv_sem=recv.at[a, k - 1], device_id=_peer(k), device_id_type=MESH_IDS)
                cp.start()
                started.append(cp)
        for cp in started:
            cp.wait()

    return pl.pallas_call(
        body, name=name, in_specs=[ANY] * n, out_specs=[ANY] * len(shapes), out_shape=shapes,
        scratch_shapes=[pltpu.SemaphoreType.DMA((n, N_DEV - 1)), pltpu.SemaphoreType.DMA((n, N_DEV - 1)),
                        pltpu.SemaphoreType.DMA((n,))],
        compiler_params=pltpu.CompilerParams(has_side_effects=True),
    )(*flat, pack)


def slot_sum(x, *, name):
    _, A, R, C = x.shape
    br = R
    while br * C * 4 * N_DEV > (1 << 23) and br % 16 == 0:
        br //= 2

    def body(x_ref, o_ref):
        acc = x_ref[0, 0].astype(F32)
        for s in range(1, N_DEV):
            acc = acc + x_ref[s, 0].astype(F32)
        o_ref[0] = acc

    return pl.pallas_call(
        body, name=name, grid=(A, R // br),
        in_specs=[pl.BlockSpec((N_DEV, 1, br, C), lambda a, i: (0, a, i, 0))],
        out_specs=pl.BlockSpec((1, br, C), lambda a, i: (a, i, 0)),
        out_shape=jax.ShapeDtypeStruct((A, R, C), F32),
        compiler_params=_cparams(("parallel", "parallel")),
    )(x)


HBM_SPEC = pl.BlockSpec(memory_space=pltpu.HBM)
SEM_SPEC = pl.BlockSpec(memory_space=pltpu.SEMAPHORE)
DATAFLOW = pltpu.SideEffectType.DATAFLOW_SIDE_EFFECTING


def _push_copies(src_refs, land_refs, send_sems, recv_sems, by_rows):
    me = _my_id()
    copies = []
    for a, (src, land) in enumerate(zip(src_refs, land_refs)):
        rows = land.shape[1]
        for k in range(1, N_DEV):
            piece = src.at[pl.ds(pl.multiple_of((me ^ k) * rows, 8), rows), :] if by_rows else src
            copies.append(pltpu.make_async_remote_copy(
                src_ref=piece, dst_ref=land.at[me], send_sem=send_sems[a].at[k - 1], recv_sem=recv_sems[a].at[k - 1],
                device_id=_peer(k), device_id_type=MESH_IDS))
    return copies


def push_start(srcs, lands, *, by_rows, name):
    n = len(srcs)

    def body(*refs):
        src_refs, land_refs = refs[:n], refs[n:2 * n]
        send_sems, recv_sems = refs[2 * n:3 * n], refs[3 * n:4 * n]
        token = refs[6 * n]
        for cp in _push_copies(src_refs, land_refs, send_sems, recv_sems, by_rows):
            cp.start()
        token[...] = jnp.zeros_like(token)

    sems = [pltpu.SemaphoreType.DMA((N_DEV - 1,))] * (2 * n)
    bufs = [pltpu.HBM(a.shape, a.dtype) for a in list(srcs) + list(lands)]
    outs = pl.pallas_call(
        body, name=name, out_shape=tuple(sems + bufs + [jax.ShapeDtypeStruct((8, 128), F32)]),
        in_specs=[HBM_SPEC] * (2 * n), out_specs=tuple([SEM_SPEC] * (2 * n) + [HBM_SPEC] * (2 * n) + [pl.BlockSpec(memory_space=pltpu.VMEM)]),
        input_output_aliases={i: 2 * n + i for i in range(2 * n)},
        compiler_params=pltpu.CompilerParams(has_side_effects=DATAFLOW),
    )(*[pltpu.with_memory_space_constraint(a, pltpu.HBM) for a in list(srcs) + list(lands)])
    return outs[:n], outs[n:2 * n], outs[2 * n:3 * n], outs[3 * n:4 * n], outs[4 * n]


def push_wait(send_sems, recv_sems, srcs, lands, after, *, by_rows, name):
    n = len(srcs)

    def body(*refs):
        src_refs, land_refs = refs[:n], refs[n:2 * n]
        send, recv = refs[2 * n:3 * n], refs[3 * n:4 * n]
        for cp in _push_copies(src_refs, land_refs, send, recv, by_rows):
            cp.wait_send()
            cp.wait_recv()

    outs = pl.pallas_call(
        body, name=name, out_shape=tuple(pltpu.HBM(a.shape, a.dtype) for a in list(srcs) + list(lands)),
        in_specs=[HBM_SPEC] * (2 * n) + [SEM_SPEC] * (2 * n) + [ANY], out_specs=tuple([HBM_SPEC] * (2 * n)),
        input_output_aliases={i: i for i in range(2 * n)},
        compiler_params=pltpu.CompilerParams(has_side_effects=DATAFLOW),
    )(*srcs, *lands, *send_sems, *recv_sems, after)
    return outs[n:]


def _landing(own, slots_shape):
    return lax.dynamic_update_index_in_dim(lax.empty(slots_shape, own.dtype), own, _my_id(), 0)


def _pack(arrs):
    flat = []
    for a in arrs:
        f = a.reshape(-1).astype(F32)
        flat.append(jnp.pad(f, (0, (-f.size) % 128)))
    f = jnp.concatenate(flat)
    return jnp.pad(f, (0, (-f.size) % 1024)).reshape(-1, 128)


def _unpack(p, shapes):
    f = p.reshape(-1)
    out, off = [], 0
    for s in shapes:
        n = math.prod(s)
        out.append(f[off:off + n].reshape(s))
        off += n + (-n) % 128
    return out


def _to_padded_cols(w):
    z = lambda n: jnp.zeros(w.shape[:-1] + (n,), w.dtype)
    return jnp.concatenate([w[..., 0:4608], w[..., 4620:7692], w[..., 4608:4614], z(122), w[..., 4614:4620], z(378)], axis=-1)


def _from_padded_cols(w):
    return jnp.concatenate([w[..., 0:4608], w[..., C_BETA:C_BETA + 6], w[..., C_ALPHA:C_ALPHA + 6], w[..., 4608:7680]], axis=-1)


def _lane_pad(v):
    return jnp.pad(v, (0, 128 - v.shape[0]))[None, :]


TM_MM, TN_MM, TK_MM = 1024, 1024, 2048
TM_ROW = 512


def layer_fwd(x, p, tabs, l):
    h = rms_fwd(x, p["norm_w"], tm=TM_ROW, name=f"rms_fwd_{l}")
    u = matmul(h, p["w_in"], mode="nn", tm=TM_MM, tn=TN_MM, tk=TK_MM, name=f"in_proj_{l}")
    y_conv = conf_fwd(u, p["dw_w"], p["dw_b"], p["ln_w"], p["ln_b"], p["pw"], tm=TM_ROW, name=f"conf_fwd_{l}")
    q, k, v, beta, gc = gdn_prep_fwd(u, p["conv_w"], p["a_log"], p["dt_bias"], tm=TM_ROW, name=f"gdn_prep_fwd_{l}")
    y_gdn, states = gdn_core_fwd(q, k, v, u, beta, gc, p["gdn_nw"], name=f"gdn_core_fwd_{l}")
    qr, kr, vb = att_prep_fwd(u, tabs, tm=TM_ROW, name=f"att_prep_fwd_{l}")
    os_, ls = [], []
    for _, dil in DIL_PATTERNS:
        o_p, l_p = att_pattern_fwd(qr, kr, vb, dil, name=f"att_fwd_d{dil}_{l}")
        os_.append(o_p)
        ls.append(l_p)
    y_att, o, lse = att_combine_fwd(os_, ls, u, tm=TM_ROW, name=f"att_combine_fwd_{l}")
    y = jnp.concatenate([y_conv, y_gdn, y_att], axis=1)
    x_new = matmul(y, p["w_out"], mode="nn", tm=TM_MM, tn=TN_MM, tk=TK_MM, residual=x, name=f"out_proj_{l}")
    saved = dict(x=x, h=h, u=u, y=y, q=q, k=k, v=v, beta=beta, gc=gc, states=states, qr=qr, kr=kr, vb=vb, o=o, lse=lse)
    return x_new, saved


def layer_bwd(dx_out, s, p, tabs, l):
    S = dx_out.shape[0]
    u = s["u"]
    dy = matmul(dx_out, p["w_out"], mode="nt", tm=TM_MM, tn=TN_MM, tk=TK_MM, name=f"out_proj_dy_{l}")
    g_w_out = matmul(s["y"], dx_out, mode="tn", tm=TM_MM, tn=TN_MM, tk=TK_MM, out_dtype=BF16, name=f"out_proj_dw_{l}")
    dc, d_cg, g_ln_w, g_ln_b, g_pw, g_dw_b = conf_bwd_post(u, dy, p["dw_w"], p["dw_b"], p["ln_w"], p["ln_b"], p["pw"],
                                                         tm=TM_ROW, name=f"conf_bwd_post_{l}")
    d_ca, d_cb, g_dw_w = conv_bwd(dc, [(u, C_CA), (u, C_CB)], p["dw_w"], K=CONV_WIDTH, H=CONF_HALO, tm=TM_ROW, cw=CONV_CH,
                                  glu=True, name=f"conf_bwd_conv_{l}")
    dq, dk, dv, d_gz, dbeta, dgc, g_gdn_nw = gdn_core_bwd(s["q"], s["k"], s["v"], u, s["beta"], s["gc"], p["gdn_nw"],
                                                          s["states"], dy, name=f"gdn_core_bwd_{l}")
    dpq, dpk, dpv, d_bi, d_ai, g_a_log, g_dt_bias = gdn_prep_bwd(u, p["conv_w"], p["a_log"], p["dt_bias"], dq, dk, dv, dbeta, dgc,
                                                                 tm=TM_ROW, name=f"gdn_prep_bwd_{l}")
    d_gqkv, g_conv_w = conv_bwd(jnp.concatenate([dpq, dpk, dpv], axis=1), [(u, C_GQ)], p["conv_w"], K=SHORT_CONV, H=GDN_HALO,
                                tm=TM_ROW, cw=GDN_W, glu=False, name=f"gdn_bwd_conv_{l}")
    do, d_ag = att_combine_bwd(dy, s["o"], u, tm=TM_ROW, name=f"att_combine_bwd_{l}")
    dqs, dks, dvs = [], [], []
    for _, dil in DIL_PATTERNS:
        a, b, c = att_pattern_bwd(s["qr"], s["kr"], s["vb"], do, s["o"], s["lse"], dil, name=f"att_bwd_d{dil}_{l}")
        dqs.append(a)
        dks.append(b)
        dvs.append(c)
    d_aq, d_ak, d_av = att_prep_bwd(dqs, dks, dvs, tabs, tm=TM_ROW, name=f"att_prep_bwd_{l}")
    zeros = jnp.zeros((S, IN_PAD - C_ALPHA - 128), BF16)
    du = jnp.concatenate([d_ca, d_cb, d_cg, d_gqkv, d_gz, d_aq, d_ak, d_av, d_ag, d_bi, d_ai, zeros], axis=1)
    dh = matmul(du, p["w_in"], mode="nt", tm=TM_MM, tn=TN_MM, tk=TK_MM, name=f"in_proj_dh_{l}")
    g_w_in = matmul(s["h"], du, mode="tn", tm=TM_MM, tn=TN_MM, tk=TK_MM, out_dtype=BF16, name=f"in_proj_dw_{l}")
    dx, g_norm_w = rms_bwd(s["x"], p["norm_w"], dh, dx_out, tm=TM_ROW // 2, name=f"rms_bwd_{l}")
    grads = dict(norm_w=g_norm_w[0], w_in=g_w_in, conv_qkv_w=g_conv_w, a_log=g_a_log[0, :GDN_HEADS], dt_bias=g_dt_bias[0, :GDN_HEADS],
                 gdn_norm_w=g_gdn_nw[0], conf_dw_w=g_dw_w, conf_dw_b=g_dw_b[0], conf_ln_w=g_ln_w[0], conf_ln_b=g_ln_b[0],
                 conf_pw_w=g_pw, w_out=g_w_out)
    return dx, grads


WEIGHTS = ("norm_w", "w_in", "conv_qkv_w", "a_log", "dt_bias", "gdn_norm_w", "conf_dw_w", "conf_dw_b", "conf_ln_w",
           "conf_ln_b", "conf_pw_w", "w_out", "final_norm_w")
SMALL_REPLICATED = ("norm_w", "a_log", "dt_bias", "gdn_norm_w", "conf_dw_b", "conf_ln_w", "conf_ln_b")


def kernel(x, norm_w, w_in, conv_qkv_w, a_log, dt_bias, gdn_norm_w, conf_dw_w, conf_dw_b, conf_ln_w, conf_ln_b, conf_pw_w, w_out, final_norm_w, loss_target, m_norm_w, m_w_in, m_conv_qkv_w, m_a_log, m_dt_bias, m_gdn_norm_w, m_conf_dw_w, m_conf_dw_b, m_conf_ln_w, m_conf_ln_b, m_conf_pw_w, m_w_out, m_final_norm_w, v_norm_w, v_w_in, v_conv_qkv_w, v_a_log, v_dt_bias, v_gdn_norm_w, v_conf_dw_w, v_conf_dw_b, v_conf_ln_w, v_conf_ln_b, v_conf_pw_w, v_w_out, v_final_norm_w):
    w = dict(norm_w=norm_w, w_in=w_in, conv_qkv_w=conv_qkv_w, a_log=a_log, dt_bias=dt_bias, gdn_norm_w=gdn_norm_w,
             conf_dw_w=conf_dw_w, conf_dw_b=conf_dw_b, conf_ln_w=conf_ln_w, conf_ln_b=conf_ln_b, conf_pw_w=conf_pw_w,
             w_out=w_out, final_norm_w=final_norm_w)
    m = dict(zip(WEIGHTS, (m_norm_w, m_w_in, m_conv_qkv_w, m_a_log, m_dt_bias, m_gdn_norm_w, m_conf_dw_w, m_conf_dw_b,
                           m_conf_ln_w, m_conf_ln_b, m_conf_pw_w, m_w_out, m_final_norm_w)))
    v = dict(zip(WEIGHTS, (v_norm_w, v_w_in, v_conv_qkv_w, v_a_log, v_dt_bias, v_gdn_norm_w, v_conf_dw_w, v_conf_dw_b,
                           v_conf_ln_w, v_conf_ln_b, v_conf_pw_w, v_w_out, v_final_norm_w)))
    S = x.shape[1]
    L = norm_w.shape[0]
    me = _my_id()

    small_shapes = [conv_qkv_w.shape, conf_dw_w.shape, conf_pw_w.shape]
    w_in_b, w_out_b = _to_padded_cols(w_in).astype(BF16), w_out.astype(BF16)
    in_slots, out_slots = (N_DEV,) + w_in_b.shape[1:], (N_DEV,) + w_out_b.shape[1:]
    g_in0, g_out0, g_small = all_gather([w_in_b[0], w_out_b[0], _pack([conv_qkv_w, conf_dw_w, conf_pw_w])], name="gather_first")
    gathers, tie = {}, jnp.zeros((1, 1), F32)
    for l in range(1, L):
        *flight, token = push_start([w_in_b[l], w_out_b[l]], [_landing(w_in_b[l], in_slots), _landing(w_out_b[l], out_slots)],
                                    by_rows=False, name=f"gather_start_{l}")
        gathers[l] = flight
        tie = tie + token[0:1, 0:1]
    parts = [_unpack(g_small[s], small_shapes) for s in range(N_DEV)]
    conv_full = jnp.concatenate([pt[0] for pt in parts], axis=2)
    dw_full = jnp.concatenate([pt[1] for pt in parts], axis=2)
    pw_full = jnp.concatenate([pt[2] for pt in parts], axis=1)
    tabs = rope_tables(S)

    def layer_params(l, full_in, full_out):
        return dict(
            norm_w=norm_w[l][None], w_in=full_in.reshape(D_MODEL, IN_PAD), w_out=full_out.reshape(D_MODEL, D_MODEL),
            conv_w=conv_full[l], a_log=_lane_pad(a_log[l]), dt_bias=_lane_pad(dt_bias[l]), gdn_nw=gdn_norm_w[l][None],
            dw_w=dw_full[l], dw_b=conf_dw_b[l][None], ln_w=conf_ln_w[l][None], ln_b=conf_ln_b[l][None], pw=pw_full[l])

    xs = x[0]
    params, saved = [], []
    for l in range(L):
        if l == 0:
            p = layer_params(0, g_in0, g_out0)
            p["norm_w"] = p["norm_w"] + tie
        else:
            p = layer_params(l, *push_wait(*gathers[l], xs, by_rows=False, name=f"gather_wait_{l}"))
        params.append(p)
        xs, sv = layer_fwd(xs, p, tabs, l)
        saved.append(sv)
    loss_part, dx, g_final = loss_head(xs, final_norm_w[None], loss_target[0], tm=TM_ROW // 2, name="loss_head")

    layer_grads, scatters = [None] * L, {}
    for l in reversed(range(L)):
        dx, layer_grads[l] = layer_bwd(dx, saved[l], params[l], tabs, l)
        big = [layer_grads[l]["w_in"], layer_grads[l]["w_out"]]
        lands = [_landing(lax.dynamic_slice_in_dim(a, me * sl[1], sl[1], axis=0), sl) for a, sl in zip(big, (in_slots, out_slots))]
        *flight, token = push_start(big, lands, by_rows=True, name=f"scatter_start_{l}")
        scatters[l] = flight
        if l > 0:
            params[l - 1]["dw_b"] = params[l - 1]["dw_b"] + token[0:1, 0:1]
    stack = lambda name: jnp.stack([layer_grads[l][name] for l in range(L)])
    small = [loss_part] + [stack(n) for n in SMALL_REPLICATED] + [g_final[0], stack("conv_qkv_w"), stack("conf_dw_w")]
    small_shapes = [a.shape for a in small]
    r_pw, r_small = scatter_exchange([[layer_grads[l]["conf_pw_w"] for l in range(L)]], _pack(small), name="scatter_small")
    sums_in, sums_out = [], []
    for l in range(L):
        land_in, land_out = push_wait(*scatters[l], dx, by_rows=True, name=f"scatter_wait_{l}")
        sums_in.append(slot_sum(land_in[:, None], name=f"sum_w_in_{l}"))
        sums_out.append(slot_sum(land_out[:, None], name=f"sum_w_out_{l}"))
    g = {}
    g["w_in"] = _from_padded_cols(jnp.concatenate(sums_in, axis=0))
    g["w_out"] = jnp.concatenate(sums_out, axis=0)
    g["conf_pw_w"] = slot_sum(r_pw, name="sum_pw")
    summed = _unpack(slot_sum(r_small[:, None], name="sum_small")[0], small_shapes)
    loss = summed[0].reshape(())
    for n, a in zip(SMALL_REPLICATED, summed[1:1 + len(SMALL_REPLICATED)]):
        g[n] = a
    g["final_norm_w"] = summed[-3]
    g["conv_qkv_w"] = lax.dynamic_slice_in_dim(summed[-2], me * conv_qkv_w.shape[2], conv_qkv_w.shape[2], axis=2)
    g["conf_dw_w"] = lax.dynamic_slice_in_dim(summed[-1], me * conf_dw_w.shape[2], conf_dw_w.shape[2], axis=2)

    deltas, new_m, new_v = {}, {}, {}
    for n in WEIGHTS:
        deltas[n], new_m[n], new_v[n] = adam(w[n], g[n], m[n], v[n], name=f"adam_{n}")
    return (loss, dx[None], *[g[n] for n in WEIGHTS], *[deltas[n] for n in WEIGHTS],
            *[new_m[n] for n in WEIGHTS], *[new_v[n] for n in WEIGHTS])
```

```python
import functools
import math

import jax
import jax.numpy as jnp
from jax import lax
from jax.experimental import pallas as pl
from jax.experimental.pallas import tpu as pltpu

D_MODEL = 2048
DEPTH = 4
N_DEV = 8
GDN_DK = 128
GDN_HEADS = 6
GDN_W = 768
ATT_HD = 64
ATT_HEADS = 12
ATT_W = 768
CONV_CH = 512
CONV_WIDTH = 31
SHORT_CONV = 4
GDN_CHUNK = 64
ROPE_THETA = 500000.0
ROPE_DIM = 16
DIL_PATTERNS = ((128, 1), (512, 4), (2048, 16))
ATT_BLOCK = 128
NEG_INF = -1e30
IN_W = 7692

ADAM_LR = 0.001
ADAM_B1 = 0.9
ADAM_B2 = 0.999
ADAM_EPS = 1e-08
ADAM_WD = 0.01
ADAM_STEP = 10

C_CA, C_CB, C_CG = 0, 512, 1024
C_GQ, C_GK, C_GV, C_GZ = 1536, 2304, 3072, 3840
C_AQ, C_AK, C_AV, C_AG = 4608, 5376, 6144, 6912
C_BETA, C_ALPHA = 7680, 7808
IN_PAD = 8192

VMEM_LIMIT = 56 * 1024 * 1024
CONF_HALO = 32
GDN_HALO = 8
GDN_GROUP = 4
GDN_HEADS_PER_STEP = 6

F32 = jnp.float32
BF16 = jnp.bfloat16
HI = lax.Precision.HIGHEST


def _cparams(sem, vmem=VMEM_LIMIT):
    return pltpu.CompilerParams(dimension_semantics=sem, vmem_limit_bytes=vmem)


def _dg(a, b, ca, cb, prec):
    nb = a.ndim - 2
    batch = tuple(range(nb))
    dn = (((ca + nb,), (cb + nb,)), (batch, batch))
    if prec == "bf16":
        return lax.dot_general(a.astype(BF16), b.astype(BF16), dn, preferred_element_type=F32)
    if prec == "bf16x3":
        ah, bh = a.astype(BF16), b.astype(BF16)
        al, bl = (a - ah.astype(F32)).astype(BF16), (b - bh.astype(F32)).astype(BF16)
        dot = lambda x, y: lax.dot_general(x, y, dn, preferred_element_type=F32)
        return dot(ah, bh) + (dot(ah, bl) + dot(al, bh))
    return lax.dot_general(a.astype(F32), b.astype(F32), dn, precision=HI, preferred_element_type=F32)


def _nn_raw(a, b, prec):
    return _dg(a, b, 1, 0, prec)


def _nt_raw(a, b, prec):
    return _dg(a, b, 1, 1, prec)


def _tn_raw(a, b, prec):
    return _dg(a, b, 0, 0, prec)


@functools.partial(jax.custom_vjp, nondiff_argnums=(2,))
def mm_nn(a, b, prec="bf16"):
    return _nn_raw(a, b, prec)


def _mm_nn_f(a, b, prec):
    return _nn_raw(a, b, prec), (a, b)


def _mm_nn_b(prec, res, g):
    a, b = res
    return _nt_raw(g, b, prec).astype(a.dtype), _tn_raw(a, g, prec).astype(b.dtype)


mm_nn.defvjp(_mm_nn_f, _mm_nn_b)


@functools.partial(jax.custom_vjp, nondiff_argnums=(2,))
def mm_nt(a, b, prec="bf16"):
    return _nt_raw(a, b, prec)


def _mm_nt_f(a, b, prec):
    return _nt_raw(a, b, prec), (a, b)


def _mm_nt_b(prec, res, g):
    a, b = res
    return _nn_raw(g, b, prec).astype(a.dtype), _tn_raw(g, a, prec).astype(b.dtype)


mm_nt.defvjp(_mm_nt_f, _mm_nt_b)


@functools.partial(jax.custom_vjp, nondiff_argnums=(2,))
def mm_tn(a, b, prec="bf16"):
    return _tn_raw(a, b, prec)


def _mm_tn_f(a, b, prec):
    return _tn_raw(a, b, prec), (a, b)


def _mm_tn_b(prec, res, g):
    a, b = res
    return _nt_raw(b, g, prec).astype(a.dtype), _nn_raw(a, g, prec).astype(b.dtype)


mm_tn.defvjp(_mm_tn_f, _mm_tn_b)


def _sigmoid(x):
    return 1.0 / (1.0 + jnp.exp(-x))


def _silu(x):
    return x * _sigmoid(x)


def _softplus(x):
    return jnp.maximum(x, 0.0) + jnp.log(1.0 + jnp.exp(-jnp.abs(x)))


def matmul(a, b, *, mode, tm, tn, tk, out_dtype=F32, residual=None, name):
    if mode == "tn":
        K, M = a.shape
    else:
        M, K = a.shape
    N = b.shape[0] if mode == "nt" else b.shape[1]
    assert M % tm == 0 and N % tn == 0 and K % tk == 0, (a.shape, b.shape, tm, tn, tk)
    nk = K // tk
    a_spec = pl.BlockSpec((tk, tm), lambda i, j, k: (k, i)) if mode == "tn" else pl.BlockSpec((tm, tk), lambda i, j, k: (i, k))
    b_spec = pl.BlockSpec((tn, tk), lambda i, j, k: (j, k)) if mode == "nt" else pl.BlockSpec((tk, tn), lambda i, j, k: (k, j))
    o_spec = pl.BlockSpec((tm, tn), lambda i, j, k: (i, j))
    raw = {"nn": _nn_raw, "nt": _nt_raw, "tn": _tn_raw}[mode]
    has_res = residual is not None

    def body(*refs):
        if has_res:
            a_ref, b_ref, r_ref, o_ref, acc_ref = refs
        else:
            a_ref, b_ref, o_ref, acc_ref = refs
        k = pl.program_id(2)
        part = raw(a_ref[...], b_ref[...], "bf16")

        @pl.when(k == 0)
        def _():
            acc_ref[...] = part

        @pl.when(k > 0)
        def _():
            acc_ref[...] += part

        @pl.when(k == nk - 1)
        def _():
            r = acc_ref[...]
            if has_res:
                r = r + r_ref[...].astype(F32)
            o_ref[...] = r.astype(out_dtype)

    in_specs = [a_spec, b_spec] + ([o_spec] if has_res else [])
    args = (a, b) + ((residual,) if has_res else ())
    return pl.pallas_call(
        body, name=name, grid=(M // tm, N // tn, nk), in_specs=in_specs, out_specs=o_spec,
        out_shape=jax.ShapeDtypeStruct((M, N), out_dtype),
        scratch_shapes=[pltpu.VMEM((tm, tn), F32)],
        compiler_params=_cparams(("parallel", "parallel", "arbitrary")),
    )(*args)


def _rms_fn(x, w, eps=1e-6):
    return x * lax.rsqrt(jnp.mean(x * x, axis=-1, keepdims=True) + eps) * w


def rms_fwd(x, w, *, tm, name):
    S, D = x.shape

    def body(x_ref, w_ref, o_ref):
        o_ref[...] = _rms_fn(x_ref[...], w_ref[...]).astype(BF16)

    return pl.pallas_call(
        body, name=name, grid=(S // tm,),
        in_specs=[pl.BlockSpec((tm, D), lambda i: (i, 0)), pl.BlockSpec((1, D), lambda i: (0, 0))],
        out_specs=pl.BlockSpec((tm, D), lambda i: (i, 0)),
        out_shape=jax.ShapeDtypeStruct((S, D), BF16),
        compiler_params=_cparams(("parallel",)),
    )(x, w)


def rms_bwd(x, w, dh, dres, *, tm, name):
    S, D = x.shape

    def body(x_ref, w_ref, dh_ref, dr_ref, dx_ref, dw_ref):
        _, vjp = jax.vjp(_rms_fn, x_ref[...], w_ref[...])
        dx, dw = vjp(dh_ref[...].astype(F32))
        dx_ref[...] = dx + dr_ref[...]

        @pl.when(pl.program_id(0) == 0)
        def _():
            dw_ref[...] = jnp.zeros_like(dw_ref)

        dw_ref[...] += dw

    row = pl.BlockSpec((tm, D), lambda i: (i, 0))
    vec = pl.BlockSpec((1, D), lambda i: (0, 0))
    return pl.pallas_call(
        body, name=name, grid=(S // tm,), in_specs=[row, vec, row, row], out_specs=[row, vec],
        out_shape=[jax.ShapeDtypeStruct((S, D), F32), jax.ShapeDtypeStruct((1, D), F32)],
        compiler_params=_cparams(("arbitrary",)),
    )(x, w, dh, dres)


def _fill_ext(ext_ref, halo, tile, first, H):
    ext_ref[pl.ds(0, H), :] = jnp.where(first, 0.0, halo)
    ext_ref[pl.ds(H, tile.shape[0]), :] = tile


def _conv_taps(ext_ref, w_ref, K, H, tm):
    assert H >= 8 * ((K - 1) // 8 + 1)
    total = None
    for b in range(min(8, K)):
        y = None
        for a in range((K - 1 - b) // 8 + 1):
            term = ext_ref[pl.ds(H - 8 - 8 * a, tm + 8), :] * w_ref[pl.ds(K - 1 - 8 * a - b, 1), :]
            y = term if y is None else y + term
        y = y if b == 0 else pltpu.roll(y, b, 0)
        total = y if total is None else total + y
    return total[8:, :]


def _halo_spec(H, tm, cw, col):
    return pl.BlockSpec((H, cw), lambda *g, _c=col: (jnp.maximum(g[-1] * (tm // H) - 1, 0), _c))


def conv_bwd(dc, srcs, w, *, K, H, tm, cw, glu, name):
    S, C = dc.shape
    nc, nt = C // cw, S // tm
    last_halo = S // H - 1
    n_src = 2 if glu else 1
    bases = [c0 // cw for _, c0 in srcs]

    def body(*refs):
        dc_ref, dcn_ref = refs[0], refs[1]
        src_refs = refs[2:2 + 2 * n_src]
        w_ref = refs[2 + 2 * n_src]
        outs = refs[3 + 2 * n_src:3 + 3 * n_src]
        dw_ref = refs[3 + 3 * n_src]
        ext_ref, dext_ref = refs[4 + 3 * n_src:]
        i = pl.program_id(1)
        first, last = i == 0, i == nt - 1
        if glu:
            a_ref, ah_ref, b_ref, bh_ref = src_refs
            sg = _sigmoid(b_ref[...])
            _fill_ext(ext_ref, ah_ref[...] * _sigmoid(bh_ref[...]), a_ref[...] * sg, first, H)
        else:
            x_ref, xh_ref = src_refs
            _fill_ext(ext_ref, xh_ref[...], x_ref[...], first, H)
        dc_t = dc_ref[...]
        dext_ref[pl.ds(0, tm), :] = dc_t
        dext_ref[pl.ds(tm, H), :] = jnp.where(last, 0.0, dcn_ref[...])
        N = tm + 8
        dx = None
        for b in range(min(8, K)):
            z = None
            for a_ in range((K - 1 - b) // 8 + 1):
                term = dext_ref[pl.ds(8 * a_, N), :] * w_ref[pl.ds(K - 1 - 8 * a_ - b, 1), :]
                z = term if z is None else z + term
            z = z if b == 0 else pltpu.roll(z, N - b, 0)
            dx = z if dx is None else dx + z
        dx = dx[:tm, :]
        if glu:
            a = a_ref[...]
            outs[0][...] = (dx * sg).astype(BF16)
            outs[1][...] = (dx * a * sg * (1.0 - sg)).astype(BF16)
        else:
            outs[0][...] = dx.astype(BF16)

        @pl.when(first)
        def _():
            dw_ref[...] = jnp.zeros_like(dw_ref)

        dpad = jnp.concatenate([jnp.zeros((8, cw), F32), dc_t], axis=0)
        for b in range(min(8, K)):
            shifted = dpad if b == 0 else pltpu.roll(dpad, N - b, 0)
            for a_ in range((K - 1 - b) // 8 + 1):
                k = K - 1 - 8 * a_ - b
                dw_ref[pl.ds(k, 1), :] += jnp.sum(shifted * ext_ref[pl.ds(H - 8 - 8 * a_, N), :], axis=0, keepdims=True)

    tile = lambda base: pl.BlockSpec((tm, cw), lambda j, i, _b=base: (i, _b + j))
    halo = lambda base: pl.BlockSpec((H, cw), lambda j, i, _b=base: (jnp.maximum(i * (tm // H) - 1, 0), _b + j))
    in_specs = [tile(0), pl.BlockSpec((H, cw), lambda j, i: (jnp.minimum((i + 1) * (tm // H), last_halo), j))]
    args = [dc, dc]
    for (arr, _), base in zip(srcs, bases):
        in_specs += [tile(base), halo(base)]
        args += [arr, arr]
    in_specs.append(pl.BlockSpec((K, cw), lambda j, i: (0, j)))
    args.append(w)
    out_specs = [tile(0)] * n_src + [pl.BlockSpec((K, cw), lambda j, i: (0, j))]
    out_shape = [jax.ShapeDtypeStruct((S, C), BF16)] * n_src + [jax.ShapeDtypeStruct((K, C), F32)]
    return pl.pallas_call(
        body, name=name, grid=(nc, nt), in_specs=in_specs, out_specs=out_specs, out_shape=out_shape,
        scratch_shapes=[pltpu.VMEM((tm + H, cw), F32), pltpu.VMEM((tm + H, cw), F32)],
        compiler_params=_cparams(("parallel", "arbitrary")),
    )(*args)


def _conf_post(c, gate, ln_w, ln_b, pw):
    mu = jnp.mean(c, axis=-1, keepdims=True)
    cc = c - mu
    var = jnp.mean(cc * cc, axis=-1, keepdims=True)
    hn = cc * lax.rsqrt(var + 1e-5) * ln_w + ln_b
    return mm_nn(_silu(hn), pw) * _silu(gate)


def _conf_specs(tm):
    H = CONF_HALO
    blk = lambda col: pl.BlockSpec((tm, CONV_CH), lambda i, _c=col: (i, _c))
    vec = pl.BlockSpec((1, CONV_CH), lambda i: (0, 0))
    specs = [blk(0), blk(1), blk(2), _halo_spec(H, tm, CONV_CH, 0), _halo_spec(H, tm, CONV_CH, 1),
             pl.BlockSpec((CONV_WIDTH, CONV_CH), lambda i: (0, 0)), vec, vec, vec,
             pl.BlockSpec((CONV_CH, CONV_CH), lambda i: (0, 0))]
    return specs, blk, vec


def _conf_conv(a_ref, b_ref, ah_ref, bh_ref, dww_ref, dwb_ref, ext_ref, tm):
    first = pl.program_id(0) == 0
    _fill_ext(ext_ref, ah_ref[...] * _sigmoid(bh_ref[...]), a_ref[...] * _sigmoid(b_ref[...]), first, CONF_HALO)
    return _conv_taps(ext_ref, dww_ref, CONV_WIDTH, CONF_HALO, tm) + dwb_ref[...]


def conf_fwd(u, dw_w, dw_b, ln_w, ln_b, pw, *, tm, name):
    S = u.shape[0]
    specs, blk, vec = _conf_specs(tm)

    def body(a_ref, b_ref, g_ref, ah_ref, bh_ref, dww_ref, dwb_ref, lnw_ref, lnb_ref, pw_ref, y_ref, ext_ref):
        c = _conf_conv(a_ref, b_ref, ah_ref, bh_ref, dww_ref, dwb_ref, ext_ref, tm)
        y_ref[...] = _conf_post(c, g_ref[...], lnw_ref[...], lnb_ref[...], pw_ref[...]).astype(BF16)

    return pl.pallas_call(
        body, name=name, grid=(S // tm,), in_specs=specs, out_specs=blk(0),
        out_shape=jax.ShapeDtypeStruct((S, CONV_CH), BF16),
        scratch_shapes=[pltpu.VMEM((tm + CONF_HALO, CONV_CH), F32)],
        compiler_params=_cparams(("parallel",)),
    )(u, u, u, u, u, dw_w, dw_b, ln_w, ln_b, pw)


def conf_bwd_post(u, dy, dw_w, dw_b, ln_w, ln_b, pw, *, tm, name):
    S = u.shape[0]
    specs, blk, vec = _conf_specs(tm)
    mat = pl.BlockSpec((CONV_CH, CONV_CH), lambda i: (0, 0))

    def body(a_ref, b_ref, g_ref, ah_ref, bh_ref, dww_ref, dwb_ref, lnw_ref, lnb_ref, pw_ref, dy_ref,
             dc_ref, dg_ref, dlnw_ref, dlnb_ref, dpw_ref, ddwb_ref, ext_ref):
        c = _conf_conv(a_ref, b_ref, ah_ref, bh_ref, dww_ref, dwb_ref, ext_ref, tm)
        _, vjp = jax.vjp(_conf_post, c, g_ref[...], lnw_ref[...], lnb_ref[...], pw_ref[...])
        dc, dg, dlnw, dlnb, dpw = vjp(dy_ref[...])
        dc_ref[...] = dc
        dg_ref[...] = dg.astype(BF16)

        @pl.when(pl.program_id(0) == 0)
        def _():
            dlnw_ref[...] = jnp.zeros_like(dlnw_ref)
            dlnb_ref[...] = jnp.zeros_like(dlnb_ref)
            dpw_ref[...] = jnp.zeros_like(dpw_ref)

            ddwb_ref[...] = jnp.zeros_like(ddwb_ref)

        dlnw_ref[...] += dlnw
        dlnb_ref[...] += dlnb
        dpw_ref[...] += dpw
        ddwb_ref[...] += jnp.sum(dc, axis=0, keepdims=True)

    return pl.pallas_call(
        body, name=name, grid=(S // tm,), in_specs=specs + [blk(0)], out_specs=[blk(0), blk(0), vec, vec, mat, vec],
        out_shape=[jax.ShapeDtypeStruct((S, CONV_CH), F32), jax.ShapeDtypeStruct((S, CONV_CH), BF16),
                   jax.ShapeDtypeStruct((1, CONV_CH), F32), jax.ShapeDtypeStruct((1, CONV_CH), F32),
                   jax.ShapeDtypeStruct((CONV_CH, CONV_CH), F32), jax.ShapeDtypeStruct((1, CONV_CH), F32)],
        scratch_shapes=[pltpu.VMEM((tm + CONF_HALO, CONV_CH), F32)],
        compiler_params=_cparams(("arbitrary",)),
    )(u, u, u, u, u, dw_w, dw_b, ln_w, ln_b, pw, dy)


def _iota2(shape, dim):
    return lax.broadcasted_iota(jnp.int32, shape, dim)


def _gdn_post(pre_q, pre_k, pre_v, b_in, a_in, a_log, dt_bias):
    tm = pre_q.shape[0]
    q, k, v = _silu(pre_q), _silu(pre_k), _silu(pre_v)
    qs, ks = [], []
    for h in range(GDN_HEADS):
        sl = slice(h * GDN_DK, (h + 1) * GDN_DK)
        qh, kh = q[:, sl], k[:, sl]
        qs.append(qh * lax.rsqrt(jnp.sum(qh * qh, axis=-1, keepdims=True) + 1e-6) * (GDN_DK ** -0.5))
        ks.append(kh * lax.rsqrt(jnp.sum(kh * kh, axis=-1, keepdims=True) + 1e-6))
    beta = _sigmoid(b_in)
    g = -jnp.exp(a_log) * _softplus(a_in + dt_bias)
    nb = tm // GDN_CHUNK
    tril = (_iota2((nb, GDN_CHUNK, GDN_CHUNK), 1) >= _iota2((nb, GDN_CHUNK, GDN_CHUNK), 2)).astype(F32)
    gc = mm_nn(tril, g.reshape(nb, GDN_CHUNK, 128), "f32").reshape(tm, 128)
    return jnp.concatenate(qs, axis=1), jnp.concatenate(ks, axis=1), v, beta, gc


def _gdn_prep_specs(tm):
    H = GDN_HALO
    blk = lambda col: pl.BlockSpec((tm, GDN_W), lambda i, _c=col: (i, _c))
    lane = lambda col: pl.BlockSpec((tm, 128), lambda i, _c=col: (i, _c))
    vec = pl.BlockSpec((1, 128), lambda i: (0, 0))
    q0 = C_GQ // GDN_W
    specs = [blk(q0), blk(q0 + 1), blk(q0 + 2),
             _halo_spec(H, tm, GDN_W, q0), _halo_spec(H, tm, GDN_W, q0 + 1), _halo_spec(H, tm, GDN_W, q0 + 2),
             lane(C_BETA // 128), lane(C_ALPHA // 128),
             pl.BlockSpec((SHORT_CONV, GDN_W), lambda i: (0, 0)), pl.BlockSpec((SHORT_CONV, GDN_W), lambda i: (0, 1)),
             pl.BlockSpec((SHORT_CONV, GDN_W), lambda i: (0, 2)), vec, vec]
    return specs, blk, lane, vec


def _gdn_pre(x_refs, h_refs, w_refs, ext_ref, tm):
    first = pl.program_id(0) == 0
    pres = []
    for x_ref, h_ref, w_ref in zip(x_refs, h_refs, w_refs):
        _fill_ext(ext_ref, h_ref[...], x_ref[...], first, GDN_HALO)
        pres.append(_conv_taps(ext_ref, w_ref, SHORT_CONV, GDN_HALO, tm))
    return pres


def gdn_prep_fwd(u, conv_w, a_log, dt_bias, *, tm, name):
    S = u.shape[0]
    specs, blk, lane, vec = _gdn_prep_specs(tm)

    def body(xq, xk, xv, hq, hk, hv, bi, ai, wq, wk, wv, al, db, q_ref, k_ref, v_ref, beta_ref, gc_ref, ext_ref):
        pres = _gdn_pre((xq, xk, xv), (hq, hk, hv), (wq, wk, wv), ext_ref, tm)
        q, k, v, beta, gc = _gdn_post(*pres, bi[...], ai[...], al[...], db[...])
        q_ref[...] = q
        k_ref[...] = k
        v_ref[...] = v
        beta_ref[...] = beta
        gc_ref[...] = gc

    wide = jax.ShapeDtypeStruct((S, GDN_W), F32)
    narrow = jax.ShapeDtypeStruct((S, 128), F32)
    return pl.pallas_call(
        body, name=name, grid=(S // tm,), in_specs=specs,
        out_specs=[blk(0), blk(0), blk(0), lane(0), lane(0)], out_shape=[wide, wide, wide, narrow, narrow],
        scratch_shapes=[pltpu.VMEM((tm + GDN_HALO, GDN_W), F32)],
        compiler_params=_cparams(("parallel",)),
    )(u, u, u, u, u, u, u, u, conv_w, conv_w, conv_w, a_log, dt_bias)


def gdn_prep_bwd(u, conv_w, a_log, dt_bias, dq, dk, dv, dbeta, dgc, *, tm, name):
    S = u.shape[0]
    specs, blk, lane, vec = _gdn_prep_specs(tm)

    def body(xq, xk, xv, hq, hk, hv, bi, ai, wq, wk, wv, al, db, dq_ref, dk_ref, dv_ref, dbe_ref, dgc_ref,
             dpq_ref, dpk_ref, dpv_ref, dbi_ref, dai_ref, dal_ref, ddb_ref, ext_ref):
        pres = _gdn_pre((xq, xk, xv), (hq, hk, hv), (wq, wk, wv), ext_ref, tm)
        _, vjp = jax.vjp(_gdn_post, *pres, bi[...], ai[...], al[...], db[...])
        dpq, dpk, dpv, dbi, dai, dal, ddb = vjp((dq_ref[...], dk_ref[...], dv_ref[...], dbe_ref[...], dgc_ref[...]))
        dpq_ref[...] = dpq
        dpk_ref[...] = dpk
        dpv_ref[...] = dpv
        dbi_ref[...] = dbi.astype(BF16)
        dai_ref[...] = dai.astype(BF16)

        @pl.when(pl.program_id(0) == 0)
        def _():
            dal_ref[...] = jnp.zeros_like(dal_ref)
            ddb_ref[...] = jnp.zeros_like(ddb_ref)

        dal_ref[...] += dal
        ddb_ref[...] += ddb

    wide = jax.ShapeDtypeStruct((S, GDN_W), F32)
    outs = pl.pallas_call(
        body, name=name, grid=(S // tm,), in_specs=specs + [blk(0), blk(0), blk(0), lane(0), lane(0)],
        out_specs=[blk(0), blk(0), blk(0), lane(0), lane(0), vec, vec],
        out_shape=[wide, wide, wide, jax.ShapeDtypeStruct((S, 128), BF16), jax.ShapeDtypeStruct((S, 128), BF16),
                   jax.ShapeDtypeStruct((1, 128), F32), jax.ShapeDtypeStruct((1, 128), F32)],
        scratch_shapes=[pltpu.VMEM((tm + GDN_HALO, GDN_W), F32)],
        compiler_params=_cparams(("arbitrary",)),
    )(u, u, u, u, u, u, u, u, conv_w, conv_w, conv_w, a_log, dt_bias, dq, dk, dv, dbeta, dgc)
    return outs


def _lane_col(blk, h):
    return jnp.sum(jnp.where(_iota2(blk.shape, 1) == h, blk, 0.0), axis=1, keepdims=True)


@jax.custom_vjp
def _tri_inv(low):
    n = low.shape[-1]
    r, c = _iota2(low.shape, low.ndim - 2), _iota2(low.shape, low.ndim - 1)
    eye = (r == c).astype(F32)
    t = eye - jnp.where((r // 2 == c // 2) & (r > c), low, 0.0)
    s = 2
    while s < n:
        off = jnp.where((r // (2 * s) == c // (2 * s)) & (r // s > c // s), low, 0.0)
        t = t - _nn_raw(t, _nn_raw(off, t, "bf16x3"), "bf16x3")
        s *= 2
    return t


def _tri_inv_f(low):
    t = _tri_inv(low)
    return t, t


def _tri_inv_b(t, dt):
    d = -_nt_raw(_tn_raw(t, dt, "bf16x3"), t, "bf16x3")
    r, c = _iota2(d.shape, d.ndim - 2), _iota2(d.shape, d.ndim - 1)
    return (jnp.where(r > c, d, 0.0),)


_tri_inv.defvjp(_tri_inv_f, _tri_inv_b)


def _gdn_group(s0, q, k, v, z, beta_blk, gc_blk, nw, h0):
    C = GDN_CHUNK
    HP, R, _ = q.shape
    nb = R // C
    B = HP * nb
    q3, k3, v3 = (t.reshape(B, C, GDN_DK) for t in (q, k, v))
    b3 = jnp.stack([_lane_col(beta_blk, h0 + j) for j in range(HP)]).reshape(B, C, 1)
    g3 = jnp.stack([_lane_col(gc_blk, h0 + j) for j in range(HP)]).reshape(B, C, 1)
    r, c = _iota2((B, C, C), 1), _iota2((B, C, C), 2)
    causal, strict = r >= c, r > c
    g_t = gc_blk.T
    rows = [jnp.sum(jnp.where(_iota2((128, R), 0) == h0 + j, g_t, 0.0), axis=0, keepdims=True) for j in range(HP)]
    g_row = jnp.stack([rows[j][:, i * C:(i + 1) * C] for j in range(HP) for i in range(nb)])
    decay = jnp.where(causal, jnp.exp(jnp.where(causal, g3 - g_row, 0.0)), 0.0)
    low = jnp.where(strict, b3 * mm_nt(k3, k3) * decay, 0.0)
    t = _tri_inv(low)
    eg = jnp.exp(g3)
    four = lambda x: x.reshape((HP, nb) + x.shape[1:])
    w_v = four(mm_nn(t, v3 * b3))
    w_k = four(mm_nn(t, k3 * (b3 * eg)))
    qk = four(jnp.where(causal, mm_nt(q3, k3) * decay, 0.0))
    q_dec = four(q3 * eg)
    g_last = jnp.sum(jnp.where(_iota2((B, C, 1), 1) == C - 1, g3, 0.0), axis=1, keepdims=True)
    k_dec = four(k3 * jnp.exp(g_last - g3))
    e_last = four(jnp.exp(g_last))
    s, outs = s0, []
    for i in range(nb):
        v_new = w_v[:, i] - mm_nn(w_k[:, i], s)
        outs.append(mm_nn(q_dec[:, i], s) + mm_nn(qk[:, i], v_new))
        s = s * e_last[:, i] + mm_tn(k_dec[:, i], v_new)
    o = jnp.concatenate(outs, axis=1)
    y = o * lax.rsqrt(jnp.mean(o * o, axis=-1, keepdims=True) + 1e-6) * nw * _silu(z)
    return s, y


def _heads(ref, HP):
    return jnp.stack([ref[:, j * GDN_DK:(j + 1) * GDN_DK] for j in range(HP)])


def gdn_core_fwd(q, k, v, u, beta, gc, nw, *, name):
    S = q.shape[0]
    R = GDN_CHUNK * GDN_GROUP
    G = S // R
    HP = GDN_HEADS_PER_STEP
    W = HP * GDN_DK
    blk = pl.BlockSpec((R, W), lambda g, h: (g, h))
    lane = pl.BlockSpec((R, 128), lambda g, h: (g, 0))
    st = pl.BlockSpec((1, HP, GDN_DK, GDN_DK), lambda g, h: (g, h, 0, 0))

    def body(q_ref, k_ref, v_ref, z_ref, be_ref, gc_ref, nw_ref, y_ref, st_ref, s_ref):
        g, hs = pl.program_id(0), pl.program_id(1)
        s0 = jnp.where(g == 0, 0.0, s_ref[hs])
        st_ref[0] = s0
        s1, y = _gdn_group(s0, _heads(q_ref, HP), _heads(k_ref, HP), _heads(v_ref, HP), _heads(z_ref, HP), be_ref[...],
                           gc_ref[...], nw_ref[...], hs * HP)
        s_ref[hs] = s1
        for j in range(HP):
            y_ref[:, j * GDN_DK:(j + 1) * GDN_DK] = y[j].astype(BF16)

    return pl.pallas_call(
        body, name=name, grid=(G, GDN_HEADS // HP),
        in_specs=[blk, blk, blk, pl.BlockSpec((R, W), lambda g, h: (g, C_GZ // W + h)), lane, lane,
                  pl.BlockSpec((1, 128), lambda g, h: (0, 0))],
        out_specs=[blk, st],
        out_shape=[jax.ShapeDtypeStruct((S, GDN_W), BF16), jax.ShapeDtypeStruct((G, GDN_HEADS, GDN_DK, GDN_DK), F32)],
        scratch_shapes=[pltpu.VMEM((GDN_HEADS // HP, HP, GDN_DK, GDN_DK), F32)],
        compiler_params=_cparams(("arbitrary", "arbitrary")),
    )(q, k, v, u, beta, gc, nw)


def gdn_core_bwd(q, k, v, u, beta, gc, nw, states, dy, *, name):
    S = q.shape[0]
    R = GDN_CHUNK * GDN_GROUP
    G = S // R
    HP = GDN_HEADS_PER_STEP
    W = HP * GDN_DK
    blk = pl.BlockSpec((R, W), lambda g, h: (G - 1 - g, h))
    lane = pl.BlockSpec((R, 128), lambda g, h: (G - 1 - g, 0))
    vec = pl.BlockSpec((1, 128), lambda g, h: (0, 0))

    def body(q_ref, k_ref, v_ref, z_ref, be_ref, gc_ref, nw_ref, st_ref, *rest):
        dy_refs = rest[:HP]
        dq_ref, dk_ref, dv_ref, dz_ref, dbe_ref, dgc_ref, dnw_ref, ds_ref = rest[HP:]
        g, hs = pl.program_id(0), pl.program_id(1)

        @pl.when(hs == 0)
        def _():
            dbe_ref[...] = jnp.zeros_like(dbe_ref)
            dgc_ref[...] = jnp.zeros_like(dgc_ref)

        @pl.when((hs == 0) & (g == 0))
        def _():
            dnw_ref[...] = jnp.zeros_like(dnw_ref)

        _, vjp = jax.vjp(functools.partial(_gdn_group, h0=hs * HP), st_ref[0], _heads(q_ref, HP), _heads(k_ref, HP),
                         _heads(v_ref, HP), _heads(z_ref, HP), be_ref[...], gc_ref[...], nw_ref[...])
        ds_in = jnp.where(g == 0, 0.0, ds_ref[hs])
        dy = jnp.stack([r[...] for r in dy_refs])
        ds0, dq, dk, dv, dz, dbe, dgc, dnw = vjp((ds_in, dy))
        ds_ref[hs] = ds0
        for j in range(HP):
            sl = slice(j * GDN_DK, (j + 1) * GDN_DK)
            dq_ref[:, sl] = dq[j]
            dk_ref[:, sl] = dk[j]
            dv_ref[:, sl] = dv[j]
            dz_ref[:, sl] = dz[j].astype(BF16)
        dbe_ref[...] += dbe
        dgc_ref[...] += dgc
        dnw_ref[...] += dnw

    wide = jax.ShapeDtypeStruct((S, GDN_W), F32)
    narrow = jax.ShapeDtypeStruct((S, 128), F32)
    return pl.pallas_call(
        body, name=name, grid=(G, GDN_HEADS // HP),
        in_specs=[blk, blk, blk, pl.BlockSpec((R, W), lambda g, h: (G - 1 - g, C_GZ // W + h)), lane, lane, vec,
                  pl.BlockSpec((1, HP, GDN_DK, GDN_DK), lambda g, h: (G - 1 - g, h, 0, 0))]
        + [pl.BlockSpec((R, GDN_DK), lambda g, h, _j=j: (G - 1 - g, CONV_CH // GDN_DK + h * HP + _j)) for j in range(HP)],
        out_specs=[blk, blk, blk, blk, lane, lane, vec],
        out_shape=[wide, wide, wide, jax.ShapeDtypeStruct((S, GDN_W), BF16), narrow, narrow,
                   jax.ShapeDtypeStruct((1, 128), F32)],
        scratch_shapes=[pltpu.VMEM((GDN_HEADS // HP, HP, GDN_DK, GDN_DK), F32)],
        compiler_params=_cparams(("arbitrary", "arbitrary")),
    )(q, k, v, u, beta, gc, nw, states, *([dy] * HP))


def rope_tables(S):
    half = ROPE_DIM // 2
    inv = ROPE_THETA ** (-jnp.arange(half, dtype=F32) / half)
    ang = jnp.arange(S, dtype=F32)[:, None] * inv[None, :]
    cos, sin = jnp.cos(ang), jnp.sin(ang)
    rest = ATT_HD - ROPE_DIM
    c = jnp.concatenate([cos, cos, jnp.ones((S, rest), F32)], axis=1)
    s1 = jnp.concatenate([-sin, jnp.zeros((S, ATT_HD - half), F32)], axis=1)
    s2 = jnp.concatenate([jnp.zeros((S, half), F32), sin, jnp.zeros((S, rest), F32)], axis=1)
    return tuple(jnp.tile(t, (1, 2)) for t in (c, s1, s2))


def _rope(x, c, s1, s2):
    half = ROPE_DIM // 2
    return x * c + pltpu.roll(x, ATT_W - half, 1) * s1 + pltpu.roll(x, half, 1) * s2


def _unrope(dy, c, s1, s2):
    half = ROPE_DIM // 2
    return dy * c + pltpu.roll(dy * s1, half, 1) + pltpu.roll(dy * s2, ATT_W - half, 1)


def att_prep_fwd(u, tables, *, tm, name):
    S = u.shape[0]
    blk = lambda col: pl.BlockSpec((tm, ATT_W), lambda i, _c=col: (i, _c))
    tab = pl.BlockSpec((tm, 128), lambda i: (i, 0))

    def body(q_ref, k_ref, v_ref, c_ref, s1_ref, s2_ref, qo_ref, ko_ref, vo_ref):
        reps = ATT_W // 128
        c, s1, s2 = (jnp.tile(t[...], (1, reps)) for t in (c_ref, s1_ref, s2_ref))
        qo_ref[...] = (_rope(q_ref[...], c, s1, s2) * (ATT_HD ** -0.5)).astype(BF16)
        ko_ref[...] = _rope(k_ref[...], c, s1, s2).astype(BF16)
        vo_ref[...] = v_ref[...].astype(BF16)

    out = jax.ShapeDtypeStruct((S, ATT_W), BF16)
    return pl.pallas_call(
        body, name=name, grid=(S // tm,),
        in_specs=[blk(C_AQ // ATT_W), blk(C_AK // ATT_W), blk(C_AV // ATT_W), tab, tab, tab],
        out_specs=[blk(0)] * 3, out_shape=[out] * 3, compiler_params=_cparams(("parallel",)),
    )(u, u, u, *tables)


def att_prep_bwd(dqs, dks, dvs, tables, *, tm, name):
    S = dqs[0].shape[0]
    blk = pl.BlockSpec((tm, ATT_W), lambda i: (i, 0))
    tab = pl.BlockSpec((tm, 128), lambda i: (i, 0))

    def body(*refs):
        dq, dk, dv = (refs[3 * j][...].astype(F32) + refs[3 * j + 1][...].astype(F32) + refs[3 * j + 2][...].astype(F32)
                      for j in range(3))
        c_ref, s1_ref, s2_ref, qo_ref, ko_ref, vo_ref = refs[9:]
        reps = ATT_W // 128
        c, s1, s2 = (jnp.tile(t[...], (1, reps)) for t in (c_ref, s1_ref, s2_ref))
        qo_ref[...] = (_unrope(dq, c, s1, s2) * (ATT_HD ** -0.5)).astype(BF16)
        ko_ref[...] = _unrope(dk, c, s1, s2).astype(BF16)
        vo_ref[...] = dv.astype(BF16)

    out = jax.ShapeDtypeStruct((S, ATT_W), BF16)
    return pl.pallas_call(
        body, name=name, grid=(S // tm,), in_specs=[blk] * 9 + [tab] * 3, out_specs=[blk] * 3, out_shape=[out] * 3,
        compiler_params=_cparams(("parallel",)),
    )(*dqs, *dks, *dvs, *tables)


def _band_masks():
    qi, ki = _iota2((ATT_BLOCK, ATT_BLOCK), 0), _iota2((ATT_BLOCK, ATT_BLOCK), 1)
    return qi <= ki, ki <= qi


def _pair_diag(x):
    first = _iota2(x.shape, 1) < ATT_HD
    zero = jnp.zeros_like(x)
    return jnp.concatenate([jnp.where(first, x, zero), jnp.where(first, zero, x)], axis=0)


def att_pattern_fwd(qr, kr, vb, dil, *, name):
    S = qr.shape[0]
    L = S // dil
    nb = L // ATT_BLOCK
    view = lambda t: t.reshape(L, dil * t.shape[1])
    cur = pl.BlockSpec((ATT_BLOCK, ATT_W), lambda r, n: (n, r))
    prev = pl.BlockSpec((ATT_BLOCK, ATT_W), lambda r, n: (jnp.maximum(n - 1, 0), r))

    def body(q_ref, kc_ref, kp_ref, vc_ref, vp_ref, o_ref, l_ref):
        has_prev = pl.program_id(1) > 0
        m_prev, m_cur = _band_masks()
        m_prev = m_prev & has_prev
        first = _iota2((ATT_BLOCK, 128), 1) < ATT_HD
        lane = _iota2((ATT_BLOCK, 128), 1)
        stats = jnp.zeros((ATT_BLOCK, 128), F32)
        for p in range(ATT_HEADS // 2):
            sl = slice(p * 128, (p + 1) * 128)
            q = q_ref[:, sl]
            sp = _nt_raw(q, _pair_diag(kp_ref[:, sl]), "bf16")
            sc = _nt_raw(q, _pair_diag(kc_ref[:, sl]), "bf16")
            pps, pcs, dens, lses = [], [], [], []
            for half in range(2):
                hs = slice(half * 128, (half + 1) * 128)
                sp_h, sc_h = jnp.where(m_prev, sp[:, hs], NEG_INF), jnp.where(m_cur, sc[:, hs], NEG_INF)
                m = jnp.maximum(jnp.max(sp_h, axis=1, keepdims=True), jnp.max(sc_h, axis=1, keepdims=True))
                pp, pc = jnp.exp(sp_h - m), jnp.exp(sc_h - m)
                den = jnp.sum(pp, axis=1, keepdims=True) + jnp.sum(pc, axis=1, keepdims=True)
                pps.append(pp)
                pcs.append(pc)
                dens.append(den)
                lses.append(m + jnp.log(den))
            o = (_nn_raw(jnp.concatenate(pps, axis=1), _pair_diag(vp_ref[:, sl]), "bf16")
                 + _nn_raw(jnp.concatenate(pcs, axis=1), _pair_diag(vc_ref[:, sl]), "bf16"))
            o_ref[:, sl] = (o / jnp.where(first, dens[0], dens[1])).astype(BF16)
            stats = jnp.where(lane == 2 * p, lses[0], jnp.where(lane == 2 * p + 1, lses[1], stats))
        l_ref[...] = stats

    narrow = pl.BlockSpec((ATT_BLOCK, 128), lambda r, n: (n, r))
    o, l = pl.pallas_call(
        body, name=name, grid=(dil, nb), in_specs=[cur, cur, prev, cur, prev], out_specs=[cur, narrow],
        out_shape=[jax.ShapeDtypeStruct((L, dil * ATT_W), BF16), jax.ShapeDtypeStruct((L, dil * 128), F32)],
        compiler_params=_cparams(("parallel", "arbitrary")),
    )(view(qr), view(kr), view(kr), view(vb), view(vb))
    return o.reshape(S, ATT_W), l.reshape(S, 128)


def _head_spread():
    return (_iota2((128, ATT_W), 1) // ATT_HD == _iota2((128, ATT_W), 0)).astype(F32)


def att_combine_fwd(os_, ls, u, *, tm, name):
    S = u.shape[0]
    blk = lambda col: pl.BlockSpec((tm, ATT_W), lambda i, _c=col: (i, _c))
    lane = pl.BlockSpec((tm, 128), lambda i: (i, 0))

    def body(o1, o2, o3, l1, l2, l3, g_ref, y_ref, o_ref, lse_ref):
        a, b, c = l1[...], l2[...], l3[...]
        m = jnp.maximum(jnp.maximum(a, b), c)
        ea, eb, ec = jnp.exp(a - m), jnp.exp(b - m), jnp.exp(c - m)
        den = ea + eb + ec
        spread = _head_spread()
        wa, wb, wc = (_nn_raw(e / den, spread, "bf16x3") for e in (ea, eb, ec))
        o = wa * o1[...].astype(F32) + wb * o2[...].astype(F32) + wc * o3[...].astype(F32)
        o_ref[...] = o
        lse_ref[...] = m + jnp.log(den)
        y_ref[...] = (o * _silu(g_ref[...])).astype(BF16)

    return pl.pallas_call(
        body, name=name, grid=(S // tm,), in_specs=[blk(0)] * 3 + [lane] * 3 + [blk(C_AG // ATT_W)],
        out_specs=[blk(0), blk(0), lane],
        out_shape=[jax.ShapeDtypeStruct((S, ATT_W), BF16), jax.ShapeDtypeStruct((S, ATT_W), F32),
                   jax.ShapeDtypeStruct((S, 128), F32)],
        compiler_params=_cparams(("parallel",)),
    )(*os_, *ls, u)


def att_combine_bwd(dy, o, u, *, tm, name):
    S = u.shape[0]
    cw = 256
    base = (CONV_CH + GDN_W) // cw
    blk = lambda col: pl.BlockSpec((tm, ATT_W), lambda i, _c=col: (i, _c))

    def body(dy0, dy1, dy2, o_ref, g_ref, do_ref, dg_ref, dl_ref):
        g, d, o = g_ref[...], jnp.concatenate([dy0[...], dy1[...], dy2[...]], axis=1), o_ref[...]
        sg = _sigmoid(g)
        d_o = d * (g * sg)
        do_ref[...] = d_o.astype(BF16)
        dg_ref[...] = (d * o * (sg * (1.0 + g * (1.0 - sg)))).astype(BF16)
        dl_ref[...] = _nt_raw(d_o * o, _head_spread(), "bf16x3")

    return pl.pallas_call(
        body, name=name, grid=(S // tm,),
        in_specs=[pl.BlockSpec((tm, cw), lambda i, _j=j: (i, base + _j)) for j in range(ATT_W // cw)] + [blk(0), blk(C_AG // ATT_W)],
        out_specs=[blk(0), blk(0), pl.BlockSpec((tm, 128), lambda i: (i, 0))],
        out_shape=[jax.ShapeDtypeStruct((S, ATT_W), BF16), jax.ShapeDtypeStruct((S, ATT_W), BF16),
                   jax.ShapeDtypeStruct((S, 128), F32)],
        compiler_params=_cparams(("parallel",)),
    )(dy, dy, dy, o, u)


def att_pattern_bwd(qr, kr, vb, do, delta, lse, dil, *, name):
    S = qr.shape[0]
    L = S // dil
    nb = L // ATT_BLOCK
    view = lambda t: t.reshape(L, dil * t.shape[1])
    cur = pl.BlockSpec((ATT_BLOCK, ATT_W), lambda r, n: (jnp.minimum(n, nb - 1), r))
    prev = pl.BlockSpec((ATT_BLOCK, ATT_W), lambda r, n: (jnp.maximum(n - 1, 0), r))
    narrow = pl.BlockSpec((ATT_BLOCK, 128), lambda r, n: (jnp.minimum(n, nb - 1), r))

    def body(q_ref, kc_ref, kp_ref, vc_ref, vp_ref, do_ref, dl_ref, l_ref, dq_ref, dk_ref, dv_ref, ck_ref, cv_ref):
        n = pl.program_id(1)

        @pl.when(n < nb)
        def _():
            m_prev, m_cur = _band_masks()
            m_prev = m_prev & (n > 0)
            m_prev2, m_cur2 = jnp.concatenate([m_prev, m_prev], axis=1), jnp.concatenate([m_cur, m_cur], axis=1)
            first = _iota2((ATT_BLOCK, 128), 1) < ATT_HD
            wide = (ATT_BLOCK, 128)
            for p in range(ATT_HEADS // 2):
                sl = slice(p * 128, (p + 1) * 128)
                q, d_o = q_ref[:, sl], do_ref[:, sl]
                kp, kc, vp, vc = (_pair_diag(r[:, sl]) for r in (kp_ref, kc_ref, vp_ref, vc_ref))
                halves = lambda a, b: jnp.concatenate([jnp.broadcast_to(a, wide), jnp.broadcast_to(b, wide)], axis=1)
                delta = halves(dl_ref[:, 2 * p:2 * p + 1], dl_ref[:, 2 * p + 1:2 * p + 2])
                lse2 = halves(l_ref[:, 2 * p:2 * p + 1], l_ref[:, 2 * p + 1:2 * p + 2])
                pp = jnp.where(m_prev2, jnp.exp(_nt_raw(q, kp, "bf16") - lse2), 0.0)
                pc = jnp.where(m_cur2, jnp.exp(_nt_raw(q, kc, "bf16") - lse2), 0.0)
                dsp = pp * (_nt_raw(d_o, vp, "bf16") - delta)
                dsc = pc * (_nt_raw(d_o, vc, "bf16") - delta)
                dq_ref[:, sl] = (_nn_raw(dsp, kp, "bf16") + _nn_raw(dsc, kc, "bf16")).astype(BF16)
                fold = lambda t: jnp.where(first, t[:ATT_BLOCK], t[ATT_BLOCK:])
                dk_prev, dv_prev = fold(_tn_raw(dsp, q, "bf16")), fold(_tn_raw(pp, d_o, "bf16"))

                @pl.when(n > 0)
                def _():
                    dk_ref[:, sl] = (ck_ref[:, sl] + dk_prev).astype(BF16)
                    dv_ref[:, sl] = (cv_ref[:, sl] + dv_prev).astype(BF16)

                ck_ref[:, sl] = fold(_tn_raw(dsc, q, "bf16"))
                cv_ref[:, sl] = fold(_tn_raw(pc, d_o, "bf16"))

        @pl.when(n == nb)
        def _():
            dk_ref[...] = ck_ref[...].astype(BF16)
            dv_ref[...] = cv_ref[...].astype(BF16)

    out = jax.ShapeDtypeStruct((L, dil * ATT_W), BF16)
    dq, dk, dv = pl.pallas_call(
        body, name=name, grid=(dil, nb + 1), in_specs=[cur, cur, prev, cur, prev, cur, narrow, narrow],
        out_specs=[cur, prev, prev], out_shape=[out, out, out],
        scratch_shapes=[pltpu.VMEM((ATT_BLOCK, ATT_W), F32), pltpu.VMEM((ATT_BLOCK, ATT_W), F32)],
        compiler_params=_cparams(("arbitrary", "arbitrary")),
    )(view(qr), view(kr), view(kr), view(vb), view(vb), view(do), view(delta), view(lse))
    return dq.reshape(S, ATT_W), dk.reshape(S, ATT_W), dv.reshape(S, ATT_W)


def _loss_rows(x, w, tgt):
    err = _rms_fn(x, w) - tgt
    return jnp.sum(0.5 * jnp.mean(err * err, axis=-1, keepdims=True), axis=0, keepdims=True)


def loss_head(x, w, tgt, *, tm, name):
    S, D = x.shape

    def body(x_ref, w_ref, t_ref, l_ref, dx_ref, dw_ref):
        val, vjp = jax.vjp(_loss_rows, x_ref[...], w_ref[...], t_ref[...])
        dx, dw, _ = vjp(jnp.ones((1, 1), F32))
        dx_ref[...] = dx

        @pl.when(pl.program_id(0) == 0)
        def _():
            l_ref[...] = jnp.zeros_like(l_ref)
            dw_ref[...] = jnp.zeros_like(dw_ref)

        l_ref[...] += val
        dw_ref[...] += dw

    row = pl.BlockSpec((tm, D), lambda i: (i, 0))
    vec = pl.BlockSpec((1, D), lambda i: (0, 0))
    one = pl.BlockSpec((1, 1), lambda i: (0, 0))
    return pl.pallas_call(
        body, name=name, grid=(S // tm,), in_specs=[row, vec, row], out_specs=[one, row, vec],
        out_shape=[jax.ShapeDtypeStruct((1, 1), F32), jax.ShapeDtypeStruct((S, D), F32), jax.ShapeDtypeStruct((1, D), F32)],
        compiler_params=_cparams(("arbitrary",)),
    )(x, w, tgt)


def adam(w, g, m, v, *, name):
    shape = w.shape
    C = shape[-1]
    R = w.size // C
    br = R
    while br * C * 4 > (1 << 21) and br % 16 == 0:
        br //= 2
    two = lambda t: t.reshape(R, C)

    def body(w_ref, g_ref, m_ref, v_ref, d_ref, mo_ref, vo_ref):
        gg = g_ref[...]
        m_new = ADAM_B1 * m_ref[...] + (1.0 - ADAM_B1) * gg
        v_new = ADAM_B2 * v_ref[...] + (1.0 - ADAM_B2) * jnp.square(gg)
        m_hat = m_new / (1.0 - ADAM_B1 ** ADAM_STEP)
        v_hat = v_new / (1.0 - ADAM_B2 ** ADAM_STEP)
        d_ref[...] = -ADAM_LR * (m_hat / (jnp.sqrt(v_hat) + ADAM_EPS) + ADAM_WD * w_ref[...])
        mo_ref[...] = m_new
        vo_ref[...] = v_new

    blk = pl.BlockSpec((br, C), lambda i: (i, 0))
    out = jax.ShapeDtypeStruct((R, C), F32)
    d, mo, vo = pl.pallas_call(
        body, name=name, grid=(R // br,), in_specs=[blk] * 4, out_specs=[blk] * 3, out_shape=[out] * 3,
        compiler_params=_cparams(("parallel",)),
    )(two(w), two(g), two(m), two(v))
    return d.reshape(shape), mo.reshape(shape), vo.reshape(shape)


MESH_IDS = pl.DeviceIdType.MESH
ANY = pl.BlockSpec(memory_space=pl.ANY)


def _my_id():
    return 4 * lax.axis_index("x") + 2 * lax.axis_index("y") + lax.axis_index("c")


def _peer(k):
    x, y, c = lax.axis_index("x"), lax.axis_index("y"), lax.axis_index("c")
    flip = lambda v, bit: 1 - v if bit else v
    return (flip(x, k & 4), flip(y, k & 2), flip(c, k & 1))


def all_gather(arrs, *, name):
    n = len(arrs)

    def body(*refs):
        ins, outs = refs[:n], refs[n:2 * n]
        send, recv, local = refs[2 * n:]
        me = _my_id()
        started = []
        for a in range(n):
            lc = pltpu.make_async_copy(ins[a], outs[a].at[me], local.at[a])
            lc.start()
            started.append(lc)
            for k in range(1, N_DEV):
                cp = pltpu.make_async_remote_copy(src_ref=ins[a], dst_ref=outs[a].at[me], send_sem=send.at[a, k - 1],
                                                  recv_sem=recv.at[a, k - 1], device_id=_peer(k), device_id_type=MESH_IDS)
                cp.start()
                started.append(cp)
        for cp in started:
            cp.wait()

    return pl.pallas_call(
        body, name=name, in_specs=[ANY] * n, out_specs=[ANY] * n,
        out_shape=[jax.ShapeDtypeStruct((N_DEV,) + a.shape, a.dtype) for a in arrs],
        scratch_shapes=[pltpu.SemaphoreType.DMA((n, N_DEV - 1)), pltpu.SemaphoreType.DMA((n, N_DEV - 1)),
                        pltpu.SemaphoreType.DMA((n,))],
        compiler_params=pltpu.CompilerParams(has_side_effects=True),
    )(*arrs)


def scatter_exchange(groups, pack, *, name):
    flat = [a for grp in groups for a in grp]
    n = len(flat) + 1
    shapes = [jax.ShapeDtypeStruct((N_DEV, len(grp), grp[0].shape[0] // N_DEV, grp[0].shape[1]), grp[0].dtype) for grp in groups]
    shapes.append(jax.ShapeDtypeStruct((N_DEV,) + pack.shape, pack.dtype))
    index = [(gi, li) for gi, grp in enumerate(groups) for li in range(len(grp))]

    def body(*refs):
        ins, outs = refs[:n], refs[n:n + len(shapes)]
        send, recv, local = refs[n + len(shapes):]
        me = _my_id()
        started = []
        for a in range(n):
            if a < n - 1:
                gi, li = index[a]
                r = ins[a].shape[0] // N_DEV
                src = lambda j, _a=a, _r=r: ins[_a].at[pl.ds(pl.multiple_of(j * _r, 8), _r), :]
                dst = outs[gi].at[me, li]
            else:
                src = lambda j, _a=a: ins[_a]
                dst = outs[-1].at[me]
            lc = pltpu.make_async_copy(src(me), dst, local.at[a])
            lc.start()
            started.append(lc)
            for k in range(1, N_DEV):
                cp = pltpu.make_async_remote_copy(src_ref=src(me ^ k), dst_ref=dst, send_sem=send.at[a, k - 1],
                                                  recv_sem=recv.at[a, k - 1], device_id=_peer(k), device_id_type=MESH_IDS)
                cp.start()
                started.append(cp)
        for cp in started:
            cp.wait()

    return pl.pallas_call(
        body, name=name, in_specs=[ANY] * n, out_specs=[ANY] * len(shapes), out_shape=shapes,
        scratch_shapes=[pltpu.SemaphoreType.DMA((n, N_DEV - 1)), pltpu.SemaphoreType.DMA((n, N_DEV - 1)),
                        pltpu.SemaphoreType.DMA((n,))],
        compiler_params=pltpu.CompilerParams(has_side_effects=True),
    )(*flat, pack)


def slot_sum(x, *, name):
    _, A, R, C = x.shape
    br = R
    while br * C * 4 * N_DEV > (1 << 23) and br % 16 == 0:
        br //= 2

    def body(x_ref, o_ref):
        acc = x_ref[0, 0].astype(F32)
        for s in range(1, N_DEV):
            acc = acc + x_ref[s, 0].astype(F32)
        o_ref[0] = acc

    return pl.pallas_call(
        body, name=name, grid=(A, R // br),
        in_specs=[pl.BlockSpec((N_DEV, 1, br, C), lambda a, i: (0, a, i, 0))],
        out_specs=pl.BlockSpec((1, br, C), lambda a, i: (a, i, 0)),
        out_shape=jax.ShapeDtypeStruct((A, R, C), F32),
        compiler_params=_cparams(("parallel", "parallel")),
    )(x)


HBM_SPEC = pl.BlockSpec(memory_space=pltpu.HBM)
SEM_SPEC = pl.BlockSpec(memory_space=pltpu.SEMAPHORE)
DATAFLOW = pltpu.SideEffectType.DATAFLOW_SIDE_EFFECTING


def _push_copies(src_refs, land_refs, send_sems, recv_sems, by_rows):
    me = _my_id()
    copies = []
    for a, (src, land) in enumerate(zip(src_refs, land_refs)):
        rows = land.shape[1]
        for k in range(1, N_DEV):
            piece = src.at[pl.ds(pl.multiple_of((me ^ k) * rows, 8), rows), :] if by_rows else src
            copies.append(pltpu.make_async_remote_copy(
                src_ref=piece, dst_ref=land.at[me], send_sem=send_sems[a].at[k - 1], recv_sem=recv_sems[a].at[k - 1],
                device_id=_peer(k), device_id_type=MESH_IDS))
    return copies


def push_start(srcs, lands, *, by_rows, name):
    n = len(srcs)

    def body(*refs):
        src_refs, land_refs = refs[:n], refs[n:2 * n]
        send_sems, recv_sems = refs[2 * n:3 * n], refs[3 * n:4 * n]
        token = refs[6 * n]
        for cp in _push_copies(src_refs, land_refs, send_sems, recv_sems, by_rows):
            cp.start()
        token[...] = jnp.zeros_like(token)

    sems = [pltpu.SemaphoreType.DMA((N_DEV - 1,))] * (2 * n)
    bufs = [pltpu.HBM(a.shape, a.dtype) for a in list(srcs) + list(lands)]
    outs = pl.pallas_call(
        body, name=name, out_shape=tuple(sems + bufs + [jax.ShapeDtypeStruct((8, 128), F32)]),
        in_specs=[HBM_SPEC] * (2 * n), out_specs=tuple([SEM_SPEC] * (2 * n) + [HBM_SPEC] * (2 * n) + [pl.BlockSpec(memory_space=pltpu.VMEM)]),
        input_output_aliases={i: 2 * n + i for i in range(2 * n)},
        compiler_params=pltpu.CompilerParams(has_side_effects=DATAFLOW),
    )(*[pltpu.with_memory_space_constraint(a, pltpu.HBM) for a in list(srcs) + list(lands)])
    return outs[:n], outs[n:2 * n], outs[2 * n:3 * n], outs[3 * n:4 * n], outs[4 * n]


def push_wait(send_sems, recv_sems, srcs, lands, after, *, by_rows, name):
    n = len(srcs)

    def body(*refs):
        src_refs, land_refs = refs[:n], refs[n:2 * n]
        send, recv = refs[2 * n:3 * n], refs[3 * n:4 * n]
        for cp in _push_copies(src_refs, land_refs, send, recv, by_rows):
            cp.wait_send()
            cp.wait_recv()

    outs = pl.pallas_call(
        body, name=name, out_shape=tuple(pltpu.HBM(a.shape, a.dtype) for a in list(srcs) + list(lands)),
        in_specs=[HBM_SPEC] * (2 * n) + [SEM_SPEC] * (2 * n) + [ANY], out_specs=tuple([HBM_SPEC] * (2 * n)),
        input_output_aliases={i: i for i in range(2 * n)},
        compiler_params=pltpu.CompilerParams(has_side_effects=DATAFLOW),
    )(*srcs, *lands, *send_sems, *recv_sems, after)
    return outs[n:]


def _landing(own, slots_shape):
    return lax.dynamic_update_index_in_dim(lax.empty(slots_shape, own.dtype), own, _my_id(), 0)


def _pack(arrs):
    flat = []
    for a in arrs:
        f = a.reshape(-1).astype(F32)
        flat.append(jnp.pad(f, (0, (-f.size) % 128)))
    f = jnp.concatenate(flat)
    return jnp.pad(f, (0, (-f.size) % 1024)).reshape(-1, 128)


def _unpack(p, shapes):
    f = p.reshape(-1)
    out, off = [], 0
    for s in shapes:
        n = math.prod(s)
        out.append(f[off:off + n].reshape(s))
        off += n + (-n) % 128
    return out


def _to_padded_cols(w):
    z = lambda n: jnp.zeros(w.shape[:-1] + (n,), w.dtype)
    return jnp.concatenate([w[..., 0:4608], w[..., 4620:7692], w[..., 4608:4614], z(122), w[..., 4614:4620], z(378)], axis=-1)


def _from_padded_cols(w):
    return jnp.concatenate([w[..., 0:4608], w[..., C_BETA:C_BETA + 6], w[..., C_ALPHA:C_ALPHA + 6], w[..., 4608:7680]], axis=-1)


def _lane_pad(v):
    return jnp.pad(v, (0, 128 - v.shape[0]))[None, :]


TM_MM, TN_MM, TK_MM = 1024, 1024, 2048
TM_ROW = 512


def layer_fwd(x, p, tabs, l):
    h = rms_fwd(x, p["norm_w"], tm=TM_ROW, name=f"rms_fwd_{l}")
    u = matmul(h, p["w_in"], mode="nn", tm=TM_MM, tn=TN_MM, tk=TK_MM, name=f"in_proj_{l}")
    y_conv = conf_fwd(u, p["dw_w"], p["dw_b"], p["ln_w"], p["ln_b"], p["pw"], tm=TM_ROW, name=f"conf_fwd_{l}")
    q, k, v, beta, gc = gdn_prep_fwd(u, p["conv_w"], p["a_log"], p["dt_bias"], tm=TM_ROW, name=f"gdn_prep_fwd_{l}")
    y_gdn, states = gdn_core_fwd(q, k, v, u, beta, gc, p["gdn_nw"], name=f"gdn_core_fwd_{l}")
    qr, kr, vb = att_prep_fwd(u, tabs, tm=TM_ROW, name=f"att_prep_fwd_{l}")
    os_, ls = [], []
    for _, dil in DIL_PATTERNS:
        o_p, l_p = att_pattern_fwd(qr, kr, vb, dil, name=f"att_fwd_d{dil}_{l}")
        os_.append(o_p)
        ls.append(l_p)
    y_att, o, lse = att_combine_fwd(os_, ls, u, tm=TM_ROW, name=f"att_combine_fwd_{l}")
    y = jnp.concatenate([y_conv, y_gdn, y_att], axis=1)
    x_new = matmul(y, p["w_out"], mode="nn", tm=TM_MM, tn=TN_MM, tk=TK_MM, residual=x, name=f"out_proj_{l}")
    saved = dict(x=x, h=h, u=u, y=y, q=q, k=k, v=v, beta=beta, gc=gc, states=states, qr=qr, kr=kr, vb=vb, o=o, lse=lse)
    return x_new, saved


def layer_bwd(dx_out, s, p, tabs, l):
    S = dx_out.shape[0]
    u = s["u"]
    dy = matmul(dx_out, p["w_out"], mode="nt", tm=TM_MM, tn=TN_MM, tk=TK_MM, name=f"out_proj_dy_{l}")
    g_w_out = matmul(s["y"], dx_out, mode="tn", tm=TM_MM, tn=TN_MM, tk=TK_MM, out_dtype=BF16, name=f"out_proj_dw_{l}")
    dc, d_cg, g_ln_w, g_ln_b, g_pw, g_dw_b = conf_bwd_post(u, dy, p["dw_w"], p["dw_b"], p["ln_w"], p["ln_b"], p["pw"],
                                                         tm=TM_ROW, name=f"conf_bwd_post_{l}")
    d_ca, d_cb, g_dw_w = conv_bwd(dc, [(u, C_CA), (u, C_CB)], p["dw_w"], K=CONV_WIDTH, H=CONF_HALO, tm=TM_ROW, cw=CONV_CH,
                                  glu=True, name=f"conf_bwd_conv_{l}")
    dq, dk, dv, d_gz, dbeta, dgc, g_gdn_nw = gdn_core_bwd(s["q"], s["k"], s["v"], u, s["beta"], s["gc"], p["gdn_nw"],
                                                          s["states"], dy, name=f"gdn_core_bwd_{l}")
    dpq, dpk, dpv, d_bi, d_ai, g_a_log, g_dt_bias = gdn_prep_bwd(u, p["conv_w"], p["a_log"], p["dt_bias"], dq, dk, dv, dbeta, dgc,
                                                                 tm=TM_ROW, name=f"gdn_prep_bwd_{l}")
    d_gqkv, g_conv_w = conv_bwd(jnp.concatenate([dpq, dpk, dpv], axis=1), [(u, C_GQ)], p["conv_w"], K=SHORT_CONV, H=GDN_HALO,
                                tm=TM_ROW, cw=GDN_W, glu=False, name=f"gdn_bwd_conv_{l}")
    do, d_ag, delta = att_combine_bwd(dy, s["o"], u, tm=TM_ROW, name=f"att_combine_bwd_{l}")
    dqs, dks, dvs = [], [], []
    for _, dil in DIL_PATTERNS:
        a, b, c = att_pattern_bwd(s["qr"], s["kr"], s["vb"], do, delta, s["lse"], dil, name=f"att_bwd_d{dil}_{l}")
        dqs.append(a)
        dks.append(b)
        dvs.append(c)
    d_aq, d_ak, d_av = att_prep_bwd(dqs, dks, dvs, tabs, tm=TM_ROW, name=f"att_prep_bwd_{l}")
    zeros = jnp.zeros((S, IN_PAD - C_ALPHA - 128), BF16)
    du = jnp.concatenate([d_ca, d_cb, d_cg, d_gqkv, d_gz, d_aq, d_ak, d_av, d_ag, d_bi, d_ai, zeros], axis=1)
    dh = matmul(du, p["w_in"], mode="nt", tm=TM_MM, tn=TN_MM, tk=TK_MM, name=f"in_proj_dh_{l}")
    g_w_in = matmul(s["h"], du, mode="tn", tm=TM_MM, tn=TN_MM, tk=TK_MM, out_dtype=BF16, name=f"in_proj_dw_{l}")
    dx, g_norm_w = rms_bwd(s["x"], p["norm_w"], dh, dx_out, tm=TM_ROW // 2, name=f"rms_bwd_{l}")
    grads = dict(norm_w=g_norm_w[0], w_in=g_w_in, conv_qkv_w=g_conv_w, a_log=g_a_log[0, :GDN_HEADS], dt_bias=g_dt_bias[0, :GDN_HEADS],
                 gdn_norm_w=g_gdn_nw[0], conf_dw_w=g_dw_w, conf_dw_b=g_dw_b[0], conf_ln_w=g_ln_w[0], conf_ln_b=g_ln_b[0],
                 conf_pw_w=g_pw, w_out=g_w_out)
    return dx, grads


WEIGHTS = ("norm_w", "w_in", "conv_qkv_w", "a_log", "dt_bias", "gdn_norm_w", "conf_dw_w", "conf_dw_b", "conf_ln_w",
           "conf_ln_b", "conf_pw_w", "w_out", "final_norm_w")
SMALL_REPLICATED = ("norm_w", "a_log", "dt_bias", "gdn_norm_w", "conf_dw_b", "conf_ln_w", "conf_ln_b")


def kernel(x, norm_w, w_in, conv_qkv_w, a_log, dt_bias, gdn_norm_w, conf_dw_w, conf_dw_b, conf_ln_w, conf_ln_b, conf_pw_w, w_out, final_norm_w, loss_target, m_norm_w, m_w_in, m_conv_qkv_w, m_a_log, m_dt_bias, m_gdn_norm_w, m_conf_dw_w, m_conf_dw_b, m_conf_ln_w, m_conf_ln_b, m_conf_pw_w, m_w_out, m_final_norm_w, v_norm_w, v_w_in, v_conv_qkv_w, v_a_log, v_dt_bias, v_gdn_norm_w, v_conf_dw_w, v_conf_dw_b, v_conf_ln_w, v_conf_ln_b, v_conf_pw_w, v_w_out, v_final_norm_w):
    w = dict(norm_w=norm_w, w_in=w_in, conv_qkv_w=conv_qkv_w, a_log=a_log, dt_bias=dt_bias, gdn_norm_w=gdn_norm_w,
             conf_dw_w=conf_dw_w, conf_dw_b=conf_dw_b, conf_ln_w=conf_ln_w, conf_ln_b=conf_ln_b, conf_pw_w=conf_pw_w,
             w_out=w_out, final_norm_w=final_norm_w)
    m = dict(zip(WEIGHTS, (m_norm_w, m_w_in, m_conv_qkv_w, m_a_log, m_dt_bias, m_gdn_norm_w, m_conf_dw_w, m_conf_dw_b,
                           m_conf_ln_w, m_conf_ln_b, m_conf_pw_w, m_w_out, m_final_norm_w)))
    v = dict(zip(WEIGHTS, (v_norm_w, v_w_in, v_conv_qkv_w, v_a_log, v_dt_bias, v_gdn_norm_w, v_conf_dw_w, v_conf_dw_b,
                           v_conf_ln_w, v_conf_ln_b, v_conf_pw_w, v_w_out, v_final_norm_w)))
    S = x.shape[1]
    L = norm_w.shape[0]
    me = _my_id()

    small_shapes = [conv_qkv_w.shape, conf_dw_w.shape, conf_pw_w.shape]
    w_in_b, w_out_b = _to_padded_cols(w_in).astype(BF16), w_out.astype(BF16)
    in_slots, out_slots = (N_DEV,) + w_in_b.shape[1:], (N_DEV,) + w_out_b.shape[1:]
    g_in0, g_out0, g_small = all_gather([w_in_b[0], w_out_b[0], _pack([conv_qkv_w, conf_dw_w, conf_pw_w])], name="gather_first")
    gathers, tie = {}, jnp.zeros((1, 1), F32)
    for l in range(1, L):
        *flight, token = push_start([w_in_b[l], w_out_b[l]], [_landing(w_in_b[l], in_slots), _landing(w_out_b[l], out_slots)],
                                    by_rows=False, name=f"gather_start_{l}")
        gathers[l] = flight
        tie = tie + token[0:1, 0:1]
    parts = [_unpack(g_small[s], small_shapes) for s in range(N_DEV)]
    conv_full = jnp.concatenate([pt[0] for pt in parts], axis=2)
    dw_full = jnp.concatenate([pt[1] for pt in parts], axis=2)
    pw_full = jnp.concatenate([pt[2] for pt in parts], axis=1)
    tabs = rope_tables(S)

    def layer_params(l, full_in, full_out):
        return dict(
            norm_w=norm_w[l][None], w_in=full_in.reshape(D_MODEL, IN_PAD), w_out=full_out.reshape(D_MODEL, D_MODEL),
            conv_w=conv_full[l], a_log=_lane_pad(a_log[l]), dt_bias=_lane_pad(dt_bias[l]), gdn_nw=gdn_norm_w[l][None],
            dw_w=dw_full[l], dw_b=conf_dw_b[l][None], ln_w=conf_ln_w[l][None], ln_b=conf_ln_b[l][None], pw=pw_full[l])

    xs = x[0]
    params, saved = [], []
    for l in range(L):
        if l == 0:
            p = layer_params(0, g_in0, g_out0)
            p["norm_w"] = p["norm_w"] + tie
        else:
            p = layer_params(l, *push_wait(*gathers[l], xs, by_rows=False, name=f"gather_wait_{l}"))
        params.append(p)
        xs, sv = layer_fwd(xs, p, tabs, l)
        saved.append(sv)
    loss_part, dx, g_final = loss_head(xs, final_norm_w[None], loss_target[0], tm=TM_ROW // 2, name="loss_head")

    layer_grads, scatters = [None] * L, {}
    for l in reversed(range(L)):
        dx, layer_grads[l] = layer_bwd(dx, saved[l], params[l], tabs, l)
        big = [layer_grads[l]["w_in"], layer_grads[l]["w_out"]]
        lands = [_landing(lax.dynamic_slice_in_dim(a, me * sl[1], sl[1], axis=0), sl) for a, sl in zip(big, (in_slots, out_slots))]
        *flight, token = push_start(big, lands, by_rows=True, name=f"scatter_start_{l}")
        scatters[l] = flight
        if l > 0:
            params[l - 1]["dw_b"] = params[l - 1]["dw_b"] + token[0:1, 0:1]
    stack = lambda name: jnp.stack([layer_grads[l][name] for l in range(L)])
    small = [loss_part] + [stack(n) for n in SMALL_REPLICATED] + [g_final[0], stack("conv_qkv_w"), stack("conf_dw_w")]
    small_shapes = [a.shape for a in small]
    r_pw, r_small = scatter_exchange([[layer_grads[l]["conf_pw_w"] for l in range(L)]], _pack(small), name="scatter_small")
    sums_in, sums_out = [], []
    for l in range(L):
        land_in, land_out = push_wait(*scatters[l], dx, by_rows=True, name=f"scatter_wait_{l}")
        sums_in.append(slot_sum(land_in[:, None], name=f"sum_w_in_{l}"))
        sums_out.append(slot_sum(land_out[:, None], name=f"sum_w_out_{l}"))
    g = {}
    g["w_in"] = _from_padded_cols(jnp.concatenate(sums_in, axis=0))
    g["w_out"] = jnp.concatenate(sums_out, axis=0)
    g["conf_pw_w"] = slot_sum(r_pw, name="sum_pw")
    summed = _unpack(slot_sum(r_small[:, None], name="sum_small")[0], small_shapes)
    loss = summed[0].reshape(())
    for n, a in zip(SMALL_REPLICATED, summed[1:1 + len(SMALL_REPLICATED)]):
        g[n] = a
    g["final_norm_w"] = summed[-3]
    g["conv_qkv_w"] = lax.dynamic_slice_in_dim(summed[-2], me * conv_qkv_w.shape[2], conv_qkv_w.shape[2], axis=2)
    g["conf_dw_w"] = lax.dynamic_slice_in_dim(summed[-1], me * conf_dw_w.shape[2], conf_dw_w.shape[2], axis=2)

    deltas, new_m, new_v = {}, {}, {}
    for n in WEIGHTS:
        deltas[n], new_m[n], new_v[n] = adam(w[n], g[n], m[n], v[n], name=f"adam_{n}")
    return (loss, dx[None], *[g[n] for n in WEIGHTS], *[deltas[n] for n in WEIGHTS],
            *[new_m[n] for n in WEIGHTS], *[new_v[n] for n in WEIGHTS])
```

```python
import functools
import math

import jax
import jax.numpy as jnp
from jax import lax
from jax.experimental import pallas as pl
from jax.experimental.pallas import tpu as pltpu

D_MODEL = 2048
DEPTH = 4
N_DEV = 8
GDN_DK = 128
GDN_HEADS = 6
GDN_W = 768
ATT_HD = 64
ATT_HEADS = 12
ATT_W = 768
CONV_CH = 512
CONV_WIDTH = 31
SHORT_CONV = 4
GDN_CHUNK = 64
ROPE_THETA = 500000.0
ROPE_DIM = 16
DIL_PATTERNS = ((128, 1), (512, 4), (2048, 16))
ATT_BLOCK = 128
NEG_INF = -1e30
IN_W = 7692

ADAM_LR = 0.001
ADAM_B1 = 0.9
ADAM_B2 = 0.999
ADAM_EPS = 1e-08
ADAM_WD = 0.01
ADAM_STEP = 10

C_CA, C_CB, C_CG = 0, 512, 1024
C_GQ, C_GK, C_GV, C_GZ = 1536, 2304, 3072, 3840
C_AQ, C_AK, C_AV, C_AG = 4608, 5376, 6144, 6912
C_BETA, C_ALPHA = 7680, 7808
IN_PAD = 8192

VMEM_LIMIT = 56 * 1024 * 1024
CONF_HALO = 32
GDN_HALO = 8
GDN_GROUP = 4
GDN_HEADS_PER_STEP = 6

F32 = jnp.float32
BF16 = jnp.bfloat16
HI = lax.Precision.HIGHEST


def _cparams(sem, vmem=VMEM_LIMIT):
    return pltpu.CompilerParams(dimension_semantics=sem, vmem_limit_bytes=vmem)


def _dg(a, b, ca, cb, prec):
    nb = a.ndim - 2
    batch = tuple(range(nb))
    dn = (((ca + nb,), (cb + nb,)), (batch, batch))
    if prec == "bf16":
        return lax.dot_general(a.astype(BF16), b.astype(BF16), dn, preferred_element_type=F32)
    if prec == "bf16x3":
        ah, bh = a.astype(BF16), b.astype(BF16)
        al, bl = (a - ah.astype(F32)).astype(BF16), (b - bh.astype(F32)).astype(BF16)
        dot = lambda x, y: lax.dot_general(x, y, dn, preferred_element_type=F32)
        return dot(ah, bh) + (dot(ah, bl) + dot(al, bh))
    return lax.dot_general(a.astype(F32), b.astype(F32), dn, precision=HI, preferred_element_type=F32)


def _nn_raw(a, b, prec):
    return _dg(a, b, 1, 0, prec)


def _nt_raw(a, b, prec):
    return _dg(a, b, 1, 1, prec)


def _tn_raw(a, b, prec):
    return _dg(a, b, 0, 0, prec)


@functools.partial(jax.custom_vjp, nondiff_argnums=(2,))
def mm_nn(a, b, prec="bf16"):
    return _nn_raw(a, b, prec)


def _mm_nn_f(a, b, prec):
    return _nn_raw(a, b, prec), (a, b)


def _mm_nn_b(prec, res, g):
    a, b = res
    return _nt_raw(g, b, prec).astype(a.dtype), _tn_raw(a, g, prec).astype(b.dtype)


mm_nn.defvjp(_mm_nn_f, _mm_nn_b)


@functools.partial(jax.custom_vjp, nondiff_argnums=(2,))
def mm_nt(a, b, prec="bf16"):
    return _nt_raw(a, b, prec)


def _mm_nt_f(a, b, prec):
    return _nt_raw(a, b, prec), (a, b)


def _mm_nt_b(prec, res, g):
    a, b = res
    return _nn_raw(g, b, prec).astype(a.dtype), _tn_raw(g, a, prec).astype(b.dtype)


mm_nt.defvjp(_mm_nt_f, _mm_nt_b)


@functools.partial(jax.custom_vjp, nondiff_argnums=(2,))
def mm_tn(a, b, prec="bf16"):
    return _tn_raw(a, b, prec)


def _mm_tn_f(a, b, prec):
    return _tn_raw(a, b, prec), (a, b)


def _mm_tn_b(prec, res, g):
    a, b = res
    return _nt_raw(b, g, prec).astype(a.dtype), _nn_raw(a, g, prec).astype(b.dtype)


mm_tn.defvjp(_mm_tn_f, _mm_tn_b)


def _sigmoid(x):
    return 1.0 / (1.0 + jnp.exp(-x))


def _silu(x):
    return x * _sigmoid(x)


def _softplus(x):
    return jnp.maximum(x, 0.0) + jnp.log(1.0 + jnp.exp(-jnp.abs(x)))


def matmul(a, b, *, mode, tm, tn, tk, out_dtype=F32, residual=None, name):
    if mode == "tn":
        K, M = a.shape
    else:
        M, K = a.shape
    N = b.shape[0] if mode == "nt" else b.shape[1]
    assert M % tm == 0 and N % tn == 0 and K % tk == 0, (a.shape, b.shape, tm, tn, tk)
    nk = K // tk
    a_spec = pl.BlockSpec((tk, tm), lambda i, j, k: (k, i)) if mode == "tn" else pl.BlockSpec((tm, tk), lambda i, j, k: (i, k))
    b_spec = pl.BlockSpec((tn, tk), lambda i, j, k: (j, k)) if mode == "nt" else pl.BlockSpec((tk, tn), lambda i, j, k: (k, j))
    o_spec = pl.BlockSpec((tm, tn), lambda i, j, k: (i, j))
    raw = {"nn": _nn_raw, "nt": _nt_raw, "tn": _tn_raw}[mode]
    has_res = residual is not None

    def body(*refs):
        if has_res:
            a_ref, b_ref, r_ref, o_ref, acc_ref = refs
        else:
            a_ref, b_ref, o_ref, acc_ref = refs
        k = pl.program_id(2)
        part = raw(a_ref[...], b_ref[...], "bf16")

        @pl.when(k == 0)
        def _():
            acc_ref[...] = part

        @pl.when(k > 0)
        def _():
            acc_ref[...] += part

        @pl.when(k == nk - 1)
        def _():
            r = acc_ref[...]
            if has_res:
                r = r + r_ref[...].astype(F32)
            o_ref[...] = r.astype(out_dtype)

    in_specs = [a_spec, b_spec] + ([o_spec] if has_res else [])
    args = (a, b) + ((residual,) if has_res else ())
    return pl.pallas_call(
        body, name=name, grid=(M // tm, N // tn, nk), in_specs=in_specs, out_specs=o_spec,
        out_shape=jax.ShapeDtypeStruct((M, N), out_dtype),
        scratch_shapes=[pltpu.VMEM((tm, tn), F32)],
        compiler_params=_cparams(("parallel", "parallel", "arbitrary")),
    )(*args)


def _rms_fn(x, w, eps=1e-6):
    return x * lax.rsqrt(jnp.mean(x * x, axis=-1, keepdims=True) + eps) * w


def rms_fwd(x, w, *, tm, name):
    S, D = x.shape

    def body(x_ref, w_ref, o_ref):
        o_ref[...] = _rms_fn(x_ref[...], w_ref[...]).astype(BF16)

    return pl.pallas_call(
        body, name=name, grid=(S // tm,),
        in_specs=[pl.BlockSpec((tm, D), lambda i: (i, 0)), pl.BlockSpec((1, D), lambda i: (0, 0))],
        out_specs=pl.BlockSpec((tm, D), lambda i: (i, 0)),
        out_shape=jax.ShapeDtypeStruct((S, D), BF16),
        compiler_params=_cparams(("parallel",)),
    )(x, w)


def rms_bwd(x, w, dh, dres, *, tm, name):
    S, D = x.shape

    def body(x_ref, w_ref, dh_ref, dr_ref, dx_ref, dw_ref):
        _, vjp = jax.vjp(_rms_fn, x_ref[...], w_ref[...])
        dx, dw = vjp(dh_ref[...].astype(F32))
        dx_ref[...] = dx + dr_ref[...]

        @pl.when(pl.program_id(0) == 0)
        def _():
            dw_ref[...] = jnp.zeros_like(dw_ref)

        dw_ref[...] += dw

    row = pl.BlockSpec((tm, D), lambda i: (i, 0))
    vec = pl.BlockSpec((1, D), lambda i: (0, 0))
    return pl.pallas_call(
        body, name=name, grid=(S // tm,), in_specs=[row, vec, row, row], out_specs=[row, vec],
        out_shape=[jax.ShapeDtypeStruct((S, D), F32), jax.ShapeDtypeStruct((1, D), F32)],
        compiler_params=_cparams(("arbitrary",)),
    )(x, w, dh, dres)


def _fill_ext(ext_ref, halo, tile, first, H):
    ext_ref[pl.ds(0, H), :] = jnp.where(first, 0.0, halo)
    ext_ref[pl.ds(H, tile.shape[0]), :] = tile


def _conv_taps(ext_ref, w_ref, K, H, tm):
    assert H >= 8 * ((K - 1) // 8 + 1)
    total = None
    for b in range(min(8, K)):
        y = None
        for a in range((K - 1 - b) // 8 + 1):
            term = ext_ref[pl.ds(H - 8 - 8 * a, tm + 8), :] * w_ref[pl.ds(K - 1 - 8 * a - b, 1), :]
            y = term if y is None else y + term
        y = y if b == 0 else pltpu.roll(y, b, 0)
        total = y if total is None else total + y
    return total[8:, :]


def _halo_spec(H, tm, cw, col):
    return pl.BlockSpec((H, cw), lambda *g, _c=col: (jnp.maximum(g[-1] * (tm // H) - 1, 0), _c))


def conv_bwd(dc, srcs, w, du, du_col, *, K, H, tm, cw, glu, name):
    S, C = dc.shape
    nc, nt = C // cw, S // tm
    last_halo = S // H - 1
    n_src = 2 if glu else 1
    bases = [c0 // cw for _, c0 in srcs]

    def body(*refs):
        dc_ref, dcn_ref = refs[0], refs[1]
        src_refs = refs[2:2 + 2 * n_src]
        w_ref = refs[2 + 2 * n_src]
        out_ref, dw_ref, ext_ref, dext_ref = refs[4 + 2 * n_src:]
        i = pl.program_id(1)
        first, last = i == 0, i == nt - 1
        if glu:
            a_ref, ah_ref, b_ref, bh_ref = src_refs
            sg = _sigmoid(b_ref[...])
            _fill_ext(ext_ref, ah_ref[...] * _sigmoid(bh_ref[...]), a_ref[...] * sg, first, H)
        else:
            x_ref, xh_ref = src_refs
            _fill_ext(ext_ref, xh_ref[...], x_ref[...], first, H)
        dc_t = dc_ref[...]
        dext_ref[pl.ds(0, tm), :] = dc_t
        dext_ref[pl.ds(tm, H), :] = jnp.where(last, 0.0, dcn_ref[...])
        N = tm + 8
        dx = None
        for b in range(min(8, K)):
            z = None
            for a_ in range((K - 1 - b) // 8 + 1):
                term = dext_ref[pl.ds(8 * a_, N), :] * w_ref[pl.ds(K - 1 - 8 * a_ - b, 1), :]
                z = term if z is None else z + term
            z = z if b == 0 else pltpu.roll(z, N - b, 0)
            dx = z if dx is None else dx + z
        dx = dx[:tm, :]
        if glu:
            a = a_ref[...]
            out_ref[:, :cw] = (dx * sg).astype(BF16)
            out_ref[:, cw:] = (dx * a * sg * (1.0 - sg)).astype(BF16)
        else:
            out_ref[...] = dx.astype(BF16)

        @pl.when(first)
        def _():
            dw_ref[...] = jnp.zeros_like(dw_ref)

        dpad = jnp.concatenate([jnp.zeros((8, cw), F32), dc_t], axis=0)
        for b in range(min(8, K)):
            shifted = dpad if b == 0 else pltpu.roll(dpad, N - b, 0)
            for a_ in range((K - 1 - b) // 8 + 1):
                k = K - 1 - 8 * a_ - b
                dw_ref[pl.ds(k, 1), :] += jnp.sum(shifted * ext_ref[pl.ds(H - 8 - 8 * a_, N), :], axis=0, keepdims=True)

    tile = lambda base: pl.BlockSpec((tm, cw), lambda j, i, _b=base: (i, _b + j))
    halo = lambda base: pl.BlockSpec((H, cw), lambda j, i, _b=base: (jnp.maximum(i * (tm // H) - 1, 0), _b + j))
    in_specs = [tile(0), pl.BlockSpec((H, cw), lambda j, i: (jnp.minimum((i + 1) * (tm // H), last_halo), j))]
    args = [dc, dc]
    for (arr, _), base in zip(srcs, bases):
        in_specs += [tile(base), halo(base)]
        args += [arr, arr]
    in_specs += [pl.BlockSpec((K, cw), lambda j, i: (0, j)), ANY]
    args += [w, du]
    ow = n_src * cw
    out_specs = [pl.BlockSpec((tm, ow), lambda j, i: (i, du_col // ow + j)), pl.BlockSpec((K, cw), lambda j, i: (0, j))]
    out_shape = [jax.ShapeDtypeStruct(du.shape, du.dtype), jax.ShapeDtypeStruct((K, C), F32)]
    return pl.pallas_call(
        body, name=name, grid=(nc, nt), in_specs=in_specs, out_specs=out_specs, out_shape=out_shape,
        input_output_aliases={len(args) - 1: 0},
        scratch_shapes=[pltpu.VMEM((tm + H, cw), F32), pltpu.VMEM((tm + H, cw), F32)],
        compiler_params=_cparams(("parallel", "arbitrary")),
    )(*args)


def _conf_post(c, gate, ln_w, ln_b, pw):
    mu = jnp.mean(c, axis=-1, keepdims=True)
    cc = c - mu
    var = jnp.mean(cc * cc, axis=-1, keepdims=True)
    hn = cc * lax.rsqrt(var + 1e-5) * ln_w + ln_b
    return mm_nn(_silu(hn), pw) * _silu(gate)


def _conf_specs(tm):
    H = CONF_HALO
    blk = lambda col: pl.BlockSpec((tm, CONV_CH), lambda i, _c=col: (i, _c))
    vec = pl.BlockSpec((1, CONV_CH), lambda i: (0, 0))
    specs = [blk(0), blk(1), blk(2), _halo_spec(H, tm, CONV_CH, 0), _halo_spec(H, tm, CONV_CH, 1),
             pl.BlockSpec((CONV_WIDTH, CONV_CH), lambda i: (0, 0)), vec, vec, vec,
             pl.BlockSpec((CONV_CH, CONV_CH), lambda i: (0, 0))]
    return specs, blk, vec


def _conf_conv(a_ref, b_ref, ah_ref, bh_ref, dww_ref, dwb_ref, ext_ref, tm):
    first = pl.program_id(0) == 0
    _fill_ext(ext_ref, ah_ref[...] * _sigmoid(bh_ref[...]), a_ref[...] * _sigmoid(b_ref[...]), first, CONF_HALO)
    return _conv_taps(ext_ref, dww_ref, CONV_WIDTH, CONF_HALO, tm) + dwb_ref[...]


def conf_fwd(u, dw_w, dw_b, ln_w, ln_b, pw, *, tm, name):
    S = u.shape[0]
    specs, blk, vec = _conf_specs(tm)

    def body(a_ref, b_ref, g_ref, ah_ref, bh_ref, dww_ref, dwb_ref, lnw_ref, lnb_ref, pw_ref, y_ref, ext_ref):
        c = _conf_conv(a_ref, b_ref, ah_ref, bh_ref, dww_ref, dwb_ref, ext_ref, tm)
        y_ref[...] = _conf_post(c, g_ref[...], lnw_ref[...], lnb_ref[...], pw_ref[...]).astype(BF16)

    return pl.pallas_call(
        body, name=name, grid=(S // tm,), in_specs=specs, out_specs=blk(0),
        out_shape=jax.ShapeDtypeStruct((S, CONV_CH), BF16),
        scratch_shapes=[pltpu.VMEM((tm + CONF_HALO, CONV_CH), F32)],
        compiler_params=_cparams(("parallel",)),
    )(u, u, u, u, u, dw_w, dw_b, ln_w, ln_b, pw)


def conf_bwd_post(u, dy, dw_w, dw_b, ln_w, ln_b, pw, du, *, tm, name):
    S = u.shape[0]
    specs, blk, vec = _conf_specs(tm)
    mat = pl.BlockSpec((CONV_CH, CONV_CH), lambda i: (0, 0))

    def body(a_ref, b_ref, g_ref, ah_ref, bh_ref, dww_ref, dwb_ref, lnw_ref, lnb_ref, pw_ref, dy_ref, du_in,
             dc_ref, dg_ref, dlnw_ref, dlnb_ref, dpw_ref, ddwb_ref, ext_ref):
        c = _conf_conv(a_ref, b_ref, ah_ref, bh_ref, dww_ref, dwb_ref, ext_ref, tm)
        _, vjp = jax.vjp(_conf_post, c, g_ref[...], lnw_ref[...], lnb_ref[...], pw_ref[...])
        dc, dg, dlnw, dlnb, dpw = vjp(dy_ref[...])
        dc_ref[...] = dc
        dg_ref[...] = dg.astype(BF16)

        @pl.when(pl.program_id(0) == 0)
        def _():
            dlnw_ref[...] = jnp.zeros_like(dlnw_ref)
            dlnb_ref[...] = jnp.zeros_like(dlnb_ref)
            dpw_ref[...] = jnp.zeros_like(dpw_ref)

            ddwb_ref[...] = jnp.zeros_like(ddwb_ref)

        dlnw_ref[...] += dlnw
        dlnb_ref[...] += dlnb
        dpw_ref[...] += dpw
        ddwb_ref[...] += jnp.sum(dc, axis=0, keepdims=True)

    return pl.pallas_call(
        body, name=name, grid=(S // tm,), in_specs=specs + [blk(0), ANY],
        out_specs=[blk(0), blk(C_CG // CONV_CH), vec, vec, mat, vec],
        out_shape=[jax.ShapeDtypeStruct((S, CONV_CH), F32), jax.ShapeDtypeStruct(du.shape, du.dtype),
                   jax.ShapeDtypeStruct((1, CONV_CH), F32), jax.ShapeDtypeStruct((1, CONV_CH), F32),
                   jax.ShapeDtypeStruct((CONV_CH, CONV_CH), F32), jax.ShapeDtypeStruct((1, CONV_CH), F32)],
        input_output_aliases={11: 1},
        scratch_shapes=[pltpu.VMEM((tm + CONF_HALO, CONV_CH), F32)],
        compiler_params=_cparams(("arbitrary",)),
    )(u, u, u, u, u, dw_w, dw_b, ln_w, ln_b, pw, dy, du)


def _iota2(shape, dim):
    return lax.broadcasted_iota(jnp.int32, shape, dim)


def _gdn_post(pre_q, pre_k, pre_v, b_in, a_in, a_log, dt_bias):
    tm = pre_q.shape[0]
    q, k, v = _silu(pre_q), _silu(pre_k), _silu(pre_v)
    qs, ks = [], []
    for h in range(GDN_HEADS):
        sl = slice(h * GDN_DK, (h + 1) * GDN_DK)
        qh, kh = q[:, sl], k[:, sl]
        qs.append(qh * lax.rsqrt(jnp.sum(qh * qh, axis=-1, keepdims=True) + 1e-6) * (GDN_DK ** -0.5))
        ks.append(kh * lax.rsqrt(jnp.sum(kh * kh, axis=-1, keepdims=True) + 1e-6))
    beta = _sigmoid(b_in)
    g = -jnp.exp(a_log) * _softplus(a_in + dt_bias)
    nb = tm // GDN_CHUNK
    tril = (_iota2((nb, GDN_CHUNK, GDN_CHUNK), 1) >= _iota2((nb, GDN_CHUNK, GDN_CHUNK), 2)).astype(F32)
    gc = mm_nn(tril, g.reshape(nb, GDN_CHUNK, 128), "f32").reshape(tm, 128)
    return jnp.concatenate(qs, axis=1), jnp.concatenate(ks, axis=1), v, beta, gc


def _gdn_prep_specs(tm):
    H = GDN_HALO
    blk = lambda col: pl.BlockSpec((tm, GDN_W), lambda i, _c=col: (i, _c))
    lane = lambda col: pl.BlockSpec((tm, 128), lambda i, _c=col: (i, _c))
    vec = pl.BlockSpec((1, 128), lambda i: (0, 0))
    q0 = C_GQ // GDN_W
    specs = [blk(q0), blk(q0 + 1), blk(q0 + 2),
             _halo_spec(H, tm, GDN_W, q0), _halo_spec(H, tm, GDN_W, q0 + 1), _halo_spec(H, tm, GDN_W, q0 + 2),
             lane(C_BETA // 128), lane(C_ALPHA // 128),
             pl.BlockSpec((SHORT_CONV, GDN_W), lambda i: (0, 0)), pl.BlockSpec((SHORT_CONV, GDN_W), lambda i: (0, 1)),
             pl.BlockSpec((SHORT_CONV, GDN_W), lambda i: (0, 2)), vec, vec]
    return specs, blk, lane, vec


def _gdn_pre(x_refs, h_refs, w_refs, ext_ref, tm):
    first = pl.program_id(0) == 0
    pres = []
    for x_ref, h_ref, w_ref in zip(x_refs, h_refs, w_refs):
        _fill_ext(ext_ref, h_ref[...], x_ref[...], first, GDN_HALO)
        pres.append(_conv_taps(ext_ref, w_ref, SHORT_CONV, GDN_HALO, tm))
    return pres


def gdn_prep_fwd(u, conv_w, a_log, dt_bias, *, tm, name):
    S = u.shape[0]
    specs, blk, lane, vec = _gdn_prep_specs(tm)

    def body(xq, xk, xv, hq, hk, hv, bi, ai, wq, wk, wv, al, db, q_ref, k_ref, v_ref, beta_ref, gc_ref, ext_ref):
        pres = _gdn_pre((xq, xk, xv), (hq, hk, hv), (wq, wk, wv), ext_ref, tm)
        q, k, v, beta, gc = _gdn_post(*pres, bi[...], ai[...], al[...], db[...])
        q_ref[...] = q
        k_ref[...] = k
        v_ref[...] = v
        beta_ref[...] = beta
        gc_ref[...] = gc

    wide = jax.ShapeDtypeStruct((S, GDN_W), F32)
    narrow = jax.ShapeDtypeStruct((S, 128), F32)
    return pl.pallas_call(
        body, name=name, grid=(S // tm,), in_specs=specs,
        out_specs=[blk(0), blk(0), blk(0), lane(0), lane(0)], out_shape=[wide, wide, wide, narrow, narrow],
        scratch_shapes=[pltpu.VMEM((tm + GDN_HALO, GDN_W), F32)],
        compiler_params=_cparams(("parallel",)),
    )(u, u, u, u, u, u, u, u, conv_w, conv_w, conv_w, a_log, dt_bias)


def gdn_prep_bwd(u, conv_w, a_log, dt_bias, dq, dk, dv, dbeta, dgc, du, *, tm, name):
    S = u.shape[0]
    specs, blk, lane, vec = _gdn_prep_specs(tm)
    tail = IN_PAD - C_BETA

    def body(xq, xk, xv, hq, hk, hv, bi, ai, wq, wk, wv, al, db, dq_ref, dk_ref, dv_ref, dbe_ref, dgc_ref, du_in,
             dpre_ref, du_ref, dal_ref, ddb_ref, ext_ref):
        pres = _gdn_pre((xq, xk, xv), (hq, hk, hv), (wq, wk, wv), ext_ref, tm)
        _, vjp = jax.vjp(_gdn_post, *pres, bi[...], ai[...], al[...], db[...])
        dpq, dpk, dpv, dbi, dai, dal, ddb = vjp((dq_ref[...], dk_ref[...], dv_ref[...], dbe_ref[...], dgc_ref[...]))
        dpre_ref[:, 0:GDN_W] = dpq
        dpre_ref[:, GDN_W:2 * GDN_W] = dpk
        dpre_ref[:, 2 * GDN_W:] = dpv
        du_ref[:, 0:128] = dbi.astype(BF16)
        du_ref[:, 128:256] = dai.astype(BF16)
        du_ref[:, 256:] = jnp.zeros((tm, tail - 256), BF16)

        @pl.when(pl.program_id(0) == 0)
        def _():
            dal_ref[...] = jnp.zeros_like(dal_ref)
            ddb_ref[...] = jnp.zeros_like(ddb_ref)

        dal_ref[...] += dal
        ddb_ref[...] += ddb

    n_in = len(specs) + 6
    return pl.pallas_call(
        body, name=name, grid=(S // tm,), in_specs=specs + [blk(0), blk(0), blk(0), lane(0), lane(0), ANY],
        out_specs=[pl.BlockSpec((tm, 3 * GDN_W), lambda i: (i, 0)), pl.BlockSpec((tm, tail), lambda i: (i, C_BETA // tail)),
                   vec, vec],
        out_shape=[jax.ShapeDtypeStruct((S, 3 * GDN_W), F32), jax.ShapeDtypeStruct(du.shape, du.dtype),
                   jax.ShapeDtypeStruct((1, 128), F32), jax.ShapeDtypeStruct((1, 128), F32)],
        input_output_aliases={n_in - 1: 1},
        scratch_shapes=[pltpu.VMEM((tm + GDN_HALO, GDN_W), F32)],
        compiler_params=_cparams(("arbitrary",)),
    )(u, u, u, u, u, u, u, u, conv_w, conv_w, conv_w, a_log, dt_bias, dq, dk, dv, dbeta, dgc, du)


def _lane_col(blk, h):
    return jnp.sum(jnp.where(_iota2(blk.shape, 1) == h, blk, 0.0), axis=1, keepdims=True)


@jax.custom_vjp
def _tri_inv(low):
    n = low.shape[-1]
    r, c = _iota2(low.shape, low.ndim - 2), _iota2(low.shape, low.ndim - 1)
    eye = (r == c).astype(F32)
    t = eye - jnp.where((r // 2 == c // 2) & (r > c), low, 0.0)
    s = 2
    while s < n:
        off = jnp.where((r // (2 * s) == c // (2 * s)) & (r // s > c // s), low, 0.0)
        prec = "bf16" if s <= 8 else "bf16x3"
        t = t - _nn_raw(t, _nn_raw(off, t, prec), prec)
        s *= 2
    return t


def _tri_inv_f(low):
    t = _tri_inv(low)
    return t, t


def _tri_inv_b(t, dt):
    d = -_nt_raw(_tn_raw(t, dt, "bf16x3"), t, "bf16x3")
    r, c = _iota2(d.shape, d.ndim - 2), _iota2(d.shape, d.ndim - 1)
    return (jnp.where(r > c, d, 0.0),)


_tri_inv.defvjp(_tri_inv_f, _tri_inv_b)


def _gdn_group(s0, q, k, v, z, beta_blk, gc_blk, nw, h0):
    C = GDN_CHUNK
    HP, R, _ = q.shape
    nb = R // C
    B = HP * nb
    q3, k3, v3 = (t.reshape(B, C, GDN_DK) for t in (q, k, v))
    b3 = jnp.stack([_lane_col(beta_blk, h0 + j) for j in range(HP)]).reshape(B, C, 1)
    g3 = jnp.stack([_lane_col(gc_blk, h0 + j) for j in range(HP)]).reshape(B, C, 1)
    r, c = _iota2((B, C, C), 1), _iota2((B, C, C), 2)
    causal, strict = r >= c, r > c
    g_t = gc_blk.T
    rows = [jnp.sum(jnp.where(_iota2((128, R), 0) == h0 + j, g_t, 0.0), axis=0, keepdims=True) for j in range(HP)]
    g_row = jnp.stack([rows[j][:, i * C:(i + 1) * C] for j in range(HP) for i in range(nb)])
    decay = jnp.where(causal, jnp.exp(jnp.where(causal, g3 - g_row, 0.0)), 0.0)
    low = jnp.where(strict, b3 * mm_nt(k3, k3) * decay, 0.0)
    t = _tri_inv(low)
    eg = jnp.exp(g3)
    four = lambda x: x.reshape((HP, nb) + x.shape[1:])
    w_v = four(mm_nn(t, v3 * b3))
    w_k = four(mm_nn(t, k3 * (b3 * eg)))
    qk = four(jnp.where(causal, mm_nt(q3, k3) * decay, 0.0))
    q_dec = four(q3 * eg)
    g_last = jnp.sum(jnp.where(_iota2((B, C, 1), 1) == C - 1, g3, 0.0), axis=1, keepdims=True)
    k_dec = four(k3 * jnp.exp(g_last - g3))
    e_last = four(jnp.exp(g_last))
    s, outs = s0, []
    for i in range(nb):
        v_new = w_v[:, i] - mm_nn(w_k[:, i], s)
        outs.append(mm_nn(q_dec[:, i], s) + mm_nn(qk[:, i], v_new))
        s = s * e_last[:, i] + mm_tn(k_dec[:, i], v_new)
    o = jnp.concatenate(outs, axis=1)
    y = o * lax.rsqrt(jnp.mean(o * o, axis=-1, keepdims=True) + 1e-6) * nw * _silu(z)
    return s, y


def _heads(ref, HP):
    return jnp.stack([ref[:, j * GDN_DK:(j + 1) * GDN_DK] for j in range(HP)])


def gdn_core_fwd(q, k, v, u, beta, gc, nw, *, name):
    S = q.shape[0]
    R = GDN_CHUNK * GDN_GROUP
    G = S // R
    HP = GDN_HEADS_PER_STEP
    W = HP * GDN_DK
    blk = pl.BlockSpec((R, W), lambda g, h: (g, h))
    lane = pl.BlockSpec((R, 128), lambda g, h: (g, 0))
    st = pl.BlockSpec((1, HP, GDN_DK, GDN_DK), lambda g, h: (g, h, 0, 0))

    def body(q_ref, k_ref, v_ref, z_ref, be_ref, gc_ref, nw_ref, y_ref, st_ref, s_ref):
        g, hs = pl.program_id(0), pl.program_id(1)
        s0 = jnp.where(g == 0, 0.0, s_ref[hs])
        st_ref[0] = s0
        s1, y = _gdn_group(s0, _heads(q_ref, HP), _heads(k_ref, HP), _heads(v_ref, HP), _heads(z_ref, HP), be_ref[...],
                           gc_ref[...], nw_ref[...], hs * HP)
        s_ref[hs] = s1
        for j in range(HP):
            y_ref[:, j * GDN_DK:(j + 1) * GDN_DK] = y[j].astype(BF16)

    return pl.pallas_call(
        body, name=name, grid=(G, GDN_HEADS // HP),
        in_specs=[blk, blk, blk, pl.BlockSpec((R, W), lambda g, h: (g, C_GZ // W + h)), lane, lane,
                  pl.BlockSpec((1, 128), lambda g, h: (0, 0))],
        out_specs=[blk, st],
        out_shape=[jax.ShapeDtypeStruct((S, GDN_W), BF16), jax.ShapeDtypeStruct((G, GDN_HEADS, GDN_DK, GDN_DK), F32)],
        scratch_shapes=[pltpu.VMEM((GDN_HEADS // HP, HP, GDN_DK, GDN_DK), F32)],
        compiler_params=_cparams(("arbitrary", "arbitrary")),
    )(q, k, v, u, beta, gc, nw)


def gdn_core_bwd(q, k, v, u, beta, gc, nw, states, dy, du, *, name):
    S = q.shape[0]
    R = GDN_CHUNK * GDN_GROUP
    G = S // R
    HP = GDN_HEADS_PER_STEP
    W = HP * GDN_DK
    blk = pl.BlockSpec((R, W), lambda g, h: (G - 1 - g, h))
    lane = pl.BlockSpec((R, 128), lambda g, h: (G - 1 - g, 0))
    vec = pl.BlockSpec((1, 128), lambda g, h: (0, 0))

    def body(q_ref, k_ref, v_ref, z_ref, be_ref, gc_ref, nw_ref, st_ref, *rest):
        dy_refs = rest[:HP]
        dq_ref, dk_ref, dv_ref, dz_ref, dbe_ref, dgc_ref, dnw_ref, ds_ref = rest[HP + 1:]
        g, hs = pl.program_id(0), pl.program_id(1)

        @pl.when(hs == 0)
        def _():
            dbe_ref[...] = jnp.zeros_like(dbe_ref)
            dgc_ref[...] = jnp.zeros_like(dgc_ref)

        @pl.when((hs == 0) & (g == 0))
        def _():
            dnw_ref[...] = jnp.zeros_like(dnw_ref)

        _, vjp = jax.vjp(functools.partial(_gdn_group, h0=hs * HP), st_ref[0], _heads(q_ref, HP), _heads(k_ref, HP),
                         _heads(v_ref, HP), _heads(z_ref, HP), be_ref[...], gc_ref[...], nw_ref[...])
        ds_in = jnp.where(g == 0, 0.0, ds_ref[hs])
        dy = jnp.stack([r[...] for r in dy_refs])
        ds0, dq, dk, dv, dz, dbe, dgc, dnw = vjp((ds_in, dy))
        ds_ref[hs] = ds0
        for j in range(HP):
            sl = slice(j * GDN_DK, (j + 1) * GDN_DK)
            dq_ref[:, sl] = dq[j]
            dk_ref[:, sl] = dk[j]
            dv_ref[:, sl] = dv[j]
            dz_ref[:, sl] = dz[j].astype(BF16)
        dbe_ref[...] += dbe
        dgc_ref[...] += dgc
        dnw_ref[...] += dnw

    wide = jax.ShapeDtypeStruct((S, GDN_W), F32)
    narrow = jax.ShapeDtypeStruct((S, 128), F32)
    return pl.pallas_call(
        body, name=name, grid=(G, GDN_HEADS // HP),
        in_specs=[blk, blk, blk, pl.BlockSpec((R, W), lambda g, h: (G - 1 - g, C_GZ // W + h)), lane, lane, vec,
                  pl.BlockSpec((1, HP, GDN_DK, GDN_DK), lambda g, h: (G - 1 - g, h, 0, 0))]
        + [pl.BlockSpec((R, GDN_DK), lambda g, h, _j=j: (G - 1 - g, CONV_CH // GDN_DK + h * HP + _j)) for j in range(HP)]
        + [ANY],
        out_specs=[blk, blk, blk, pl.BlockSpec((R, W), lambda g, h: (G - 1 - g, C_GZ // W + h)), lane, lane, vec],
        out_shape=[wide, wide, wide, jax.ShapeDtypeStruct(du.shape, du.dtype), narrow, narrow,
                   jax.ShapeDtypeStruct((1, 128), F32)],
        input_output_aliases={8 + HP: 3},
        scratch_shapes=[pltpu.VMEM((GDN_HEADS // HP, HP, GDN_DK, GDN_DK), F32)],
        compiler_params=_cparams(("arbitrary", "arbitrary")),
    )(q, k, v, u, beta, gc, nw, states, *([dy] * HP), du)


def rope_tables(S):
    half = ROPE_DIM // 2
    inv = ROPE_THETA ** (-jnp.arange(half, dtype=F32) / half)
    ang = jnp.arange(S, dtype=F32)[:, None] * inv[None, :]
    cos, sin = jnp.cos(ang), jnp.sin(ang)
    rest = ATT_HD - ROPE_DIM
    c = jnp.concatenate([cos, cos, jnp.ones((S, rest), F32)], axis=1)
    s1 = jnp.concatenate([-sin, jnp.zeros((S, ATT_HD - half), F32)], axis=1)
    s2 = jnp.concatenate([jnp.zeros((S, half), F32), sin, jnp.zeros((S, rest), F32)], axis=1)
    return tuple(jnp.tile(t, (1, 2)) for t in (c, s1, s2))


def _rope(x, c, s1, s2):
    half = ROPE_DIM // 2
    return x * c + pltpu.roll(x, ATT_W - half, 1) * s1 + pltpu.roll(x, half, 1) * s2


def _unrope(dy, c, s1, s2):
    half = ROPE_DIM // 2
    return dy * c + pltpu.roll(dy * s1, half, 1) + pltpu.roll(dy * s2, ATT_W - half, 1)


def att_prep_fwd(u, tables, *, tm, name):
    S = u.shape[0]
    blk = lambda col: pl.BlockSpec((tm, ATT_W), lambda i, _c=col: (i, _c))
    tab = pl.BlockSpec((tm, 128), lambda i: (i, 0))

    def body(q_ref, k_ref, v_ref, c_ref, s1_ref, s2_ref, qo_ref, ko_ref, vo_ref):
        reps = ATT_W // 128
        c, s1, s2 = (jnp.tile(t[...], (1, reps)) for t in (c_ref, s1_ref, s2_ref))
        qo_ref[...] = (_rope(q_ref[...], c, s1, s2) * (ATT_HD ** -0.5)).astype(BF16)
        ko_ref[...] = _rope(k_ref[...], c, s1, s2).astype(BF16)
        vo_ref[...] = v_ref[...].astype(BF16)

    out = jax.ShapeDtypeStruct((S, ATT_W), BF16)
    return pl.pallas_call(
        body, name=name, grid=(S // tm,),
        in_specs=[blk(C_AQ // ATT_W), blk(C_AK // ATT_W), blk(C_AV // ATT_W), tab, tab, tab],
        out_specs=[blk(0)] * 3, out_shape=[out] * 3, compiler_params=_cparams(("parallel",)),
    )(u, u, u, *tables)


def att_prep_bwd(dqs, dks, dvs, tables, du, *, tm, name):
    S = dqs[0].shape[0]
    blk = pl.BlockSpec((tm, ATT_W), lambda i: (i, 0))
    tab = pl.BlockSpec((tm, 128), lambda i: (i, 0))

    def body(*refs):
        dq, dk, dv = (refs[3 * j][...].astype(F32) + refs[3 * j + 1][...].astype(F32) + refs[3 * j + 2][...].astype(F32)
                      for j in range(3))
        c_ref, s1_ref, s2_ref, _, o_ref = refs[9:]
        reps = ATT_W // 128
        c, s1, s2 = (jnp.tile(t[...], (1, reps)) for t in (c_ref, s1_ref, s2_ref))
        o_ref[:, 0:ATT_W] = (_unrope(dq, c, s1, s2) * (ATT_HD ** -0.5)).astype(BF16)
        o_ref[:, ATT_W:2 * ATT_W] = _unrope(dk, c, s1, s2).astype(BF16)
        o_ref[:, 2 * ATT_W:] = dv.astype(BF16)

    return pl.pallas_call(
        body, name=name, grid=(S // tm,), in_specs=[blk] * 9 + [tab] * 3 + [ANY],
        out_specs=pl.BlockSpec((tm, 3 * ATT_W), lambda i: (i, C_AQ // (3 * ATT_W))),
        out_shape=jax.ShapeDtypeStruct(du.shape, du.dtype), input_output_aliases={12: 0},
        compiler_params=_cparams(("parallel",)),
    )(*dqs, *dks, *dvs, *tables, du)


def _band_masks():
    qi, ki = _iota2((ATT_BLOCK, ATT_BLOCK), 0), _iota2((ATT_BLOCK, ATT_BLOCK), 1)
    return qi <= ki, ki <= qi


def _pair_diag(x):
    first = _iota2(x.shape, 1) < ATT_HD
    zero = jnp.zeros_like(x)
    return jnp.concatenate([jnp.where(first, x, zero), jnp.where(first, zero, x)], axis=0)


def att_pattern_fwd(qr, kr, vb, dil, *, name):
    S = qr.shape[0]
    L = S // dil
    nb = L // ATT_BLOCK
    view = lambda t: t.reshape(L, dil * t.shape[1])
    cur = pl.BlockSpec((ATT_BLOCK, ATT_W), lambda r, n: (n, r))
    prev = pl.BlockSpec((ATT_BLOCK, ATT_W), lambda r, n: (jnp.maximum(n - 1, 0), r))

    def body(q_ref, kc_ref, kp_ref, vc_ref, vp_ref, o_ref, l_ref):
        has_prev = pl.program_id(1) > 0
        m_prev, m_cur = _band_masks()
        m_prev = m_prev & has_prev
        first = _iota2((ATT_BLOCK, 128), 1) < ATT_HD
        lane = _iota2((ATT_BLOCK, 128), 1)
        stats = jnp.zeros((ATT_BLOCK, 128), F32)
        pairs = range(ATT_HEADS // 2)
        sls = [slice(p * 128, (p + 1) * 128) for p in pairs]
        sps = [_nt_raw(q_ref[:, sl], _pair_diag(kp_ref[:, sl]), "bf16") for sl in sls]
        scs = [_nt_raw(q_ref[:, sl], _pair_diag(kc_ref[:, sl]), "bf16") for sl in sls]
        probs, inv_dens = [], []
        for p in pairs:
            pps, pcs, dens, lses = [], [], [], []
            for half in range(2):
                hs = slice(half * 128, (half + 1) * 128)
                sp_h, sc_h = jnp.where(m_prev, sps[p][:, hs], NEG_INF), jnp.where(m_cur, scs[p][:, hs], NEG_INF)
                m = jnp.maximum(jnp.max(sp_h, axis=1, keepdims=True), jnp.max(sc_h, axis=1, keepdims=True))
                pp, pc = jnp.exp(sp_h - m), jnp.exp(sc_h - m)
                den = jnp.sum(pp, axis=1, keepdims=True) + jnp.sum(pc, axis=1, keepdims=True)
                pps.append(pp.astype(BF16))
                pcs.append(pc.astype(BF16))
                dens.append(den)
                lses.append(m + jnp.log(den))
            probs.append((jnp.concatenate(pps, axis=1), jnp.concatenate(pcs, axis=1)))
            inv_dens.append(1.0 / jnp.where(first, dens[0], dens[1]))
            stats = jnp.where(lane == 2 * p, lses[0], jnp.where(lane == 2 * p + 1, lses[1], stats))
        outs = [_nn_raw(probs[p][0], _pair_diag(vp_ref[:, sls[p]]), "bf16")
                + _nn_raw(probs[p][1], _pair_diag(vc_ref[:, sls[p]]), "bf16") for p in pairs]
        for p in pairs:
            o_ref[:, sls[p]] = (outs[p] * inv_dens[p]).astype(BF16)
        l_ref[...] = stats

    narrow = pl.BlockSpec((ATT_BLOCK, 128), lambda r, n: (n, r))
    o, l = pl.pallas_call(
        body, name=name, grid=(dil, nb), in_specs=[cur, cur, prev, cur, prev], out_specs=[cur, narrow],
        out_shape=[jax.ShapeDtypeStruct((L, dil * ATT_W), BF16), jax.ShapeDtypeStruct((L, dil * 128), F32)],
        compiler_params=_cparams(("parallel", "arbitrary")),
    )(view(qr), view(kr), view(kr), view(vb), view(vb))
    return o.reshape(S, ATT_W), l.reshape(S, 128)


def _head_spread():
    return (_iota2((128, ATT_W), 1) // ATT_HD == _iota2((128, ATT_W), 0)).astype(F32)


def att_combine_fwd(os_, ls, u, *, tm, name):
    S = u.shape[0]
    blk = lambda col: pl.BlockSpec((tm, ATT_W), lambda i, _c=col: (i, _c))
    lane = pl.BlockSpec((tm, 128), lambda i: (i, 0))

    def body(o1, o2, o3, l1, l2, l3, g_ref, y_ref, o_ref, lse_ref):
        a, b, c = l1[...], l2[...], l3[...]
        m = jnp.maximum(jnp.maximum(a, b), c)
        ea, eb, ec = jnp.exp(a - m), jnp.exp(b - m), jnp.exp(c - m)
        den = ea + eb + ec
        spread = _head_spread()
        wa, wb, wc = (_nn_raw(e / den, spread, "bf16x3") for e in (ea, eb, ec))
        o = wa * o1[...].astype(F32) + wb * o2[...].astype(F32) + wc * o3[...].astype(F32)
        o_ref[...] = o
        lse_ref[...] = m + jnp.log(den)
        y_ref[...] = (o * _silu(g_ref[...])).astype(BF16)

    return pl.pallas_call(
        body, name=name, grid=(S // tm,), in_specs=[blk(0)] * 3 + [lane] * 3 + [blk(C_AG // ATT_W)],
        out_specs=[blk(0), blk(0), lane],
        out_shape=[jax.ShapeDtypeStruct((S, ATT_W), BF16), jax.ShapeDtypeStruct((S, ATT_W), F32),
                   jax.ShapeDtypeStruct((S, 128), F32)],
        compiler_params=_cparams(("parallel",)),
    )(*os_, *ls, u)


def att_combine_bwd(dy, o, u, du, *, tm, name):
    S = u.shape[0]
    cw = 256
    base = (CONV_CH + GDN_W) // cw
    blk = lambda col: pl.BlockSpec((tm, ATT_W), lambda i, _c=col: (i, _c))

    def body(dy0, dy1, dy2, o_ref, g_ref, du_in, do_ref, dg_ref, dl_ref):
        g, d, o = g_ref[...], jnp.concatenate([dy0[...], dy1[...], dy2[...]], axis=1), o_ref[...]
        sg = _sigmoid(g)
        d_o = d * (g * sg)
        do_ref[...] = d_o.astype(BF16)
        dg_ref[...] = (d * o * (sg * (1.0 + g * (1.0 - sg)))).astype(BF16)
        dl_ref[...] = _nt_raw(d_o * o, _head_spread(), "bf16x3")

    return pl.pallas_call(
        body, name=name, grid=(S // tm,),
        in_specs=[pl.BlockSpec((tm, cw), lambda i, _j=j: (i, base + _j)) for j in range(ATT_W // cw)]
        + [blk(0), blk(C_AG // ATT_W), ANY],
        out_specs=[blk(0), blk(C_AG // ATT_W), pl.BlockSpec((tm, 128), lambda i: (i, 0))],
        out_shape=[jax.ShapeDtypeStruct((S, ATT_W), BF16), jax.ShapeDtypeStruct(du.shape, du.dtype),
                   jax.ShapeDtypeStruct((S, 128), F32)],
        input_output_aliases={5: 1},
        compiler_params=_cparams(("parallel",)),
    )(dy, dy, dy, o, u, du)


def att_pattern_bwd(qr, kr, vb, do, delta, lse, dil, *, name):
    S = qr.shape[0]
    L = S // dil
    nb = L // ATT_BLOCK
    view = lambda t: t.reshape(L, dil * t.shape[1])
    cur = pl.BlockSpec((ATT_BLOCK, ATT_W), lambda r, n: (jnp.minimum(n, nb - 1), r))
    prev = pl.BlockSpec((ATT_BLOCK, ATT_W), lambda r, n: (jnp.maximum(n - 1, 0), r))
    narrow = pl.BlockSpec((ATT_BLOCK, 128), lambda r, n: (jnp.minimum(n, nb - 1), r))

    def body(q_ref, kc_ref, kp_ref, vc_ref, vp_ref, do_ref, dl_ref, l_ref, dq_ref, dk_ref, dv_ref, ck_ref, cv_ref):
        n = pl.program_id(1)

        @pl.when(n < nb)
        def _():
            m_prev, m_cur = _band_masks()
            m_prev = m_prev & (n > 0)
            m_prev2, m_cur2 = jnp.concatenate([m_prev, m_prev], axis=1), jnp.concatenate([m_cur, m_cur], axis=1)
            first = _iota2((ATT_BLOCK, 128), 1) < ATT_HD
            wide = (ATT_BLOCK, 128)
            halves = lambda a, b: jnp.concatenate([jnp.broadcast_to(a, wide), jnp.broadcast_to(b, wide)], axis=1)
            fold = lambda t: jnp.where(first, t[:ATT_BLOCK], t[ATT_BLOCK:])
            pairs = range(ATT_HEADS // 2)
            sls = [slice(p * 128, (p + 1) * 128) for p in pairs]
            qs, dos = [q_ref[:, sl] for sl in sls], [do_ref[:, sl] for sl in sls]
            kps, kcs, vps, vcs = ([_pair_diag(r[:, sl]) for sl in sls] for r in (kp_ref, kc_ref, vp_ref, vc_ref))
            s_p = [_nt_raw(qs[p], kps[p], "bf16") for p in pairs]
            s_c = [_nt_raw(qs[p], kcs[p], "bf16") for p in pairs]
            dp_p = [_nt_raw(dos[p], vps[p], "bf16") for p in pairs]
            dp_c = [_nt_raw(dos[p], vcs[p], "bf16") for p in pairs]
            pps, pcs, dsps, dscs = [], [], [], []
            for p in pairs:
                delta = halves(dl_ref[:, 2 * p:2 * p + 1], dl_ref[:, 2 * p + 1:2 * p + 2])
                lse2 = halves(l_ref[:, 2 * p:2 * p + 1], l_ref[:, 2 * p + 1:2 * p + 2])
                pp = jnp.where(m_prev2, jnp.exp(s_p[p] - lse2), 0.0)
                pc = jnp.where(m_cur2, jnp.exp(s_c[p] - lse2), 0.0)
                dsps.append((pp * (dp_p[p] - delta)).astype(BF16))
                dscs.append((pc * (dp_c[p] - delta)).astype(BF16))
                pps.append(pp.astype(BF16))
                pcs.append(pc.astype(BF16))
            dqs = [_nn_raw(dsps[p], kps[p], "bf16") + _nn_raw(dscs[p], kcs[p], "bf16") for p in pairs]
            dk_prev = [fold(_tn_raw(dsps[p], qs[p], "bf16")) for p in pairs]
            dv_prev = [fold(_tn_raw(pps[p], dos[p], "bf16")) for p in pairs]
            dk_cur = [fold(_tn_raw(dscs[p], qs[p], "bf16")) for p in pairs]
            dv_cur = [fold(_tn_raw(pcs[p], dos[p], "bf16")) for p in pairs]
            for p in pairs:
                dq_ref[:, sls[p]] = dqs[p].astype(BF16)

            @pl.when(n > 0)
            def _():
                for p in pairs:
                    dk_ref[:, sls[p]] = (ck_ref[:, sls[p]] + dk_prev[p]).astype(BF16)
                    dv_ref[:, sls[p]] = (cv_ref[:, sls[p]] + dv_prev[p]).astype(BF16)

            for p in pairs:
                ck_ref[:, sls[p]] = dk_cur[p]
                cv_ref[:, sls[p]] = dv_cur[p]

        @pl.when(n == nb)
        def _():
            dk_ref[...] = ck_ref[...].astype(BF16)
            dv_ref[...] = cv_ref[...].astype(BF16)

    out = jax.ShapeDtypeStruct((L, dil * ATT_W), BF16)
    dq, dk, dv = pl.pallas_call(
        body, name=name, grid=(dil, nb + 1), in_specs=[cur, cur, prev, cur, prev, cur, narrow, narrow],
        out_specs=[cur, prev, prev], out_shape=[out, out, out],
        scratch_shapes=[pltpu.VMEM((ATT_BLOCK, ATT_W), F32), pltpu.VMEM((ATT_BLOCK, ATT_W), F32)],
        compiler_params=_cparams(("arbitrary", "arbitrary")),
    )(view(qr), view(kr), view(kr), view(vb), view(vb), view(do), view(delta), view(lse))
    return dq.reshape(S, ATT_W), dk.reshape(S, ATT_W), dv.reshape(S, ATT_W)


def _loss_rows(x, w, tgt):
    err = _rms_fn(x, w) - tgt
    return jnp.sum(0.5 * jnp.mean(err * err, axis=-1, keepdims=True), axis=0, keepdims=True)


def loss_head(x, w, tgt, *, tm, name):
    S, D = x.shape

    def body(x_ref, w_ref, t_ref, l_ref, dx_ref, dw_ref):
        val, vjp = jax.vjp(_loss_rows, x_ref[...], w_ref[...], t_ref[...])
        dx, dw, _ = vjp(jnp.ones((1, 1), F32))
        dx_ref[...] = dx

        @pl.when(pl.program_id(0) == 0)
        def _():
            l_ref[...] = jnp.zeros_like(l_ref)
            dw_ref[...] = jnp.zeros_like(dw_ref)

        l_ref[...] += val
        dw_ref[...] += dw

    row = pl.BlockSpec((tm, D), lambda i: (i, 0))
    vec = pl.BlockSpec((1, D), lambda i: (0, 0))
    one = pl.BlockSpec((1, 1), lambda i: (0, 0))
    return pl.pallas_call(
        body, name=name, grid=(S // tm,), in_specs=[row, vec, row], out_specs=[one, row, vec],
        out_shape=[jax.ShapeDtypeStruct((1, 1), F32), jax.ShapeDtypeStruct((S, D), F32), jax.ShapeDtypeStruct((1, D), F32)],
        compiler_params=_cparams(("arbitrary",)),
    )(x, w, tgt)


def adam(w, g, m, v, *, name):
    shape = w.shape
    C = shape[-1]
    R = w.size // C
    br = R
    while br * C * 4 > (1 << 21) and br % 16 == 0:
        br //= 2
    two = lambda t: t.reshape(R, C)

    def body(w_ref, g_ref, m_ref, v_ref, d_ref, mo_ref, vo_ref):
        gg = g_ref[...]
        m_new = ADAM_B1 * m_ref[...] + (1.0 - ADAM_B1) * gg
        v_new = ADAM_B2 * v_ref[...] + (1.0 - ADAM_B2) * jnp.square(gg)
        m_hat = m_new / (1.0 - ADAM_B1 ** ADAM_STEP)
        v_hat = v_new / (1.0 - ADAM_B2 ** ADAM_STEP)
        d_ref[...] = -ADAM_LR * (m_hat / (jnp.sqrt(v_hat) + ADAM_EPS) + ADAM_WD * w_ref[...])
        mo_ref[...] = m_new
        vo_ref[...] = v_new

    blk = pl.BlockSpec((br, C), lambda i: (i, 0))
    out = jax.ShapeDtypeStruct((R, C), F32)
    d, mo, vo = pl.pallas_call(
        body, name=name, grid=(R // br,), in_specs=[blk] * 4, out_specs=[blk] * 3, out_shape=[out] * 3,
        compiler_params=_cparams(("parallel",)),
    )(two(w), two(g), two(m), two(v))
    return d.reshape(shape), mo.reshape(shape), vo.reshape(shape)


MESH_IDS = pl.DeviceIdType.MESH
ANY = pl.BlockSpec(memory_space=pl.ANY)


def _my_id():
    return 4 * lax.axis_index("x") + 2 * lax.axis_index("y") + lax.axis_index("c")


def _peer(k):
    x, y, c = lax.axis_index("x"), lax.axis_index("y"), lax.axis_index("c")
    flip = lambda v, bit: 1 - v if bit else v
    return (flip(x, k & 4), flip(y, k & 2), flip(c, k & 1))


def all_gather(arrs, *, name):
    n = len(arrs)

    def body(*refs):
        ins, outs = refs[:n], refs[n:2 * n]
        send, recv, local = refs[2 * n:]
        me = _my_id()
        started = []
        for a in range(n):
            lc = pltpu.make_async_copy(ins[a], outs[a].at[me], local.at[a])
            lc.start()
            started.append(lc)
            for k in range(1, N_DEV):
                cp = pltpu.make_async_remote_copy(src_ref=ins[a], dst_ref=outs[a].at[me], send_sem=send.at[a, k - 1],
                                                  recv_sem=recv.at[a, k - 1], device_id=_peer(k), device_id_type=MESH_IDS)
                cp.start()
                started.append(cp)
        for cp in started:
            cp.wait()

    return pl.pallas_call(
        body, name=name, in_specs=[ANY] * n, out_specs=[ANY] * n,
        out_shape=[jax.ShapeDtypeStruct((N_DEV,) + a.shape, a.dtype) for a in arrs],
        scratch_shapes=[pltpu.SemaphoreType.DMA((n, N_DEV - 1)), pltpu.SemaphoreType.DMA((n, N_DEV - 1)),
                        pltpu.SemaphoreType.DMA((n,))],
        compiler_params=pltpu.CompilerParams(has_side_effects=True),
    )(*arrs)


def scatter_exchange(groups, pack, *, name):
    flat = [a for grp in groups for a in grp]
    n = len(flat) + 1
    shapes = [jax.ShapeDtypeStruct((N_DEV, len(grp), grp[0].shape[0] // N_DEV, grp[0].shape[1]), grp[0].dtype) for grp in groups]
    shapes.append(jax.ShapeDtypeStruct((N_DEV,) + pack.shape, pack.dtype))
    index = [(gi, li) for gi, grp in enumerate(groups) for li in range(len(grp))]

    def body(*refs):
        ins, outs = refs[:n], refs[n:n + len(shapes)]
        send, recv, local = refs[n + len(shapes):]
        me = _my_id()
        started = []
        for a in range(n):
            if a < n - 1:
                gi, li = index[a]
                r = ins[a].shape[0] // N_DEV
                src = lambda j, _a=a, _r=r: ins[_a].at[pl.ds(pl.multiple_of(j * _r, 8), _r), :]
                dst = outs[gi].at[me, li]
            else:
                src = lambda j, _a=a: ins[_a]
                dst = outs[-1].at[me]
            lc = pltpu.make_async_copy(src(me), dst, local.at[a])
            lc.start()
            started.append(lc)
            for k in range(1, N_DEV):
                cp = pltpu.make_async_remote_copy(src_ref=src(me ^ k), dst_ref=dst, send_sem=send.at[a, k - 1],
                                                  recv_sem=recv.at[a, k - 1], device_id=_peer(k), device_id_type=MESH_IDS)
                cp.start()
                started.append(cp)
        for cp in started:
            cp.wait()

    return pl.pallas_call(
        body, name=name, in_specs=[ANY] * n, out_specs=[ANY] * len(shapes), out_shape=shapes,
        scratch_shapes=[pltpu.SemaphoreType.DMA((n, N_DEV - 1)), pltpu.SemaphoreType.DMA((n, N_DEV - 1)),
                        pltpu.SemaphoreType.DMA((n,))],
        compiler_params=pltpu.CompilerParams(has_side_effects=True),
    )(*flat, pack)


def slot_sum(x, *, name):
    _, A, R, C = x.shape
    br = R
    while br * C * 4 * N_DEV > (1 << 23) and br % 16 == 0:
        br //= 2

    def body(x_ref, o_ref):
        acc = x_ref[0, 0].astype(F32)
        for s in range(1, N_DEV):
            acc = acc + x_ref[s, 0].astype(F32)
        o_ref[0] = acc

    return pl.pallas_call(
        body, name=name, grid=(A, R // br),
        in_specs=[pl.BlockSpec((N_DEV, 1, br, C), lambda a, i: (0, a, i, 0))],
        out_specs=pl.BlockSpec((1, br, C), lambda a, i: (a, i, 0)),
        out_shape=jax.ShapeDtypeStruct((A, R, C), F32),
        compiler_params=_cparams(("parallel", "parallel")),
    )(x)


HBM_SPEC = pl.BlockSpec(memory_space=pltpu.HBM)
SEM_SPEC = pl.BlockSpec(memory_space=pltpu.SEMAPHORE)
DATAFLOW = pltpu.SideEffectType.DATAFLOW_SIDE_EFFECTING


def _push_copies(src_refs, land_refs, send_sems, recv_sems, by_rows):
    me = _my_id()
    copies = []
    for a, (src, land) in enumerate(zip(src_refs, land_refs)):
        rows = land.shape[1]
        for k in range(1, N_DEV):
            piece = src.at[pl.ds(pl.multiple_of((me ^ k) * rows, 8), rows), :] if by_rows else src
            copies.append(pltpu.make_async_remote_copy(
                src_ref=piece, dst_ref=land.at[me], send_sem=send_sems[a].at[k - 1], recv_sem=recv_sems[a].at[k - 1],
                device_id=_peer(k), device_id_type=MESH_IDS))
    return copies


def push_start(srcs, lands, *, by_rows, name):
    n = len(srcs)

    def body(*refs):
        src_refs, land_refs = refs[:n], refs[n:2 * n]
        send_sems, recv_sems = refs[2 * n:3 * n], refs[3 * n:4 * n]
        token = refs[6 * n]
        for cp in _push_copies(src_refs, land_refs, send_sems, recv_sems, by_rows):
            cp.start()
        token[...] = jnp.zeros_like(token)

    sems = [pltpu.SemaphoreType.DMA((N_DEV - 1,))] * (2 * n)
    bufs = [pltpu.HBM(a.shape, a.dtype) for a in list(srcs) + list(lands)]
    outs = pl.pallas_call(
        body, name=name, out_shape=tuple(sems + bufs + [jax.ShapeDtypeStruct((8, 128), F32)]),
        in_specs=[HBM_SPEC] * (2 * n), out_specs=tuple([SEM_SPEC] * (2 * n) + [HBM_SPEC] * (2 * n) + [pl.BlockSpec(memory_space=pltpu.VMEM)]),
        input_output_aliases={i: 2 * n + i for i in range(2 * n)},
        compiler_params=pltpu.CompilerParams(has_side_effects=DATAFLOW),
    )(*[pltpu.with_memory_space_constraint(a, pltpu.HBM) for a in list(srcs) + list(lands)])
    return outs[:n], outs[n:2 * n], outs[2 * n:3 * n], outs[3 * n:4 * n], outs[4 * n]


def push_wait(send_sems, recv_sems, srcs, lands, after, *, by_rows, name):
    n = len(srcs)

    def body(*refs):
        src_refs, land_refs = refs[:n], refs[n:2 * n]
        send, recv = refs[2 * n:3 * n], refs[3 * n:4 * n]
        for cp in _push_copies(src_refs, land_refs, send, recv, by_rows):
            cp.wait_send()
            cp.wait_recv()

    outs = pl.pallas_call(
        body, name=name, out_shape=tuple(pltpu.HBM(a.shape, a.dtype) for a in list(srcs) + list(lands)),
        in_specs=[HBM_SPEC] * (2 * n) + [SEM_SPEC] * (2 * n) + [ANY], out_specs=tuple([HBM_SPEC] * (2 * n)),
        input_output_aliases={i: i for i in range(2 * n)},
        compiler_params=pltpu.CompilerParams(has_side_effects=DATAFLOW),
    )(*srcs, *lands, *send_sems, *recv_sems, after)
    return outs[n:]


def _landing(src, slots_shape, *, by_rows, name):
    rows = slots_shape[1]

    def body(src_ref, out_ref, sem):
        me = _my_id()
        piece = src_ref.at[pl.ds(pl.multiple_of(me * rows, 8), rows), :] if by_rows else src_ref
        cp = pltpu.make_async_copy(piece, out_ref.at[me], sem)
        cp.start()
        cp.wait()

    return pl.pallas_call(
        body, name=name, in_specs=[ANY], out_specs=ANY, out_shape=jax.ShapeDtypeStruct(slots_shape, src.dtype),
        scratch_shapes=[pltpu.SemaphoreType.DMA],
    )(src)


def _pack(arrs):
    flat = []
    for a in arrs:
        f = a.reshape(-1).astype(F32)
        flat.append(jnp.pad(f, (0, (-f.size) % 128)))
    f = jnp.concatenate(flat)
    return jnp.pad(f, (0, (-f.size) % 1024)).reshape(-1, 128)


def _unpack(p, shapes):
    f = p.reshape(-1)
    out, off = [], 0
    for s in shapes:
        n = math.prod(s)
        out.append(f[off:off + n].reshape(s))
        off += n + (-n) % 128
    return out


def _to_padded_cols(w):
    z = lambda n: jnp.zeros(w.shape[:-1] + (n,), w.dtype)
    return jnp.concatenate([w[..., 0:4608], w[..., 4620:7692], w[..., 4608:4614], z(122), w[..., 4614:4620], z(378)], axis=-1)


def _from_padded_cols(w):
    return jnp.concatenate([w[..., 0:4608], w[..., C_BETA:C_BETA + 6], w[..., C_ALPHA:C_ALPHA + 6], w[..., 4608:7680]], axis=-1)


def _lane_pad(v):
    return jnp.pad(v, (0, 128 - v.shape[0]))[None, :]


TM_MM, TN_MM, TK_MM = 1024, 1024, 2048
TM_ROW = 512


def layer_fwd(x, p, tabs, l):
    h = rms_fwd(x, p["norm_w"], tm=TM_ROW, name=f"rms_fwd_{l}")
    u = matmul(h, p["w_in"], mode="nn", tm=TM_MM, tn=TN_MM, tk=TK_MM, name=f"in_proj_{l}")
    y_conv = conf_fwd(u, p["dw_w"], p["dw_b"], p["ln_w"], p["ln_b"], p["pw"], tm=TM_ROW, name=f"conf_fwd_{l}")
    q, k, v, beta, gc = gdn_prep_fwd(u, p["conv_w"], p["a_log"], p["dt_bias"], tm=TM_ROW, name=f"gdn_prep_fwd_{l}")
    y_gdn, states = gdn_core_fwd(q, k, v, u, beta, gc, p["gdn_nw"], name=f"gdn_core_fwd_{l}")
    qr, kr, vb = att_prep_fwd(u, tabs, tm=TM_ROW, name=f"att_prep_fwd_{l}")
    os_, ls = [], []
    for _, dil in DIL_PATTERNS:
        o_p, l_p = att_pattern_fwd(qr, kr, vb, dil, name=f"att_fwd_d{dil}_{l}")
        os_.append(o_p)
        ls.append(l_p)
    y_att, o, lse = att_combine_fwd(os_, ls, u, tm=TM_ROW, name=f"att_combine_fwd_{l}")
    y = jnp.concatenate([y_conv, y_gdn, y_att], axis=1)
    if callable(p["w_out"]):
        p["w_out"] = p["w_out"](y)
    x_new = matmul(y, p["w_out"], mode="nn", tm=TM_MM, tn=TN_MM, tk=TK_MM, residual=x, name=f"out_proj_{l}")
    saved = dict(x=x, h=h, u=u, y=y, q=q, k=k, v=v, beta=beta, gc=gc, states=states, qr=qr, kr=kr, vb=vb, o=o, lse=lse)
    return x_new, saved


def layer_bwd(dx_out, s, p, tabs, l, send_w_out=None):
    S = dx_out.shape[0]
    u = s["u"]
    dy = matmul(dx_out, p["w_out"], mode="nt", tm=TM_MM, tn=TN_MM, tk=TK_MM, name=f"out_proj_dy_{l}")
    g_w_out = matmul(s["y"], dx_out, mode="tn", tm=TM_MM, tn=TN_MM, tk=TK_MM, out_dtype=BF16, name=f"out_proj_dw_{l}")
    dw_b = p["dw_b"] if send_w_out is None else p["dw_b"] + send_w_out(g_w_out)
    du = lax.empty((S, IN_PAD), BF16)
    dc, du, g_ln_w, g_ln_b, g_pw, g_dw_b = conf_bwd_post(u, dy, p["dw_w"], dw_b, p["ln_w"], p["ln_b"], p["pw"], du,
                                                       tm=TM_ROW, name=f"conf_bwd_post_{l}")
    du, g_dw_w = conv_bwd(dc, [(u, C_CA), (u, C_CB)], p["dw_w"], du, C_CA, K=CONV_WIDTH, H=CONF_HALO, tm=TM_ROW, cw=CONV_CH,
                          glu=True, name=f"conf_bwd_conv_{l}")
    dq, dk, dv, du, dbeta, dgc, g_gdn_nw = gdn_core_bwd(s["q"], s["k"], s["v"], u, s["beta"], s["gc"], p["gdn_nw"],
                                                        s["states"], dy, du, name=f"gdn_core_bwd_{l}")
    dpre, du, g_a_log, g_dt_bias = gdn_prep_bwd(u, p["conv_w"], p["a_log"], p["dt_bias"], dq, dk, dv, dbeta, dgc, du,
                                                tm=TM_ROW, name=f"gdn_prep_bwd_{l}")
    du, g_conv_w = conv_bwd(dpre, [(u, C_GQ)], p["conv_w"], du, C_GQ, K=SHORT_CONV, H=GDN_HALO, tm=TM_ROW, cw=GDN_W,
                            glu=False, name=f"gdn_bwd_conv_{l}")
    do, du, delta = att_combine_bwd(dy, s["o"], u, du, tm=TM_ROW, name=f"att_combine_bwd_{l}")
    dqs, dks, dvs = [], [], []
    for _, dil in DIL_PATTERNS:
        a, b, c = att_pattern_bwd(s["qr"], s["kr"], s["vb"], do, delta, s["lse"], dil, name=f"att_bwd_d{dil}_{l}")
        dqs.append(a)
        dks.append(b)
        dvs.append(c)
    du = att_prep_bwd(dqs, dks, dvs, tabs, du, tm=TM_ROW, name=f"att_prep_bwd_{l}")
    dh = matmul(du, p["w_in"], mode="nt", tm=TM_MM, tn=TN_MM, tk=TK_MM, name=f"in_proj_dh_{l}")
    g_w_in = matmul(s["h"], du, mode="tn", tm=TM_MM, tn=TN_MM, tk=TK_MM, out_dtype=BF16, name=f"in_proj_dw_{l}")
    dx, g_norm_w = rms_bwd(s["x"], p["norm_w"], dh, dx_out, tm=TM_ROW // 2, name=f"rms_bwd_{l}")
    grads = dict(norm_w=g_norm_w[0], w_in=g_w_in, conv_qkv_w=g_conv_w, a_log=g_a_log[0, :GDN_HEADS], dt_bias=g_dt_bias[0, :GDN_HEADS],
                 gdn_norm_w=g_gdn_nw[0], conf_dw_w=g_dw_w, conf_dw_b=g_dw_b[0], conf_ln_w=g_ln_w[0], conf_ln_b=g_ln_b[0],
                 conf_pw_w=g_pw, w_out=g_w_out)
    return dx, grads


WEIGHTS = ("norm_w", "w_in", "conv_qkv_w", "a_log", "dt_bias", "gdn_norm_w", "conf_dw_w", "conf_dw_b", "conf_ln_w",
           "conf_ln_b", "conf_pw_w", "w_out", "final_norm_w")
SMALL_REPLICATED = ("norm_w", "a_log", "dt_bias", "gdn_norm_w", "conf_dw_b", "conf_ln_w", "conf_ln_b")


def kernel(x, norm_w, w_in, conv_qkv_w, a_log, dt_bias, gdn_norm_w, conf_dw_w, conf_dw_b, conf_ln_w, conf_ln_b, conf_pw_w, w_out, final_norm_w, loss_target, m_norm_w, m_w_in, m_conv_qkv_w, m_a_log, m_dt_bias, m_gdn_norm_w, m_conf_dw_w, m_conf_dw_b, m_conf_ln_w, m_conf_ln_b, m_conf_pw_w, m_w_out, m_final_norm_w, v_norm_w, v_w_in, v_conv_qkv_w, v_a_log, v_dt_bias, v_gdn_norm_w, v_conf_dw_w, v_conf_dw_b, v_conf_ln_w, v_conf_ln_b, v_conf_pw_w, v_w_out, v_final_norm_w):
    w = dict(norm_w=norm_w, w_in=w_in, conv_qkv_w=conv_qkv_w, a_log=a_log, dt_bias=dt_bias, gdn_norm_w=gdn_norm_w,
             conf_dw_w=conf_dw_w, conf_dw_b=conf_dw_b, conf_ln_w=conf_ln_w, conf_ln_b=conf_ln_b, conf_pw_w=conf_pw_w,
             w_out=w_out, final_norm_w=final_norm_w)
    m = dict(zip(WEIGHTS, (m_norm_w, m_w_in, m_conv_qkv_w, m_a_log, m_dt_bias, m_gdn_norm_w, m_conf_dw_w, m_conf_dw_b,
                           m_conf_ln_w, m_conf_ln_b, m_conf_pw_w, m_w_out, m_final_norm_w)))
    v = dict(zip(WEIGHTS, (v_norm_w, v_w_in, v_conv_qkv_w, v_a_log, v_dt_bias, v_gdn_norm_w, v_conf_dw_w, v_conf_dw_b,
                           v_conf_ln_w, v_conf_ln_b, v_conf_pw_w, v_w_out, v_final_norm_w)))
    S = x.shape[1]
    L = norm_w.shape[0]
    me = _my_id()

    small_shapes = [conv_qkv_w.shape, conf_dw_w.shape, conf_pw_w.shape]
    w_in_b, w_out_b = _to_padded_cols(w_in).astype(BF16), w_out.astype(BF16)
    in_slots, out_slots = (N_DEV,) + w_in_b.shape[1:], (N_DEV,) + w_out_b.shape[1:]
    g_in0, g_small = all_gather([w_in_b[0], _pack([conv_qkv_w, conf_dw_w, conf_pw_w])], name="gather_first")
    gathers, tie = {}, jnp.zeros((1, 1), F32)
    for l in range(L):
        srcs = [w_out_b[0]] if l == 0 else [w_in_b[l], w_out_b[l]]
        slots = [out_slots] if l == 0 else [in_slots, out_slots]
        lands = [_landing(a, sl, by_rows=False, name=f"gather_own_{l}_{j}") for j, (a, sl) in enumerate(zip(srcs, slots))]
        *flight, token = push_start(srcs, lands, by_rows=False, name=f"gather_start_{l}")
        gathers[l] = flight
        tie = tie + token[0:1, 0:1]
    parts = [_unpack(g_small[s], small_shapes) for s in range(N_DEV)]
    conv_full = jnp.concatenate([pt[0] for pt in parts], axis=2)
    dw_full = jnp.concatenate([pt[1] for pt in parts], axis=2)
    pw_full = jnp.concatenate([pt[2] for pt in parts], axis=1)
    tabs = rope_tables(S)

    def layer_params(l, full_in, full_out):
        return dict(
            norm_w=norm_w[l][None], w_in=full_in.reshape(D_MODEL, IN_PAD), w_out=full_out,
            conv_w=conv_full[l], a_log=_lane_pad(a_log[l]), dt_bias=_lane_pad(dt_bias[l]), gdn_nw=gdn_norm_w[l][None],
            dw_w=dw_full[l], dw_b=conf_dw_b[l][None], ln_w=conf_ln_w[l][None], ln_b=conf_ln_b[l][None], pw=pw_full[l])

    xs = x[0]
    params, saved = [], []
    for l in range(L):
        if l == 0:
            late_out = lambda after: push_wait(*gathers[0], after, by_rows=False, name="gather_wait_0")[0].reshape(D_MODEL, D_MODEL)
            p = layer_params(0, g_in0, late_out)
            p["norm_w"] = p["norm_w"] + tie
        else:
            full_in, full_out = push_wait(*gathers[l], xs, by_rows=False, name=f"gather_wait_{l}")
            p = layer_params(l, full_in, full_out.reshape(D_MODEL, D_MODEL))
        params.append(p)
        xs, sv = layer_fwd(xs, p, tabs, l)
        saved.append(sv)
    loss_part, dx, g_final = loss_head(xs, final_norm_w[None], loss_target[0], tm=TM_ROW // 2, name="loss_head")

    layer_grads, scatters = [None] * L, {}

    def send(kind, l, grad, slots):
        land = _landing(grad, slots, by_rows=True, name=f"scatter_own_{kind}_{l}")
        *flight, token = push_start([grad], [land], by_rows=True, name=f"scatter_start_{kind}_{l}")
        scatters[kind, l] = flight
        return token[0:1, 0:1]

    for l in reversed(range(L)):
        dx, layer_grads[l] = layer_bwd(dx, saved[l], params[l], tabs, l, functools.partial(send, "out", l, slots=out_slots))
        token = send("in", l, layer_grads[l]["w_in"], in_slots)
        if l > 0:
            params[l - 1]["dw_b"] = params[l - 1]["dw_b"] + token

    stack = lambda name: jnp.stack([layer_grads[l][name] for l in range(L)])
    small = [loss_part] + [stack(n) for n in SMALL_REPLICATED] + [g_final[0], stack("conv_qkv_w"), stack("conf_dw_w")]
    small_shapes = [a.shape for a in small]
    r_pw, r_small = scatter_exchange([[layer_grads[l]["conf_pw_w"] for l in range(L)]], _pack(small), name="scatter_small")
    g = {}
    g["conf_pw_w"] = slot_sum(r_pw, name="sum_pw")
    summed = _unpack(slot_sum(r_small[:, None], name="sum_small")[0], small_shapes)
    loss = summed[0].reshape(())
    for n, a in zip(SMALL_REPLICATED, summed[1:1 + len(SMALL_REPLICATED)]):
        g[n] = a
    g["final_norm_w"] = summed[-3]
    g["conv_qkv_w"] = lax.dynamic_slice_in_dim(summed[-2], me * conv_qkv_w.shape[2], conv_qkv_w.shape[2], axis=2)
    g["conf_dw_w"] = lax.dynamic_slice_in_dim(summed[-1], me * conf_dw_w.shape[2], conf_dw_w.shape[2], axis=2)
    deltas, new_m, new_v = {}, {}, {}
    for n in WEIGHTS:
        if n not in ("w_in", "w_out"):
            deltas[n], new_m[n], new_v[n] = adam(w[n], g[n], m[n], v[n], name=f"adam_{n}")
    sums = {}
    order = [(kind, l) for l in reversed(range(L)) for kind in ("out", "in")]
    done_first = loss_part
    for kind, l in order:
        last = (kind, l) == order[-1]
        after = (done_first + deltas["a_log"][0:1, 0:1]) if last else dx
        land, = push_wait(*scatters[kind, l], after, by_rows=True, name=f"scatter_wait_{kind}_{l}")
        sums[kind, l] = slot_sum(land[:, None], name=f"sum_w_{kind}_{l}")
        if not last:
            done_first = done_first + sums[kind, l][0, 0:1, 0:1]
    g["w_in"] = _from_padded_cols(jnp.concatenate([sums["in", l] for l in range(L)], axis=0))
    g["w_out"] = jnp.concatenate([sums["out", l] for l in range(L)], axis=0)
    for n in ("w_in", "w_out"):
        deltas[n], new_m[n], new_v[n] = adam(w[n], g[n], m[n], v[n], name=f"adam_{n}")
    return (loss, dx[None], *[g[n] for n in WEIGHTS], *[deltas[n] for n in WEIGHTS],
            *[new_m[n] for n in WEIGHTS], *[new_v[n] for n in WEIGHTS])
```

```python
import functools
import math

import jax
import jax.numpy as jnp
from jax import lax
from jax.experimental import pallas as pl
from jax.experimental.pallas import tpu as pltpu

D_MODEL = 2048
DEPTH = 4
N_DEV = 8
GDN_DK = 128
GDN_HEADS = 6
GDN_W = 768
ATT_HD = 64
ATT_HEADS = 12
ATT_W = 768
CONV_CH = 512
CONV_WIDTH = 31
SHORT_CONV = 4
GDN_CHUNK = 64
ROPE_THETA = 500000.0
ROPE_DIM = 16
DIL_PATTERNS = ((128, 1), (512, 4), (2048, 16))
ATT_BLOCK = 128
NEG_INF = -1e30
IN_W = 7692

ADAM_LR = 0.001
ADAM_B1 = 0.9
ADAM_B2 = 0.999
ADAM_EPS = 1e-08
ADAM_WD = 0.01
ADAM_STEP = 10

C_CA, C_CB, C_CG = 0, 512, 1024
C_GQ, C_GK, C_GV, C_GZ = 1536, 2304, 3072, 3840
C_AQ, C_AK, C_AV, C_AG = 4608, 5376, 6144, 6912
C_BETA, C_ALPHA = 7680, 7808
IN_PAD = 8192

VMEM_LIMIT = 56 * 1024 * 1024
CONF_HALO = 32
GDN_HALO = 8
GDN_GROUP = 4
GDN_HEADS_PER_STEP = 6

F32 = jnp.float32
BF16 = jnp.bfloat16
HI = lax.Precision.HIGHEST


def _cparams(sem, vmem=VMEM_LIMIT):
    return pltpu.CompilerParams(dimension_semantics=sem, vmem_limit_bytes=vmem)


def _dg(a, b, ca, cb, prec):
    nb = a.ndim - 2
    batch = tuple(range(nb))
    dn = (((ca + nb,), (cb + nb,)), (batch, batch))
    if prec == "bf16":
        return lax.dot_general(a.astype(BF16), b.astype(BF16), dn, preferred_element_type=F32)
    if prec == "bf16x3":
        ah, bh = a.astype(BF16), b.astype(BF16)
        al, bl = (a - ah.astype(F32)).astype(BF16), (b - bh.astype(F32)).astype(BF16)
        dot = lambda x, y: lax.dot_general(x, y, dn, preferred_element_type=F32)
        return dot(ah, bh) + (dot(ah, bl) + dot(al, bh))
    return lax.dot_general(a.astype(F32), b.astype(F32), dn, precision=HI, preferred_element_type=F32)


def _nn_raw(a, b, prec):
    return _dg(a, b, 1, 0, prec)


def _nt_raw(a, b, prec):
    return _dg(a, b, 1, 1, prec)


def _tn_raw(a, b, prec):
    return _dg(a, b, 0, 0, prec)


@functools.partial(jax.custom_vjp, nondiff_argnums=(2,))
def mm_nn(a, b, prec="bf16"):
    return _nn_raw(a, b, prec)


def _mm_nn_f(a, b, prec):
    return _nn_raw(a, b, prec), (a, b)


def _mm_nn_b(prec, res, g):
    a, b = res
    return _nt_raw(g, b, prec).astype(a.dtype), _tn_raw(a, g, prec).astype(b.dtype)


mm_nn.defvjp(_mm_nn_f, _mm_nn_b)


@functools.partial(jax.custom_vjp, nondiff_argnums=(2,))
def mm_nt(a, b, prec="bf16"):
    return _nt_raw(a, b, prec)


def _mm_nt_f(a, b, prec):
    return _nt_raw(a, b, prec), (a, b)


def _mm_nt_b(prec, res, g):
    a, b = res
    return _nn_raw(g, b, prec).astype(a.dtype), _tn_raw(g, a, prec).astype(b.dtype)


mm_nt.defvjp(_mm_nt_f, _mm_nt_b)


@functools.partial(jax.custom_vjp, nondiff_argnums=(2,))
def mm_tn(a, b, prec="bf16"):
    return _tn_raw(a, b, prec)


def _mm_tn_f(a, b, prec):
    return _tn_raw(a, b, prec), (a, b)


def _mm_tn_b(prec, res, g):
    a, b = res
    return _nt_raw(b, g, prec).astype(a.dtype), _nn_raw(a, g, prec).astype(b.dtype)


mm_tn.defvjp(_mm_tn_f, _mm_tn_b)


def _sigmoid(x):
    return 1.0 / (1.0 + jnp.exp(-x))


def _silu(x):
    return x * _sigmoid(x)


def _softplus(x):
    return jnp.maximum(x, 0.0) + jnp.log(1.0 + jnp.exp(-jnp.abs(x)))


def matmul(a, b, *, mode, tm, tn, tk, out_dtype=F32, residual=None, after=None, name):
    if mode == "tn":
        K, M = a.shape
    else:
        M, K = a.shape
    N = b.shape[0] if mode == "nt" else b.shape[1]
    assert M % tm == 0 and N % tn == 0 and K % tk == 0, (a.shape, b.shape, tm, tn, tk)
    nk = K // tk
    a_spec = pl.BlockSpec((tk, tm), lambda i, j, k: (k, i)) if mode == "tn" else pl.BlockSpec((tm, tk), lambda i, j, k: (i, k))
    b_spec = pl.BlockSpec((tn, tk), lambda i, j, k: (j, k)) if mode == "nt" else pl.BlockSpec((tk, tn), lambda i, j, k: (k, j))
    o_spec = pl.BlockSpec((tm, tn), lambda i, j, k: (i, j))
    raw = {"nn": _nn_raw, "nt": _nt_raw, "tn": _tn_raw}[mode]
    has_res = residual is not None

    def body(*refs):
        a_ref, b_ref = refs[:2]
        r_ref = refs[2] if has_res else None
        o_ref, acc_ref = refs[-2:]
        k = pl.program_id(2)
        part = raw(a_ref[...], b_ref[...], "bf16")

        @pl.when(k == 0)
        def _():
            acc_ref[...] = part

        @pl.when(k > 0)
        def _():
            acc_ref[...] += part

        @pl.when(k == nk - 1)
        def _():
            r = acc_ref[...]
            if has_res:
                r = r + r_ref[...].astype(F32)
            o_ref[...] = r.astype(out_dtype)

    in_specs = [a_spec, b_spec] + ([o_spec] if has_res else []) + ([ANY] if after is not None else [])
    args = (a, b) + ((residual,) if has_res else ()) + ((after,) if after is not None else ())
    return pl.pallas_call(
        body, name=name, grid=(M // tm, N // tn, nk), in_specs=in_specs, out_specs=o_spec,
        out_shape=jax.ShapeDtypeStruct((M, N), out_dtype),
        scratch_shapes=[pltpu.VMEM((tm, tn), F32)],
        compiler_params=_cparams(("parallel", "parallel", "arbitrary")),
    )(*args)


def _rms_fn(x, w, eps=1e-6):
    return x * lax.rsqrt(jnp.mean(x * x, axis=-1, keepdims=True) + eps) * w


def rms_fwd(x, w, *, tm, name):
    S, D = x.shape

    def body(x_ref, w_ref, o_ref):
        o_ref[...] = _rms_fn(x_ref[...], w_ref[...]).astype(BF16)

    return pl.pallas_call(
        body, name=name, grid=(S // tm,),
        in_specs=[pl.BlockSpec((tm, D), lambda i: (i, 0)), pl.BlockSpec((1, D), lambda i: (0, 0))],
        out_specs=pl.BlockSpec((tm, D), lambda i: (i, 0)),
        out_shape=jax.ShapeDtypeStruct((S, D), BF16),
        compiler_params=_cparams(("parallel",)),
    )(x, w)


def rms_bwd(x, w, dh, dres, *, tm, name):
    S, D = x.shape

    def body(x_ref, w_ref, dh_ref, dr_ref, dx_ref, dw_ref):
        _, vjp = jax.vjp(_rms_fn, x_ref[...], w_ref[...])
        dx, dw = vjp(dh_ref[...].astype(F32))
        dx_ref[...] = dx + dr_ref[...]

        @pl.when(pl.program_id(0) == 0)
        def _():
            dw_ref[...] = jnp.zeros_like(dw_ref)

        dw_ref[...] += dw

    row = pl.BlockSpec((tm, D), lambda i: (i, 0))
    vec = pl.BlockSpec((1, D), lambda i: (0, 0))
    return pl.pallas_call(
        body, name=name, grid=(S // tm,), in_specs=[row, vec, row, row], out_specs=[row, vec],
        out_shape=[jax.ShapeDtypeStruct((S, D), F32), jax.ShapeDtypeStruct((1, D), F32)],
        compiler_params=_cparams(("arbitrary",)),
    )(x, w, dh, dres)


def _fill_ext(ext_ref, halo, tile, first, H):
    ext_ref[pl.ds(0, H), :] = jnp.where(first, 0.0, halo)
    ext_ref[pl.ds(H, tile.shape[0]), :] = tile


def _conv_taps(ext_ref, w_ref, K, H, tm):
    assert H >= 8 * ((K - 1) // 8 + 1)
    total = None
    for b in range(min(8, K)):
        y = None
        for a in range((K - 1 - b) // 8 + 1):
            term = ext_ref[pl.ds(H - 8 - 8 * a, tm + 8), :] * w_ref[pl.ds(K - 1 - 8 * a - b, 1), :]
            y = term if y is None else y + term
        y = y if b == 0 else pltpu.roll(y, b, 0)
        total = y if total is None else total + y
    return total[8:, :]


def _halo_spec(H, tm, cw, col):
    return pl.BlockSpec((H, cw), lambda *g, _c=col: (jnp.maximum(g[-1] * (tm // H) - 1, 0), _c))


def conv_bwd(dc, srcs, w, du, du_col, *, K, H, tm, cw, glu, name):
    S, C = dc.shape
    nc, nt = C // cw, S // tm
    last_halo = S // H - 1
    n_src = 2 if glu else 1
    bases = [c0 // cw for _, c0 in srcs]

    def body(*refs):
        dc_ref, dcn_ref = refs[0], refs[1]
        src_refs = refs[2:2 + 2 * n_src]
        w_ref = refs[2 + 2 * n_src]
        out_ref, dw_ref, ext_ref, dext_ref = refs[4 + 2 * n_src:]
        i = pl.program_id(1)
        first, last = i == 0, i == nt - 1
        if glu:
            a_ref, ah_ref, b_ref, bh_ref = src_refs
            sg = _sigmoid(b_ref[...])
            _fill_ext(ext_ref, ah_ref[...] * _sigmoid(bh_ref[...]), a_ref[...] * sg, first, H)
        else:
            x_ref, xh_ref = src_refs
            _fill_ext(ext_ref, xh_ref[...], x_ref[...], first, H)
        dc_t = dc_ref[...]
        dext_ref[pl.ds(0, tm), :] = dc_t
        dext_ref[pl.ds(tm, H), :] = jnp.where(last, 0.0, dcn_ref[...])
        N = tm + 8
        dx = None
        for b in range(min(8, K)):
            z = None
            for a_ in range((K - 1 - b) // 8 + 1):
                term = dext_ref[pl.ds(8 * a_, N), :] * w_ref[pl.ds(K - 1 - 8 * a_ - b, 1), :]
                z = term if z is None else z + term
            z = z if b == 0 else pltpu.roll(z, N - b, 0)
            dx = z if dx is None else dx + z
        dx = dx[:tm, :]
        if glu:
            a = a_ref[...]
            out_ref[:, :cw] = (dx * sg).astype(BF16)
            out_ref[:, cw:] = (dx * a * sg * (1.0 - sg)).astype(BF16)
        else:
            out_ref[...] = dx.astype(BF16)

        @pl.when(first)
        def _():
            dw_ref[...] = jnp.zeros_like(dw_ref)

        dpad = jnp.concatenate([jnp.zeros((8, cw), F32), dc_t], axis=0)
        for b in range(min(8, K)):
            shifted = dpad if b == 0 else pltpu.roll(dpad, N - b, 0)
            for a_ in range((K - 1 - b) // 8 + 1):
                k = K - 1 - 8 * a_ - b
                dw_ref[pl.ds(k, 1), :] += jnp.sum(shifted * ext_ref[pl.ds(H - 8 - 8 * a_, N), :], axis=0, keepdims=True)

    tile = lambda base: pl.BlockSpec((tm, cw), lambda j, i, _b=base: (i, _b + j))
    halo = lambda base: pl.BlockSpec((H, cw), lambda j, i, _b=base: (jnp.maximum(i * (tm // H) - 1, 0), _b + j))
    in_specs = [tile(0), pl.BlockSpec((H, cw), lambda j, i: (jnp.minimum((i + 1) * (tm // H), last_halo), j))]
    args = [dc, dc]
    for (arr, _), base in zip(srcs, bases):
        in_specs += [tile(base), halo(base)]
        args += [arr, arr]
    in_specs += [pl.BlockSpec((K, cw), lambda j, i: (0, j)), ANY]
    args += [w, du]
    ow = n_src * cw
    out_specs = [pl.BlockSpec((tm, ow), lambda j, i: (i, du_col // ow + j)), pl.BlockSpec((K, cw), lambda j, i: (0, j))]
    out_shape = [jax.ShapeDtypeStruct(du.shape, du.dtype), jax.ShapeDtypeStruct((K, C), F32)]
    return pl.pallas_call(
        body, name=name, grid=(nc, nt), in_specs=in_specs, out_specs=out_specs, out_shape=out_shape,
        input_output_aliases={len(args) - 1: 0},
        scratch_shapes=[pltpu.VMEM((tm + H, cw), F32), pltpu.VMEM((tm + H, cw), F32)],
        compiler_params=_cparams(("parallel", "arbitrary")),
    )(*args)


def _conf_post(c, gate, ln_w, ln_b, pw):
    mu = jnp.mean(c, axis=-1, keepdims=True)
    cc = c - mu
    var = jnp.mean(cc * cc, axis=-1, keepdims=True)
    hn = cc * lax.rsqrt(var + 1e-5) * ln_w + ln_b
    return mm_nn(_silu(hn), pw) * _silu(gate)


def _conf_specs(tm):
    H = CONF_HALO
    blk = lambda col: pl.BlockSpec((tm, CONV_CH), lambda i, _c=col: (i, _c))
    vec = pl.BlockSpec((1, CONV_CH), lambda i: (0, 0))
    specs = [blk(0), blk(1), blk(2), _halo_spec(H, tm, CONV_CH, 0), _halo_spec(H, tm, CONV_CH, 1),
             pl.BlockSpec((CONV_WIDTH, CONV_CH), lambda i: (0, 0)), vec, vec, vec,
             pl.BlockSpec((CONV_CH, CONV_CH), lambda i: (0, 0))]
    return specs, blk, vec


def _conf_conv(a_ref, b_ref, ah_ref, bh_ref, dww_ref, dwb_ref, ext_ref, tm):
    first = pl.program_id(0) == 0
    _fill_ext(ext_ref, ah_ref[...] * _sigmoid(bh_ref[...]), a_ref[...] * _sigmoid(b_ref[...]), first, CONF_HALO)
    return _conv_taps(ext_ref, dww_ref, CONV_WIDTH, CONF_HALO, tm) + dwb_ref[...]


def conf_fwd(u, dw_w, dw_b, ln_w, ln_b, pw, *, tm, name):
    S = u.shape[0]
    specs, blk, vec = _conf_specs(tm)

    def body(a_ref, b_ref, g_ref, ah_ref, bh_ref, dww_ref, dwb_ref, lnw_ref, lnb_ref, pw_ref, y_ref, ext_ref):
        c = _conf_conv(a_ref, b_ref, ah_ref, bh_ref, dww_ref, dwb_ref, ext_ref, tm)
        y_ref[...] = _conf_post(c, g_ref[...], lnw_ref[...], lnb_ref[...], pw_ref[...]).astype(BF16)

    return pl.pallas_call(
        body, name=name, grid=(S // tm,), in_specs=specs, out_specs=blk(0),
        out_shape=jax.ShapeDtypeStruct((S, CONV_CH), BF16),
        scratch_shapes=[pltpu.VMEM((tm + CONF_HALO, CONV_CH), F32)],
        compiler_params=_cparams(("parallel",)),
    )(u, u, u, u, u, dw_w, dw_b, ln_w, ln_b, pw)


def conf_bwd_post(u, dy, dw_w, dw_b, ln_w, ln_b, pw, du, *, tm, name):
    S = u.shape[0]
    specs, blk, vec = _conf_specs(tm)
    mat = pl.BlockSpec((CONV_CH, CONV_CH), lambda i: (0, 0))

    def body(a_ref, b_ref, g_ref, ah_ref, bh_ref, dww_ref, dwb_ref, lnw_ref, lnb_ref, pw_ref, dy_ref, du_in,
             dc_ref, dg_ref, dlnw_ref, dlnb_ref, dpw_ref, ddwb_ref, ext_ref):
        c = _conf_conv(a_ref, b_ref, ah_ref, bh_ref, dww_ref, dwb_ref, ext_ref, tm)
        _, vjp = jax.vjp(_conf_post, c, g_ref[...], lnw_ref[...], lnb_ref[...], pw_ref[...])
        dc, dg, dlnw, dlnb, dpw = vjp(dy_ref[...])
        dc_ref[...] = dc
        dg_ref[...] = dg.astype(BF16)

        @pl.when(pl.program_id(0) == 0)
        def _():
            dlnw_ref[...] = jnp.zeros_like(dlnw_ref)
            dlnb_ref[...] = jnp.zeros_like(dlnb_ref)
            dpw_ref[...] = jnp.zeros_like(dpw_ref)

            ddwb_ref[...] = jnp.zeros_like(ddwb_ref)

        dlnw_ref[...] += dlnw
        dlnb_ref[...] += dlnb
        dpw_ref[...] += dpw
        ddwb_ref[...] += jnp.sum(dc, axis=0, keepdims=True)

    return pl.pallas_call(
        body, name=name, grid=(S // tm,), in_specs=specs + [blk(0), ANY],
        out_specs=[blk(0), blk(C_CG // CONV_CH), vec, vec, mat, vec],
        out_shape=[jax.ShapeDtypeStruct((S, CONV_CH), F32), jax.ShapeDtypeStruct(du.shape, du.dtype),
                   jax.ShapeDtypeStruct((1, CONV_CH), F32), jax.ShapeDtypeStruct((1, CONV_CH), F32),
                   jax.ShapeDtypeStruct((CONV_CH, CONV_CH), F32), jax.ShapeDtypeStruct((1, CONV_CH), F32)],
        input_output_aliases={11: 1},
        scratch_shapes=[pltpu.VMEM((tm + CONF_HALO, CONV_CH), F32)],
        compiler_params=_cparams(("arbitrary",)),
    )(u, u, u, u, u, dw_w, dw_b, ln_w, ln_b, pw, dy, du)


def _iota2(shape, dim):
    return lax.broadcasted_iota(jnp.int32, shape, dim)


def _gdn_post(pre_q, pre_k, pre_v, b_in, a_in, a_log, dt_bias):
    tm = pre_q.shape[0]
    q, k, v = _silu(pre_q), _silu(pre_k), _silu(pre_v)
    qs, ks = [], []
    for h in range(GDN_HEADS):
        sl = slice(h * GDN_DK, (h + 1) * GDN_DK)
        qh, kh = q[:, sl], k[:, sl]
        qs.append(qh * lax.rsqrt(jnp.sum(qh * qh, axis=-1, keepdims=True) + 1e-6) * (GDN_DK ** -0.5))
        ks.append(kh * lax.rsqrt(jnp.sum(kh * kh, axis=-1, keepdims=True) + 1e-6))
    beta = _sigmoid(b_in)
    g = -jnp.exp(a_log) * _softplus(a_in + dt_bias)
    nb = tm // GDN_CHUNK
    tril = (_iota2((nb, GDN_CHUNK, GDN_CHUNK), 1) >= _iota2((nb, GDN_CHUNK, GDN_CHUNK), 2)).astype(F32)
    gc = mm_nn(tril, g.reshape(nb, GDN_CHUNK, 128), "f32").reshape(tm, 128)
    return jnp.concatenate(qs, axis=1), jnp.concatenate(ks, axis=1), v, beta, gc


def _gdn_prep_specs(tm):
    H = GDN_HALO
    blk = lambda col: pl.BlockSpec((tm, GDN_W), lambda i, _c=col: (i, _c))
    lane = lambda col: pl.BlockSpec((tm, 128), lambda i, _c=col: (i, _c))
    vec = pl.BlockSpec((1, 128), lambda i: (0, 0))
    q0 = C_GQ // GDN_W
    specs = [blk(q0), blk(q0 + 1), blk(q0 + 2),
             _halo_spec(H, tm, GDN_W, q0), _halo_spec(H, tm, GDN_W, q0 + 1), _halo_spec(H, tm, GDN_W, q0 + 2),
             lane(C_BETA // 128), lane(C_ALPHA // 128),
             pl.BlockSpec((SHORT_CONV, GDN_W), lambda i: (0, 0)), pl.BlockSpec((SHORT_CONV, GDN_W), lambda i: (0, 1)),
             pl.BlockSpec((SHORT_CONV, GDN_W), lambda i: (0, 2)), vec, vec]
    return specs, blk, lane, vec


def _gdn_pre(x_refs, h_refs, w_refs, ext_ref, tm):
    first = pl.program_id(0) == 0
    pres = []
    for x_ref, h_ref, w_ref in zip(x_refs, h_refs, w_refs):
        _fill_ext(ext_ref, h_ref[...], x_ref[...], first, GDN_HALO)
        pres.append(_conv_taps(ext_ref, w_ref, SHORT_CONV, GDN_HALO, tm))
    return pres


def gdn_prep_fwd(u, conv_w, a_log, dt_bias, *, tm, name):
    S = u.shape[0]
    specs, blk, lane, vec = _gdn_prep_specs(tm)

    def body(xq, xk, xv, hq, hk, hv, bi, ai, wq, wk, wv, al, db, q_ref, k_ref, v_ref, beta_ref, gc_ref, ext_ref):
        pres = _gdn_pre((xq, xk, xv), (hq, hk, hv), (wq, wk, wv), ext_ref, tm)
        q, k, v, beta, gc = _gdn_post(*pres, bi[...], ai[...], al[...], db[...])
        q_ref[...] = q
        k_ref[...] = k
        v_ref[...] = v
        beta_ref[...] = beta
        gc_ref[...] = gc

    wide = jax.ShapeDtypeStruct((S, GDN_W), F32)
    narrow = jax.ShapeDtypeStruct((S, 128), F32)
    return pl.pallas_call(
        body, name=name, grid=(S // tm,), in_specs=specs,
        out_specs=[blk(0), blk(0), blk(0), lane(0), lane(0)], out_shape=[wide, wide, wide, narrow, narrow],
        scratch_shapes=[pltpu.VMEM((tm + GDN_HALO, GDN_W), F32)],
        compiler_params=_cparams(("parallel",)),
    )(u, u, u, u, u, u, u, u, conv_w, conv_w, conv_w, a_log, dt_bias)


def gdn_prep_bwd(u, conv_w, a_log, dt_bias, dq, dk, dv, dbeta, dgc, du, *, tm, name):
    S = u.shape[0]
    specs, blk, lane, vec = _gdn_prep_specs(tm)
    tail = IN_PAD - C_BETA

    def body(xq, xk, xv, hq, hk, hv, bi, ai, wq, wk, wv, al, db, dq_ref, dk_ref, dv_ref, dbe_ref, dgc_ref, du_in,
             dpre_ref, du_ref, dal_ref, ddb_ref, ext_ref):
        pres = _gdn_pre((xq, xk, xv), (hq, hk, hv), (wq, wk, wv), ext_ref, tm)
        _, vjp = jax.vjp(_gdn_post, *pres, bi[...], ai[...], al[...], db[...])
        dpq, dpk, dpv, dbi, dai, dal, ddb = vjp((dq_ref[...], dk_ref[...], dv_ref[...], dbe_ref[...], dgc_ref[...]))
        dpre_ref[:, 0:GDN_W] = dpq
        dpre_ref[:, GDN_W:2 * GDN_W] = dpk
        dpre_ref[:, 2 * GDN_W:] = dpv
        du_ref[:, 0:128] = dbi.astype(BF16)
        du_ref[:, 128:256] = dai.astype(BF16)
        du_ref[:, 256:] = jnp.zeros((tm, tail - 256), BF16)

        @pl.when(pl.program_id(0) == 0)
        def _():
            dal_ref[...] = jnp.zeros_like(dal_ref)
            ddb_ref[...] = jnp.zeros_like(ddb_ref)

        dal_ref[...] += dal
        ddb_ref[...] += ddb

    n_in = len(specs) + 6
    return pl.pallas_call(
        body, name=name, grid=(S // tm,), in_specs=specs + [blk(0), blk(0), blk(0), lane(0), lane(0), ANY],
        out_specs=[pl.BlockSpec((tm, 3 * GDN_W), lambda i: (i, 0)), pl.BlockSpec((tm, tail), lambda i: (i, C_BETA // tail)),
                   vec, vec],
        out_shape=[jax.ShapeDtypeStruct((S, 3 * GDN_W), F32), jax.ShapeDtypeStruct(du.shape, du.dtype),
                   jax.ShapeDtypeStruct((1, 128), F32), jax.ShapeDtypeStruct((1, 128), F32)],
        input_output_aliases={n_in - 1: 1},
        scratch_shapes=[pltpu.VMEM((tm + GDN_HALO, GDN_W), F32)],
        compiler_params=_cparams(("arbitrary",)),
    )(u, u, u, u, u, u, u, u, conv_w, conv_w, conv_w, a_log, dt_bias, dq, dk, dv, dbeta, dgc, du)


def _lane_col(blk, h):
    return jnp.sum(jnp.where(_iota2(blk.shape, 1) == h, blk, 0.0), axis=1, keepdims=True)


@jax.custom_vjp
def _tri_inv(low):
    n = low.shape[-1]
    r, c = _iota2(low.shape, low.ndim - 2), _iota2(low.shape, low.ndim - 1)
    eye = (r == c).astype(F32)
    t = eye - jnp.where((r // 2 == c // 2) & (r > c), low, 0.0)
    s = 2
    while s < n:
        off = jnp.where((r // (2 * s) == c // (2 * s)) & (r // s > c // s), low, 0.0)
        prec = "bf16" if s <= 8 else "bf16x3"
        t = t - _nn_raw(t, _nn_raw(off, t, prec), prec)
        s *= 2
    return t


def _tri_inv_f(low):
    t = _tri_inv(low)
    return t, t


def _tri_inv_b(t, dt):
    d = -_nt_raw(_tn_raw(t, dt, "bf16x3"), t, "bf16x3")
    r, c = _iota2(d.shape, d.ndim - 2), _iota2(d.shape, d.ndim - 1)
    return (jnp.where(r > c, d, 0.0),)


_tri_inv.defvjp(_tri_inv_f, _tri_inv_b)


def _gdn_group(s0, q, k, v, z, beta_blk, gc_blk, nw, h0):
    C = GDN_CHUNK
    HP, R, _ = q.shape
    nb = R // C
    B = HP * nb
    q3, k3, v3 = (t.reshape(B, C, GDN_DK) for t in (q, k, v))
    b3 = jnp.stack([_lane_col(beta_blk, h0 + j) for j in range(HP)]).reshape(B, C, 1)
    g3 = jnp.stack([_lane_col(gc_blk, h0 + j) for j in range(HP)]).reshape(B, C, 1)
    r, c = _iota2((B, C, C), 1), _iota2((B, C, C), 2)
    causal, strict = r >= c, r > c
    g_t = gc_blk.T
    rows = [jnp.sum(jnp.where(_iota2((128, R), 0) == h0 + j, g_t, 0.0), axis=0, keepdims=True) for j in range(HP)]
    g_row = jnp.stack([rows[j][:, i * C:(i + 1) * C] for j in range(HP) for i in range(nb)])
    decay = jnp.where(causal, jnp.exp(jnp.where(causal, g3 - g_row, 0.0)), 0.0)
    low = jnp.where(strict, b3 * mm_nt(k3, k3) * decay, 0.0)
    t = _tri_inv(low)
    eg = jnp.exp(g3)
    four = lambda x: x.reshape((HP, nb) + x.shape[1:])
    w_v = four(mm_nn(t, v3 * b3))
    w_k = four(mm_nn(t, k3 * (b3 * eg)))
    qk = four(jnp.where(causal, mm_nt(q3, k3) * decay, 0.0))
    q_dec = four(q3 * eg)
    g_last = jnp.sum(jnp.where(_iota2((B, C, 1), 1) == C - 1, g3, 0.0), axis=1, keepdims=True)
    k_dec = four(k3 * jnp.exp(g_last - g3))
    e_last = four(jnp.exp(g_last))
    s, outs = s0, []
    for i in range(nb):
        v_new = w_v[:, i] - mm_nn(w_k[:, i], s)
        outs.append(mm_nn(q_dec[:, i], s) + mm_nn(qk[:, i], v_new))
        s = s * e_last[:, i] + mm_tn(k_dec[:, i], v_new)
    o = jnp.concatenate(outs, axis=1)
    y = o * lax.rsqrt(jnp.mean(o * o, axis=-1, keepdims=True) + 1e-6) * nw * _silu(z)
    return s, y


def _heads(ref, HP):
    return jnp.stack([ref[:, j * GDN_DK:(j + 1) * GDN_DK] for j in range(HP)])


def gdn_core_fwd(q, k, v, u, beta, gc, nw, *, name):
    S = q.shape[0]
    R = GDN_CHUNK * GDN_GROUP
    G = S // R
    HP = GDN_HEADS_PER_STEP
    W = HP * GDN_DK
    blk = pl.BlockSpec((R, W), lambda g, h: (g, h))
    lane = pl.BlockSpec((R, 128), lambda g, h: (g, 0))
    st = pl.BlockSpec((1, HP, GDN_DK, GDN_DK), lambda g, h: (g, h, 0, 0))

    def body(q_ref, k_ref, v_ref, z_ref, be_ref, gc_ref, nw_ref, y_ref, st_ref, s_ref):
        g, hs = pl.program_id(0), pl.program_id(1)
        s0 = jnp.where(g == 0, 0.0, s_ref[hs])
        st_ref[0] = s0
        s1, y = _gdn_group(s0, _heads(q_ref, HP), _heads(k_ref, HP), _heads(v_ref, HP), _heads(z_ref, HP), be_ref[...],
                           gc_ref[...], nw_ref[...], hs * HP)
        s_ref[hs] = s1
        for j in range(HP):
            y_ref[:, j * GDN_DK:(j + 1) * GDN_DK] = y[j].astype(BF16)

    return pl.pallas_call(
        body, name=name, grid=(G, GDN_HEADS // HP),
        in_specs=[blk, blk, blk, pl.BlockSpec((R, W), lambda g, h: (g, C_GZ // W + h)), lane, lane,
                  pl.BlockSpec((1, 128), lambda g, h: (0, 0))],
        out_specs=[blk, st],
        out_shape=[jax.ShapeDtypeStruct((S, GDN_W), BF16), jax.ShapeDtypeStruct((G, GDN_HEADS, GDN_DK, GDN_DK), F32)],
        scratch_shapes=[pltpu.VMEM((GDN_HEADS // HP, HP, GDN_DK, GDN_DK), F32)],
        compiler_params=_cparams(("arbitrary", "arbitrary")),
    )(q, k, v, u, beta, gc, nw)


def gdn_core_bwd(q, k, v, u, beta, gc, nw, states, dy, du, *, name):
    S = q.shape[0]
    R = GDN_CHUNK * GDN_GROUP
    G = S // R
    HP = GDN_HEADS_PER_STEP
    W = HP * GDN_DK
    blk = pl.BlockSpec((R, W), lambda g, h: (G - 1 - g, h))
    lane = pl.BlockSpec((R, 128), lambda g, h: (G - 1 - g, 0))
    vec = pl.BlockSpec((1, 128), lambda g, h: (0, 0))

    def body(q_ref, k_ref, v_ref, z_ref, be_ref, gc_ref, nw_ref, st_ref, *rest):
        dy_refs = rest[:HP]
        dq_ref, dk_ref, dv_ref, dz_ref, dbe_ref, dgc_ref, dnw_ref, ds_ref = rest[HP + 1:]
        g, hs = pl.program_id(0), pl.program_id(1)

        @pl.when(hs == 0)
        def _():
            dbe_ref[...] = jnp.zeros_like(dbe_ref)
            dgc_ref[...] = jnp.zeros_like(dgc_ref)

        @pl.when((hs == 0) & (g == 0))
        def _():
            dnw_ref[...] = jnp.zeros_like(dnw_ref)

        _, vjp = jax.vjp(functools.partial(_gdn_group, h0=hs * HP), st_ref[0], _heads(q_ref, HP), _heads(k_ref, HP),
                         _heads(v_ref, HP), _heads(z_ref, HP), be_ref[...], gc_ref[...], nw_ref[...])
        ds_in = jnp.where(g == 0, 0.0, ds_ref[hs])
        dy = jnp.stack([r[...] for r in dy_refs])
        ds0, dq, dk, dv, dz, dbe, dgc, dnw = vjp((ds_in, dy))
        ds_ref[hs] = ds0
        for j in range(HP):
            sl = slice(j * GDN_DK, (j + 1) * GDN_DK)
            dq_ref[:, sl] = dq[j]
            dk_ref[:, sl] = dk[j]
            dv_ref[:, sl] = dv[j]
            dz_ref[:, sl] = dz[j].astype(BF16)
        dbe_ref[...] += dbe
        dgc_ref[...] += dgc
        dnw_ref[...] += dnw

    wide = jax.ShapeDtypeStruct((S, GDN_W), F32)
    narrow = jax.ShapeDtypeStruct((S, 128), F32)
    return pl.pallas_call(
        body, name=name, grid=(G, GDN_HEADS // HP),
        in_specs=[blk, blk, blk, pl.BlockSpec((R, W), lambda g, h: (G - 1 - g, C_GZ // W + h)), lane, lane, vec,
                  pl.BlockSpec((1, HP, GDN_DK, GDN_DK), lambda g, h: (G - 1 - g, h, 0, 0))]
        + [pl.BlockSpec((R, GDN_DK), lambda g, h, _j=j: (G - 1 - g, CONV_CH // GDN_DK + h * HP + _j)) for j in range(HP)]
        + [ANY],
        out_specs=[blk, blk, blk, pl.BlockSpec((R, W), lambda g, h: (G - 1 - g, C_GZ // W + h)), lane, lane, vec],
        out_shape=[wide, wide, wide, jax.ShapeDtypeStruct(du.shape, du.dtype), narrow, narrow,
                   jax.ShapeDtypeStruct((1, 128), F32)],
        input_output_aliases={8 + HP: 3},
        scratch_shapes=[pltpu.VMEM((GDN_HEADS // HP, HP, GDN_DK, GDN_DK), F32)],
        compiler_params=_cparams(("arbitrary", "arbitrary")),
    )(q, k, v, u, beta, gc, nw, states, *([dy] * HP), du)


def rope_tables(S):
    half = ROPE_DIM // 2
    inv = ROPE_THETA ** (-jnp.arange(half, dtype=F32) / half)
    ang = jnp.arange(S, dtype=F32)[:, None] * inv[None, :]
    cos, sin = jnp.cos(ang), jnp.sin(ang)
    rest = ATT_HD - ROPE_DIM
    c = jnp.concatenate([cos, cos, jnp.ones((S, rest), F32)], axis=1)
    s1 = jnp.concatenate([-sin, jnp.zeros((S, ATT_HD - half), F32)], axis=1)
    s2 = jnp.concatenate([jnp.zeros((S, half), F32), sin, jnp.zeros((S, rest), F32)], axis=1)
    return tuple(jnp.tile(t, (1, 2)) for t in (c, s1, s2))


def _rope(x, c, s1, s2):
    half = ROPE_DIM // 2
    return x * c + pltpu.roll(x, ATT_W - half, 1) * s1 + pltpu.roll(x, half, 1) * s2


def _unrope(dy, c, s1, s2):
    half = ROPE_DIM // 2
    return dy * c + pltpu.roll(dy * s1, half, 1) + pltpu.roll(dy * s2, ATT_W - half, 1)


def att_prep_fwd(u, tables, *, tm, name):
    S = u.shape[0]
    blk = lambda col: pl.BlockSpec((tm, ATT_W), lambda i, _c=col: (i, _c))
    tab = pl.BlockSpec((tm, 128), lambda i: (i, 0))

    def body(q_ref, k_ref, v_ref, c_ref, s1_ref, s2_ref, qo_ref, ko_ref, vo_ref):
        reps = ATT_W // 128
        c, s1, s2 = (jnp.tile(t[...], (1, reps)) for t in (c_ref, s1_ref, s2_ref))
        qo_ref[...] = (_rope(q_ref[...], c, s1, s2) * (ATT_HD ** -0.5)).astype(BF16)
        ko_ref[...] = _rope(k_ref[...], c, s1, s2).astype(BF16)
        vo_ref[...] = v_ref[...].astype(BF16)

    out = jax.ShapeDtypeStruct((S, ATT_W), BF16)
    return pl.pallas_call(
        body, name=name, grid=(S // tm,),
        in_specs=[blk(C_AQ // ATT_W), blk(C_AK // ATT_W), blk(C_AV // ATT_W), tab, tab, tab],
        out_specs=[blk(0)] * 3, out_shape=[out] * 3, compiler_params=_cparams(("parallel",)),
    )(u, u, u, *tables)


def att_prep_bwd(dqs, dks, dvs, tables, du, *, tm, name):
    S = dqs[0].shape[0]
    blk = pl.BlockSpec((tm, ATT_W), lambda i: (i, 0))
    tab = pl.BlockSpec((tm, 128), lambda i: (i, 0))

    def body(*refs):
        dq, dk, dv = (refs[3 * j][...].astype(F32) + refs[3 * j + 1][...].astype(F32) + refs[3 * j + 2][...].astype(F32)
                      for j in range(3))
        c_ref, s1_ref, s2_ref, _, o_ref = refs[9:]
        reps = ATT_W // 128
        c, s1, s2 = (jnp.tile(t[...], (1, reps)) for t in (c_ref, s1_ref, s2_ref))
        o_ref[:, 0:ATT_W] = (_unrope(dq, c, s1, s2) * (ATT_HD ** -0.5)).astype(BF16)
        o_ref[:, ATT_W:2 * ATT_W] = _unrope(dk, c, s1, s2).astype(BF16)
        o_ref[:, 2 * ATT_W:] = dv.astype(BF16)

    return pl.pallas_call(
        body, name=name, grid=(S // tm,), in_specs=[blk] * 9 + [tab] * 3 + [ANY],
        out_specs=pl.BlockSpec((tm, 3 * ATT_W), lambda i: (i, C_AQ // (3 * ATT_W))),
        out_shape=jax.ShapeDtypeStruct(du.shape, du.dtype), input_output_aliases={12: 0},
        compiler_params=_cparams(("parallel",)),
    )(*dqs, *dks, *dvs, *tables, du)


def _band_masks():
    qi, ki = _iota2((ATT_BLOCK, ATT_BLOCK), 0), _iota2((ATT_BLOCK, ATT_BLOCK), 1)
    return qi <= ki, ki <= qi


def _pair_diag(x):
    first = _iota2(x.shape, 1) < ATT_HD
    zero = jnp.zeros_like(x)
    return jnp.concatenate([jnp.where(first, x, zero), jnp.where(first, zero, x)], axis=0)


def att_pattern_fwd(qr, kr, vb, dil, *, name):
    S = qr.shape[0]
    L = S // dil
    nb = L // ATT_BLOCK
    view = lambda t: t.reshape(L, dil * t.shape[1])
    cur = pl.BlockSpec((ATT_BLOCK, ATT_W), lambda r, n: (n, r))
    prev = pl.BlockSpec((ATT_BLOCK, ATT_W), lambda r, n: (jnp.maximum(n - 1, 0), r))

    def body(q_ref, kc_ref, kp_ref, vc_ref, vp_ref, o_ref, l_ref):
        has_prev = pl.program_id(1) > 0
        m_prev, m_cur = _band_masks()
        m_prev = m_prev & has_prev
        first = _iota2((ATT_BLOCK, 128), 1) < ATT_HD
        lane = _iota2((ATT_BLOCK, 128), 1)
        stats = jnp.zeros((ATT_BLOCK, 128), F32)
        pairs = range(ATT_HEADS // 2)
        sls = [slice(p * 128, (p + 1) * 128) for p in pairs]
        sps = [_nt_raw(q_ref[:, sl], _pair_diag(kp_ref[:, sl]), "bf16") for sl in sls]
        scs = [_nt_raw(q_ref[:, sl], _pair_diag(kc_ref[:, sl]), "bf16") for sl in sls]
        probs, inv_dens = [], []
        for p in pairs:
            pps, pcs, dens, lses = [], [], [], []
            for half in range(2):
                hs = slice(half * 128, (half + 1) * 128)
                sp_h, sc_h = jnp.where(m_prev, sps[p][:, hs], NEG_INF), jnp.where(m_cur, scs[p][:, hs], NEG_INF)
                m = jnp.maximum(jnp.max(sp_h, axis=1, keepdims=True), jnp.max(sc_h, axis=1, keepdims=True))
                pp, pc = jnp.exp(sp_h - m), jnp.exp(sc_h - m)
                den = jnp.sum(pp, axis=1, keepdims=True) + jnp.sum(pc, axis=1, keepdims=True)
                pps.append(pp.astype(BF16))
                pcs.append(pc.astype(BF16))
                dens.append(den)
                lses.append(m + jnp.log(den))
            probs.append((jnp.concatenate(pps, axis=1), jnp.concatenate(pcs, axis=1)))
            inv_dens.append(1.0 / jnp.where(first, dens[0], dens[1]))
            stats = jnp.where(lane == 2 * p, lses[0], jnp.where(lane == 2 * p + 1, lses[1], stats))
        outs = [_nn_raw(probs[p][0], _pair_diag(vp_ref[:, sls[p]]), "bf16")
                + _nn_raw(probs[p][1], _pair_diag(vc_ref[:, sls[p]]), "bf16") for p in pairs]
        for p in pairs:
            o_ref[:, sls[p]] = (outs[p] * inv_dens[p]).astype(BF16)
        l_ref[...] = stats

    narrow = pl.BlockSpec((ATT_BLOCK, 128), lambda r, n: (n, r))
    o, l = pl.pallas_call(
        body, name=name, grid=(dil, nb), in_specs=[cur, cur, prev, cur, prev], out_specs=[cur, narrow],
        out_shape=[jax.ShapeDtypeStruct((L, dil * ATT_W), BF16), jax.ShapeDtypeStruct((L, dil * 128), F32)],
        compiler_params=_cparams(("parallel", "arbitrary")),
    )(view(qr), view(kr), view(kr), view(vb), view(vb))
    return o.reshape(S, ATT_W), l.reshape(S, 128)


def _head_spread():
    return (_iota2((128, ATT_W), 1) // ATT_HD == _iota2((128, ATT_W), 0)).astype(F32)


def att_combine_fwd(os_, ls, u, *, tm, name):
    S = u.shape[0]
    blk = lambda col: pl.BlockSpec((tm, ATT_W), lambda i, _c=col: (i, _c))
    lane = pl.BlockSpec((tm, 128), lambda i: (i, 0))

    def body(o1, o2, o3, l1, l2, l3, g_ref, y_ref, o_ref, lse_ref):
        a, b, c = l1[...], l2[...], l3[...]
        m = jnp.maximum(jnp.maximum(a, b), c)
        ea, eb, ec = jnp.exp(a - m), jnp.exp(b - m), jnp.exp(c - m)
        den = ea + eb + ec
        spread = _head_spread()
        wa, wb, wc = (_nn_raw(e / den, spread, "bf16x3") for e in (ea, eb, ec))
        o = wa * o1[...].astype(F32) + wb * o2[...].astype(F32) + wc * o3[...].astype(F32)
        o_ref[...] = o
        lse_ref[...] = m + jnp.log(den)
        y_ref[...] = (o * _silu(g_ref[...])).astype(BF16)

    return pl.pallas_call(
        body, name=name, grid=(S // tm,), in_specs=[blk(0)] * 3 + [lane] * 3 + [blk(C_AG // ATT_W)],
        out_specs=[blk(0), blk(0), lane],
        out_shape=[jax.ShapeDtypeStruct((S, ATT_W), BF16), jax.ShapeDtypeStruct((S, ATT_W), F32),
                   jax.ShapeDtypeStruct((S, 128), F32)],
        compiler_params=_cparams(("parallel",)),
    )(*os_, *ls, u)


def att_combine_bwd(dy, o, u, du, *, tm, name):
    S = u.shape[0]
    cw = 256
    base = (CONV_CH + GDN_W) // cw
    blk = lambda col: pl.BlockSpec((tm, ATT_W), lambda i, _c=col: (i, _c))

    def body(dy0, dy1, dy2, o_ref, g_ref, du_in, do_ref, dg_ref, dl_ref):
        g, d, o = g_ref[...], jnp.concatenate([dy0[...], dy1[...], dy2[...]], axis=1), o_ref[...]
        sg = _sigmoid(g)
        d_o = d * (g * sg)
        do_ref[...] = d_o.astype(BF16)
        dg_ref[...] = (d * o * (sg * (1.0 + g * (1.0 - sg)))).astype(BF16)
        dl_ref[...] = _nt_raw(d_o * o, _head_spread(), "bf16x3")

    return pl.pallas_call(
        body, name=name, grid=(S // tm,),
        in_specs=[pl.BlockSpec((tm, cw), lambda i, _j=j: (i, base + _j)) for j in range(ATT_W // cw)]
        + [blk(0), blk(C_AG // ATT_W), ANY],
        out_specs=[blk(0), blk(C_AG // ATT_W), pl.BlockSpec((tm, 128), lambda i: (i, 0))],
        out_shape=[jax.ShapeDtypeStruct((S, ATT_W), BF16), jax.ShapeDtypeStruct(du.shape, du.dtype),
                   jax.ShapeDtypeStruct((S, 128), F32)],
        input_output_aliases={5: 1},
        compiler_params=_cparams(("parallel",)),
    )(dy, dy, dy, o, u, du)


def att_pattern_bwd(qr, kr, vb, do, delta, lse, dil, *, name):
    S = qr.shape[0]
    L = S // dil
    nb = L // ATT_BLOCK
    view = lambda t: t.reshape(L, dil * t.shape[1])
    cur = pl.BlockSpec((ATT_BLOCK, ATT_W), lambda r, n: (jnp.minimum(n, nb - 1), r))
    prev = pl.BlockSpec((ATT_BLOCK, ATT_W), lambda r, n: (jnp.maximum(n - 1, 0), r))
    narrow = pl.BlockSpec((ATT_BLOCK, 128), lambda r, n: (jnp.minimum(n, nb - 1), r))

    def body(q_ref, kc_ref, kp_ref, vc_ref, vp_ref, do_ref, dl_ref, l_ref, dq_ref, dk_ref, dv_ref, ck_ref, cv_ref):
        n = pl.program_id(1)

        @pl.when(n < nb)
        def _():
            m_prev, m_cur = _band_masks()
            m_prev = m_prev & (n > 0)
            m_prev2, m_cur2 = jnp.concatenate([m_prev, m_prev], axis=1), jnp.concatenate([m_cur, m_cur], axis=1)
            first = _iota2((ATT_BLOCK, 128), 1) < ATT_HD
            wide = (ATT_BLOCK, 128)
            halves = lambda a, b: jnp.concatenate([jnp.broadcast_to(a, wide), jnp.broadcast_to(b, wide)], axis=1)
            fold = lambda t: jnp.where(first, t[:ATT_BLOCK], t[ATT_BLOCK:])
            pairs = range(ATT_HEADS // 2)
            sls = [slice(p * 128, (p + 1) * 128) for p in pairs]
            qs, dos = [q_ref[:, sl] for sl in sls], [do_ref[:, sl] for sl in sls]
            kps, kcs, vps, vcs = ([_pair_diag(r[:, sl]) for sl in sls] for r in (kp_ref, kc_ref, vp_ref, vc_ref))
            s_p = [_nt_raw(qs[p], kps[p], "bf16") for p in pairs]
            s_c = [_nt_raw(qs[p], kcs[p], "bf16") for p in pairs]
            dp_p = [_nt_raw(dos[p], vps[p], "bf16") for p in pairs]
            dp_c = [_nt_raw(dos[p], vcs[p], "bf16") for p in pairs]
            pps, pcs, dsps, dscs = [], [], [], []
            for p in pairs:
                delta = halves(dl_ref[:, 2 * p:2 * p + 1], dl_ref[:, 2 * p + 1:2 * p + 2])
                lse2 = halves(l_ref[:, 2 * p:2 * p + 1], l_ref[:, 2 * p + 1:2 * p + 2])
                pp = jnp.where(m_prev2, jnp.exp(s_p[p] - lse2), 0.0)
                pc = jnp.where(m_cur2, jnp.exp(s_c[p] - lse2), 0.0)
                dsps.append((pp * (dp_p[p] - delta)).astype(BF16))
                dscs.append((pc * (dp_c[p] - delta)).astype(BF16))
                pps.append(pp.astype(BF16))
                pcs.append(pc.astype(BF16))
            dqs = [_nn_raw(dsps[p], kps[p], "bf16") + _nn_raw(dscs[p], kcs[p], "bf16") for p in pairs]
            dk_prev = [fold(_tn_raw(dsps[p], qs[p], "bf16")) for p in pairs]
            dv_prev = [fold(_tn_raw(pps[p], dos[p], "bf16")) for p in pairs]
            dk_cur = [fold(_tn_raw(dscs[p], qs[p], "bf16")) for p in pairs]
            dv_cur = [fold(_tn_raw(pcs[p], dos[p], "bf16")) for p in pairs]
            for p in pairs:
                dq_ref[:, sls[p]] = dqs[p].astype(BF16)

            @pl.when(n > 0)
            def _():
                for p in pairs:
                    dk_ref[:, sls[p]] = (ck_ref[:, sls[p]] + dk_prev[p]).astype(BF16)
                    dv_ref[:, sls[p]] = (cv_ref[:, sls[p]] + dv_prev[p]).astype(BF16)

            for p in pairs:
                ck_ref[:, sls[p]] = dk_cur[p]
                cv_ref[:, sls[p]] = dv_cur[p]

        @pl.when(n == nb)
        def _():
            dk_ref[...] = ck_ref[...].astype(BF16)
            dv_ref[...] = cv_ref[...].astype(BF16)

    out = jax.ShapeDtypeStruct((L, dil * ATT_W), BF16)
    dq, dk, dv = pl.pallas_call(
        body, name=name, grid=(dil, nb + 1), in_specs=[cur, cur, prev, cur, prev, cur, narrow, narrow],
        out_specs=[cur, prev, prev], out_shape=[out, out, out],
        scratch_shapes=[pltpu.VMEM((ATT_BLOCK, ATT_W), F32), pltpu.VMEM((ATT_BLOCK, ATT_W), F32)],
        compiler_params=_cparams(("arbitrary", "arbitrary")),
    )(view(qr), view(kr), view(kr), view(vb), view(vb), view(do), view(delta), view(lse))
    return dq.reshape(S, ATT_W), dk.reshape(S, ATT_W), dv.reshape(S, ATT_W)


def _loss_rows(x, w, tgt):
    err = _rms_fn(x, w) - tgt
    return jnp.sum(0.5 * jnp.mean(err * err, axis=-1, keepdims=True), axis=0, keepdims=True)


def loss_head(x, w, tgt, *, tm, name):
    S, D = x.shape

    def body(x_ref, w_ref, t_ref, l_ref, dx_ref, dw_ref):
        val, vjp = jax.vjp(_loss_rows, x_ref[...], w_ref[...], t_ref[...])
        dx, dw, _ = vjp(jnp.ones((1, 1), F32))
        dx_ref[...] = dx

        @pl.when(pl.program_id(0) == 0)
        def _():
            l_ref[...] = jnp.zeros_like(l_ref)
            dw_ref[...] = jnp.zeros_like(dw_ref)

        l_ref[...] += val
        dw_ref[...] += dw

    row = pl.BlockSpec((tm, D), lambda i: (i, 0))
    vec = pl.BlockSpec((1, D), lambda i: (0, 0))
    one = pl.BlockSpec((1, 1), lambda i: (0, 0))
    return pl.pallas_call(
        body, name=name, grid=(S // tm,), in_specs=[row, vec, row], out_specs=[one, row, vec],
        out_shape=[jax.ShapeDtypeStruct((1, 1), F32), jax.ShapeDtypeStruct((S, D), F32), jax.ShapeDtypeStruct((1, D), F32)],
        compiler_params=_cparams(("arbitrary",)),
    )(x, w, tgt)


def adam(w, g, m, v, *, name):
    shape = w.shape
    C = shape[-1]
    R = w.size // C
    br = R
    while br * C * 4 > (1 << 21) and br % 16 == 0:
        br //= 2
    two = lambda t: t.reshape(R, C)

    def body(w_ref, g_ref, m_ref, v_ref, d_ref, mo_ref, vo_ref):
        gg = g_ref[...]
        m_new = ADAM_B1 * m_ref[...] + (1.0 - ADAM_B1) * gg
        v_new = ADAM_B2 * v_ref[...] + (1.0 - ADAM_B2) * jnp.square(gg)
        m_hat = m_new / (1.0 - ADAM_B1 ** ADAM_STEP)
        v_hat = v_new / (1.0 - ADAM_B2 ** ADAM_STEP)
        d_ref[...] = -ADAM_LR * (m_hat / (jnp.sqrt(v_hat) + ADAM_EPS) + ADAM_WD * w_ref[...])
        mo_ref[...] = m_new
        vo_ref[...] = v_new

    blk = pl.BlockSpec((br, C), lambda i: (i, 0))
    out = jax.ShapeDtypeStruct((R, C), F32)
    d, mo, vo = pl.pallas_call(
        body, name=name, grid=(R // br,), in_specs=[blk] * 4, out_specs=[blk] * 3, out_shape=[out] * 3,
        compiler_params=_cparams(("parallel",)),
    )(two(w), two(g), two(m), two(v))
    return d.reshape(shape), mo.reshape(shape), vo.reshape(shape)


MESH_IDS = pl.DeviceIdType.MESH
ANY = pl.BlockSpec(memory_space=pl.ANY)


def _my_id():
    return 4 * lax.axis_index("x") + 2 * lax.axis_index("y") + lax.axis_index("c")


def _peer(k):
    x, y, c = lax.axis_index("x"), lax.axis_index("y"), lax.axis_index("c")
    flip = lambda v, bit: 1 - v if bit else v
    return (flip(x, k & 4), flip(y, k & 2), flip(c, k & 1))


def all_gather(arrs, *, name):
    n = len(arrs)

    def body(*refs):
        ins, outs = refs[:n], refs[n:2 * n]
        send, recv, local = refs[2 * n:]
        me = _my_id()
        started = []
        for a in range(n):
            lc = pltpu.make_async_copy(ins[a], outs[a].at[me], local.at[a])
            lc.start()
            started.append(lc)
            for k in range(1, N_DEV):
                cp = pltpu.make_async_remote_copy(src_ref=ins[a], dst_ref=outs[a].at[me], send_sem=send.at[a, k - 1],
                                                  recv_sem=recv.at[a, k - 1], device_id=_peer(k), device_id_type=MESH_IDS)
                cp.start()
                started.append(cp)
        for cp in started:
            cp.wait()

    return pl.pallas_call(
        body, name=name, in_specs=[ANY] * n, out_specs=[ANY] * n,
        out_shape=[jax.ShapeDtypeStruct((N_DEV,) + a.shape, a.dtype) for a in arrs],
        scratch_shapes=[pltpu.SemaphoreType.DMA((n, N_DEV - 1)), pltpu.SemaphoreType.DMA((n, N_DEV - 1)),
                        pltpu.SemaphoreType.DMA((n,))],
        compiler_params=pltpu.CompilerParams(has_side_effects=True),
    )(*arrs)


def scatter_exchange(groups, pack, *, name):
    flat = [a for grp in groups for a in grp]
    n = len(flat) + 1
    shapes = [jax.ShapeDtypeStruct((N_DEV, len(grp), grp[0].shape[0] // N_DEV, grp[0].shape[1]), grp[0].dtype) for grp in groups]
    shapes.append(jax.ShapeDtypeStruct((N_DEV,) + pack.shape, pack.dtype))
    index = [(gi, li) for gi, grp in enumerate(groups) for li in range(len(grp))]

    def body(*refs):
        ins, outs = refs[:n], refs[n:n + len(shapes)]
        send, recv, local = refs[n + len(shapes):]
        me = _my_id()
        started = []
        for a in range(n):
            if a < n - 1:
                gi, li = index[a]
                r = ins[a].shape[0] // N_DEV
                src = lambda j, _a=a, _r=r: ins[_a].at[pl.ds(pl.multiple_of(j * _r, 8), _r), :]
                dst = outs[gi].at[me, li]
            else:
                src = lambda j, _a=a: ins[_a]
                dst = outs[-1].at[me]
            lc = pltpu.make_async_copy(src(me), dst, local.at[a])
            lc.start()
            started.append(lc)
            for k in range(1, N_DEV):
                cp = pltpu.make_async_remote_copy(src_ref=src(me ^ k), dst_ref=dst, send_sem=send.at[a, k - 1],
                                                  recv_sem=recv.at[a, k - 1], device_id=_peer(k), device_id_type=MESH_IDS)
                cp.start()
                started.append(cp)
        for cp in started:
            cp.wait()

    return pl.pallas_call(
        body, name=name, in_specs=[ANY] * n, out_specs=[ANY] * len(shapes), out_shape=shapes,
        scratch_shapes=[pltpu.SemaphoreType.DMA((n, N_DEV - 1)), pltpu.SemaphoreType.DMA((n, N_DEV - 1)),
                        pltpu.SemaphoreType.DMA((n,))],
        compiler_params=pltpu.CompilerParams(has_side_effects=True),
    )(*flat, pack)


def slot_sum(x, *, name):
    _, A, R, C = x.shape
    br = R
    while br * C * 4 * N_DEV > (1 << 23) and br % 16 == 0:
        br //= 2

    def body(x_ref, o_ref):
        acc = x_ref[0, 0].astype(F32)
        for s in range(1, N_DEV):
            acc = acc + x_ref[s, 0].astype(F32)
        o_ref[0] = acc

    return pl.pallas_call(
        body, name=name, grid=(A, R // br),
        in_specs=[pl.BlockSpec((N_DEV, 1, br, C), lambda a, i: (0, a, i, 0))],
        out_specs=pl.BlockSpec((1, br, C), lambda a, i: (a, i, 0)),
        out_shape=jax.ShapeDtypeStruct((A, R, C), F32),
        compiler_params=_cparams(("parallel", "parallel")),
    )(x)


HBM_SPEC = pl.BlockSpec(memory_space=pltpu.HBM)
SEM_SPEC = pl.BlockSpec(memory_space=pltpu.SEMAPHORE)
DATAFLOW = pltpu.SideEffectType.DATAFLOW_SIDE_EFFECTING


def _push_copies(src_refs, land_refs, send_sems, recv_sems, by_rows):
    me = _my_id()
    copies = []
    for a, (src, land) in enumerate(zip(src_refs, land_refs)):
        rows = land.shape[1]
        for k in range(1, N_DEV):
            piece = src.at[pl.ds(pl.multiple_of((me ^ k) * rows, 8), rows), :] if by_rows else src
            copies.append(pltpu.make_async_remote_copy(
                src_ref=piece, dst_ref=land.at[me], send_sem=send_sems[a].at[k - 1], recv_sem=recv_sems[a].at[k - 1],
                device_id=_peer(k), device_id_type=MESH_IDS))
    return copies


def push_start(srcs, lands, *, by_rows, name):
    n = len(srcs)

    def body(*refs):
        src_refs, land_refs = refs[:n], refs[n:2 * n]
        send_sems, recv_sems = refs[2 * n:3 * n], refs[3 * n:4 * n]
        token = refs[6 * n]
        for cp in _push_copies(src_refs, land_refs, send_sems, recv_sems, by_rows):
            cp.start()
        token[...] = jnp.zeros_like(token)

    sems = [pltpu.SemaphoreType.DMA((N_DEV - 1,))] * (2 * n)
    bufs = [pltpu.HBM(a.shape, a.dtype) for a in list(srcs) + list(lands)]
    outs = pl.pallas_call(
        body, name=name, out_shape=tuple(sems + bufs + [jax.ShapeDtypeStruct((8, 128), F32)]),
        in_specs=[HBM_SPEC] * (2 * n), out_specs=tuple([SEM_SPEC] * (2 * n) + [HBM_SPEC] * (2 * n) + [pl.BlockSpec(memory_space=pltpu.VMEM)]),
        input_output_aliases={i: 2 * n + i for i in range(2 * n)},
        compiler_params=pltpu.CompilerParams(has_side_effects=DATAFLOW),
    )(*[pltpu.with_memory_space_constraint(a, pltpu.HBM) for a in list(srcs) + list(lands)])
    return outs[:n], outs[n:2 * n], outs[2 * n:3 * n], outs[3 * n:4 * n], outs[4 * n]


def push_wait(send_sems, recv_sems, srcs, lands, after, *, by_rows, name):
    n = len(srcs)

    def body(*refs):
        src_refs, land_refs = refs[:n], refs[n:2 * n]
        send, recv = refs[2 * n:3 * n], refs[3 * n:4 * n]
        for cp in _push_copies(src_refs, land_refs, send, recv, by_rows):
            cp.wait_send()
            cp.wait_recv()

    outs = pl.pallas_call(
        body, name=name, out_shape=tuple(pltpu.HBM(a.shape, a.dtype) for a in list(srcs) + list(lands)),
        in_specs=[HBM_SPEC] * (2 * n) + [SEM_SPEC] * (2 * n) + [ANY], out_specs=tuple([HBM_SPEC] * (2 * n)),
        input_output_aliases={i: i for i in range(2 * n)},
        compiler_params=pltpu.CompilerParams(has_side_effects=DATAFLOW),
    )(*srcs, *lands, *send_sems, *recv_sems, after)
    return outs[n:]


LANDING_CHUNKS = 16


def _landing(src, slots_shape, after=None, *, by_rows, name):
    rows = slots_shape[1]
    step = rows // LANDING_CHUNKS

    def body(src_ref, *rest):
        out_ref, sems = rest[-2:]
        me = _my_id()
        base = pl.multiple_of(me * rows, 8) if by_rows else 0
        copies = [pltpu.make_async_copy(src_ref.at[pl.ds(base + c * step, step), :],
                                        out_ref.at[me, pl.ds(c * step, step), :], sems.at[c]) for c in range(LANDING_CHUNKS)]
        for cp in copies:
            cp.start()
        for cp in copies:
            cp.wait()

    extra = [] if after is None else [after]
    return pl.pallas_call(
        body, name=name, in_specs=[ANY] * (1 + len(extra)), out_specs=ANY,
        out_shape=jax.ShapeDtypeStruct(slots_shape, src.dtype),
        scratch_shapes=[pltpu.SemaphoreType.DMA((LANDING_CHUNKS,))],
    )(src, *extra)


def _pack(arrs):
    flat = []
    for a in arrs:
        f = a.reshape(-1).astype(F32)
        flat.append(jnp.pad(f, (0, (-f.size) % 128)))
    f = jnp.concatenate(flat)
    return jnp.pad(f, (0, (-f.size) % 1024)).reshape(-1, 128)


def _unpack(p, shapes):
    f = p.reshape(-1)
    out, off = [], 0
    for s in shapes:
        n = math.prod(s)
        out.append(f[off:off + n].reshape(s))
        off += n + (-n) % 128
    return out


def _to_padded_cols(w):
    z = lambda n: jnp.zeros(w.shape[:-1] + (n,), w.dtype)
    return jnp.concatenate([w[..., 0:4608], w[..., 4620:7692], w[..., 4608:4614], z(122), w[..., 4614:4620], z(378)], axis=-1)


def _from_padded_cols(w):
    return jnp.concatenate([w[..., 0:4608], w[..., C_BETA:C_BETA + 6], w[..., C_ALPHA:C_ALPHA + 6], w[..., 4608:7680]], axis=-1)


def _lane_pad(v):
    return jnp.pad(v, (0, 128 - v.shape[0]))[None, :]


TM_MM, TN_MM, TK_MM = 1024, 1024, 2048
TM_ROW = 512


def layer_fwd(x, p, tabs, l):
    h = rms_fwd(x, p["norm_w"], tm=TM_ROW, name=f"rms_fwd_{l}")
    u = matmul(h, p["w_in"], mode="nn", tm=TM_MM, tn=TN_MM, tk=TK_MM, name=f"in_proj_{l}")
    y_conv = conf_fwd(u, p["dw_w"], p["dw_b"], p["ln_w"], p["ln_b"], p["pw"], tm=TM_ROW, name=f"conf_fwd_{l}")
    q, k, v, beta, gc = gdn_prep_fwd(u, p["conv_w"], p["a_log"], p["dt_bias"], tm=TM_ROW, name=f"gdn_prep_fwd_{l}")
    y_gdn, states = gdn_core_fwd(q, k, v, u, beta, gc, p["gdn_nw"], name=f"gdn_core_fwd_{l}")
    qr, kr, vb = att_prep_fwd(u, tabs, tm=TM_ROW, name=f"att_prep_fwd_{l}")
    os_, ls = [], []
    for _, dil in DIL_PATTERNS:
        o_p, l_p = att_pattern_fwd(qr, kr, vb, dil, name=f"att_fwd_d{dil}_{l}")
        os_.append(o_p)
        ls.append(l_p)
    y_att, o, lse = att_combine_fwd(os_, ls, u, tm=TM_ROW, name=f"att_combine_fwd_{l}")
    y = jnp.concatenate([y_conv, y_gdn, y_att], axis=1)
    if callable(p["w_out"]):
        p["w_out"] = p["w_out"](y)
    x_new = matmul(y, p["w_out"], mode="nn", tm=TM_MM, tn=TN_MM, tk=TK_MM, residual=x, name=f"out_proj_{l}")
    saved = dict(x=x, h=h, u=u, y=y, q=q, k=k, v=v, beta=beta, gc=gc, states=states, qr=qr, kr=kr, vb=vb, o=o, lse=lse)
    return x_new, saved


def layer_bwd(dx_out, s, p, tabs, l, send_w_out=None, send_w_in=None):
    S = dx_out.shape[0]
    u = s["u"]
    dy = matmul(dx_out, p["w_out"], mode="nt", tm=TM_MM, tn=TN_MM, tk=TK_MM, name=f"out_proj_dy_{l}")
    g_w_out = matmul(s["y"], dx_out, mode="tn", tm=TM_MM, tn=TN_MM, tk=TK_MM, out_dtype=BF16, name=f"out_proj_dw_{l}")
    dw_b = p["dw_b"] if send_w_out is None else p["dw_b"] + send_w_out(g_w_out)
    du = lax.empty((S, IN_PAD), BF16)
    dc, du, g_ln_w, g_ln_b, g_pw, g_dw_b = conf_bwd_post(u, dy, p["dw_w"], dw_b, p["ln_w"], p["ln_b"], p["pw"], du,
                                                       tm=TM_ROW, name=f"conf_bwd_post_{l}")
    du, g_dw_w = conv_bwd(dc, [(u, C_CA), (u, C_CB)], p["dw_w"], du, C_CA, K=CONV_WIDTH, H=CONF_HALO, tm=TM_ROW, cw=CONV_CH,
                          glu=True, name=f"conf_bwd_conv_{l}")
    dq, dk, dv, du, dbeta, dgc, g_gdn_nw = gdn_core_bwd(s["q"], s["k"], s["v"], u, s["beta"], s["gc"], p["gdn_nw"],
                                                        s["states"], dy, du, name=f"gdn_core_bwd_{l}")
    dpre, du, g_a_log, g_dt_bias = gdn_prep_bwd(u, p["conv_w"], p["a_log"], p["dt_bias"], dq, dk, dv, dbeta, dgc, du,
                                                tm=TM_ROW, name=f"gdn_prep_bwd_{l}")
    du, g_conv_w = conv_bwd(dpre, [(u, C_GQ)], p["conv_w"], du, C_GQ, K=SHORT_CONV, H=GDN_HALO, tm=TM_ROW, cw=GDN_W,
                            glu=False, name=f"gdn_bwd_conv_{l}")
    do, du, delta = att_combine_bwd(dy, s["o"], u, du, tm=TM_ROW, name=f"att_combine_bwd_{l}")
    dqs, dks, dvs = [], [], []
    for _, dil in DIL_PATTERNS:
        a, b, c = att_pattern_bwd(s["qr"], s["kr"], s["vb"], do, delta, s["lse"], dil, name=f"att_bwd_d{dil}_{l}")
        dqs.append(a)
        dks.append(b)
        dvs.append(c)
    du = att_prep_bwd(dqs, dks, dvs, tabs, du, tm=TM_ROW, name=f"att_prep_bwd_{l}")
    g_w_in = matmul(s["h"], du, mode="tn", tm=TM_MM, tn=TN_MM, tk=TK_MM, out_dtype=BF16, name=f"in_proj_dw_{l}")
    sent = None if send_w_in is None else send_w_in(g_w_in)
    dh = matmul(du, p["w_in"], mode="nt", tm=TM_MM, tn=TN_MM, tk=TK_MM, after=sent, name=f"in_proj_dh_{l}")
    dx, g_norm_w = rms_bwd(s["x"], p["norm_w"], dh, dx_out, tm=TM_ROW // 2, name=f"rms_bwd_{l}")
    grads = dict(norm_w=g_norm_w[0], w_in=g_w_in, conv_qkv_w=g_conv_w, a_log=g_a_log[0, :GDN_HEADS], dt_bias=g_dt_bias[0, :GDN_HEADS],
                 gdn_norm_w=g_gdn_nw[0], conf_dw_w=g_dw_w, conf_dw_b=g_dw_b[0], conf_ln_w=g_ln_w[0], conf_ln_b=g_ln_b[0],
                 conf_pw_w=g_pw, w_out=g_w_out)
    return dx, grads


WEIGHTS = ("norm_w", "w_in", "conv_qkv_w", "a_log", "dt_bias", "gdn_norm_w", "conf_dw_w", "conf_dw_b", "conf_ln_w",
           "conf_ln_b", "conf_pw_w", "w_out", "final_norm_w")
SMALL_REPLICATED = ("norm_w", "a_log", "dt_bias", "gdn_norm_w", "conf_dw_b", "conf_ln_w", "conf_ln_b")


def kernel(x, norm_w, w_in, conv_qkv_w, a_log, dt_bias, gdn_norm_w, conf_dw_w, conf_dw_b, conf_ln_w, conf_ln_b, conf_pw_w, w_out, final_norm_w, loss_target, m_norm_w, m_w_in, m_conv_qkv_w, m_a_log, m_dt_bias, m_gdn_norm_w, m_conf_dw_w, m_conf_dw_b, m_conf_ln_w, m_conf_ln_b, m_conf_pw_w, m_w_out, m_final_norm_w, v_norm_w, v_w_in, v_conv_qkv_w, v_a_log, v_dt_bias, v_gdn_norm_w, v_conf_dw_w, v_conf_dw_b, v_conf_ln_w, v_conf_ln_b, v_conf_pw_w, v_w_out, v_final_norm_w):
    w = dict(norm_w=norm_w, w_in=w_in, conv_qkv_w=conv_qkv_w, a_log=a_log, dt_bias=dt_bias, gdn_norm_w=gdn_norm_w,
             conf_dw_w=conf_dw_w, conf_dw_b=conf_dw_b, conf_ln_w=conf_ln_w, conf_ln_b=conf_ln_b, conf_pw_w=conf_pw_w,
             w_out=w_out, final_norm_w=final_norm_w)
    m = dict(zip(WEIGHTS, (m_norm_w, m_w_in, m_conv_qkv_w, m_a_log, m_dt_bias, m_gdn_norm_w, m_conf_dw_w, m_conf_dw_b,
                           m_conf_ln_w, m_conf_ln_b, m_conf_pw_w, m_w_out, m_final_norm_w)))
    v = dict(zip(WEIGHTS, (v_norm_w, v_w_in, v_conv_qkv_w, v_a_log, v_dt_bias, v_gdn_norm_w, v_conf_dw_w, v_conf_dw_b,
                           v_conf_ln_w, v_conf_ln_b, v_conf_pw_w, v_w_out, v_final_norm_w)))
    S = x.shape[1]
    L = norm_w.shape[0]
    me = _my_id()

    small_shapes = [conv_qkv_w.shape, conf_dw_w.shape, conf_pw_w.shape]
    w_in_b, w_out_b = _to_padded_cols(w_in).astype(BF16), w_out.astype(BF16)
    in_slots, out_slots = (N_DEV,) + w_in_b.shape[1:], (N_DEV,) + w_out_b.shape[1:]
    g_in0, g_small = all_gather([w_in_b[0], _pack([conv_qkv_w, conf_dw_w, conf_pw_w])], name="gather_first")
    gathers, tie = {}, jnp.zeros((1, 1), F32)
    for l in range(L):
        srcs = [w_out_b[0]] if l == 0 else [w_in_b[l], w_out_b[l]]
        slots = [out_slots] if l == 0 else [in_slots, out_slots]
        lands = [_landing(a, sl, g_in0, by_rows=False, name=f"gather_own_{l}_{j}") for j, (a, sl) in enumerate(zip(srcs, slots))]
        *flight, token = push_start(srcs, lands, by_rows=False, name=f"gather_start_{l}")
        gathers[l] = flight
        tie = tie + token[0:1, 0:1]
    parts = [_unpack(g_small[s], small_shapes) for s in range(N_DEV)]
    conv_full = jnp.concatenate([pt[0] for pt in parts], axis=2)
    dw_full = jnp.concatenate([pt[1] for pt in parts], axis=2)
    pw_full = jnp.concatenate([pt[2] for pt in parts], axis=1)
    tabs = rope_tables(S)

    def layer_params(l, full_in, full_out):
        return dict(
            norm_w=norm_w[l][None], w_in=full_in.reshape(D_MODEL, IN_PAD), w_out=full_out,
            conv_w=conv_full[l], a_log=_lane_pad(a_log[l]), dt_bias=_lane_pad(dt_bias[l]), gdn_nw=gdn_norm_w[l][None],
            dw_w=dw_full[l], dw_b=conf_dw_b[l][None], ln_w=conf_ln_w[l][None], ln_b=conf_ln_b[l][None], pw=pw_full[l])

    xs = x[0]
    params, saved = [], []
    for l in range(L):
        if l == 0:
            late_out = lambda after: push_wait(*gathers[0], after, by_rows=False, name="gather_wait_0")[0].reshape(D_MODEL, D_MODEL)
            p = layer_params(0, g_in0, late_out)
            p["norm_w"] = p["norm_w"] + tie
        else:
            full_in, full_out = push_wait(*gathers[l], xs, by_rows=False, name=f"gather_wait_{l}")
            p = layer_params(l, full_in, full_out.reshape(D_MODEL, D_MODEL))
        params.append(p)
        xs, sv = layer_fwd(xs, p, tabs, l)
        saved.append(sv)
    loss_part, dx, g_final = loss_head(xs, final_norm_w[None], loss_target[0], tm=TM_ROW // 2, name="loss_head")

    layer_grads, scatters = [None] * L, {}

    def send(kind, l, grad, slots):
        land = _landing(grad, slots, by_rows=True, name=f"scatter_own_{kind}_{l}")
        *flight, token = push_start([grad], [land], by_rows=True, name=f"scatter_start_{kind}_{l}")
        scatters[kind, l] = flight
        return token[0:1, 0:1]

    for l in reversed(range(L)):
        dx, layer_grads[l] = layer_bwd(dx, saved[l], params[l], tabs, l, functools.partial(send, "out", l, slots=out_slots),
                                       functools.partial(send, "in", l, slots=in_slots))

    stack = lambda name: jnp.stack([layer_grads[l][name] for l in range(L)])
    small = [loss_part] + [stack(n) for n in SMALL_REPLICATED] + [g_final[0], stack("conv_qkv_w"), stack("conf_dw_w")]
    small_shapes = [a.shape for a in small]
    r_pw, r_small = scatter_exchange([[layer_grads[l]["conf_pw_w"] for l in range(L)]], _pack(small), name="scatter_small")
    g = {}
    g["conf_pw_w"] = slot_sum(r_pw, name="sum_pw")
    summed = _unpack(slot_sum(r_small[:, None], name="sum_small")[0], small_shapes)
    loss = summed[0].reshape(())
    for n, a in zip(SMALL_REPLICATED, summed[1:1 + len(SMALL_REPLICATED)]):
        g[n] = a
    g["final_norm_w"] = summed[-3]
    g["conv_qkv_w"] = lax.dynamic_slice_in_dim(summed[-2], me * conv_qkv_w.shape[2], conv_qkv_w.shape[2], axis=2)
    g["conf_dw_w"] = lax.dynamic_slice_in_dim(summed[-1], me * conf_dw_w.shape[2], conf_dw_w.shape[2], axis=2)
    deltas, new_m, new_v = {}, {}, {}
    for n in WEIGHTS:
        if n not in ("w_in", "w_out"):
            deltas[n], new_m[n], new_v[n] = adam(w[n], g[n], m[n], v[n], name=f"adam_{n}")
    sums = {}
    order = [(kind, l) for l in reversed(range(L)) for kind in ("out", "in")]
    done_first = loss_part
    for kind, l in order:
        last = (kind, l) == order[-1]
        after = (done_first + deltas["a_log"][0:1, 0:1]) if last else dx
        land, = push_wait(*scatters[kind, l], after, by_rows=True, name=f"scatter_wait_{kind}_{l}")
        sums[kind, l] = slot_sum(land[:, None], name=f"sum_w_{kind}_{l}")
        if not last:
            done_first = done_first + sums[kind, l][0, 0:1, 0:1]
    g["w_in"] = _from_padded_cols(jnp.concatenate([sums["in", l] for l in range(L)], axis=0))
    g["w_out"] = jnp.concatenate([sums["out", l] for l in range(L)], axis=0)
    for n in ("w_in", "w_out"):
        deltas[n], new_m[n], new_v[n] = adam(w[n], g[n], m[n], v[n], name=f"adam_{n}")
    return (loss, dx[None], *[g[n] for n in WEIGHTS], *[deltas[n] for n in WEIGHTS],
            *[new_m[n] for n in WEIGHTS], *[new_v[n] for n in WEIGHTS])
```

```python
import functools
import math

import jax
import jax.numpy as jnp
from jax import lax
from jax.experimental import pallas as pl
from jax.experimental.pallas import tpu as pltpu

D_MODEL = 2048
DEPTH = 4
N_DEV = 8
GDN_DK = 128
GDN_HEADS = 6
GDN_W = 768
ATT_HD = 64
ATT_HEADS = 12
ATT_W = 768
CONV_CH = 512
CONV_WIDTH = 31
SHORT_CONV = 4
GDN_CHUNK = 64
ROPE_THETA = 500000.0
ROPE_DIM = 16
DIL_PATTERNS = ((128, 1), (512, 4), (2048, 16))
ATT_BLOCK = 128
NEG_INF = -1e30
IN_W = 7692

ADAM_LR = 0.001
ADAM_B1 = 0.9
ADAM_B2 = 0.999
ADAM_EPS = 1e-08
ADAM_WD = 0.01
ADAM_STEP = 10

C_CA, C_CB, C_CG = 0, 512, 1024
C_GQ, C_GK, C_GV, C_GZ = 1536, 2304, 3072, 3840
C_AQ, C_AK, C_AV, C_AG = 4608, 5376, 6144, 6912
C_BETA, C_ALPHA = 7680, 7808
IN_PAD = 8192

VMEM_LIMIT = 56 * 1024 * 1024
CONF_HALO = 32
GDN_HALO = 8
GDN_GROUP = 4
GDN_HEADS_PER_STEP = 6

F32 = jnp.float32
BF16 = jnp.bfloat16
HI = lax.Precision.HIGHEST


def _cparams(sem, vmem=VMEM_LIMIT):
    return pltpu.CompilerParams(dimension_semantics=sem, vmem_limit_bytes=vmem)


def _dg(a, b, ca, cb, prec):
    nb = a.ndim - 2
    batch = tuple(range(nb))
    dn = (((ca + nb,), (cb + nb,)), (batch, batch))
    if prec == "bf16":
        return lax.dot_general(a.astype(BF16), b.astype(BF16), dn, preferred_element_type=F32)
    if prec == "bf16x3":
        ah, bh = a.astype(BF16), b.astype(BF16)
        al, bl = (a - ah.astype(F32)).astype(BF16), (b - bh.astype(F32)).astype(BF16)
        dot = lambda x, y: lax.dot_general(x, y, dn, preferred_element_type=F32)
        return dot(ah, bh) + (dot(ah, bl) + dot(al, bh))
    return lax.dot_general(a.astype(F32), b.astype(F32), dn, precision=HI, preferred_element_type=F32)


def _nn_raw(a, b, prec):
    return _dg(a, b, 1, 0, prec)


def _nt_raw(a, b, prec):
    return _dg(a, b, 1, 1, prec)


def _tn_raw(a, b, prec):
    return _dg(a, b, 0, 0, prec)


@functools.partial(jax.custom_vjp, nondiff_argnums=(2,))
def mm_nn(a, b, prec="bf16"):
    return _nn_raw(a, b, prec)


def _mm_nn_f(a, b, prec):
    return _nn_raw(a, b, prec), (a, b)


def _mm_nn_b(prec, res, g):
    a, b = res
    return _nt_raw(g, b, prec).astype(a.dtype), _tn_raw(a, g, prec).astype(b.dtype)


mm_nn.defvjp(_mm_nn_f, _mm_nn_b)


@functools.partial(jax.custom_vjp, nondiff_argnums=(2,))
def mm_nt(a, b, prec="bf16"):
    return _nt_raw(a, b, prec)


def _mm_nt_f(a, b, prec):
    return _nt_raw(a, b, prec), (a, b)


def _mm_nt_b(prec, res, g):
    a, b = res
    return _nn_raw(g, b, prec).astype(a.dtype), _tn_raw(g, a, prec).astype(b.dtype)


mm_nt.defvjp(_mm_nt_f, _mm_nt_b)


@functools.partial(jax.custom_vjp, nondiff_argnums=(2,))
def mm_tn(a, b, prec="bf16"):
    return _tn_raw(a, b, prec)


def _mm_tn_f(a, b, prec):
    return _tn_raw(a, b, prec), (a, b)


def _mm_tn_b(prec, res, g):
    a, b = res
    return _nt_raw(b, g, prec).astype(a.dtype), _nn_raw(a, g, prec).astype(b.dtype)


mm_tn.defvjp(_mm_tn_f, _mm_tn_b)


def _sigmoid(x):
    return 1.0 / (1.0 + jnp.exp(-x))


def _silu(x):
    return x * _sigmoid(x)


def _softplus(x):
    return jnp.maximum(x, 0.0) + jnp.log(1.0 + jnp.exp(-jnp.abs(x)))


def matmul(a, b, *, mode, tm, tn, tk, out_dtype=F32, residual=None, after=None, name):
    if mode == "tn":
        K, M = a.shape
    else:
        M, K = a.shape
    N = b.shape[0] if mode == "nt" else b.shape[1]
    assert M % tm == 0 and N % tn == 0 and K % tk == 0, (a.shape, b.shape, tm, tn, tk)
    nk = K // tk
    a_spec = pl.BlockSpec((tk, tm), lambda i, j, k: (k, i)) if mode == "tn" else pl.BlockSpec((tm, tk), lambda i, j, k: (i, k))
    b_spec = pl.BlockSpec((tn, tk), lambda i, j, k: (j, k)) if mode == "nt" else pl.BlockSpec((tk, tn), lambda i, j, k: (k, j))
    o_spec = pl.BlockSpec((tm, tn), lambda i, j, k: (i, j))
    raw = {"nn": _nn_raw, "nt": _nt_raw, "tn": _tn_raw}[mode]
    has_res = residual is not None

    def body(*refs):
        a_ref, b_ref = refs[:2]
        r_ref = refs[2] if has_res else None
        o_ref, acc_ref = refs[-2:]
        k = pl.program_id(2)
        part = raw(a_ref[...], b_ref[...], "bf16")

        @pl.when(k == 0)
        def _():
            acc_ref[...] = part

        @pl.when(k > 0)
        def _():
            acc_ref[...] += part

        @pl.when(k == nk - 1)
        def _():
            r = acc_ref[...]
            if has_res:
                r = r + r_ref[...].astype(F32)
            o_ref[...] = r.astype(out_dtype)

    in_specs = [a_spec, b_spec] + ([o_spec] if has_res else []) + ([ANY] if after is not None else [])
    args = (a, b) + ((residual,) if has_res else ()) + ((after,) if after is not None else ())
    return pl.pallas_call(
        body, name=name, grid=(M // tm, N // tn, nk), in_specs=in_specs, out_specs=o_spec,
        out_shape=jax.ShapeDtypeStruct((M, N), out_dtype),
        scratch_shapes=[pltpu.VMEM((tm, tn), F32)],
        compiler_params=_cparams(("parallel", "parallel", "arbitrary")),
    )(*args)


def _rms_fn(x, w, eps=1e-6):
    return x * lax.rsqrt(jnp.mean(x * x, axis=-1, keepdims=True) + eps) * w


def rms_fwd(x, w, *, tm, name):
    S, D = x.shape

    def body(x_ref, w_ref, o_ref):
        o_ref[...] = _rms_fn(x_ref[...], w_ref[...]).astype(BF16)

    return pl.pallas_call(
        body, name=name, grid=(S // tm,),
        in_specs=[pl.BlockSpec((tm, D), lambda i: (i, 0)), pl.BlockSpec((1, D), lambda i: (0, 0))],
        out_specs=pl.BlockSpec((tm, D), lambda i: (i, 0)),
        out_shape=jax.ShapeDtypeStruct((S, D), BF16),
        compiler_params=_cparams(("parallel",)),
    )(x, w)


def rms_bwd(x, w, dh, dres, *, tm, name):
    S, D = x.shape

    def body(x_ref, w_ref, dh_ref, dr_ref, dx_ref, dw_ref):
        _, vjp = jax.vjp(_rms_fn, x_ref[...], w_ref[...])
        dx, dw = vjp(dh_ref[...].astype(F32))
        dx_ref[...] = dx + dr_ref[...]

        @pl.when(pl.program_id(0) == 0)
        def _():
            dw_ref[...] = jnp.zeros_like(dw_ref)

        dw_ref[...] += dw

    row = pl.BlockSpec((tm, D), lambda i: (i, 0))
    vec = pl.BlockSpec((1, D), lambda i: (0, 0))
    return pl.pallas_call(
        body, name=name, grid=(S // tm,), in_specs=[row, vec, row, row], out_specs=[row, vec],
        out_shape=[jax.ShapeDtypeStruct((S, D), F32), jax.ShapeDtypeStruct((1, D), F32)],
        compiler_params=_cparams(("arbitrary",)),
    )(x, w, dh, dres)


def _fill_ext(ext_ref, halo, tile, first, H):
    ext_ref[pl.ds(0, H), :] = jnp.where(first, 0.0, halo)
    ext_ref[pl.ds(H, tile.shape[0]), :] = tile


def _conv_taps(ext_ref, w_ref, K, H, tm):
    assert H >= 8 * ((K - 1) // 8 + 1)
    total = None
    for b in range(min(8, K)):
        y = None
        for a in range((K - 1 - b) // 8 + 1):
            term = ext_ref[pl.ds(H - 8 - 8 * a, tm + 8), :] * w_ref[pl.ds(K - 1 - 8 * a - b, 1), :]
            y = term if y is None else y + term
        y = y if b == 0 else pltpu.roll(y, b, 0)
        total = y if total is None else total + y
    return total[8:, :]


def _halo_spec(H, tm, cw, col):
    return pl.BlockSpec((H, cw), lambda *g, _c=col: (jnp.maximum(g[-1] * (tm // H) - 1, 0), _c))


def conv_bwd(dc, srcs, w, du, du_col, *, K, H, tm, cw, glu, name):
    S, C = dc.shape
    nc, nt = C // cw, S // tm
    last_halo = S // H - 1
    n_src = 2 if glu else 1
    bases = [c0 // cw for _, c0 in srcs]

    def body(*refs):
        dc_ref, dcn_ref = refs[0], refs[1]
        src_refs = refs[2:2 + 2 * n_src]
        w_ref = refs[2 + 2 * n_src]
        out_ref, dw_ref, ext_ref, dext_ref = refs[4 + 2 * n_src:]
        i = pl.program_id(1)
        first, last = i == 0, i == nt - 1
        if glu:
            a_ref, ah_ref, b_ref, bh_ref = src_refs
            sg = _sigmoid(b_ref[...])
            _fill_ext(ext_ref, ah_ref[...] * _sigmoid(bh_ref[...]), a_ref[...] * sg, first, H)
        else:
            x_ref, xh_ref = src_refs
            _fill_ext(ext_ref, xh_ref[...], x_ref[...], first, H)
        dc_t = dc_ref[...]
        dext_ref[pl.ds(0, tm), :] = dc_t
        dext_ref[pl.ds(tm, H), :] = jnp.where(last, 0.0, dcn_ref[...])
        N = tm + 8
        dx = None
        for b in range(min(8, K)):
            z = None
            for a_ in range((K - 1 - b) // 8 + 1):
                term = dext_ref[pl.ds(8 * a_, N), :] * w_ref[pl.ds(K - 1 - 8 * a_ - b, 1), :]
                z = term if z is None else z + term
            z = z if b == 0 else pltpu.roll(z, N - b, 0)
            dx = z if dx is None else dx + z
        dx = dx[:tm, :]
        if glu:
            a = a_ref[...]
            out_ref[:, :cw] = (dx * sg).astype(BF16)
            out_ref[:, cw:] = (dx * a * sg * (1.0 - sg)).astype(BF16)
        else:
            out_ref[...] = dx.astype(BF16)

        @pl.when(first)
        def _():
            dw_ref[...] = jnp.zeros_like(dw_ref)

        dpad = jnp.concatenate([jnp.zeros((8, cw), F32), dc_t], axis=0)
        for b in range(min(8, K)):
            shifted = dpad if b == 0 else pltpu.roll(dpad, N - b, 0)
            for a_ in range((K - 1 - b) // 8 + 1):
                k = K - 1 - 8 * a_ - b
                dw_ref[pl.ds(k, 1), :] += jnp.sum(shifted * ext_ref[pl.ds(H - 8 - 8 * a_, N), :], axis=0, keepdims=True)

    tile = lambda base: pl.BlockSpec((tm, cw), lambda j, i, _b=base: (i, _b + j))
    halo = lambda base: pl.BlockSpec((H, cw), lambda j, i, _b=base: (jnp.maximum(i * (tm // H) - 1, 0), _b + j))
    in_specs = [tile(0), pl.BlockSpec((H, cw), lambda j, i: (jnp.minimum((i + 1) * (tm // H), last_halo), j))]
    args = [dc, dc]
    for (arr, _), base in zip(srcs, bases):
        in_specs += [tile(base), halo(base)]
        args += [arr, arr]
    in_specs += [pl.BlockSpec((K, cw), lambda j, i: (0, j)), ANY]
    args += [w, du]
    ow = n_src * cw
    out_specs = [pl.BlockSpec((tm, ow), lambda j, i: (i, du_col // ow + j)), pl.BlockSpec((K, cw), lambda j, i: (0, j))]
    out_shape = [jax.ShapeDtypeStruct(du.shape, du.dtype), jax.ShapeDtypeStruct((K, C), F32)]
    return pl.pallas_call(
        body, name=name, grid=(nc, nt), in_specs=in_specs, out_specs=out_specs, out_shape=out_shape,
        input_output_aliases={len(args) - 1: 0},
        scratch_shapes=[pltpu.VMEM((tm + H, cw), F32), pltpu.VMEM((tm + H, cw), F32)],
        compiler_params=_cparams(("parallel", "arbitrary")),
    )(*args)


def _conf_post(c, gate, ln_w, ln_b, pw):
    mu = jnp.mean(c, axis=-1, keepdims=True)
    cc = c - mu
    var = jnp.mean(cc * cc, axis=-1, keepdims=True)
    hn = cc * lax.rsqrt(var + 1e-5) * ln_w + ln_b
    return mm_nn(_silu(hn), pw) * _silu(gate)


def _conf_specs(tm):
    H = CONF_HALO
    blk = lambda col: pl.BlockSpec((tm, CONV_CH), lambda i, _c=col: (i, _c))
    vec = pl.BlockSpec((1, CONV_CH), lambda i: (0, 0))
    specs = [blk(0), blk(1), blk(2), _halo_spec(H, tm, CONV_CH, 0), _halo_spec(H, tm, CONV_CH, 1),
             pl.BlockSpec((CONV_WIDTH, CONV_CH), lambda i: (0, 0)), vec, vec, vec,
             pl.BlockSpec((CONV_CH, CONV_CH), lambda i: (0, 0))]
    return specs, blk, vec


def _conf_conv(a_ref, b_ref, ah_ref, bh_ref, dww_ref, dwb_ref, ext_ref, tm):
    first = pl.program_id(0) == 0
    _fill_ext(ext_ref, ah_ref[...] * _sigmoid(bh_ref[...]), a_ref[...] * _sigmoid(b_ref[...]), first, CONF_HALO)
    return _conv_taps(ext_ref, dww_ref, CONV_WIDTH, CONF_HALO, tm) + dwb_ref[...]


def conf_fwd(u, dw_w, dw_b, ln_w, ln_b, pw, *, tm, name):
    S = u.shape[0]
    specs, blk, vec = _conf_specs(tm)

    def body(a_ref, b_ref, g_ref, ah_ref, bh_ref, dww_ref, dwb_ref, lnw_ref, lnb_ref, pw_ref, y_ref, ext_ref):
        c = _conf_conv(a_ref, b_ref, ah_ref, bh_ref, dww_ref, dwb_ref, ext_ref, tm)
        y_ref[...] = _conf_post(c, g_ref[...], lnw_ref[...], lnb_ref[...], pw_ref[...]).astype(BF16)

    return pl.pallas_call(
        body, name=name, grid=(S // tm,), in_specs=specs, out_specs=blk(0),
        out_shape=jax.ShapeDtypeStruct((S, CONV_CH), BF16),
        scratch_shapes=[pltpu.VMEM((tm + CONF_HALO, CONV_CH), F32)],
        compiler_params=_cparams(("parallel",)),
    )(u, u, u, u, u, dw_w, dw_b, ln_w, ln_b, pw)


def conf_bwd_post(u, dy, dw_w, dw_b, ln_w, ln_b, pw, du, *, tm, name):
    S = u.shape[0]
    specs, blk, vec = _conf_specs(tm)
    mat = pl.BlockSpec((CONV_CH, CONV_CH), lambda i: (0, 0))

    def body(a_ref, b_ref, g_ref, ah_ref, bh_ref, dww_ref, dwb_ref, lnw_ref, lnb_ref, pw_ref, dy_ref, du_in,
             dc_ref, dg_ref, dlnw_ref, dlnb_ref, dpw_ref, ddwb_ref, ext_ref):
        c = _conf_conv(a_ref, b_ref, ah_ref, bh_ref, dww_ref, dwb_ref, ext_ref, tm)
        _, vjp = jax.vjp(_conf_post, c, g_ref[...], lnw_ref[...], lnb_ref[...], pw_ref[...])
        dc, dg, dlnw, dlnb, dpw = vjp(dy_ref[...])
        dc_ref[...] = dc
        dg_ref[...] = dg.astype(BF16)

        @pl.when(pl.program_id(0) == 0)
        def _():
            dlnw_ref[...] = jnp.zeros_like(dlnw_ref)
            dlnb_ref[...] = jnp.zeros_like(dlnb_ref)
            dpw_ref[...] = jnp.zeros_like(dpw_ref)

            ddwb_ref[...] = jnp.zeros_like(ddwb_ref)

        dlnw_ref[...] += dlnw
        dlnb_ref[...] += dlnb
        dpw_ref[...] += dpw
        ddwb_ref[...] += jnp.sum(dc, axis=0, keepdims=True)

    return pl.pallas_call(
        body, name=name, grid=(S // tm,), in_specs=specs + [blk(0), ANY],
        out_specs=[blk(0), blk(C_CG // CONV_CH), vec, vec, mat, vec],
        out_shape=[jax.ShapeDtypeStruct((S, CONV_CH), F32), jax.ShapeDtypeStruct(du.shape, du.dtype),
                   jax.ShapeDtypeStruct((1, CONV_CH), F32), jax.ShapeDtypeStruct((1, CONV_CH), F32),
                   jax.ShapeDtypeStruct((CONV_CH, CONV_CH), F32), jax.ShapeDtypeStruct((1, CONV_CH), F32)],
        input_output_aliases={11: 1},
        scratch_shapes=[pltpu.VMEM((tm + CONF_HALO, CONV_CH), F32)],
        compiler_params=_cparams(("arbitrary",)),
    )(u, u, u, u, u, dw_w, dw_b, ln_w, ln_b, pw, dy, du)


def _iota2(shape, dim):
    return lax.broadcasted_iota(jnp.int32, shape, dim)


def _gdn_post(pre_q, pre_k, pre_v, b_in, a_in, a_log, dt_bias):
    tm = pre_q.shape[0]
    q, k, v = _silu(pre_q), _silu(pre_k), _silu(pre_v)
    qs, ks = [], []
    for h in range(GDN_HEADS):
        sl = slice(h * GDN_DK, (h + 1) * GDN_DK)
        qh, kh = q[:, sl], k[:, sl]
        qs.append(qh * lax.rsqrt(jnp.sum(qh * qh, axis=-1, keepdims=True) + 1e-6) * (GDN_DK ** -0.5))
        ks.append(kh * lax.rsqrt(jnp.sum(kh * kh, axis=-1, keepdims=True) + 1e-6))
    beta = _sigmoid(b_in)
    g = -jnp.exp(a_log) * _softplus(a_in + dt_bias)
    nb = tm // GDN_CHUNK
    tril = (_iota2((nb, GDN_CHUNK, GDN_CHUNK), 1) >= _iota2((nb, GDN_CHUNK, GDN_CHUNK), 2)).astype(F32)
    gc = mm_nn(tril, g.reshape(nb, GDN_CHUNK, 128), "f32").reshape(tm, 128)
    return jnp.concatenate(qs, axis=1), jnp.concatenate(ks, axis=1), v, beta, gc


def _gdn_prep_specs(tm):
    H = GDN_HALO
    blk = lambda col: pl.BlockSpec((tm, GDN_W), lambda i, _c=col: (i, _c))
    lane = lambda col: pl.BlockSpec((tm, 128), lambda i, _c=col: (i, _c))
    vec = pl.BlockSpec((1, 128), lambda i: (0, 0))
    q0 = C_GQ // GDN_W
    specs = [blk(q0), blk(q0 + 1), blk(q0 + 2),
             _halo_spec(H, tm, GDN_W, q0), _halo_spec(H, tm, GDN_W, q0 + 1), _halo_spec(H, tm, GDN_W, q0 + 2),
             lane(C_BETA // 128), lane(C_ALPHA // 128),
             pl.BlockSpec((SHORT_CONV, GDN_W), lambda i: (0, 0)), pl.BlockSpec((SHORT_CONV, GDN_W), lambda i: (0, 1)),
             pl.BlockSpec((SHORT_CONV, GDN_W), lambda i: (0, 2)), vec, vec]
    return specs, blk, lane, vec


def _gdn_pre(x_refs, h_refs, w_refs, ext_ref, tm):
    first = pl.program_id(0) == 0
    pres = []
    for x_ref, h_ref, w_ref in zip(x_refs, h_refs, w_refs):
        _fill_ext(ext_ref, h_ref[...], x_ref[...], first, GDN_HALO)
        pres.append(_conv_taps(ext_ref, w_ref, SHORT_CONV, GDN_HALO, tm))
    return pres


def gdn_prep_fwd(u, conv_w, a_log, dt_bias, *, tm, name):
    S = u.shape[0]
    specs, blk, lane, vec = _gdn_prep_specs(tm)

    def body(xq, xk, xv, hq, hk, hv, bi, ai, wq, wk, wv, al, db, q_ref, k_ref, v_ref, beta_ref, gc_ref, ext_ref):
        pres = _gdn_pre((xq, xk, xv), (hq, hk, hv), (wq, wk, wv), ext_ref, tm)
        q, k, v, beta, gc = _gdn_post(*pres, bi[...], ai[...], al[...], db[...])
        q_ref[...] = q
        k_ref[...] = k
        v_ref[...] = v
        beta_ref[...] = beta
        gc_ref[...] = gc

    wide = jax.ShapeDtypeStruct((S, GDN_W), F32)
    narrow = jax.ShapeDtypeStruct((S, 128), F32)
    return pl.pallas_call(
        body, name=name, grid=(S // tm,), in_specs=specs,
        out_specs=[blk(0), blk(0), blk(0), lane(0), lane(0)], out_shape=[wide, wide, wide, narrow, narrow],
        scratch_shapes=[pltpu.VMEM((tm + GDN_HALO, GDN_W), F32)],
        compiler_params=_cparams(("parallel",)),
    )(u, u, u, u, u, u, u, u, conv_w, conv_w, conv_w, a_log, dt_bias)


def gdn_prep_bwd(u, conv_w, a_log, dt_bias, dq, dk, dv, dbeta, dgc, du, *, tm, name):
    S = u.shape[0]
    specs, blk, lane, vec = _gdn_prep_specs(tm)
    tail = IN_PAD - C_BETA

    def body(xq, xk, xv, hq, hk, hv, bi, ai, wq, wk, wv, al, db, dq_ref, dk_ref, dv_ref, dbe_ref, dgc_ref, du_in,
             dpre_ref, du_ref, dal_ref, ddb_ref, ext_ref):
        pres = _gdn_pre((xq, xk, xv), (hq, hk, hv), (wq, wk, wv), ext_ref, tm)
        _, vjp = jax.vjp(_gdn_post, *pres, bi[...], ai[...], al[...], db[...])
        dpq, dpk, dpv, dbi, dai, dal, ddb = vjp((dq_ref[...], dk_ref[...], dv_ref[...], dbe_ref[...], dgc_ref[...]))
        dpre_ref[:, 0:GDN_W] = dpq
        dpre_ref[:, GDN_W:2 * GDN_W] = dpk
        dpre_ref[:, 2 * GDN_W:] = dpv
        du_ref[:, 0:128] = dbi.astype(BF16)
        du_ref[:, 128:256] = dai.astype(BF16)
        du_ref[:, 256:] = jnp.zeros((tm, tail - 256), BF16)

        @pl.when(pl.program_id(0) == 0)
        def _():
            dal_ref[...] = jnp.zeros_like(dal_ref)
            ddb_ref[...] = jnp.zeros_like(ddb_ref)

        dal_ref[...] += dal
        ddb_ref[...] += ddb

    n_in = len(specs) + 6
    return pl.pallas_call(
        body, name=name, grid=(S // tm,), in_specs=specs + [blk(0), blk(0), blk(0), lane(0), lane(0), ANY],
        out_specs=[pl.BlockSpec((tm, 3 * GDN_W), lambda i: (i, 0)), pl.BlockSpec((tm, tail), lambda i: (i, C_BETA // tail)),
                   vec, vec],
        out_shape=[jax.ShapeDtypeStruct((S, 3 * GDN_W), F32), jax.ShapeDtypeStruct(du.shape, du.dtype),
                   jax.ShapeDtypeStruct((1, 128), F32), jax.ShapeDtypeStruct((1, 128), F32)],
        input_output_aliases={n_in - 1: 1},
        scratch_shapes=[pltpu.VMEM((tm + GDN_HALO, GDN_W), F32)],
        compiler_params=_cparams(("arbitrary",)),
    )(u, u, u, u, u, u, u, u, conv_w, conv_w, conv_w, a_log, dt_bias, dq, dk, dv, dbeta, dgc, du)


def _lane_col(blk, h):
    return jnp.sum(jnp.where(_iota2(blk.shape, 1) == h, blk, 0.0), axis=1, keepdims=True)


@jax.custom_vjp
def _tri_inv(low):
    n = low.shape[-1]
    r, c = _iota2(low.shape, low.ndim - 2), _iota2(low.shape, low.ndim - 1)
    eye = (r == c).astype(F32)
    t = eye - jnp.where((r // 2 == c // 2) & (r > c), low, 0.0)
    s = 2
    while s < n:
        off = jnp.where((r // (2 * s) == c // (2 * s)) & (r // s > c // s), low, 0.0)
        prec = "bf16" if s <= 8 else "bf16x3"
        t = t - _nn_raw(t, _nn_raw(off, t, prec), prec)
        s *= 2
    return t


def _tri_inv_f(low):
    t = _tri_inv(low)
    return t, t


def _tri_inv_b(t, dt):
    d = -_nt_raw(_tn_raw(t, dt, "bf16x3"), t, "bf16x3")
    r, c = _iota2(d.shape, d.ndim - 2), _iota2(d.shape, d.ndim - 1)
    return (jnp.where(r > c, d, 0.0),)


_tri_inv.defvjp(_tri_inv_f, _tri_inv_b)


def _gdn_group(s0, q, k, v, z, beta_blk, gc_blk, nw, h0):
    C = GDN_CHUNK
    HP, R, _ = q.shape
    nb = R // C
    B = HP * nb
    q3, k3, v3 = (t.reshape(B, C, GDN_DK) for t in (q, k, v))
    b3 = jnp.stack([_lane_col(beta_blk, h0 + j) for j in range(HP)]).reshape(B, C, 1)
    g3 = jnp.stack([_lane_col(gc_blk, h0 + j) for j in range(HP)]).reshape(B, C, 1)
    r, c = _iota2((B, C, C), 1), _iota2((B, C, C), 2)
    causal, strict = r >= c, r > c
    g_t = gc_blk.T
    rows = [jnp.sum(jnp.where(_iota2((128, R), 0) == h0 + j, g_t, 0.0), axis=0, keepdims=True) for j in range(HP)]
    g_row = jnp.stack([rows[j][:, i * C:(i + 1) * C] for j in range(HP) for i in range(nb)])
    decay = jnp.where(causal, jnp.exp(jnp.where(causal, g3 - g_row, 0.0)), 0.0)
    low = jnp.where(strict, b3 * mm_nt(k3, k3) * decay, 0.0)
    t = _tri_inv(low)
    eg = jnp.exp(g3)
    four = lambda x: x.reshape((HP, nb) + x.shape[1:])
    w_v = four(mm_nn(t, v3 * b3))
    w_k = four(mm_nn(t, k3 * (b3 * eg)))
    qk = four(jnp.where(causal, mm_nt(q3, k3) * decay, 0.0))
    q_dec = four(q3 * eg)
    g_last = jnp.sum(jnp.where(_iota2((B, C, 1), 1) == C - 1, g3, 0.0), axis=1, keepdims=True)
    k_dec = four(k3 * jnp.exp(g_last - g3))
    e_last = four(jnp.exp(g_last))
    s, outs = s0, []
    for i in range(nb):
        v_new = w_v[:, i] - mm_nn(w_k[:, i], s)
        outs.append(mm_nn(q_dec[:, i], s) + mm_nn(qk[:, i], v_new))
        s = s * e_last[:, i] + mm_tn(k_dec[:, i], v_new)
    o = jnp.concatenate(outs, axis=1)
    y = o * lax.rsqrt(jnp.mean(o * o, axis=-1, keepdims=True) + 1e-6) * nw * _silu(z)
    return s, y


def _heads(ref, HP):
    return jnp.stack([ref[:, j * GDN_DK:(j + 1) * GDN_DK] for j in range(HP)])


def gdn_core_fwd(q, k, v, u, beta, gc, nw, *, name):
    S = q.shape[0]
    R = GDN_CHUNK * GDN_GROUP
    G = S // R
    HP = GDN_HEADS_PER_STEP
    W = HP * GDN_DK
    blk = pl.BlockSpec((R, W), lambda g, h: (g, h))
    lane = pl.BlockSpec((R, 128), lambda g, h: (g, 0))
    st = pl.BlockSpec((1, HP, GDN_DK, GDN_DK), lambda g, h: (g, h, 0, 0))

    def body(q_ref, k_ref, v_ref, z_ref, be_ref, gc_ref, nw_ref, y_ref, st_ref, s_ref):
        g, hs = pl.program_id(0), pl.program_id(1)
        s0 = jnp.where(g == 0, 0.0, s_ref[hs])
        st_ref[0] = s0
        s1, y = _gdn_group(s0, _heads(q_ref, HP), _heads(k_ref, HP), _heads(v_ref, HP), _heads(z_ref, HP), be_ref[...],
                           gc_ref[...], nw_ref[...], hs * HP)
        s_ref[hs] = s1
        for j in range(HP):
            y_ref[:, j * GDN_DK:(j + 1) * GDN_DK] = y[j].astype(BF16)

    return pl.pallas_call(
        body, name=name, grid=(G, GDN_HEADS // HP),
        in_specs=[blk, blk, blk, pl.BlockSpec((R, W), lambda g, h: (g, C_GZ // W + h)), lane, lane,
                  pl.BlockSpec((1, 128), lambda g, h: (0, 0))],
        out_specs=[blk, st],
        out_shape=[jax.ShapeDtypeStruct((S, GDN_W), BF16), jax.ShapeDtypeStruct((G, GDN_HEADS, GDN_DK, GDN_DK), F32)],
        scratch_shapes=[pltpu.VMEM((GDN_HEADS // HP, HP, GDN_DK, GDN_DK), F32)],
        compiler_params=_cparams(("arbitrary", "arbitrary")),
    )(q, k, v, u, beta, gc, nw)


def gdn_core_bwd(q, k, v, u, beta, gc, nw, states, dy, du, *, name):
    S = q.shape[0]
    R = GDN_CHUNK * GDN_GROUP
    G = S // R
    HP = GDN_HEADS_PER_STEP
    W = HP * GDN_DK
    blk = pl.BlockSpec((R, W), lambda g, h: (G - 1 - g, h))
    lane = pl.BlockSpec((R, 128), lambda g, h: (G - 1 - g, 0))
    vec = pl.BlockSpec((1, 128), lambda g, h: (0, 0))

    def body(q_ref, k_ref, v_ref, z_ref, be_ref, gc_ref, nw_ref, st_ref, *rest):
        dy_refs = rest[:HP]
        dq_ref, dk_ref, dv_ref, dz_ref, dbe_ref, dgc_ref, dnw_ref, ds_ref = rest[HP + 1:]
        g, hs = pl.program_id(0), pl.program_id(1)

        @pl.when(hs == 0)
        def _():
            dbe_ref[...] = jnp.zeros_like(dbe_ref)
            dgc_ref[...] = jnp.zeros_like(dgc_ref)

        @pl.when((hs == 0) & (g == 0))
        def _():
            dnw_ref[...] = jnp.zeros_like(dnw_ref)

        _, vjp = jax.vjp(functools.partial(_gdn_group, h0=hs * HP), st_ref[0], _heads(q_ref, HP), _heads(k_ref, HP),
                         _heads(v_ref, HP), _heads(z_ref, HP), be_ref[...], gc_ref[...], nw_ref[...])
        ds_in = jnp.where(g == 0, 0.0, ds_ref[hs])
        dy = jnp.stack([r[...] for r in dy_refs])
        ds0, dq, dk, dv, dz, dbe, dgc, dnw = vjp((ds_in, dy))
        ds_ref[hs] = ds0
        for j in range(HP):
            sl = slice(j * GDN_DK, (j + 1) * GDN_DK)
            dq_ref[:, sl] = dq[j]
            dk_ref[:, sl] = dk[j]
            dv_ref[:, sl] = dv[j]
            dz_ref[:, sl] = dz[j].astype(BF16)
        dbe_ref[...] += dbe
        dgc_ref[...] += dgc
        dnw_ref[...] += dnw

    wide = jax.ShapeDtypeStruct((S, GDN_W), F32)
    narrow = jax.ShapeDtypeStruct((S, 128), F32)
    return pl.pallas_call(
        body, name=name, grid=(G, GDN_HEADS // HP),
        in_specs=[blk, blk, blk, pl.BlockSpec((R, W), lambda g, h: (G - 1 - g, C_GZ // W + h)), lane, lane, vec,
                  pl.BlockSpec((1, HP, GDN_DK, GDN_DK), lambda g, h: (G - 1 - g, h, 0, 0))]
        + [pl.BlockSpec((R, GDN_DK), lambda g, h, _j=j: (G - 1 - g, CONV_CH // GDN_DK + h * HP + _j)) for j in range(HP)]
        + [ANY],
        out_specs=[blk, blk, blk, pl.BlockSpec((R, W), lambda g, h: (G - 1 - g, C_GZ // W + h)), lane, lane, vec],
        out_shape=[wide, wide, wide, jax.ShapeDtypeStruct(du.shape, du.dtype), narrow, narrow,
                   jax.ShapeDtypeStruct((1, 128), F32)],
        input_output_aliases={8 + HP: 3},
        scratch_shapes=[pltpu.VMEM((GDN_HEADS // HP, HP, GDN_DK, GDN_DK), F32)],
        compiler_params=_cparams(("arbitrary", "arbitrary")),
    )(q, k, v, u, beta, gc, nw, states, *([dy] * HP), du)


def rope_tables(S):
    half = ROPE_DIM // 2
    inv = ROPE_THETA ** (-jnp.arange(half, dtype=F32) / half)
    ang = jnp.arange(S, dtype=F32)[:, None] * inv[None, :]
    cos, sin = jnp.cos(ang), jnp.sin(ang)
    rest = ATT_HD - ROPE_DIM
    c = jnp.concatenate([cos, cos, jnp.ones((S, rest), F32)], axis=1)
    s1 = jnp.concatenate([-sin, jnp.zeros((S, ATT_HD - half), F32)], axis=1)
    s2 = jnp.concatenate([jnp.zeros((S, half), F32), sin, jnp.zeros((S, rest), F32)], axis=1)
    return tuple(jnp.tile(t, (1, 2)) for t in (c, s1, s2))


def _rope(x, c, s1, s2):
    half = ROPE_DIM // 2
    return x * c + pltpu.roll(x, ATT_W - half, 1) * s1 + pltpu.roll(x, half, 1) * s2


def _unrope(dy, c, s1, s2):
    half = ROPE_DIM // 2
    return dy * c + pltpu.roll(dy * s1, half, 1) + pltpu.roll(dy * s2, ATT_W - half, 1)


def att_prep_fwd(u, tables, *, tm, name):
    S = u.shape[0]
    blk = lambda col: pl.BlockSpec((tm, ATT_W), lambda i, _c=col: (i, _c))
    tab = pl.BlockSpec((tm, 128), lambda i: (i, 0))

    def body(q_ref, k_ref, v_ref, c_ref, s1_ref, s2_ref, qo_ref, ko_ref, vo_ref):
        reps = ATT_W // 128
        c, s1, s2 = (jnp.tile(t[...], (1, reps)) for t in (c_ref, s1_ref, s2_ref))
        qo_ref[...] = (_rope(q_ref[...], c, s1, s2) * (ATT_HD ** -0.5)).astype(BF16)
        ko_ref[...] = _rope(k_ref[...], c, s1, s2).astype(BF16)
        vo_ref[...] = v_ref[...].astype(BF16)

    out = jax.ShapeDtypeStruct((S, ATT_W), BF16)
    return pl.pallas_call(
        body, name=name, grid=(S // tm,),
        in_specs=[blk(C_AQ // ATT_W), blk(C_AK // ATT_W), blk(C_AV // ATT_W), tab, tab, tab],
        out_specs=[blk(0)] * 3, out_shape=[out] * 3, compiler_params=_cparams(("parallel",)),
    )(u, u, u, *tables)


def att_prep_bwd(dqs, dks, dvs, tables, du, *, tm, name):
    S = dqs[0].shape[0]
    blk = pl.BlockSpec((tm, ATT_W), lambda i: (i, 0))
    tab = pl.BlockSpec((tm, 128), lambda i: (i, 0))

    def body(*refs):
        dq, dk, dv = (refs[3 * j][...].astype(F32) + refs[3 * j + 1][...].astype(F32) + refs[3 * j + 2][...].astype(F32)
                      for j in range(3))
        c_ref, s1_ref, s2_ref, _, o_ref = refs[9:]
        reps = ATT_W // 128
        c, s1, s2 = (jnp.tile(t[...], (1, reps)) for t in (c_ref, s1_ref, s2_ref))
        o_ref[:, 0:ATT_W] = (_unrope(dq, c, s1, s2) * (ATT_HD ** -0.5)).astype(BF16)
        o_ref[:, ATT_W:2 * ATT_W] = _unrope(dk, c, s1, s2).astype(BF16)
        o_ref[:, 2 * ATT_W:] = dv.astype(BF16)

    return pl.pallas_call(
        body, name=name, grid=(S // tm,), in_specs=[blk] * 9 + [tab] * 3 + [ANY],
        out_specs=pl.BlockSpec((tm, 3 * ATT_W), lambda i: (i, C_AQ // (3 * ATT_W))),
        out_shape=jax.ShapeDtypeStruct(du.shape, du.dtype), input_output_aliases={12: 0},
        compiler_params=_cparams(("parallel",)),
    )(*dqs, *dks, *dvs, *tables, du)


def _band_masks():
    qi, ki = _iota2((ATT_BLOCK, ATT_BLOCK), 0), _iota2((ATT_BLOCK, ATT_BLOCK), 1)
    return qi <= ki, ki <= qi


def _pair_diag(x):
    first = _iota2(x.shape, 1) < ATT_HD
    zero = jnp.zeros_like(x)
    return jnp.concatenate([jnp.where(first, x, zero), jnp.where(first, zero, x)], axis=0)


def att_pattern_fwd(qr, kr, vb, dil, *, name):
    S = qr.shape[0]
    L = S // dil
    nb = L // ATT_BLOCK
    view = lambda t: t.reshape(L, dil * t.shape[1])
    cur = pl.BlockSpec((ATT_BLOCK, ATT_W), lambda r, n: (n, r))
    prev = pl.BlockSpec((ATT_BLOCK, ATT_W), lambda r, n: (jnp.maximum(n - 1, 0), r))

    def body(q_ref, kc_ref, kp_ref, vc_ref, vp_ref, o_ref, l_ref):
        has_prev = pl.program_id(1) > 0
        m_prev, m_cur = _band_masks()
        m_prev = m_prev & has_prev
        first = _iota2((ATT_BLOCK, 128), 1) < ATT_HD
        lane = _iota2((ATT_BLOCK, 128), 1)
        stats = jnp.zeros((ATT_BLOCK, 128), F32)
        pairs = range(ATT_HEADS // 2)
        sls = [slice(p * 128, (p + 1) * 128) for p in pairs]
        sps = [_nt_raw(q_ref[:, sl], _pair_diag(kp_ref[:, sl]), "bf16") for sl in sls]
        scs = [_nt_raw(q_ref[:, sl], _pair_diag(kc_ref[:, sl]), "bf16") for sl in sls]
        probs, inv_dens = [], []
        for p in pairs:
            pps, pcs, dens, lses = [], [], [], []
            for half in range(2):
                hs = slice(half * 128, (half + 1) * 128)
                sp_h, sc_h = jnp.where(m_prev, sps[p][:, hs], NEG_INF), jnp.where(m_cur, scs[p][:, hs], NEG_INF)
                m = jnp.maximum(jnp.max(sp_h, axis=1, keepdims=True), jnp.max(sc_h, axis=1, keepdims=True))
                pp, pc = jnp.exp(sp_h - m), jnp.exp(sc_h - m)
                den = jnp.sum(pp, axis=1, keepdims=True) + jnp.sum(pc, axis=1, keepdims=True)
                pps.append(pp.astype(BF16))
                pcs.append(pc.astype(BF16))
                dens.append(den)
                lses.append(m + jnp.log(den))
            probs.append((jnp.concatenate(pps, axis=1), jnp.concatenate(pcs, axis=1)))
            inv_dens.append(1.0 / jnp.where(first, dens[0], dens[1]))
            stats = jnp.where(lane == 2 * p, lses[0], jnp.where(lane == 2 * p + 1, lses[1], stats))
        outs = [_nn_raw(probs[p][0], _pair_diag(vp_ref[:, sls[p]]), "bf16")
                + _nn_raw(probs[p][1], _pair_diag(vc_ref[:, sls[p]]), "bf16") for p in pairs]
        for p in pairs:
            o_ref[:, sls[p]] = (outs[p] * inv_dens[p]).astype(BF16)
        l_ref[...] = stats

    narrow = pl.BlockSpec((ATT_BLOCK, 128), lambda r, n: (n, r))
    o, l = pl.pallas_call(
        body, name=name, grid=(dil, nb), in_specs=[cur, cur, prev, cur, prev], out_specs=[cur, narrow],
        out_shape=[jax.ShapeDtypeStruct((L, dil * ATT_W), BF16), jax.ShapeDtypeStruct((L, dil * 128), F32)],
        compiler_params=_cparams(("parallel", "arbitrary")),
    )(view(qr), view(kr), view(kr), view(vb), view(vb))
    return o.reshape(S, ATT_W), l.reshape(S, 128)


def _head_spread():
    return (_iota2((128, ATT_W), 1) // ATT_HD == _iota2((128, ATT_W), 0)).astype(F32)


def att_combine_fwd(os_, ls, u, *, tm, name):
    S = u.shape[0]
    blk = lambda col: pl.BlockSpec((tm, ATT_W), lambda i, _c=col: (i, _c))
    lane = pl.BlockSpec((tm, 128), lambda i: (i, 0))

    def body(o1, o2, o3, l1, l2, l3, g_ref, y_ref, o_ref, lse_ref):
        a, b, c = l1[...], l2[...], l3[...]
        m = jnp.maximum(jnp.maximum(a, b), c)
        ea, eb, ec = jnp.exp(a - m), jnp.exp(b - m), jnp.exp(c - m)
        den = ea + eb + ec
        spread = _head_spread()
        wa, wb, wc = (_nn_raw(e / den, spread, "bf16x3") for e in (ea, eb, ec))
        o = wa * o1[...].astype(F32) + wb * o2[...].astype(F32) + wc * o3[...].astype(F32)
        o_ref[...] = o
        lse_ref[...] = m + jnp.log(den)
        y_ref[...] = (o * _silu(g_ref[...])).astype(BF16)

    return pl.pallas_call(
        body, name=name, grid=(S // tm,), in_specs=[blk(0)] * 3 + [lane] * 3 + [blk(C_AG // ATT_W)],
        out_specs=[blk(0), blk(0), lane],
        out_shape=[jax.ShapeDtypeStruct((S, ATT_W), BF16), jax.ShapeDtypeStruct((S, ATT_W), F32),
                   jax.ShapeDtypeStruct((S, 128), F32)],
        compiler_params=_cparams(("parallel",)),
    )(*os_, *ls, u)


def att_combine_bwd(dy, o, u, du, *, tm, name):
    S = u.shape[0]
    cw = 256
    base = (CONV_CH + GDN_W) // cw
    blk = lambda col: pl.BlockSpec((tm, ATT_W), lambda i, _c=col: (i, _c))

    def body(dy0, dy1, dy2, o_ref, g_ref, du_in, do_ref, dg_ref, dl_ref):
        g, d, o = g_ref[...], jnp.concatenate([dy0[...], dy1[...], dy2[...]], axis=1), o_ref[...]
        sg = _sigmoid(g)
        d_o = d * (g * sg)
        do_ref[...] = d_o.astype(BF16)
        dg_ref[...] = (d * o * (sg * (1.0 + g * (1.0 - sg)))).astype(BF16)
        dl_ref[...] = _nt_raw(d_o * o, _head_spread(), "bf16x3")

    return pl.pallas_call(
        body, name=name, grid=(S // tm,),
        in_specs=[pl.BlockSpec((tm, cw), lambda i, _j=j: (i, base + _j)) for j in range(ATT_W // cw)]
        + [blk(0), blk(C_AG // ATT_W), ANY],
        out_specs=[blk(0), blk(C_AG // ATT_W), pl.BlockSpec((tm, 128), lambda i: (i, 0))],
        out_shape=[jax.ShapeDtypeStruct((S, ATT_W), BF16), jax.ShapeDtypeStruct(du.shape, du.dtype),
                   jax.ShapeDtypeStruct((S, 128), F32)],
        input_output_aliases={5: 1},
        compiler_params=_cparams(("parallel",)),
    )(dy, dy, dy, o, u, du)


def att_pattern_bwd(qr, kr, vb, do, delta, lse, dil, *, name):
    S = qr.shape[0]
    L = S // dil
    nb = L // ATT_BLOCK
    view = lambda t: t.reshape(L, dil * t.shape[1])
    cur = pl.BlockSpec((ATT_BLOCK, ATT_W), lambda r, n: (jnp.minimum(n, nb - 1), r))
    prev = pl.BlockSpec((ATT_BLOCK, ATT_W), lambda r, n: (jnp.maximum(n - 1, 0), r))
    narrow = pl.BlockSpec((ATT_BLOCK, 128), lambda r, n: (jnp.minimum(n, nb - 1), r))

    def body(q_ref, kc_ref, kp_ref, vc_ref, vp_ref, do_ref, dl_ref, l_ref, dq_ref, dk_ref, dv_ref, ck_ref, cv_ref):
        n = pl.program_id(1)

        @pl.when(n < nb)
        def _():
            m_prev, m_cur = _band_masks()
            m_prev = m_prev & (n > 0)
            m_prev2, m_cur2 = jnp.concatenate([m_prev, m_prev], axis=1), jnp.concatenate([m_cur, m_cur], axis=1)
            first = _iota2((ATT_BLOCK, 128), 1) < ATT_HD
            wide = (ATT_BLOCK, 128)
            halves = lambda a, b: jnp.concatenate([jnp.broadcast_to(a, wide), jnp.broadcast_to(b, wide)], axis=1)
            fold = lambda t: jnp.where(first, t[:ATT_BLOCK], t[ATT_BLOCK:])
            pairs = range(ATT_HEADS // 2)
            sls = [slice(p * 128, (p + 1) * 128) for p in pairs]
            qs, dos = [q_ref[:, sl] for sl in sls], [do_ref[:, sl] for sl in sls]
            kps, kcs, vps, vcs = ([_pair_diag(r[:, sl]) for sl in sls] for r in (kp_ref, kc_ref, vp_ref, vc_ref))
            s_p = [_nt_raw(qs[p], kps[p], "bf16") for p in pairs]
            s_c = [_nt_raw(qs[p], kcs[p], "bf16") for p in pairs]
            dp_p = [_nt_raw(dos[p], vps[p], "bf16") for p in pairs]
            dp_c = [_nt_raw(dos[p], vcs[p], "bf16") for p in pairs]
            pps, pcs, dsps, dscs = [], [], [], []
            for p in pairs:
                delta = halves(dl_ref[:, 2 * p:2 * p + 1], dl_ref[:, 2 * p + 1:2 * p + 2])
                lse2 = halves(l_ref[:, 2 * p:2 * p + 1], l_ref[:, 2 * p + 1:2 * p + 2])
                pp = jnp.where(m_prev2, jnp.exp(s_p[p] - lse2), 0.0)
                pc = jnp.where(m_cur2, jnp.exp(s_c[p] - lse2), 0.0)
                dsps.append((pp * (dp_p[p] - delta)).astype(BF16))
                dscs.append((pc * (dp_c[p] - delta)).astype(BF16))
                pps.append(pp.astype(BF16))
                pcs.append(pc.astype(BF16))
            dqs = [_nn_raw(dsps[p], kps[p], "bf16") + _nn_raw(dscs[p], kcs[p], "bf16") for p in pairs]
            dk_prev = [fold(_tn_raw(dsps[p], qs[p], "bf16")) for p in pairs]
            dv_prev = [fold(_tn_raw(pps[p], dos[p], "bf16")) for p in pairs]
            dk_cur = [fold(_tn_raw(dscs[p], qs[p], "bf16")) for p in pairs]
            dv_cur = [fold(_tn_raw(pcs[p], dos[p], "bf16")) for p in pairs]
            for p in pairs:
                dq_ref[:, sls[p]] = dqs[p].astype(BF16)

            @pl.when(n > 0)
            def _():
                for p in pairs:
                    dk_ref[:, sls[p]] = (ck_ref[:, sls[p]] + dk_prev[p]).astype(BF16)
                    dv_ref[:, sls[p]] = (cv_ref[:, sls[p]] + dv_prev[p]).astype(BF16)

            for p in pairs:
                ck_ref[:, sls[p]] = dk_cur[p]
                cv_ref[:, sls[p]] = dv_cur[p]

        @pl.when(n == nb)
        def _():
            dk_ref[...] = ck_ref[...].astype(BF16)
            dv_ref[...] = cv_ref[...].astype(BF16)

    out = jax.ShapeDtypeStruct((L, dil * ATT_W), BF16)
    dq, dk, dv = pl.pallas_call(
        body, name=name, grid=(dil, nb + 1), in_specs=[cur, cur, prev, cur, prev, cur, narrow, narrow],
        out_specs=[cur, prev, prev], out_shape=[out, out, out],
        scratch_shapes=[pltpu.VMEM((ATT_BLOCK, ATT_W), F32), pltpu.VMEM((ATT_BLOCK, ATT_W), F32)],
        compiler_params=_cparams(("arbitrary", "arbitrary")),
    )(view(qr), view(kr), view(kr), view(vb), view(vb), view(do), view(delta), view(lse))
    return dq.reshape(S, ATT_W), dk.reshape(S, ATT_W), dv.reshape(S, ATT_W)


def _loss_rows(x, w, tgt):
    err = _rms_fn(x, w) - tgt
    return jnp.sum(0.5 * jnp.mean(err * err, axis=-1, keepdims=True), axis=0, keepdims=True)


def loss_head(x, w, tgt, *, tm, name):
    S, D = x.shape

    def body(x_ref, w_ref, t_ref, l_ref, dx_ref, dw_ref):
        val, vjp = jax.vjp(_loss_rows, x_ref[...], w_ref[...], t_ref[...])
        dx, dw, _ = vjp(jnp.ones((1, 1), F32))
        dx_ref[...] = dx

        @pl.when(pl.program_id(0) == 0)
        def _():
            l_ref[...] = jnp.zeros_like(l_ref)
            dw_ref[...] = jnp.zeros_like(dw_ref)

        l_ref[...] += val
        dw_ref[...] += dw

    row = pl.BlockSpec((tm, D), lambda i: (i, 0))
    vec = pl.BlockSpec((1, D), lambda i: (0, 0))
    one = pl.BlockSpec((1, 1), lambda i: (0, 0))
    return pl.pallas_call(
        body, name=name, grid=(S // tm,), in_specs=[row, vec, row], out_specs=[one, row, vec],
        out_shape=[jax.ShapeDtypeStruct((1, 1), F32), jax.ShapeDtypeStruct((S, D), F32), jax.ShapeDtypeStruct((1, D), F32)],
        compiler_params=_cparams(("arbitrary",)),
    )(x, w, tgt)


def adam(w, g, m, v, *, name):
    shape = w.shape
    C = shape[-1]
    R = w.size // C
    br = R
    while br * C * 4 > (1 << 21) and br % 16 == 0:
        br //= 2
    two = lambda t: t.reshape(R, C)

    def body(w_ref, g_ref, m_ref, v_ref, d_ref, mo_ref, vo_ref):
        gg = g_ref[...]
        m_new = ADAM_B1 * m_ref[...] + (1.0 - ADAM_B1) * gg
        v_new = ADAM_B2 * v_ref[...] + (1.0 - ADAM_B2) * jnp.square(gg)
        m_hat = m_new / (1.0 - ADAM_B1 ** ADAM_STEP)
        v_hat = v_new / (1.0 - ADAM_B2 ** ADAM_STEP)
        d_ref[...] = -ADAM_LR * (m_hat / (jnp.sqrt(v_hat) + ADAM_EPS) + ADAM_WD * w_ref[...])
        mo_ref[...] = m_new
        vo_ref[...] = v_new

    blk = pl.BlockSpec((br, C), lambda i: (i, 0))
    out = jax.ShapeDtypeStruct((R, C), F32)
    d, mo, vo = pl.pallas_call(
        body, name=name, grid=(R // br,), in_specs=[blk] * 4, out_specs=[blk] * 3, out_shape=[out] * 3,
        compiler_params=_cparams(("parallel",)),
    )(two(w), two(g), two(m), two(v))
    return d.reshape(shape), mo.reshape(shape), vo.reshape(shape)


MESH_IDS = pl.DeviceIdType.MESH
ANY = pl.BlockSpec(memory_space=pl.ANY)


def _my_id():
    return 4 * lax.axis_index("x") + 2 * lax.axis_index("y") + lax.axis_index("c")


def _peer(k):
    x, y, c = lax.axis_index("x"), lax.axis_index("y"), lax.axis_index("c")
    flip = lambda v, bit: 1 - v if bit else v
    return (flip(x, k & 4), flip(y, k & 2), flip(c, k & 1))


def all_gather_two_level(arrs, *, name):
    n = len(arrs)

    def body(*refs):
        ins, outs = refs[:n], refs[n:2 * n]
        send, recv, local = refs[2 * n:]
        x, y, c = lax.axis_index("x"), lax.axis_index("y"), lax.axis_index("c")
        me, sibling = (x, y, c), (x, y, 1 - c)
        chips = [(1 - x, y), (x, 1 - y), (1 - x, 1 - y)]

        def slot(a, dev):
            return outs[a].at[4 * dev[0] + 2 * dev[1] + dev[2]]

        def copy(a, k, block, to, src=None):
            return pltpu.make_async_remote_copy(
                src_ref=slot(a, block) if src is None else src, dst_ref=slot(a, block), send_sem=send.at[a, k],
                recv_sem=recv.at[a, k], device_id=to, device_id_type=MESH_IDS)

        mine = [pltpu.make_async_copy(ins[a], slot(a, me), local.at[a]) for a in range(n)]
        first = [copy(a, 1 + j, me, (*chip, c), src=ins[a]) for a in range(n) for j, chip in enumerate(chips)]
        first += [copy(a, 0, me, sibling, src=ins[a]) for a in range(n)]
        for cp in first + mine:
            cp.start()
        passed = []
        for a in range(n):
            for j, chip in enumerate(chips):
                copy(a, 1 + j, (*chip, c), me).wait_recv()
                fwd = copy(a, 4 + j, (*chip, c), sibling)
                fwd.start()
                passed.append(fwd)
        for a in range(n):
            copy(a, 0, sibling, me).wait_recv()
            for j, chip in enumerate(chips):
                copy(a, 4 + j, (*chip, 1 - c), me).wait_recv()
        for cp in first + passed:
            cp.wait_send()
        for cp in mine:
            cp.wait()

    return pl.pallas_call(
        body, name=name, in_specs=[ANY] * n, out_specs=[ANY] * n,
        out_shape=[jax.ShapeDtypeStruct((N_DEV,) + a.shape, a.dtype) for a in arrs],
        scratch_shapes=[pltpu.SemaphoreType.DMA((n, N_DEV - 1)), pltpu.SemaphoreType.DMA((n, N_DEV - 1)),
                        pltpu.SemaphoreType.DMA((n,))],
        compiler_params=pltpu.CompilerParams(has_side_effects=True),
    )(*arrs)


def scatter_exchange(groups, pack, *, name):
    flat = [a for grp in groups for a in grp]
    n = len(flat) + 1
    shapes = [jax.ShapeDtypeStruct((N_DEV, len(grp), grp[0].shape[0] // N_DEV, grp[0].shape[1]), grp[0].dtype) for grp in groups]
    shapes.append(jax.ShapeDtypeStruct((N_DEV,) + pack.shape, pack.dtype))
    index = [(gi, li) for gi, grp in enumerate(groups) for li in range(len(grp))]

    def body(*refs):
        ins, outs = refs[:n], refs[n:n + len(shapes)]
        send, recv, local = refs[n + len(shapes):]
        me = _my_id()
        started = []
        for a in range(n):
            if a < n - 1:
                gi, li = index[a]
                r = ins[a].shape[0] // N_DEV
                src = lambda j, _a=a, _r=r: ins[_a].at[pl.ds(pl.multiple_of(j * _r, 8), _r), :]
                dst = outs[gi].at[me, li]
            else:
                src = lambda j, _a=a: ins[_a]
                dst = outs[-1].at[me]
            lc = pltpu.make_async_copy(src(me), dst, local.at[a])
            lc.start()
            started.append(lc)
            for k in range(1, N_DEV):
                cp = pltpu.make_async_remote_copy(src_ref=src(me ^ k), dst_ref=dst, send_sem=send.at[a, k - 1],
                                                  recv_sem=recv.at[a, k - 1], device_id=_peer(k), device_id_type=MESH_IDS)
                cp.start()
                started.append(cp)
        for cp in started:
            cp.wait()

    return pl.pallas_call(
        body, name=name, in_specs=[ANY] * n, out_specs=[ANY] * len(shapes), out_shape=shapes,
        scratch_shapes=[pltpu.SemaphoreType.DMA((n, N_DEV - 1)), pltpu.SemaphoreType.DMA((n, N_DEV - 1)),
                        pltpu.SemaphoreType.DMA((n,))],
        compiler_params=pltpu.CompilerParams(has_side_effects=True),
    )(*flat, pack)


def slot_sum(x, *, name):
    _, A, R, C = x.shape
    br = R
    while br * C * 4 * N_DEV > (1 << 23) and br % 16 == 0:
        br //= 2

    def body(x_ref, o_ref):
        acc = x_ref[0, 0].astype(F32)
        for s in range(1, N_DEV):
            acc = acc + x_ref[s, 0].astype(F32)
        o_ref[0] = acc

    return pl.pallas_call(
        body, name=name, grid=(A, R // br),
        in_specs=[pl.BlockSpec((N_DEV, 1, br, C), lambda a, i: (0, a, i, 0))],
        out_specs=pl.BlockSpec((1, br, C), lambda a, i: (a, i, 0)),
        out_shape=jax.ShapeDtypeStruct((A, R, C), F32),
        compiler_params=_cparams(("parallel", "parallel")),
    )(x)


HBM_SPEC = pl.BlockSpec(memory_space=pltpu.HBM)
SEM_SPEC = pl.BlockSpec(memory_space=pltpu.SEMAPHORE)
DATAFLOW = pltpu.SideEffectType.DATAFLOW_SIDE_EFFECTING


def _push_copies(src_refs, land_refs, send_sems, recv_sems, by_rows):
    me = _my_id()
    copies = []
    for a, (src, land) in enumerate(zip(src_refs, land_refs)):
        rows = land.shape[1]
        for k in range(1, N_DEV):
            piece = src.at[pl.ds(pl.multiple_of((me ^ k) * rows, 8), rows), :] if by_rows else src
            copies.append(pltpu.make_async_remote_copy(
                src_ref=piece, dst_ref=land.at[me], send_sem=send_sems[a].at[k - 1], recv_sem=recv_sems[a].at[k - 1],
                device_id=_peer(k), device_id_type=MESH_IDS))
    return copies


def push_start(srcs, lands, *, by_rows, name):
    n = len(srcs)

    def body(*refs):
        src_refs, land_refs = refs[:n], refs[n:2 * n]
        send_sems, recv_sems = refs[2 * n:3 * n], refs[3 * n:4 * n]
        token = refs[6 * n]
        for cp in _push_copies(src_refs, land_refs, send_sems, recv_sems, by_rows):
            cp.start()
        token[...] = jnp.zeros_like(token)

    sems = [pltpu.SemaphoreType.DMA((N_DEV - 1,))] * (2 * n)
    bufs = [pltpu.HBM(a.shape, a.dtype) for a in list(srcs) + list(lands)]
    outs = pl.pallas_call(
        body, name=name, out_shape=tuple(sems + bufs + [jax.ShapeDtypeStruct((8, 128), F32)]),
        in_specs=[HBM_SPEC] * (2 * n), out_specs=tuple([SEM_SPEC] * (2 * n) + [HBM_SPEC] * (2 * n) + [pl.BlockSpec(memory_space=pltpu.VMEM)]),
        input_output_aliases={i: 2 * n + i for i in range(2 * n)},
        compiler_params=pltpu.CompilerParams(has_side_effects=DATAFLOW),
    )(*[pltpu.with_memory_space_constraint(a, pltpu.HBM) for a in list(srcs) + list(lands)])
    return outs[:n], outs[n:2 * n], outs[2 * n:3 * n], outs[3 * n:4 * n], outs[4 * n]


def push_wait(send_sems, recv_sems, srcs, lands, after, *, by_rows, name):
    n = len(srcs)

    def body(*refs):
        src_refs, land_refs = refs[:n], refs[n:2 * n]
        send, recv = refs[2 * n:3 * n], refs[3 * n:4 * n]
        for cp in _push_copies(src_refs, land_refs, send, recv, by_rows):
            cp.wait_send()
            cp.wait_recv()

    outs = pl.pallas_call(
        body, name=name, out_shape=tuple(pltpu.HBM(a.shape, a.dtype) for a in list(srcs) + list(lands)),
        in_specs=[HBM_SPEC] * (2 * n) + [SEM_SPEC] * (2 * n) + [ANY], out_specs=tuple([HBM_SPEC] * (2 * n)),
        input_output_aliases={i: i for i in range(2 * n)},
        compiler_params=pltpu.CompilerParams(has_side_effects=DATAFLOW),
    )(*srcs, *lands, *send_sems, *recv_sems, after)
    return outs[n:]


LANDING_BLOCKS = 4


def _landing(src, slots_shape, after=None, *, by_rows, name):
    rows, cols = slots_shape[1], slots_shape[2]
    br = rows // LANDING_BLOCKS
    me = _my_id().astype(jnp.int32).reshape(1)

    def body(me_ref, src_ref, *rest):
        rest[-1][0] = src_ref[...]

    extra = [] if after is None else [after]
    grid_spec = pltpu.PrefetchScalarGridSpec(
        num_scalar_prefetch=1, grid=(LANDING_BLOCKS,),
        in_specs=[pl.BlockSpec((br, cols), lambda i, me_ref: ((me_ref[0] * LANDING_BLOCKS if by_rows else 0) + i, 0))]
        + [ANY] * len(extra),
        out_specs=pl.BlockSpec((1, br, cols), lambda i, me_ref: (me_ref[0], i, 0)))
    return pl.pallas_call(
        body, name=name, grid_spec=grid_spec, out_shape=jax.ShapeDtypeStruct(slots_shape, src.dtype),
    )(me, src, *extra)


def _pack(arrs):
    flat = []
    for a in arrs:
        f = a.reshape(-1).astype(F32)
        flat.append(jnp.pad(f, (0, (-f.size) % 128)))
    f = jnp.concatenate(flat)
    return jnp.pad(f, (0, (-f.size) % 1024)).reshape(-1, 128)


def _unpack(p, shapes):
    f = p.reshape(-1)
    out, off = [], 0
    for s in shapes:
        n = math.prod(s)
        out.append(f[off:off + n].reshape(s))
        off += n + (-n) % 128
    return out


def _to_padded_cols(w):
    z = lambda n: jnp.zeros(w.shape[:-1] + (n,), w.dtype)
    return jnp.concatenate([w[..., 0:4608], w[..., 4620:7692], w[..., 4608:4614], z(122), w[..., 4614:4620], z(378)], axis=-1)


def _from_padded_cols(w):
    return jnp.concatenate([w[..., 0:4608], w[..., C_BETA:C_BETA + 6], w[..., C_ALPHA:C_ALPHA + 6], w[..., 4608:7680]], axis=-1)


def _lane_pad(v):
    return jnp.pad(v, (0, 128 - v.shape[0]))[None, :]


TM_MM, TN_MM, TK_MM = 1024, 1024, 2048
TM_ROW = 512


def layer_fwd(x, p, tabs, l):
    h = rms_fwd(x, p["norm_w"], tm=TM_ROW, name=f"rms_fwd_{l}")
    u = matmul(h, p["w_in"], mode="nn", tm=TM_MM, tn=TN_MM, tk=TK_MM, name=f"in_proj_{l}")
    y_conv = conf_fwd(u, p["dw_w"], p["dw_b"], p["ln_w"], p["ln_b"], p["pw"], tm=TM_ROW, name=f"conf_fwd_{l}")
    q, k, v, beta, gc = gdn_prep_fwd(u, p["conv_w"], p["a_log"], p["dt_bias"], tm=TM_ROW, name=f"gdn_prep_fwd_{l}")
    y_gdn, states = gdn_core_fwd(q, k, v, u, beta, gc, p["gdn_nw"], name=f"gdn_core_fwd_{l}")
    qr, kr, vb = att_prep_fwd(u, tabs, tm=TM_ROW, name=f"att_prep_fwd_{l}")
    os_, ls = [], []
    for _, dil in DIL_PATTERNS:
        o_p, l_p = att_pattern_fwd(qr, kr, vb, dil, name=f"att_fwd_d{dil}_{l}")
        os_.append(o_p)
        ls.append(l_p)
    y_att, o, lse = att_combine_fwd(os_, ls, u, tm=TM_ROW, name=f"att_combine_fwd_{l}")
    y = jnp.concatenate([y_conv, y_gdn, y_att], axis=1)
    if callable(p["w_out"]):
        p["w_out"] = p["w_out"](y)
    x_new = matmul(y, p["w_out"], mode="nn", tm=TM_MM, tn=TN_MM, tk=TK_MM, residual=x, name=f"out_proj_{l}")
    saved = dict(x=x, h=h, u=u, y=y, q=q, k=k, v=v, beta=beta, gc=gc, states=states, qr=qr, kr=kr, vb=vb, o=o, lse=lse)
    return x_new, saved


def layer_bwd(dx_out, s, p, tabs, l, send_w_out=None, send_w_in=None):
    S = dx_out.shape[0]
    u = s["u"]
    dy = matmul(dx_out, p["w_out"], mode="nt", tm=TM_MM, tn=TN_MM, tk=TK_MM, name=f"out_proj_dy_{l}")
    g_w_out = matmul(s["y"], dx_out, mode="tn", tm=TM_MM, tn=TN_MM, tk=TK_MM, out_dtype=BF16, name=f"out_proj_dw_{l}")
    dw_b = p["dw_b"] if send_w_out is None else p["dw_b"] + send_w_out(g_w_out)
    du = lax.empty((S, IN_PAD), BF16)
    dc, du, g_ln_w, g_ln_b, g_pw, g_dw_b = conf_bwd_post(u, dy, p["dw_w"], dw_b, p["ln_w"], p["ln_b"], p["pw"], du,
                                                       tm=TM_ROW, name=f"conf_bwd_post_{l}")
    du, g_dw_w = conv_bwd(dc, [(u, C_CA), (u, C_CB)], p["dw_w"], du, C_CA, K=CONV_WIDTH, H=CONF_HALO, tm=TM_ROW, cw=CONV_CH,
                          glu=True, name=f"conf_bwd_conv_{l}")
    dq, dk, dv, du, dbeta, dgc, g_gdn_nw = gdn_core_bwd(s["q"], s["k"], s["v"], u, s["beta"], s["gc"], p["gdn_nw"],
                                                        s["states"], dy, du, name=f"gdn_core_bwd_{l}")
    dpre, du, g_a_log, g_dt_bias = gdn_prep_bwd(u, p["conv_w"], p["a_log"], p["dt_bias"], dq, dk, dv, dbeta, dgc, du,
                                                tm=TM_ROW, name=f"gdn_prep_bwd_{l}")
    du, g_conv_w = conv_bwd(dpre, [(u, C_GQ)], p["conv_w"], du, C_GQ, K=SHORT_CONV, H=GDN_HALO, tm=TM_ROW, cw=GDN_W,
                            glu=False, name=f"gdn_bwd_conv_{l}")
    do, du, delta = att_combine_bwd(dy, s["o"], u, du, tm=TM_ROW, name=f"att_combine_bwd_{l}")
    dqs, dks, dvs = [], [], []
    for _, dil in DIL_PATTERNS:
        a, b, c = att_pattern_bwd(s["qr"], s["kr"], s["vb"], do, delta, s["lse"], dil, name=f"att_bwd_d{dil}_{l}")
        dqs.append(a)
        dks.append(b)
        dvs.append(c)
    du = att_prep_bwd(dqs, dks, dvs, tabs, du, tm=TM_ROW, name=f"att_prep_bwd_{l}")
    g_w_in = matmul(s["h"], du, mode="tn", tm=TM_MM, tn=TN_MM, tk=TK_MM, out_dtype=BF16, name=f"in_proj_dw_{l}")
    sent = None if send_w_in is None else send_w_in(g_w_in)
    dh = matmul(du, p["w_in"], mode="nt", tm=TM_MM, tn=TN_MM, tk=TK_MM, after=sent, name=f"in_proj_dh_{l}")
    dx, g_norm_w = rms_bwd(s["x"], p["norm_w"], dh, dx_out, tm=TM_ROW // 2, name=f"rms_bwd_{l}")
    grads = dict(norm_w=g_norm_w[0], w_in=g_w_in, conv_qkv_w=g_conv_w, a_log=g_a_log[0, :GDN_HEADS], dt_bias=g_dt_bias[0, :GDN_HEADS],
                 gdn_norm_w=g_gdn_nw[0], conf_dw_w=g_dw_w, conf_dw_b=g_dw_b[0], conf_ln_w=g_ln_w[0], conf_ln_b=g_ln_b[0],
                 conf_pw_w=g_pw, w_out=g_w_out)
    return dx, grads


WEIGHTS = ("norm_w", "w_in", "conv_qkv_w", "a_log", "dt_bias", "gdn_norm_w", "conf_dw_w", "conf_dw_b", "conf_ln_w",
           "conf_ln_b", "conf_pw_w", "w_out", "final_norm_w")
SMALL_REPLICATED = ("norm_w", "a_log", "dt_bias", "gdn_norm_w", "conf_dw_b", "conf_ln_w", "conf_ln_b")


def kernel(x, norm_w, w_in, conv_qkv_w, a_log, dt_bias, gdn_norm_w, conf_dw_w, conf_dw_b, conf_ln_w, conf_ln_b, conf_pw_w, w_out, final_norm_w, loss_target, m_norm_w, m_w_in, m_conv_qkv_w, m_a_log, m_dt_bias, m_gdn_norm_w, m_conf_dw_w, m_conf_dw_b, m_conf_ln_w, m_conf_ln_b, m_conf_pw_w, m_w_out, m_final_norm_w, v_norm_w, v_w_in, v_conv_qkv_w, v_a_log, v_dt_bias, v_gdn_norm_w, v_conf_dw_w, v_conf_dw_b, v_conf_ln_w, v_conf_ln_b, v_conf_pw_w, v_w_out, v_final_norm_w):
    w = dict(norm_w=norm_w, w_in=w_in, conv_qkv_w=conv_qkv_w, a_log=a_log, dt_bias=dt_bias, gdn_norm_w=gdn_norm_w,
             conf_dw_w=conf_dw_w, conf_dw_b=conf_dw_b, conf_ln_w=conf_ln_w, conf_ln_b=conf_ln_b, conf_pw_w=conf_pw_w,
             w_out=w_out, final_norm_w=final_norm_w)
    m = dict(zip(WEIGHTS, (m_norm_w, m_w_in, m_conv_qkv_w, m_a_log, m_dt_bias, m_gdn_norm_w, m_conf_dw_w, m_conf_dw_b,
                           m_conf_ln_w, m_conf_ln_b, m_conf_pw_w, m_w_out, m_final_norm_w)))
    v = dict(zip(WEIGHTS, (v_norm_w, v_w_in, v_conv_qkv_w, v_a_log, v_dt_bias, v_gdn_norm_w, v_conf_dw_w, v_conf_dw_b,
                           v_conf_ln_w, v_conf_ln_b, v_conf_pw_w, v_w_out, v_final_norm_w)))
    S = x.shape[1]
    L = norm_w.shape[0]
    me = _my_id()

    small_shapes = [conv_qkv_w.shape, conf_dw_w.shape, conf_pw_w.shape]
    w_in_b, w_out_b = _to_padded_cols(w_in).astype(BF16), w_out.astype(BF16)
    in_slots, out_slots = (N_DEV,) + w_in_b.shape[1:], (N_DEV,) + w_out_b.shape[1:]
    g_in0, g_small = all_gather_two_level([w_in_b[0], _pack([conv_qkv_w, conf_dw_w, conf_pw_w])], name="gather_first")
    gathers, tie = {}, jnp.zeros((1, 1), F32)
    for l in range(L):
        srcs = [w_out_b[0]] if l == 0 else [w_in_b[l], w_out_b[l]]
        slots = [out_slots] if l == 0 else [in_slots, out_slots]
        lands = [_landing(a, sl, g_in0, by_rows=False, name=f"gather_own_{l}_{j}") for j, (a, sl) in enumerate(zip(srcs, slots))]
        *flight, token = push_start(srcs, lands, by_rows=False, name=f"gather_start_{l}")
        gathers[l] = flight
        tie = tie + token[0:1, 0:1]
    parts = [_unpack(g_small[s], small_shapes) for s in range(N_DEV)]
    conv_full = jnp.concatenate([pt[0] for pt in parts], axis=2)
    dw_full = jnp.concatenate([pt[1] for pt in parts], axis=2)
    pw_full = jnp.concatenate([pt[2] for pt in parts], axis=1)
    tabs = rope_tables(S)

    def layer_params(l, full_in, full_out):
        return dict(
            norm_w=norm_w[l][None], w_in=full_in.reshape(D_MODEL, IN_PAD), w_out=full_out,
            conv_w=conv_full[l], a_log=_lane_pad(a_log[l]), dt_bias=_lane_pad(dt_bias[l]), gdn_nw=gdn_norm_w[l][None],
            dw_w=dw_full[l], dw_b=conf_dw_b[l][None], ln_w=conf_ln_w[l][None], ln_b=conf_ln_b[l][None], pw=pw_full[l])

    xs = x[0]
    params, saved = [], []
    for l in range(L):
        if l == 0:
            late_out = lambda after: push_wait(*gathers[0], after, by_rows=False, name="gather_wait_0")[0].reshape(D_MODEL, D_MODEL)
            p = layer_params(0, g_in0, late_out)
            p["norm_w"] = p["norm_w"] + tie
        else:
            full_in, full_out = push_wait(*gathers[l], xs, by_rows=False, name=f"gather_wait_{l}")
            p = layer_params(l, full_in, full_out.reshape(D_MODEL, D_MODEL))
        params.append(p)
        xs, sv = layer_fwd(xs, p, tabs, l)
        saved.append(sv)
    loss_part, dx, g_final = loss_head(xs, final_norm_w[None], loss_target[0], tm=TM_ROW // 2, name="loss_head")

    layer_grads, scatters = [None] * L, {}

    def send(kind, l, grad, slots):
        land = _landing(grad, slots, by_rows=True, name=f"scatter_own_{kind}_{l}")
        *flight, token = push_start([grad], [land], by_rows=True, name=f"scatter_start_{kind}_{l}")
        scatters[kind, l] = flight
        return token[0:1, 0:1]

    for l in reversed(range(L)):
        dx, layer_grads[l] = layer_bwd(dx, saved[l], params[l], tabs, l, functools.partial(send, "out", l, slots=out_slots),
                                       functools.partial(send, "in", l, slots=in_slots))

    stack = lambda name: jnp.stack([layer_grads[l][name] for l in range(L)])
    small = [loss_part] + [stack(n) for n in SMALL_REPLICATED] + [g_final[0], stack("conv_qkv_w"), stack("conf_dw_w")]
    small_shapes = [a.shape for a in small]
    r_pw, r_small = scatter_exchange([[layer_grads[l]["conf_pw_w"] for l in range(L)]], _pack(small), name="scatter_small")
    g = {}
    g["conf_pw_w"] = slot_sum(r_pw, name="sum_pw")
    summed = _unpack(slot_sum(r_small[:, None], name="sum_small")[0], small_shapes)
    loss = summed[0].reshape(())
    for n, a in zip(SMALL_REPLICATED, summed[1:1 + len(SMALL_REPLICATED)]):
        g[n] = a
    g["final_norm_w"] = summed[-3]
    g["conv_qkv_w"] = lax.dynamic_slice_in_dim(summed[-2], me * conv_qkv_w.shape[2], conv_qkv_w.shape[2], axis=2)
    g["conf_dw_w"] = lax.dynamic_slice_in_dim(summed[-1], me * conf_dw_w.shape[2], conf_dw_w.shape[2], axis=2)
    deltas, new_m, new_v = {}, {}, {}
    for n in WEIGHTS:
        if n not in ("w_in", "w_out"):
            deltas[n], new_m[n], new_v[n] = adam(w[n], g[n], m[n], v[n], name=f"adam_{n}")
    sums = {}
    order = [(kind, l) for l in reversed(range(L)) for kind in ("out", "in")]
    done_first = loss_part
    for kind, l in order:
        last = (kind, l) == order[-1]
        after = (done_first + deltas["a_log"][0:1, 0:1]) if last else dx
        land, = push_wait(*scatters[kind, l], after, by_rows=True, name=f"scatter_wait_{kind}_{l}")
        sums[kind, l] = slot_sum(land[:, None], name=f"sum_w_{kind}_{l}")
        if not last:
            done_first = done_first + sums[kind, l][0, 0:1, 0:1]
    g["w_in"] = _from_padded_cols(jnp.concatenate([sums["in", l] for l in range(L)], axis=0))
    g["w_out"] = jnp.concatenate([sums["out", l] for l in range(L)], axis=0)
    for n in ("w_in", "w_out"):
        deltas[n], new_m[n], new_v[n] = adam(w[n], g[n], m[n], v[n], name=f"adam_{n}")
    return (loss, dx[None], *[g[n] for n in WEIGHTS], *[deltas[n] for n in WEIGHTS],
            *[new_m[n] for n in WEIGHTS], *[new_v[n] for n in WEIGHTS])
```

```python
import functools
import math

import jax
import jax.numpy as jnp
from jax import lax
from jax.experimental import pallas as pl
from jax.experimental.pallas import tpu as pltpu

D_MODEL = 2048
DEPTH = 4
N_DEV = 8
GDN_DK = 128
GDN_HEADS = 6
GDN_W = 768
ATT_HD = 64
ATT_HEADS = 12
ATT_W = 768
CONV_CH = 512
CONV_WIDTH = 31
SHORT_CONV = 4
GDN_CHUNK = 64
ROPE_THETA = 500000.0
ROPE_DIM = 16
DIL_PATTERNS = ((128, 1), (512, 4), (2048, 16))
ATT_BLOCK = 128
NEG_INF = -1e30
IN_W = 7692

ADAM_LR = 0.001
ADAM_B1 = 0.9
ADAM_B2 = 0.999
ADAM_EPS = 1e-08
ADAM_WD = 0.01
ADAM_STEP = 10

C_CA, C_CB, C_CG = 0, 512, 1024
C_GQ, C_GK, C_GV, C_GZ = 1536, 2304, 3072, 3840
C_AQ, C_AK, C_AV, C_AG = 4608, 5376, 6144, 6912
C_BETA, C_ALPHA = 7680, 7808
IN_PAD = 8192

VMEM_LIMIT = 56 * 1024 * 1024
CONF_HALO = 32
GDN_HALO = 8
GDN_GROUP = 4
GDN_HEADS_PER_STEP = 6

F32 = jnp.float32
BF16 = jnp.bfloat16
HI = lax.Precision.HIGHEST


def _cparams(sem, vmem=VMEM_LIMIT):
    return pltpu.CompilerParams(dimension_semantics=sem, vmem_limit_bytes=vmem)


def _dg(a, b, ca, cb, prec):
    nb = a.ndim - 2
    batch = tuple(range(nb))
    dn = (((ca + nb,), (cb + nb,)), (batch, batch))
    if prec == "bf16":
        return lax.dot_general(a.astype(BF16), b.astype(BF16), dn, preferred_element_type=F32)
    if prec == "bf16x3":
        ah, bh = a.astype(BF16), b.astype(BF16)
        al, bl = (a - ah.astype(F32)).astype(BF16), (b - bh.astype(F32)).astype(BF16)
        dot = lambda x, y: lax.dot_general(x, y, dn, preferred_element_type=F32)
        return dot(ah, bh) + (dot(ah, bl) + dot(al, bh))
    return lax.dot_general(a.astype(F32), b.astype(F32), dn, precision=HI, preferred_element_type=F32)


def _nn_raw(a, b, prec):
    return _dg(a, b, 1, 0, prec)


def _nt_raw(a, b, prec):
    return _dg(a, b, 1, 1, prec)


def _tn_raw(a, b, prec):
    return _dg(a, b, 0, 0, prec)


@functools.partial(jax.custom_vjp, nondiff_argnums=(2,))
def mm_nn(a, b, prec="bf16"):
    return _nn_raw(a, b, prec)


def _mm_nn_f(a, b, prec):
    return _nn_raw(a, b, prec), (a, b)


def _mm_nn_b(prec, res, g):
    a, b = res
    return _nt_raw(g, b, prec).astype(a.dtype), _tn_raw(a, g, prec).astype(b.dtype)


mm_nn.defvjp(_mm_nn_f, _mm_nn_b)


@functools.partial(jax.custom_vjp, nondiff_argnums=(2,))
def mm_nt(a, b, prec="bf16"):
    return _nt_raw(a, b, prec)


def _mm_nt_f(a, b, prec):
    return _nt_raw(a, b, prec), (a, b)


def _mm_nt_b(prec, res, g):
    a, b = res
    return _nn_raw(g, b, prec).astype(a.dtype), _tn_raw(g, a, prec).astype(b.dtype)


mm_nt.defvjp(_mm_nt_f, _mm_nt_b)


@functools.partial(jax.custom_vjp, nondiff_argnums=(2,))
def mm_tn(a, b, prec="bf16"):
    return _tn_raw(a, b, prec)


def _mm_tn_f(a, b, prec):
    return _tn_raw(a, b, prec), (a, b)


def _mm_tn_b(prec, res, g):
    a, b = res
    return _nt_raw(b, g, prec).astype(a.dtype), _nn_raw(a, g, prec).astype(b.dtype)


mm_tn.defvjp(_mm_tn_f, _mm_tn_b)


def _sigmoid(x):
    return 1.0 / (1.0 + jnp.exp(-x))


def _silu(x):
    return x * _sigmoid(x)


def _softplus(x):
    return jnp.maximum(x, 0.0) + jnp.log(1.0 + jnp.exp(-jnp.abs(x)))


def matmul(a, b, *, mode, tm, tn, tk, out_dtype=F32, residual=None, after=None, name):
    if mode == "tn":
        K, M = a.shape
    else:
        M, K = a.shape
    N = b.shape[0] if mode == "nt" else b.shape[1]
    assert M % tm == 0 and N % tn == 0 and K % tk == 0, (a.shape, b.shape, tm, tn, tk)
    nk = K // tk
    a_spec = pl.BlockSpec((tk, tm), lambda i, j, k: (k, i)) if mode == "tn" else pl.BlockSpec((tm, tk), lambda i, j, k: (i, k))
    b_spec = pl.BlockSpec((tn, tk), lambda i, j, k: (j, k)) if mode == "nt" else pl.BlockSpec((tk, tn), lambda i, j, k: (k, j))
    o_spec = pl.BlockSpec((tm, tn), lambda i, j, k: (i, j))
    raw = {"nn": _nn_raw, "nt": _nt_raw, "tn": _tn_raw}[mode]
    has_res = residual is not None

    def body(*refs):
        a_ref, b_ref = refs[:2]
        r_ref = refs[2] if has_res else None
        o_ref, acc_ref = refs[-2:]
        k = pl.program_id(2)
        part = raw(a_ref[...], b_ref[...], "bf16")

        @pl.when(k == 0)
        def _():
            acc_ref[...] = part

        @pl.when(k > 0)
        def _():
            acc_ref[...] += part

        @pl.when(k == nk - 1)
        def _():
            r = acc_ref[...]
            if has_res:
                r = r + r_ref[...].astype(F32)
            o_ref[...] = r.astype(out_dtype)

    in_specs = [a_spec, b_spec] + ([o_spec] if has_res else []) + ([ANY] if after is not None else [])
    args = (a, b) + ((residual,) if has_res else ()) + ((after,) if after is not None else ())
    return pl.pallas_call(
        body, name=name, grid=(M // tm, N // tn, nk), in_specs=in_specs, out_specs=o_spec,
        out_shape=jax.ShapeDtypeStruct((M, N), out_dtype),
        scratch_shapes=[pltpu.VMEM((tm, tn), F32)],
        compiler_params=_cparams(("parallel", "parallel", "arbitrary")),
    )(*args)


def _rms_fn(x, w, eps=1e-6):
    return x * lax.rsqrt(jnp.mean(x * x, axis=-1, keepdims=True) + eps) * w


def rms_fwd(x, w, *, tm, name):
    S, D = x.shape

    def body(x_ref, w_ref, o_ref):
        o_ref[...] = _rms_fn(x_ref[...], w_ref[...]).astype(BF16)

    return pl.pallas_call(
        body, name=name, grid=(S // tm,),
        in_specs=[pl.BlockSpec((tm, D), lambda i: (i, 0)), pl.BlockSpec((1, D), lambda i: (0, 0))],
        out_specs=pl.BlockSpec((tm, D), lambda i: (i, 0)),
        out_shape=jax.ShapeDtypeStruct((S, D), BF16),
        compiler_params=_cparams(("parallel",)),
    )(x, w)


def rms_bwd(x, w, dh, dres, *, tm, name):
    S, D = x.shape

    def body(x_ref, w_ref, dh_ref, dr_ref, dx_ref, dw_ref):
        _, vjp = jax.vjp(_rms_fn, x_ref[...], w_ref[...])
        dx, dw = vjp(dh_ref[...].astype(F32))
        dx_ref[...] = dx + dr_ref[...]

        @pl.when(pl.program_id(0) == 0)
        def _():
            dw_ref[...] = jnp.zeros_like(dw_ref)

        dw_ref[...] += dw

    row = pl.BlockSpec((tm, D), lambda i: (i, 0))
    vec = pl.BlockSpec((1, D), lambda i: (0, 0))
    return pl.pallas_call(
        body, name=name, grid=(S // tm,), in_specs=[row, vec, row, row], out_specs=[row, vec],
        out_shape=[jax.ShapeDtypeStruct((S, D), F32), jax.ShapeDtypeStruct((1, D), F32)],
        compiler_params=_cparams(("arbitrary",)),
    )(x, w, dh, dres)


def _fill_ext(ext_ref, halo, tile, first, H):
    ext_ref[pl.ds(0, H), :] = jnp.where(first, 0.0, halo)
    ext_ref[pl.ds(H, tile.shape[0]), :] = tile


def _conv_taps(ext_ref, w_ref, K, H, tm):
    assert H >= 8 * ((K - 1) // 8 + 1)
    total = None
    for b in range(min(8, K)):
        y = None
        for a in range((K - 1 - b) // 8 + 1):
            term = ext_ref[pl.ds(H - 8 - 8 * a, tm + 8), :] * w_ref[pl.ds(K - 1 - 8 * a - b, 1), :]
            y = term if y is None else y + term
        y = y if b == 0 else pltpu.roll(y, b, 0)
        total = y if total is None else total + y
    return total[8:, :]


def _halo_spec(H, tm, cw, col):
    return pl.BlockSpec((H, cw), lambda *g, _c=col: (jnp.maximum(g[-1] * (tm // H) - 1, 0), _c))


def conv_bwd(dc, srcs, w, du, du_col, *, K, H, tm, cw, glu, name):
    S, C = dc.shape
    nc, nt = C // cw, S // tm
    last_halo = S // H - 1
    n_src = 2 if glu else 1
    bases = [c0 // cw for _, c0 in srcs]

    def body(*refs):
        dc_ref, dcn_ref = refs[0], refs[1]
        src_refs = refs[2:2 + 2 * n_src]
        w_ref = refs[2 + 2 * n_src]
        out_ref, dw_ref, ext_ref, dext_ref = refs[4 + 2 * n_src:]
        i = pl.program_id(1)
        first, last = i == 0, i == nt - 1
        if glu:
            a_ref, ah_ref, b_ref, bh_ref = src_refs
            sg = _sigmoid(b_ref[...])
            _fill_ext(ext_ref, ah_ref[...] * _sigmoid(bh_ref[...]), a_ref[...] * sg, first, H)
        else:
            x_ref, xh_ref = src_refs
            _fill_ext(ext_ref, xh_ref[...], x_ref[...], first, H)
        dc_t = dc_ref[...]
        dext_ref[pl.ds(0, tm), :] = dc_t
        dext_ref[pl.ds(tm, H), :] = jnp.where(last, 0.0, dcn_ref[...])
        N = tm + 8
        dx = None
        for b in range(min(8, K)):
            z = None
            for a_ in range((K - 1 - b) // 8 + 1):
                term = dext_ref[pl.ds(8 * a_, N), :] * w_ref[pl.ds(K - 1 - 8 * a_ - b, 1), :]
                z = term if z is None else z + term
            z = z if b == 0 else pltpu.roll(z, N - b, 0)
            dx = z if dx is None else dx + z
        dx = dx[:tm, :]
        if glu:
            a = a_ref[...]
            out_ref[:, :cw] = (dx * sg).astype(BF16)
            out_ref[:, cw:] = (dx * a * sg * (1.0 - sg)).astype(BF16)
        else:
            out_ref[...] = dx.astype(BF16)

        @pl.when(first)
        def _():
            dw_ref[...] = jnp.zeros_like(dw_ref)

        dpad = jnp.concatenate([jnp.zeros((8, cw), F32), dc_t], axis=0)
        for b in range(min(8, K)):
            shifted = dpad if b == 0 else pltpu.roll(dpad, N - b, 0)
            for a_ in range((K - 1 - b) // 8 + 1):
                k = K - 1 - 8 * a_ - b
                dw_ref[pl.ds(k, 1), :] += jnp.sum(shifted * ext_ref[pl.ds(H - 8 - 8 * a_, N), :], axis=0, keepdims=True)

    tile = lambda base: pl.BlockSpec((tm, cw), lambda j, i, _b=base: (i, _b + j))
    halo = lambda base: pl.BlockSpec((H, cw), lambda j, i, _b=base: (jnp.maximum(i * (tm // H) - 1, 0), _b + j))
    in_specs = [tile(0), pl.BlockSpec((H, cw), lambda j, i: (jnp.minimum((i + 1) * (tm // H), last_halo), j))]
    args = [dc, dc]
    for (arr, _), base in zip(srcs, bases):
        in_specs += [tile(base), halo(base)]
        args += [arr, arr]
    in_specs += [pl.BlockSpec((K, cw), lambda j, i: (0, j)), ANY]
    args += [w, du]
    ow = n_src * cw
    out_specs = [pl.BlockSpec((tm, ow), lambda j, i: (i, du_col // ow + j)), pl.BlockSpec((K, cw), lambda j, i: (0, j))]
    out_shape = [jax.ShapeDtypeStruct(du.shape, du.dtype), jax.ShapeDtypeStruct((K, C), F32)]
    return pl.pallas_call(
        body, name=name, grid=(nc, nt), in_specs=in_specs, out_specs=out_specs, out_shape=out_shape,
        input_output_aliases={len(args) - 1: 0},
        scratch_shapes=[pltpu.VMEM((tm + H, cw), F32), pltpu.VMEM((tm + H, cw), F32)],
        compiler_params=_cparams(("parallel", "arbitrary")),
    )(*args)


def _conf_post(c, gate, ln_w, ln_b, pw):
    mu = jnp.mean(c, axis=-1, keepdims=True)
    cc = c - mu
    var = jnp.mean(cc * cc, axis=-1, keepdims=True)
    hn = cc * lax.rsqrt(var + 1e-5) * ln_w + ln_b
    return mm_nn(_silu(hn), pw) * _silu(gate)


def _conf_specs(tm):
    H = CONF_HALO
    blk = lambda col: pl.BlockSpec((tm, CONV_CH), lambda i, _c=col: (i, _c))
    vec = pl.BlockSpec((1, CONV_CH), lambda i: (0, 0))
    specs = [blk(0), blk(1), blk(2), _halo_spec(H, tm, CONV_CH, 0), _halo_spec(H, tm, CONV_CH, 1),
             pl.BlockSpec((CONV_WIDTH, CONV_CH), lambda i: (0, 0)), vec, vec, vec,
             pl.BlockSpec((CONV_CH, CONV_CH), lambda i: (0, 0))]
    return specs, blk, vec


def _conf_conv(a_ref, b_ref, ah_ref, bh_ref, dww_ref, dwb_ref, ext_ref, tm):
    first = pl.program_id(0) == 0
    _fill_ext(ext_ref, ah_ref[...] * _sigmoid(bh_ref[...]), a_ref[...] * _sigmoid(b_ref[...]), first, CONF_HALO)
    return _conv_taps(ext_ref, dww_ref, CONV_WIDTH, CONF_HALO, tm) + dwb_ref[...]


def conf_fwd(u, dw_w, dw_b, ln_w, ln_b, pw, *, tm, name):
    S = u.shape[0]
    specs, blk, vec = _conf_specs(tm)

    def body(a_ref, b_ref, g_ref, ah_ref, bh_ref, dww_ref, dwb_ref, lnw_ref, lnb_ref, pw_ref, y_ref, ext_ref):
        c = _conf_conv(a_ref, b_ref, ah_ref, bh_ref, dww_ref, dwb_ref, ext_ref, tm)
        y_ref[...] = _conf_post(c, g_ref[...], lnw_ref[...], lnb_ref[...], pw_ref[...]).astype(BF16)

    return pl.pallas_call(
        body, name=name, grid=(S // tm,), in_specs=specs, out_specs=blk(0),
        out_shape=jax.ShapeDtypeStruct((S, CONV_CH), BF16),
        scratch_shapes=[pltpu.VMEM((tm + CONF_HALO, CONV_CH), F32)],
        compiler_params=_cparams(("parallel",)),
    )(u, u, u, u, u, dw_w, dw_b, ln_w, ln_b, pw)


def conf_bwd_post(u, dy, dw_w, dw_b, ln_w, ln_b, pw, du, *, tm, name):
    S = u.shape[0]
    specs, blk, vec = _conf_specs(tm)
    mat = pl.BlockSpec((CONV_CH, CONV_CH), lambda i: (0, 0))

    def body(a_ref, b_ref, g_ref, ah_ref, bh_ref, dww_ref, dwb_ref, lnw_ref, lnb_ref, pw_ref, dy_ref, du_in,
             dc_ref, dg_ref, dlnw_ref, dlnb_ref, dpw_ref, ddwb_ref, ext_ref):
        c = _conf_conv(a_ref, b_ref, ah_ref, bh_ref, dww_ref, dwb_ref, ext_ref, tm)
        _, vjp = jax.vjp(_conf_post, c, g_ref[...], lnw_ref[...], lnb_ref[...], pw_ref[...])
        dc, dg, dlnw, dlnb, dpw = vjp(dy_ref[...])
        dc_ref[...] = dc
        dg_ref[...] = dg.astype(BF16)

        @pl.when(pl.program_id(0) == 0)
        def _():
            dlnw_ref[...] = jnp.zeros_like(dlnw_ref)
            dlnb_ref[...] = jnp.zeros_like(dlnb_ref)
            dpw_ref[...] = jnp.zeros_like(dpw_ref)

            ddwb_ref[...] = jnp.zeros_like(ddwb_ref)

        dlnw_ref[...] += dlnw
        dlnb_ref[...] += dlnb
        dpw_ref[...] += dpw
        ddwb_ref[...] += jnp.sum(dc, axis=0, keepdims=True)

    return pl.pallas_call(
        body, name=name, grid=(S // tm,), in_specs=specs + [blk(0), ANY],
        out_specs=[blk(0), blk(C_CG // CONV_CH), vec, vec, mat, vec],
        out_shape=[jax.ShapeDtypeStruct((S, CONV_CH), F32), jax.ShapeDtypeStruct(du.shape, du.dtype),
                   jax.ShapeDtypeStruct((1, CONV_CH), F32), jax.ShapeDtypeStruct((1, CONV_CH), F32),
                   jax.ShapeDtypeStruct((CONV_CH, CONV_CH), F32), jax.ShapeDtypeStruct((1, CONV_CH), F32)],
        input_output_aliases={11: 1},
        scratch_shapes=[pltpu.VMEM((tm + CONF_HALO, CONV_CH), F32)],
        compiler_params=_cparams(("arbitrary",)),
    )(u, u, u, u, u, dw_w, dw_b, ln_w, ln_b, pw, dy, du)


def _iota2(shape, dim):
    return lax.broadcasted_iota(jnp.int32, shape, dim)


def _gdn_post(pre_q, pre_k, pre_v, b_in, a_in, a_log, dt_bias):
    tm = pre_q.shape[0]
    q, k, v = _silu(pre_q), _silu(pre_k), _silu(pre_v)
    qs, ks = [], []
    for h in range(GDN_HEADS):
        sl = slice(h * GDN_DK, (h + 1) * GDN_DK)
        qh, kh = q[:, sl], k[:, sl]
        qs.append(qh * lax.rsqrt(jnp.sum(qh * qh, axis=-1, keepdims=True) + 1e-6) * (GDN_DK ** -0.5))
        ks.append(kh * lax.rsqrt(jnp.sum(kh * kh, axis=-1, keepdims=True) + 1e-6))
    beta = _sigmoid(b_in)
    g = -jnp.exp(a_log) * _softplus(a_in + dt_bias)
    nb = tm // GDN_CHUNK
    tril = (_iota2((nb, GDN_CHUNK, GDN_CHUNK), 1) >= _iota2((nb, GDN_CHUNK, GDN_CHUNK), 2)).astype(F32)
    gc = mm_nn(tril, g.reshape(nb, GDN_CHUNK, 128), "f32").reshape(tm, 128)
    return jnp.concatenate(qs, axis=1), jnp.concatenate(ks, axis=1), v, beta, gc


def _gdn_prep_specs(tm):
    H = GDN_HALO
    blk = lambda col: pl.BlockSpec((tm, GDN_W), lambda i, _c=col: (i, _c))
    lane = lambda col: pl.BlockSpec((tm, 128), lambda i, _c=col: (i, _c))
    vec = pl.BlockSpec((1, 128), lambda i: (0, 0))
    q0 = C_GQ // GDN_W
    specs = [blk(q0), blk(q0 + 1), blk(q0 + 2),
             _halo_spec(H, tm, GDN_W, q0), _halo_spec(H, tm, GDN_W, q0 + 1), _halo_spec(H, tm, GDN_W, q0 + 2),
             lane(C_BETA // 128), lane(C_ALPHA // 128),
             pl.BlockSpec((SHORT_CONV, GDN_W), lambda i: (0, 0)), pl.BlockSpec((SHORT_CONV, GDN_W), lambda i: (0, 1)),
             pl.BlockSpec((SHORT_CONV, GDN_W), lambda i: (0, 2)), vec, vec]
    return specs, blk, lane, vec


def _gdn_pre(x_refs, h_refs, w_refs, ext_ref, tm):
    first = pl.program_id(0) == 0
    pres = []
    for x_ref, h_ref, w_ref in zip(x_refs, h_refs, w_refs):
        _fill_ext(ext_ref, h_ref[...], x_ref[...], first, GDN_HALO)
        pres.append(_conv_taps(ext_ref, w_ref, SHORT_CONV, GDN_HALO, tm))
    return pres


def gdn_prep_fwd(u, conv_w, a_log, dt_bias, *, tm, name):
    S = u.shape[0]
    specs, blk, lane, vec = _gdn_prep_specs(tm)

    def body(xq, xk, xv, hq, hk, hv, bi, ai, wq, wk, wv, al, db, q_ref, k_ref, v_ref, beta_ref, gc_ref, ext_ref):
        pres = _gdn_pre((xq, xk, xv), (hq, hk, hv), (wq, wk, wv), ext_ref, tm)
        q, k, v, beta, gc = _gdn_post(*pres, bi[...], ai[...], al[...], db[...])
        q_ref[...] = q
        k_ref[...] = k
        v_ref[...] = v
        beta_ref[...] = beta
        gc_ref[...] = gc

    wide = jax.ShapeDtypeStruct((S, GDN_W), F32)
    narrow = jax.ShapeDtypeStruct((S, 128), F32)
    return pl.pallas_call(
        body, name=name, grid=(S // tm,), in_specs=specs,
        out_specs=[blk(0), blk(0), blk(0), lane(0), lane(0)], out_shape=[wide, wide, wide, narrow, narrow],
        scratch_shapes=[pltpu.VMEM((tm + GDN_HALO, GDN_W), F32)],
        compiler_params=_cparams(("parallel",)),
    )(u, u, u, u, u, u, u, u, conv_w, conv_w, conv_w, a_log, dt_bias)


def gdn_prep_bwd(u, conv_w, a_log, dt_bias, dq, dk, dv, dbeta, dgc, du, *, tm, name):
    S = u.shape[0]
    specs, blk, lane, vec = _gdn_prep_specs(tm)
    tail = IN_PAD - C_BETA

    def body(xq, xk, xv, hq, hk, hv, bi, ai, wq, wk, wv, al, db, dq_ref, dk_ref, dv_ref, dbe_ref, dgc_ref, du_in,
             dpre_ref, du_ref, dal_ref, ddb_ref, ext_ref):
        pres = _gdn_pre((xq, xk, xv), (hq, hk, hv), (wq, wk, wv), ext_ref, tm)
        _, vjp = jax.vjp(_gdn_post, *pres, bi[...], ai[...], al[...], db[...])
        dpq, dpk, dpv, dbi, dai, dal, ddb = vjp((dq_ref[...], dk_ref[...], dv_ref[...], dbe_ref[...], dgc_ref[...]))
        dpre_ref[:, 0:GDN_W] = dpq
        dpre_ref[:, GDN_W:2 * GDN_W] = dpk
        dpre_ref[:, 2 * GDN_W:] = dpv
        du_ref[:, 0:128] = dbi.astype(BF16)
        du_ref[:, 128:256] = dai.astype(BF16)
        du_ref[:, 256:] = jnp.zeros((tm, tail - 256), BF16)

        @pl.when(pl.program_id(0) == 0)
        def _():
            dal_ref[...] = jnp.zeros_like(dal_ref)
            ddb_ref[...] = jnp.zeros_like(ddb_ref)

        dal_ref[...] += dal
        ddb_ref[...] += ddb

    n_in = len(specs) + 6
    return pl.pallas_call(
        body, name=name, grid=(S // tm,), in_specs=specs + [blk(0), blk(0), blk(0), lane(0), lane(0), ANY],
        out_specs=[pl.BlockSpec((tm, 3 * GDN_W), lambda i: (i, 0)), pl.BlockSpec((tm, tail), lambda i: (i, C_BETA // tail)),
                   vec, vec],
        out_shape=[jax.ShapeDtypeStruct((S, 3 * GDN_W), F32), jax.ShapeDtypeStruct(du.shape, du.dtype),
                   jax.ShapeDtypeStruct((1, 128), F32), jax.ShapeDtypeStruct((1, 128), F32)],
        input_output_aliases={n_in - 1: 1},
        scratch_shapes=[pltpu.VMEM((tm + GDN_HALO, GDN_W), F32)],
        compiler_params=_cparams(("arbitrary",)),
    )(u, u, u, u, u, u, u, u, conv_w, conv_w, conv_w, a_log, dt_bias, dq, dk, dv, dbeta, dgc, du)


def _lane_col(blk, h):
    return jnp.sum(jnp.where(_iota2(blk.shape, 1) == h, blk, 0.0), axis=1, keepdims=True)


@jax.custom_vjp
def _tri_inv(low):
    n = low.shape[-1]
    r, c = _iota2(low.shape, low.ndim - 2), _iota2(low.shape, low.ndim - 1)
    eye = (r == c).astype(F32)
    t = eye - jnp.where((r // 2 == c // 2) & (r > c), low, 0.0)
    s = 2
    while s < n:
        off = jnp.where((r // (2 * s) == c // (2 * s)) & (r // s > c // s), low, 0.0)
        prec = "bf16" if s <= 8 else "bf16x3"
        t = t - _nn_raw(t, _nn_raw(off, t, prec), prec)
        s *= 2
    return t


def _tri_inv_f(low):
    t = _tri_inv(low)
    return t, t


def _tri_inv_b(t, dt):
    d = -_nt_raw(_tn_raw(t, dt, "bf16x3"), t, "bf16x3")
    r, c = _iota2(d.shape, d.ndim - 2), _iota2(d.shape, d.ndim - 1)
    return (jnp.where(r > c, d, 0.0),)


_tri_inv.defvjp(_tri_inv_f, _tri_inv_b)


@jax.custom_vjp
def _tri_inv_saved(low, t):
    return t


_tri_inv_saved.defvjp(lambda low, t: (t, t), lambda t, dt: (_tri_inv_b(t, dt)[0], jnp.zeros_like(t)))


def _gdn_group(s0, q, k, v, z, beta_blk, gc_blk, nw, h0, t_saved=None, with_t=False):
    C = GDN_CHUNK
    HP, R, _ = q.shape
    nb = R // C
    B = HP * nb
    q3, k3, v3 = (t.reshape(B, C, GDN_DK) for t in (q, k, v))
    b3 = jnp.stack([_lane_col(beta_blk, h0 + j) for j in range(HP)]).reshape(B, C, 1)
    g3 = jnp.stack([_lane_col(gc_blk, h0 + j) for j in range(HP)]).reshape(B, C, 1)
    r, c = _iota2((B, C, C), 1), _iota2((B, C, C), 2)
    causal, strict = r >= c, r > c
    g_t = gc_blk.T
    rows = [jnp.sum(jnp.where(_iota2((128, R), 0) == h0 + j, g_t, 0.0), axis=0, keepdims=True) for j in range(HP)]
    g_row = jnp.stack([rows[j][:, i * C:(i + 1) * C] for j in range(HP) for i in range(nb)])
    decay = jnp.where(causal, jnp.exp(jnp.where(causal, g3 - g_row, 0.0)), 0.0)
    low = jnp.where(strict, b3 * mm_nt(k3, k3) * decay, 0.0)
    t = _tri_inv(low) if t_saved is None else _tri_inv_saved(low, t_saved)
    eg = jnp.exp(g3)
    four = lambda x: x.reshape((HP, nb) + x.shape[1:])
    w_v = four(mm_nn(t, v3 * b3))
    w_k = four(mm_nn(t, k3 * (b3 * eg)))
    qk = four(jnp.where(causal, mm_nt(q3, k3) * decay, 0.0))
    q_dec = four(q3 * eg)
    g_last = jnp.sum(jnp.where(_iota2((B, C, 1), 1) == C - 1, g3, 0.0), axis=1, keepdims=True)
    k_dec = four(k3 * jnp.exp(g_last - g3))
    e_last = four(jnp.exp(g_last))
    s, outs = s0, []
    for i in range(nb):
        v_new = w_v[:, i] - mm_nn(w_k[:, i], s)
        outs.append(mm_nn(q_dec[:, i], s) + mm_nn(qk[:, i], v_new))
        s = s * e_last[:, i] + mm_tn(k_dec[:, i], v_new)
    o = jnp.concatenate(outs, axis=1)
    y = o * lax.rsqrt(jnp.mean(o * o, axis=-1, keepdims=True) + 1e-6) * nw * _silu(z)
    return (s, y, t) if with_t else (s, y)


def _heads(ref, HP):
    return jnp.stack([ref[:, j * GDN_DK:(j + 1) * GDN_DK] for j in range(HP)])


def gdn_core_fwd(q, k, v, u, beta, gc, nw, *, name):
    S = q.shape[0]
    R = GDN_CHUNK * GDN_GROUP
    G = S // R
    HP = GDN_HEADS_PER_STEP
    W = HP * GDN_DK
    NT = HP * GDN_GROUP
    blk = pl.BlockSpec((R, W), lambda g, h: (g, h))
    lane = pl.BlockSpec((R, 128), lambda g, h: (g, 0))
    st = pl.BlockSpec((1, HP, GDN_DK, GDN_DK), lambda g, h: (g, h, 0, 0))
    inv = pl.BlockSpec((1, NT, GDN_CHUNK, GDN_CHUNK), lambda g, h: (g, h, 0, 0))

    def body(q_ref, k_ref, v_ref, z_ref, be_ref, gc_ref, nw_ref, y_ref, st_ref, t_ref, s_ref):
        g, hs = pl.program_id(0), pl.program_id(1)
        s0 = jnp.where(g == 0, 0.0, s_ref[hs])
        st_ref[0] = s0
        s1, y, t = _gdn_group(s0, _heads(q_ref, HP), _heads(k_ref, HP), _heads(v_ref, HP), _heads(z_ref, HP), be_ref[...],
                              gc_ref[...], nw_ref[...], hs * HP, with_t=True)
        s_ref[hs] = s1
        t_ref[0] = t
        for j in range(HP):
            y_ref[:, j * GDN_DK:(j + 1) * GDN_DK] = y[j].astype(BF16)

    return pl.pallas_call(
        body, name=name, grid=(G, GDN_HEADS // HP),
        in_specs=[blk, blk, blk, pl.BlockSpec((R, W), lambda g, h: (g, C_GZ // W + h)), lane, lane,
                  pl.BlockSpec((1, 128), lambda g, h: (0, 0))],
        out_specs=[blk, st, inv],
        out_shape=[jax.ShapeDtypeStruct((S, GDN_W), BF16), jax.ShapeDtypeStruct((G, GDN_HEADS, GDN_DK, GDN_DK), F32),
                   jax.ShapeDtypeStruct((G, GDN_HEADS * GDN_GROUP, GDN_CHUNK, GDN_CHUNK), F32)],
        scratch_shapes=[pltpu.VMEM((GDN_HEADS // HP, HP, GDN_DK, GDN_DK), F32)],
        compiler_params=_cparams(("arbitrary", "arbitrary")),
    )(q, k, v, u, beta, gc, nw)


def gdn_core_bwd(q, k, v, u, beta, gc, nw, states, tinv, dy, du, *, name):
    S = q.shape[0]
    R = GDN_CHUNK * GDN_GROUP
    G = S // R
    HP = GDN_HEADS_PER_STEP
    W = HP * GDN_DK
    blk = pl.BlockSpec((R, W), lambda g, h: (G - 1 - g, h))
    lane = pl.BlockSpec((R, 128), lambda g, h: (G - 1 - g, 0))
    vec = pl.BlockSpec((1, 128), lambda g, h: (0, 0))

    def body(q_ref, k_ref, v_ref, z_ref, be_ref, gc_ref, nw_ref, st_ref, t_ref, *rest):
        dy_refs = rest[:HP]
        dq_ref, dk_ref, dv_ref, dz_ref, dbe_ref, dgc_ref, dnw_ref, ds_ref = rest[HP + 1:]
        g, hs = pl.program_id(0), pl.program_id(1)

        @pl.when(hs == 0)
        def _():
            dbe_ref[...] = jnp.zeros_like(dbe_ref)
            dgc_ref[...] = jnp.zeros_like(dgc_ref)

        @pl.when((hs == 0) & (g == 0))
        def _():
            dnw_ref[...] = jnp.zeros_like(dnw_ref)

        _, vjp = jax.vjp(functools.partial(_gdn_group, h0=hs * HP, t_saved=t_ref[0]), st_ref[0], _heads(q_ref, HP),
                         _heads(k_ref, HP), _heads(v_ref, HP), _heads(z_ref, HP), be_ref[...], gc_ref[...], nw_ref[...])
        ds_in = jnp.where(g == 0, 0.0, ds_ref[hs])
        dy = jnp.stack([r[...] for r in dy_refs])
        ds0, dq, dk, dv, dz, dbe, dgc, dnw = vjp((ds_in, dy))
        ds_ref[hs] = ds0
        for j in range(HP):
            sl = slice(j * GDN_DK, (j + 1) * GDN_DK)
            dq_ref[:, sl] = dq[j]
            dk_ref[:, sl] = dk[j]
            dv_ref[:, sl] = dv[j]
            dz_ref[:, sl] = dz[j].astype(BF16)
        dbe_ref[...] += dbe
        dgc_ref[...] += dgc
        dnw_ref[...] += dnw

    wide = jax.ShapeDtypeStruct((S, GDN_W), F32)
    narrow = jax.ShapeDtypeStruct((S, 128), F32)
    return pl.pallas_call(
        body, name=name, grid=(G, GDN_HEADS // HP),
        in_specs=[blk, blk, blk, pl.BlockSpec((R, W), lambda g, h: (G - 1 - g, C_GZ // W + h)), lane, lane, vec,
                  pl.BlockSpec((1, HP, GDN_DK, GDN_DK), lambda g, h: (G - 1 - g, h, 0, 0)),
                  pl.BlockSpec((1, HP * GDN_GROUP, GDN_CHUNK, GDN_CHUNK), lambda g, h: (G - 1 - g, h, 0, 0))]
        + [pl.BlockSpec((R, GDN_DK), lambda g, h, _j=j: (G - 1 - g, CONV_CH // GDN_DK + h * HP + _j)) for j in range(HP)]
        + [ANY],
        out_specs=[blk, blk, blk, pl.BlockSpec((R, W), lambda g, h: (G - 1 - g, C_GZ // W + h)), lane, lane, vec],
        out_shape=[wide, wide, wide, jax.ShapeDtypeStruct(du.shape, du.dtype), narrow, narrow,
                   jax.ShapeDtypeStruct((1, 128), F32)],
        input_output_aliases={9 + HP: 3},
        scratch_shapes=[pltpu.VMEM((GDN_HEADS // HP, HP, GDN_DK, GDN_DK), F32)],
        compiler_params=_cparams(("arbitrary", "arbitrary")),
    )(q, k, v, u, beta, gc, nw, states, tinv, *([dy] * HP), du)


def rope_tables(S):
    half = ROPE_DIM // 2
    inv = ROPE_THETA ** (-jnp.arange(half, dtype=F32) / half)
    ang = jnp.arange(S, dtype=F32)[:, None] * inv[None, :]
    cos, sin = jnp.cos(ang), jnp.sin(ang)
    rest = ATT_HD - ROPE_DIM
    c = jnp.concatenate([cos, cos, jnp.ones((S, rest), F32)], axis=1)
    s1 = jnp.concatenate([-sin, jnp.zeros((S, ATT_HD - half), F32)], axis=1)
    s2 = jnp.concatenate([jnp.zeros((S, half), F32), sin, jnp.zeros((S, rest), F32)], axis=1)
    return tuple(jnp.tile(t, (1, 2)) for t in (c, s1, s2))


def _rope(x, c, s1, s2):
    half = ROPE_DIM // 2
    return x * c + pltpu.roll(x, ATT_W - half, 1) * s1 + pltpu.roll(x, half, 1) * s2


def _unrope(dy, c, s1, s2):
    half = ROPE_DIM // 2
    return dy * c + pltpu.roll(dy * s1, half, 1) + pltpu.roll(dy * s2, ATT_W - half, 1)


def att_prep_fwd(u, tables, *, tm, name):
    S = u.shape[0]
    blk = lambda col: pl.BlockSpec((tm, ATT_W), lambda i, _c=col: (i, _c))
    tab = pl.BlockSpec((tm, 128), lambda i: (i, 0))

    def body(q_ref, k_ref, v_ref, c_ref, s1_ref, s2_ref, qo_ref, ko_ref, vo_ref):
        reps = ATT_W // 128
        c, s1, s2 = (jnp.tile(t[...], (1, reps)) for t in (c_ref, s1_ref, s2_ref))
        qo_ref[...] = (_rope(q_ref[...], c, s1, s2) * (ATT_HD ** -0.5)).astype(BF16)
        ko_ref[...] = _rope(k_ref[...], c, s1, s2).astype(BF16)
        vo_ref[...] = v_ref[...].astype(BF16)

    out = jax.ShapeDtypeStruct((S, ATT_W), BF16)
    return pl.pallas_call(
        body, name=name, grid=(S // tm,),
        in_specs=[blk(C_AQ // ATT_W), blk(C_AK // ATT_W), blk(C_AV // ATT_W), tab, tab, tab],
        out_specs=[blk(0)] * 3, out_shape=[out] * 3, compiler_params=_cparams(("parallel",)),
    )(u, u, u, *tables)


def att_prep_bwd(dqs, dks, dvs, tables, du, *, tm, name):
    S = dqs[0].shape[0]
    blk = pl.BlockSpec((tm, ATT_W), lambda i: (i, 0))
    tab = pl.BlockSpec((tm, 128), lambda i: (i, 0))

    def body(*refs):
        dq, dk, dv = (refs[3 * j][...].astype(F32) + refs[3 * j + 1][...].astype(F32) + refs[3 * j + 2][...].astype(F32)
                      for j in range(3))
        c_ref, s1_ref, s2_ref, _, o_ref = refs[9:]
        reps = ATT_W // 128
        c, s1, s2 = (jnp.tile(t[...], (1, reps)) for t in (c_ref, s1_ref, s2_ref))
        o_ref[:, 0:ATT_W] = (_unrope(dq, c, s1, s2) * (ATT_HD ** -0.5)).astype(BF16)
        o_ref[:, ATT_W:2 * ATT_W] = _unrope(dk, c, s1, s2).astype(BF16)
        o_ref[:, 2 * ATT_W:] = dv.astype(BF16)

    return pl.pallas_call(
        body, name=name, grid=(S // tm,), in_specs=[blk] * 9 + [tab] * 3 + [ANY],
        out_specs=pl.BlockSpec((tm, 3 * ATT_W), lambda i: (i, C_AQ // (3 * ATT_W))),
        out_shape=jax.ShapeDtypeStruct(du.shape, du.dtype), input_output_aliases={12: 0},
        compiler_params=_cparams(("parallel",)),
    )(*dqs, *dks, *dvs, *tables, du)


def _band_masks():
    qi, ki = _iota2((ATT_BLOCK, ATT_BLOCK), 0), _iota2((ATT_BLOCK, ATT_BLOCK), 1)
    return qi <= ki, ki <= qi


def _pair_diag(x):
    first = _iota2(x.shape, 1) < ATT_HD
    zero = jnp.zeros_like(x)
    return jnp.concatenate([jnp.where(first, x, zero), jnp.where(first, zero, x)], axis=0)


def att_pattern_fwd(qr, kr, vb, dil, *, name):
    S = qr.shape[0]
    L = S // dil
    nb = L // ATT_BLOCK
    view = lambda t: t.reshape(L, dil * t.shape[1])
    cur = pl.BlockSpec((ATT_BLOCK, ATT_W), lambda r, n: (n, r))
    prev = pl.BlockSpec((ATT_BLOCK, ATT_W), lambda r, n: (jnp.maximum(n - 1, 0), r))

    def body(q_ref, kc_ref, kp_ref, vc_ref, vp_ref, o_ref, l_ref):
        has_prev = pl.program_id(1) > 0
        m_prev, m_cur = _band_masks()
        m_prev = m_prev & has_prev
        first = _iota2((ATT_BLOCK, 128), 1) < ATT_HD
        lane = _iota2((ATT_BLOCK, 128), 1)
        stats = jnp.zeros((ATT_BLOCK, 128), F32)
        pairs = range(ATT_HEADS // 2)
        sls = [slice(p * 128, (p + 1) * 128) for p in pairs]
        sps = [_nt_raw(q_ref[:, sl], _pair_diag(kp_ref[:, sl]), "bf16") for sl in sls]
        scs = [_nt_raw(q_ref[:, sl], _pair_diag(kc_ref[:, sl]), "bf16") for sl in sls]
        probs, inv_dens = [], []
        for p in pairs:
            pps, pcs, dens, lses = [], [], [], []
            for half in range(2):
                hs = slice(half * 128, (half + 1) * 128)
                sp_h, sc_h = jnp.where(m_prev, sps[p][:, hs], NEG_INF), jnp.where(m_cur, scs[p][:, hs], NEG_INF)
                m = jnp.maximum(jnp.max(sp_h, axis=1, keepdims=True), jnp.max(sc_h, axis=1, keepdims=True))
                pp, pc = jnp.exp(sp_h - m), jnp.exp(sc_h - m)
                den = jnp.sum(pp, axis=1, keepdims=True) + jnp.sum(pc, axis=1, keepdims=True)
                pps.append(pp.astype(BF16))
                pcs.append(pc.astype(BF16))
                dens.append(den)
                lses.append(m + jnp.log(den))
            probs.append((jnp.concatenate(pps, axis=1), jnp.concatenate(pcs, axis=1)))
            inv_dens.append(1.0 / jnp.where(first, dens[0], dens[1]))
            stats = jnp.where(lane == 2 * p, lses[0], jnp.where(lane == 2 * p + 1, lses[1], stats))
        outs = [_nn_raw(probs[p][0], _pair_diag(vp_ref[:, sls[p]]), "bf16")
                + _nn_raw(probs[p][1], _pair_diag(vc_ref[:, sls[p]]), "bf16") for p in pairs]
        for p in pairs:
            o_ref[:, sls[p]] = (outs[p] * inv_dens[p]).astype(BF16)
        l_ref[...] = stats

    narrow = pl.BlockSpec((ATT_BLOCK, 128), lambda r, n: (n, r))
    o, l = pl.pallas_call(
        body, name=name, grid=(dil, nb), in_specs=[cur, cur, prev, cur, prev], out_specs=[cur, narrow],
        out_shape=[jax.ShapeDtypeStruct((L, dil * ATT_W), BF16), jax.ShapeDtypeStruct((L, dil * 128), F32)],
        compiler_params=_cparams(("parallel", "arbitrary")),
    )(view(qr), view(kr), view(kr), view(vb), view(vb))
    return o.reshape(S, ATT_W), l.reshape(S, 128)


def _head_spread():
    return (_iota2((128, ATT_W), 1) // ATT_HD == _iota2((128, ATT_W), 0)).astype(F32)


def att_combine_fwd(os_, ls, u, *, tm, name):
    S = u.shape[0]
    blk = lambda col: pl.BlockSpec((tm, ATT_W), lambda i, _c=col: (i, _c))
    lane = pl.BlockSpec((tm, 128), lambda i: (i, 0))

    def body(o1, o2, o3, l1, l2, l3, g_ref, y_ref, o_ref, lse_ref):
        a, b, c = l1[...], l2[...], l3[...]
        m = jnp.maximum(jnp.maximum(a, b), c)
        ea, eb, ec = jnp.exp(a - m), jnp.exp(b - m), jnp.exp(c - m)
        den = ea + eb + ec
        spread = _head_spread()
        wa, wb, wc = (_nn_raw(e / den, spread, "bf16x3") for e in (ea, eb, ec))
        o = wa * o1[...].astype(F32) + wb * o2[...].astype(F32) + wc * o3[...].astype(F32)
        o_ref[...] = o
        lse_ref[...] = m + jnp.log(den)
        y_ref[...] = (o * _silu(g_ref[...])).astype(BF16)

    return pl.pallas_call(
        body, name=name, grid=(S // tm,), in_specs=[blk(0)] * 3 + [lane] * 3 + [blk(C_AG // ATT_W)],
        out_specs=[blk(0), blk(0), lane],
        out_shape=[jax.ShapeDtypeStruct((S, ATT_W), BF16), jax.ShapeDtypeStruct((S, ATT_W), F32),
                   jax.ShapeDtypeStruct((S, 128), F32)],
        compiler_params=_cparams(("parallel",)),
    )(*os_, *ls, u)


def att_combine_bwd(dy, o, u, du, *, tm, name):
    S = u.shape[0]
    cw = 256
    base = (CONV_CH + GDN_W) // cw
    blk = lambda col: pl.BlockSpec((tm, ATT_W), lambda i, _c=col: (i, _c))

    def body(dy0, dy1, dy2, o_ref, g_ref, du_in, do_ref, dg_ref, dl_ref):
        g, d, o = g_ref[...], jnp.concatenate([dy0[...], dy1[...], dy2[...]], axis=1), o_ref[...]
        sg = _sigmoid(g)
        d_o = d * (g * sg)
        do_ref[...] = d_o.astype(BF16)
        dg_ref[...] = (d * o * (sg * (1.0 + g * (1.0 - sg)))).astype(BF16)
        dl_ref[...] = _nt_raw(d_o * o, _head_spread(), "bf16x3")

    return pl.pallas_call(
        body, name=name, grid=(S // tm,),
        in_specs=[pl.BlockSpec((tm, cw), lambda i, _j=j: (i, base + _j)) for j in range(ATT_W // cw)]
        + [blk(0), blk(C_AG // ATT_W), ANY],
        out_specs=[blk(0), blk(C_AG // ATT_W), pl.BlockSpec((tm, 128), lambda i: (i, 0))],
        out_shape=[jax.ShapeDtypeStruct((S, ATT_W), BF16), jax.ShapeDtypeStruct(du.shape, du.dtype),
                   jax.ShapeDtypeStruct((S, 128), F32)],
        input_output_aliases={5: 1},
        compiler_params=_cparams(("parallel",)),
    )(dy, dy, dy, o, u, du)


def att_pattern_bwd(qr, kr, vb, do, delta, lse, dil, *, name):
    S = qr.shape[0]
    L = S // dil
    nb = L // ATT_BLOCK
    view = lambda t: t.reshape(L, dil * t.shape[1])
    cur = pl.BlockSpec((ATT_BLOCK, ATT_W), lambda r, n: (jnp.minimum(n, nb - 1), r))
    prev = pl.BlockSpec((ATT_BLOCK, ATT_W), lambda r, n: (jnp.maximum(n - 1, 0), r))
    narrow = pl.BlockSpec((ATT_BLOCK, 128), lambda r, n: (jnp.minimum(n, nb - 1), r))

    def body(q_ref, kc_ref, kp_ref, vc_ref, vp_ref, do_ref, dl_ref, l_ref, dq_ref, dk_ref, dv_ref, ck_ref, cv_ref):
        n = pl.program_id(1)

        @pl.when(n < nb)
        def _():
            m_prev, m_cur = _band_masks()
            m_prev = m_prev & (n > 0)
            m_prev2, m_cur2 = jnp.concatenate([m_prev, m_prev], axis=1), jnp.concatenate([m_cur, m_cur], axis=1)
            first = _iota2((ATT_BLOCK, 128), 1) < ATT_HD
            wide = (ATT_BLOCK, 128)
            halves = lambda a, b: jnp.concatenate([jnp.broadcast_to(a, wide), jnp.broadcast_to(b, wide)], axis=1)
            fold = lambda t: jnp.where(first, t[:ATT_BLOCK], t[ATT_BLOCK:])
            pairs = range(ATT_HEADS // 2)
            sls = [slice(p * 128, (p + 1) * 128) for p in pairs]
            qs, dos = [q_ref[:, sl] for sl in sls], [do_ref[:, sl] for sl in sls]
            kps, kcs, vps, vcs = ([_pair_diag(r[:, sl]) for sl in sls] for r in (kp_ref, kc_ref, vp_ref, vc_ref))
            s_p = [_nt_raw(qs[p], kps[p], "bf16") for p in pairs]
            s_c = [_nt_raw(qs[p], kcs[p], "bf16") for p in pairs]
            dp_p = [_nt_raw(dos[p], vps[p], "bf16") for p in pairs]
            dp_c = [_nt_raw(dos[p], vcs[p], "bf16") for p in pairs]
            pps, pcs, dsps, dscs = [], [], [], []
            for p in pairs:
                delta = halves(dl_ref[:, 2 * p:2 * p + 1], dl_ref[:, 2 * p + 1:2 * p + 2])
                lse2 = halves(l_ref[:, 2 * p:2 * p + 1], l_ref[:, 2 * p + 1:2 * p + 2])
                pp = jnp.where(m_prev2, jnp.exp(s_p[p] - lse2), 0.0)
                pc = jnp.where(m_cur2, jnp.exp(s_c[p] - lse2), 0.0)
                dsps.append((pp * (dp_p[p] - delta)).astype(BF16))
                dscs.append((pc * (dp_c[p] - delta)).astype(BF16))
                pps.append(pp.astype(BF16))
                pcs.append(pc.astype(BF16))
            dqs = [_nn_raw(dsps[p], kps[p], "bf16") + _nn_raw(dscs[p], kcs[p], "bf16") for p in pairs]
            dk_prev = [fold(_tn_raw(dsps[p], qs[p], "bf16")) for p in pairs]
            dv_prev = [fold(_tn_raw(pps[p], dos[p], "bf16")) for p in pairs]
            dk_cur = [fold(_tn_raw(dscs[p], qs[p], "bf16")) for p in pairs]
            dv_cur = [fold(_tn_raw(pcs[p], dos[p], "bf16")) for p in pairs]
            for p in pairs:
                dq_ref[:, sls[p]] = dqs[p].astype(BF16)

            @pl.when(n > 0)
            def _():
                for p in pairs:
                    dk_ref[:, sls[p]] = (ck_ref[:, sls[p]] + dk_prev[p]).astype(BF16)
                    dv_ref[:, sls[p]] = (cv_ref[:, sls[p]] + dv_prev[p]).astype(BF16)

            for p in pairs:
                ck_ref[:, sls[p]] = dk_cur[p]
                cv_ref[:, sls[p]] = dv_cur[p]

        @pl.when(n == nb)
        def _():
            dk_ref[...] = ck_ref[...].astype(BF16)
            dv_ref[...] = cv_ref[...].astype(BF16)

    out = jax.ShapeDtypeStruct((L, dil * ATT_W), BF16)
    dq, dk, dv = pl.pallas_call(
        body, name=name, grid=(dil, nb + 1), in_specs=[cur, cur, prev, cur, prev, cur, narrow, narrow],
        out_specs=[cur, prev, prev], out_shape=[out, out, out],
        scratch_shapes=[pltpu.VMEM((ATT_BLOCK, ATT_W), F32), pltpu.VMEM((ATT_BLOCK, ATT_W), F32)],
        compiler_params=_cparams(("arbitrary", "arbitrary")),
    )(view(qr), view(kr), view(kr), view(vb), view(vb), view(do), view(delta), view(lse))
    return dq.reshape(S, ATT_W), dk.reshape(S, ATT_W), dv.reshape(S, ATT_W)


def _loss_rows(x, w, tgt):
    err = _rms_fn(x, w) - tgt
    return jnp.sum(0.5 * jnp.mean(err * err, axis=-1, keepdims=True), axis=0, keepdims=True)


def loss_head(x, w, tgt, *, tm, name):
    S, D = x.shape

    def body(x_ref, w_ref, t_ref, l_ref, dx_ref, dw_ref):
        val, vjp = jax.vjp(_loss_rows, x_ref[...], w_ref[...], t_ref[...])
        dx, dw, _ = vjp(jnp.ones((1, 1), F32))
        dx_ref[...] = dx

        @pl.when(pl.program_id(0) == 0)
        def _():
            l_ref[...] = jnp.zeros_like(l_ref)
            dw_ref[...] = jnp.zeros_like(dw_ref)

        l_ref[...] += val
        dw_ref[...] += dw

    row = pl.BlockSpec((tm, D), lambda i: (i, 0))
    vec = pl.BlockSpec((1, D), lambda i: (0, 0))
    one = pl.BlockSpec((1, 1), lambda i: (0, 0))
    return pl.pallas_call(
        body, name=name, grid=(S // tm,), in_specs=[row, vec, row], out_specs=[one, row, vec],
        out_shape=[jax.ShapeDtypeStruct((1, 1), F32), jax.ShapeDtypeStruct((S, D), F32), jax.ShapeDtypeStruct((1, D), F32)],
        compiler_params=_cparams(("arbitrary",)),
    )(x, w, tgt)


def adam(w, g, m, v, *, name):
    shape = w.shape
    C = shape[-1]
    R = w.size // C
    br = R
    while br * C * 4 > (1 << 21) and br % 16 == 0:
        br //= 2
    two = lambda t: t.reshape(R, C)

    def body(w_ref, g_ref, m_ref, v_ref, d_ref, mo_ref, vo_ref):
        gg = g_ref[...]
        m_new = ADAM_B1 * m_ref[...] + (1.0 - ADAM_B1) * gg
        v_new = ADAM_B2 * v_ref[...] + (1.0 - ADAM_B2) * jnp.square(gg)
        m_hat = m_new / (1.0 - ADAM_B1 ** ADAM_STEP)
        v_hat = v_new / (1.0 - ADAM_B2 ** ADAM_STEP)
        d_ref[...] = -ADAM_LR * (m_hat / (jnp.sqrt(v_hat) + ADAM_EPS) + ADAM_WD * w_ref[...])
        mo_ref[...] = m_new
        vo_ref[...] = v_new

    blk = pl.BlockSpec((br, C), lambda i: (i, 0))
    out = jax.ShapeDtypeStruct((R, C), F32)
    d, mo, vo = pl.pallas_call(
        body, name=name, grid=(R // br,), in_specs=[blk] * 4, out_specs=[blk] * 3, out_shape=[out] * 3,
        compiler_params=_cparams(("parallel",)),
    )(two(w), two(g), two(m), two(v))
    return d.reshape(shape), mo.reshape(shape), vo.reshape(shape)


MESH_IDS = pl.DeviceIdType.MESH
ANY = pl.BlockSpec(memory_space=pl.ANY)


def _my_id():
    return 4 * lax.axis_index("x") + 2 * lax.axis_index("y") + lax.axis_index("c")


def _peer(k):
    x, y, c = lax.axis_index("x"), lax.axis_index("y"), lax.axis_index("c")
    flip = lambda v, bit: 1 - v if bit else v
    return (flip(x, k & 4), flip(y, k & 2), flip(c, k & 1))


def all_gather_two_level(arrs, *, name):
    n = len(arrs)

    def body(*refs):
        ins, outs = refs[:n], refs[n:2 * n]
        send, recv, local = refs[2 * n:]
        x, y, c = lax.axis_index("x"), lax.axis_index("y"), lax.axis_index("c")
        me, sibling = (x, y, c), (x, y, 1 - c)
        chips = [(1 - x, y), (x, 1 - y), (1 - x, 1 - y)]

        def slot(a, dev):
            return outs[a].at[4 * dev[0] + 2 * dev[1] + dev[2]]

        def copy(a, k, block, to, src=None):
            return pltpu.make_async_remote_copy(
                src_ref=slot(a, block) if src is None else src, dst_ref=slot(a, block), send_sem=send.at[a, k],
                recv_sem=recv.at[a, k], device_id=to, device_id_type=MESH_IDS)

        mine = [pltpu.make_async_copy(ins[a], slot(a, me), local.at[a]) for a in range(n)]
        first = [copy(a, 1 + j, me, (*chip, c), src=ins[a]) for a in range(n) for j, chip in enumerate(chips)]
        first += [copy(a, 0, me, sibling, src=ins[a]) for a in range(n)]
        for cp in first + mine:
            cp.start()
        passed = []
        for a in range(n):
            for j, chip in enumerate(chips):
                copy(a, 1 + j, (*chip, c), me).wait_recv()
                fwd = copy(a, 4 + j, (*chip, c), sibling)
                fwd.start()
                passed.append(fwd)
        for a in range(n):
            copy(a, 0, sibling, me).wait_recv()
            for j, chip in enumerate(chips):
                copy(a, 4 + j, (*chip, 1 - c), me).wait_recv()
        for cp in first + passed:
            cp.wait_send()
        for cp in mine:
            cp.wait()

    return pl.pallas_call(
        body, name=name, in_specs=[ANY] * n, out_specs=[ANY] * n,
        out_shape=[jax.ShapeDtypeStruct((N_DEV,) + a.shape, a.dtype) for a in arrs],
        scratch_shapes=[pltpu.SemaphoreType.DMA((n, N_DEV - 1)), pltpu.SemaphoreType.DMA((n, N_DEV - 1)),
                        pltpu.SemaphoreType.DMA((n,))],
        compiler_params=pltpu.CompilerParams(has_side_effects=True),
    )(*arrs)


def scatter_exchange(groups, pack, *, name):
    flat = [a for grp in groups for a in grp]
    n = len(flat) + 1
    shapes = [jax.ShapeDtypeStruct((N_DEV, len(grp), grp[0].shape[0] // N_DEV, grp[0].shape[1]), grp[0].dtype) for grp in groups]
    shapes.append(jax.ShapeDtypeStruct((N_DEV,) + pack.shape, pack.dtype))
    index = [(gi, li) for gi, grp in enumerate(groups) for li in range(len(grp))]

    def body(*refs):
        ins, outs = refs[:n], refs[n:n + len(shapes)]
        send, recv, local = refs[n + len(shapes):]
        me = _my_id()
        started = []
        for a in range(n):
            if a < n - 1:
                gi, li = index[a]
                r = ins[a].shape[0] // N_DEV
                src = lambda j, _a=a, _r=r: ins[_a].at[pl.ds(pl.multiple_of(j * _r, 8), _r), :]
                dst = outs[gi].at[me, li]
            else:
                src = lambda j, _a=a: ins[_a]
                dst = outs[-1].at[me]
            lc = pltpu.make_async_copy(src(me), dst, local.at[a])
            lc.start()
            started.append(lc)
            for k in range(1, N_DEV):
                cp = pltpu.make_async_remote_copy(src_ref=src(me ^ k), dst_ref=dst, send_sem=send.at[a, k - 1],
                                                  recv_sem=recv.at[a, k - 1], device_id=_peer(k), device_id_type=MESH_IDS)
                cp.start()
                started.append(cp)
        for cp in started:
            cp.wait()

    return pl.pallas_call(
        body, name=name, in_specs=[ANY] * n, out_specs=[ANY] * len(shapes), out_shape=shapes,
        scratch_shapes=[pltpu.SemaphoreType.DMA((n, N_DEV - 1)), pltpu.SemaphoreType.DMA((n, N_DEV - 1)),
                        pltpu.SemaphoreType.DMA((n,))],
        compiler_params=pltpu.CompilerParams(has_side_effects=True),
    )(*flat, pack)


def slot_sum(x, *, name):
    _, A, R, C = x.shape
    br = R
    while br * C * 4 * N_DEV > (1 << 23) and br % 16 == 0:
        br //= 2

    def body(x_ref, o_ref):
        acc = x_ref[0, 0].astype(F32)
        for s in range(1, N_DEV):
            acc = acc + x_ref[s, 0].astype(F32)
        o_ref[0] = acc

    return pl.pallas_call(
        body, name=name, grid=(A, R // br),
        in_specs=[pl.BlockSpec((N_DEV, 1, br, C), lambda a, i: (0, a, i, 0))],
        out_specs=pl.BlockSpec((1, br, C), lambda a, i: (a, i, 0)),
        out_shape=jax.ShapeDtypeStruct((A, R, C), F32),
        compiler_params=_cparams(("parallel", "parallel")),
    )(x)


HBM_SPEC = pl.BlockSpec(memory_space=pltpu.HBM)
SEM_SPEC = pl.BlockSpec(memory_space=pltpu.SEMAPHORE)
DATAFLOW = pltpu.SideEffectType.DATAFLOW_SIDE_EFFECTING


def _push_copies(src_refs, land_refs, send_sems, recv_sems, by_rows):
    me = _my_id()
    copies = []
    for a, (src, land) in enumerate(zip(src_refs, land_refs)):
        rows = land.shape[1]
        for k in range(1, N_DEV):
            piece = src.at[pl.ds(pl.multiple_of((me ^ k) * rows, 8), rows), :] if by_rows else src
            copies.append(pltpu.make_async_remote_copy(
                src_ref=piece, dst_ref=land.at[me], send_sem=send_sems[a].at[k - 1], recv_sem=recv_sems[a].at[k - 1],
                device_id=_peer(k), device_id_type=MESH_IDS))
    return copies


def push_start(srcs, lands, *, by_rows, name):
    n = len(srcs)

    def body(*refs):
        src_refs, land_refs = refs[:n], refs[n:2 * n]
        send_sems, recv_sems = refs[2 * n:3 * n], refs[3 * n:4 * n]
        token = refs[6 * n]
        for cp in _push_copies(src_refs, land_refs, send_sems, recv_sems, by_rows):
            cp.start()
        token[...] = jnp.zeros_like(token)

    sems = [pltpu.SemaphoreType.DMA((N_DEV - 1,))] * (2 * n)
    bufs = [pltpu.HBM(a.shape, a.dtype) for a in list(srcs) + list(lands)]
    outs = pl.pallas_call(
        body, name=name, out_shape=tuple(sems + bufs + [jax.ShapeDtypeStruct((8, 128), F32)]),
        in_specs=[HBM_SPEC] * (2 * n), out_specs=tuple([SEM_SPEC] * (2 * n) + [HBM_SPEC] * (2 * n) + [pl.BlockSpec(memory_space=pltpu.VMEM)]),
        input_output_aliases={i: 2 * n + i for i in range(2 * n)},
        compiler_params=pltpu.CompilerParams(has_side_effects=DATAFLOW),
    )(*[pltpu.with_memory_space_constraint(a, pltpu.HBM) for a in list(srcs) + list(lands)])
    return outs[:n], outs[n:2 * n], outs[2 * n:3 * n], outs[3 * n:4 * n], outs[4 * n]


def push_wait(send_sems, recv_sems, srcs, lands, after, *, by_rows, name):
    n = len(srcs)

    def body(*refs):
        src_refs, land_refs = refs[:n], refs[n:2 * n]
        send, recv = refs[2 * n:3 * n], refs[3 * n:4 * n]
        for cp in _push_copies(src_refs, land_refs, send, recv, by_rows):
            cp.wait_send()
            cp.wait_recv()

    outs = pl.pallas_call(
        body, name=name, out_shape=tuple(pltpu.HBM(a.shape, a.dtype) for a in list(srcs) + list(lands)),
        in_specs=[HBM_SPEC] * (2 * n) + [SEM_SPEC] * (2 * n) + [ANY], out_specs=tuple([HBM_SPEC] * (2 * n)),
        input_output_aliases={i: i for i in range(2 * n)},
        compiler_params=pltpu.CompilerParams(has_side_effects=DATAFLOW),
    )(*srcs, *lands, *send_sems, *recv_sems, after)
    return outs[n:]


LANDING_BLOCKS = 4


def _landing(src, slots_shape, after=None, *, by_rows, name):
    rows, cols = slots_shape[1], slots_shape[2]
    br = rows // LANDING_BLOCKS
    me = _my_id().astype(jnp.int32).reshape(1)

    def body(me_ref, src_ref, *rest):
        rest[-1][0] = src_ref[...]

    extra = [] if after is None else [after]
    grid_spec = pltpu.PrefetchScalarGridSpec(
        num_scalar_prefetch=1, grid=(LANDING_BLOCKS,),
        in_specs=[pl.BlockSpec((br, cols), lambda i, me_ref: ((me_ref[0] * LANDING_BLOCKS if by_rows else 0) + i, 0))]
        + [ANY] * len(extra),
        out_specs=pl.BlockSpec((1, br, cols), lambda i, me_ref: (me_ref[0], i, 0)))
    return pl.pallas_call(
        body, name=name, grid_spec=grid_spec, out_shape=jax.ShapeDtypeStruct(slots_shape, src.dtype),
    )(me, src, *extra)


def _pack(arrs):
    flat = []
    for a in arrs:
        f = a.reshape(-1).astype(F32)
        flat.append(jnp.pad(f, (0, (-f.size) % 128)))
    f = jnp.concatenate(flat)
    return jnp.pad(f, (0, (-f.size) % 1024)).reshape(-1, 128)


def _unpack(p, shapes):
    f = p.reshape(-1)
    out, off = [], 0
    for s in shapes:
        n = math.prod(s)
        out.append(f[off:off + n].reshape(s))
        off += n + (-n) % 128
    return out


def _to_padded_cols(w):
    z = lambda n: jnp.zeros(w.shape[:-1] + (n,), w.dtype)
    return jnp.concatenate([w[..., 0:4608], w[..., 4620:7692], w[..., 4608:4614], z(122), w[..., 4614:4620], z(378)], axis=-1)


def _from_padded_cols(w):
    return jnp.concatenate([w[..., 0:4608], w[..., C_BETA:C_BETA + 6], w[..., C_ALPHA:C_ALPHA + 6], w[..., 4608:7680]], axis=-1)


def _lane_pad(v):
    return jnp.pad(v, (0, 128 - v.shape[0]))[None, :]


TM_MM, TN_MM, TK_MM = 1024, 1024, 2048
TM_ROW = 512


def layer_fwd(x, p, tabs, l):
    h = rms_fwd(x, p["norm_w"], tm=TM_ROW, name=f"rms_fwd_{l}")
    u = matmul(h, p["w_in"], mode="nn", tm=TM_MM, tn=TN_MM, tk=TK_MM, name=f"in_proj_{l}")
    y_conv = conf_fwd(u, p["dw_w"], p["dw_b"], p["ln_w"], p["ln_b"], p["pw"], tm=TM_ROW, name=f"conf_fwd_{l}")
    q, k, v, beta, gc = gdn_prep_fwd(u, p["conv_w"], p["a_log"], p["dt_bias"], tm=TM_ROW, name=f"gdn_prep_fwd_{l}")
    y_gdn, states, tinv = gdn_core_fwd(q, k, v, u, beta, gc, p["gdn_nw"], name=f"gdn_core_fwd_{l}")
    qr, kr, vb = att_prep_fwd(u, tabs, tm=TM_ROW, name=f"att_prep_fwd_{l}")
    os_, ls = [], []
    for _, dil in DIL_PATTERNS:
        o_p, l_p = att_pattern_fwd(qr, kr, vb, dil, name=f"att_fwd_d{dil}_{l}")
        os_.append(o_p)
        ls.append(l_p)
    y_att, o, lse = att_combine_fwd(os_, ls, u, tm=TM_ROW, name=f"att_combine_fwd_{l}")
    y = jnp.concatenate([y_conv, y_gdn, y_att], axis=1)
    if callable(p["w_out"]):
        p["w_out"] = p["w_out"](y)
    x_new = matmul(y, p["w_out"], mode="nn", tm=TM_MM // 2, tn=D_MODEL, tk=TK_MM, residual=x, name=f"out_proj_{l}")
    saved = dict(x=x, h=h, u=u, y=y, q=q, k=k, v=v, beta=beta, gc=gc, states=states, tinv=tinv, qr=qr, kr=kr, vb=vb, o=o,
                 lse=lse)
    return x_new, saved


def layer_bwd(dx_out, s, p, tabs, l, send_w_out=None, send_w_in=None):
    S = dx_out.shape[0]
    u = s["u"]
    dy = matmul(dx_out, p["w_out"], mode="nt", tm=TM_MM // 2, tn=D_MODEL, tk=TK_MM, name=f"out_proj_dy_{l}")
    g_w_out = matmul(s["y"], dx_out, mode="tn", tm=TM_MM, tn=TN_MM, tk=TK_MM, out_dtype=BF16, name=f"out_proj_dw_{l}")
    dw_b = p["dw_b"] if send_w_out is None else p["dw_b"] + send_w_out(g_w_out)
    du = lax.empty((S, IN_PAD), BF16)
    dc, du, g_ln_w, g_ln_b, g_pw, g_dw_b = conf_bwd_post(u, dy, p["dw_w"], dw_b, p["ln_w"], p["ln_b"], p["pw"], du,
                                                       tm=TM_ROW, name=f"conf_bwd_post_{l}")
    du, g_dw_w = conv_bwd(dc, [(u, C_CA), (u, C_CB)], p["dw_w"], du, C_CA, K=CONV_WIDTH, H=CONF_HALO, tm=TM_ROW, cw=CONV_CH,
                          glu=True, name=f"conf_bwd_conv_{l}")
    dq, dk, dv, du, dbeta, dgc, g_gdn_nw = gdn_core_bwd(s["q"], s["k"], s["v"], u, s["beta"], s["gc"], p["gdn_nw"],
                                                        s["states"], s["tinv"], dy, du, name=f"gdn_core_bwd_{l}")
    dpre, du, g_a_log, g_dt_bias = gdn_prep_bwd(u, p["conv_w"], p["a_log"], p["dt_bias"], dq, dk, dv, dbeta, dgc, du,
                                                tm=TM_ROW, name=f"gdn_prep_bwd_{l}")
    du, g_conv_w = conv_bwd(dpre, [(u, C_GQ)], p["conv_w"], du, C_GQ, K=SHORT_CONV, H=GDN_HALO, tm=TM_ROW, cw=GDN_W,
                            glu=False, name=f"gdn_bwd_conv_{l}")
    do, du, delta = att_combine_bwd(dy, s["o"], u, du, tm=TM_ROW, name=f"att_combine_bwd_{l}")
    dqs, dks, dvs = [], [], []
    for _, dil in DIL_PATTERNS:
        a, b, c = att_pattern_bwd(s["qr"], s["kr"], s["vb"], do, delta, s["lse"], dil, name=f"att_bwd_d{dil}_{l}")
        dqs.append(a)
        dks.append(b)
        dvs.append(c)
    du = att_prep_bwd(dqs, dks, dvs, tabs, du, tm=TM_ROW, name=f"att_prep_bwd_{l}")
    g_w_in = matmul(s["h"], du, mode="tn", tm=TM_MM, tn=TN_MM, tk=min(2 * TK_MM, S), out_dtype=BF16, name=f"in_proj_dw_{l}")
    sent = None if send_w_in is None else send_w_in(g_w_in)
    dh = matmul(du, p["w_in"], mode="nt", tm=TM_MM, tn=TN_MM, tk=2 * TK_MM, after=sent, name=f"in_proj_dh_{l}")
    dx, g_norm_w = rms_bwd(s["x"], p["norm_w"], dh, dx_out, tm=TM_ROW // 2, name=f"rms_bwd_{l}")
    grads = dict(norm_w=g_norm_w[0], w_in=g_w_in, conv_qkv_w=g_conv_w, a_log=g_a_log[0, :GDN_HEADS], dt_bias=g_dt_bias[0, :GDN_HEADS],
                 gdn_norm_w=g_gdn_nw[0], conf_dw_w=g_dw_w, conf_dw_b=g_dw_b[0], conf_ln_w=g_ln_w[0], conf_ln_b=g_ln_b[0],
                 conf_pw_w=g_pw, w_out=g_w_out)
    return dx, grads


WEIGHTS = ("norm_w", "w_in", "conv_qkv_w", "a_log", "dt_bias", "gdn_norm_w", "conf_dw_w", "conf_dw_b", "conf_ln_w",
           "conf_ln_b", "conf_pw_w", "w_out", "final_norm_w")
SMALL_REPLICATED = ("norm_w", "a_log", "dt_bias", "gdn_norm_w", "conf_dw_b", "conf_ln_w", "conf_ln_b")


def kernel(x, norm_w, w_in, conv_qkv_w, a_log, dt_bias, gdn_norm_w, conf_dw_w, conf_dw_b, conf_ln_w, conf_ln_b, conf_pw_w, w_out, final_norm_w, loss_target, m_norm_w, m_w_in, m_conv_qkv_w, m_a_log, m_dt_bias, m_gdn_norm_w, m_conf_dw_w, m_conf_dw_b, m_conf_ln_w, m_conf_ln_b, m_conf_pw_w, m_w_out, m_final_norm_w, v_norm_w, v_w_in, v_conv_qkv_w, v_a_log, v_dt_bias, v_gdn_norm_w, v_conf_dw_w, v_conf_dw_b, v_conf_ln_w, v_conf_ln_b, v_conf_pw_w, v_w_out, v_final_norm_w):
    w = dict(norm_w=norm_w, w_in=w_in, conv_qkv_w=conv_qkv_w, a_log=a_log, dt_bias=dt_bias, gdn_norm_w=gdn_norm_w,
             conf_dw_w=conf_dw_w, conf_dw_b=conf_dw_b, conf_ln_w=conf_ln_w, conf_ln_b=conf_ln_b, conf_pw_w=conf_pw_w,
             w_out=w_out, final_norm_w=final_norm_w)
    m = dict(zip(WEIGHTS, (m_norm_w, m_w_in, m_conv_qkv_w, m_a_log, m_dt_bias, m_gdn_norm_w, m_conf_dw_w, m_conf_dw_b,
                           m_conf_ln_w, m_conf_ln_b, m_conf_pw_w, m_w_out, m_final_norm_w)))
    v = dict(zip(WEIGHTS, (v_norm_w, v_w_in, v_conv_qkv_w, v_a_log, v_dt_bias, v_gdn_norm_w, v_conf_dw_w, v_conf_dw_b,
                           v_conf_ln_w, v_conf_ln_b, v_conf_pw_w, v_w_out, v_final_norm_w)))
    S = x.shape[1]
    L = norm_w.shape[0]
    me = _my_id()

    small_shapes = [conv_qkv_w.shape, conf_dw_w.shape, conf_pw_w.shape]
    w_in_b, w_out_b = _to_padded_cols(w_in).astype(BF16), w_out.astype(BF16)
    in_slots, out_slots = (N_DEV,) + w_in_b.shape[1:], (N_DEV,) + w_out_b.shape[1:]
    g_in0, g_small = all_gather_two_level([w_in_b[0], _pack([conv_qkv_w, conf_dw_w, conf_pw_w])], name="gather_first")
    gathers, tie = {}, jnp.zeros((1, 1), F32)
    for l in range(L):
        srcs = [w_out_b[0]] if l == 0 else [w_in_b[l], w_out_b[l]]
        slots = [out_slots] if l == 0 else [in_slots, out_slots]
        lands = [_landing(a, sl, g_in0, by_rows=False, name=f"gather_own_{l}_{j}") for j, (a, sl) in enumerate(zip(srcs, slots))]
        *flight, token = push_start(srcs, lands, by_rows=False, name=f"gather_start_{l}")
        gathers[l] = flight
        tie = tie + token[0:1, 0:1]
    parts = [_unpack(g_small[s], small_shapes) for s in range(N_DEV)]
    conv_full = jnp.concatenate([pt[0] for pt in parts], axis=2)
    dw_full = jnp.concatenate([pt[1] for pt in parts], axis=2)
    pw_full = jnp.concatenate([pt[2] for pt in parts], axis=1)
    tabs = rope_tables(S)

    def layer_params(l, full_in, full_out):
        return dict(
            norm_w=norm_w[l][None], w_in=full_in.reshape(D_MODEL, IN_PAD), w_out=full_out,
            conv_w=conv_full[l], a_log=_lane_pad(a_log[l]), dt_bias=_lane_pad(dt_bias[l]), gdn_nw=gdn_norm_w[l][None],
            dw_w=dw_full[l], dw_b=conf_dw_b[l][None], ln_w=conf_ln_w[l][None], ln_b=conf_ln_b[l][None], pw=pw_full[l])

    xs = x[0]
    params, saved = [], []
    for l in range(L):
        if l == 0:
            late_out = lambda after: push_wait(*gathers[0], after, by_rows=False, name="gather_wait_0")[0].reshape(D_MODEL, D_MODEL)
            p = layer_params(0, g_in0, late_out)
            p["norm_w"] = p["norm_w"] + tie
        else:
            full_in, full_out = push_wait(*gathers[l], xs, by_rows=False, name=f"gather_wait_{l}")
            p = layer_params(l, full_in, full_out.reshape(D_MODEL, D_MODEL))
        params.append(p)
        xs, sv = layer_fwd(xs, p, tabs, l)
        saved.append(sv)
    loss_part, dx, g_final = loss_head(xs, final_norm_w[None], loss_target[0], tm=TM_ROW // 2, name="loss_head")

    layer_grads, scatters = [None] * L, {}

    def send(kind, l, grad, slots):
        land = _landing(grad, slots, by_rows=True, name=f"scatter_own_{kind}_{l}")
        *flight, token = push_start([grad], [land], by_rows=True, name=f"scatter_start_{kind}_{l}")
        scatters[kind, l] = flight
        return token[0:1, 0:1]

    for l in reversed(range(L)):
        dx, layer_grads[l] = layer_bwd(dx, saved[l], params[l], tabs, l, functools.partial(send, "out", l, slots=out_slots),
                                       functools.partial(send, "in", l, slots=in_slots))

    stack = lambda name: jnp.stack([layer_grads[l][name] for l in range(L)])
    small = [loss_part] + [stack(n) for n in SMALL_REPLICATED] + [g_final[0], stack("conv_qkv_w"), stack("conf_dw_w")]
    small_shapes = [a.shape for a in small]
    r_pw, r_small = scatter_exchange([[layer_grads[l]["conf_pw_w"] for l in range(L)]], _pack(small), name="scatter_small")
    g = {}
    g["conf_pw_w"] = slot_sum(r_pw, name="sum_pw")
    summed = _unpack(slot_sum(r_small[:, None], name="sum_small")[0], small_shapes)
    loss = summed[0].reshape(())
    for n, a in zip(SMALL_REPLICATED, summed[1:1 + len(SMALL_REPLICATED)]):
        g[n] = a
    g["final_norm_w"] = summed[-3]
    g["conv_qkv_w"] = lax.dynamic_slice_in_dim(summed[-2], me * conv_qkv_w.shape[2], conv_qkv_w.shape[2], axis=2)
    g["conf_dw_w"] = lax.dynamic_slice_in_dim(summed[-1], me * conf_dw_w.shape[2], conf_dw_w.shape[2], axis=2)
    deltas, new_m, new_v = {}, {}, {}
    for n in WEIGHTS:
        if n not in ("w_in", "w_out"):
            deltas[n], new_m[n], new_v[n] = adam(w[n], g[n], m[n], v[n], name=f"adam_{n}")
    sums = {}
    order = [(kind, l) for l in reversed(range(L)) for kind in ("out", "in")]
    done_first = loss_part
    for kind, l in order:
        last = (kind, l) == order[-1]
        after = (done_first + deltas["a_log"][0:1, 0:1]) if last else dx
        land, = push_wait(*scatters[kind, l], after, by_rows=True, name=f"scatter_wait_{kind}_{l}")
        sums[kind, l] = slot_sum(land[:, None], name=f"sum_w_{kind}_{l}")
        if not last:
            done_first = done_first + sums[kind, l][0, 0:1, 0:1]
    g["w_in"] = _from_padded_cols(jnp.concatenate([sums["in", l] for l in range(L)], axis=0))
    g["w_out"] = jnp.concatenate([sums["out", l] for l in range(L)], axis=0)
    for n in ("w_in", "w_out"):
        deltas[n], new_m[n], new_v[n] = adam(w[n], g[n], m[n], v[n], name=f"adam_{n}")
    return (loss, dx[None], *[g[n] for n in WEIGHTS], *[deltas[n] for n in WEIGHTS],
            *[new_m[n] for n in WEIGHTS], *[new_v[n] for n in WEIGHTS])
```

```python
import functools
import math

import jax
import jax.numpy as jnp
from jax import lax
from jax.experimental import pallas as pl
from jax.experimental.pallas import tpu as pltpu

D_MODEL = 2048
DEPTH = 4
N_DEV = 8
GDN_DK = 128
GDN_HEADS = 6
GDN_W = 768
ATT_HD = 64
ATT_HEADS = 12
ATT_W = 768
CONV_CH = 512
CONV_WIDTH = 31
SHORT_CONV = 4
GDN_CHUNK = 64
ROPE_THETA = 500000.0
ROPE_DIM = 16
DIL_PATTERNS = ((128, 1), (512, 4), (2048, 16))
ATT_BLOCK = 128
NEG_INF = -1e30
IN_W = 7692

ADAM_LR = 0.001
ADAM_B1 = 0.9
ADAM_B2 = 0.999
ADAM_EPS = 1e-08
ADAM_WD = 0.01
ADAM_STEP = 10

C_CA, C_CB, C_CG = 0, 512, 1024
C_GQ, C_GK, C_GV, C_GZ = 1536, 2304, 3072, 3840
C_AQ, C_AK, C_AV, C_AG = 4608, 5376, 6144, 6912
C_BETA, C_ALPHA = 7680, 7808
IN_PAD = 8192

VMEM_LIMIT = 56 * 1024 * 1024
CONF_HALO = 32
GDN_HALO = 8
GDN_GROUP = 4
GDN_HEADS_PER_STEP = 6

F32 = jnp.float32
BF16 = jnp.bfloat16
HI = lax.Precision.HIGHEST


def _cparams(sem, vmem=VMEM_LIMIT):
    return pltpu.CompilerParams(dimension_semantics=sem, vmem_limit_bytes=vmem)


def _dg(a, b, ca, cb, prec):
    nb = a.ndim - 2
    batch = tuple(range(nb))
    dn = (((ca + nb,), (cb + nb,)), (batch, batch))
    if prec == "bf16":
        return lax.dot_general(a.astype(BF16), b.astype(BF16), dn, preferred_element_type=F32)
    if prec == "bf16x3":
        ah, bh = a.astype(BF16), b.astype(BF16)
        al, bl = (a - ah.astype(F32)).astype(BF16), (b - bh.astype(F32)).astype(BF16)
        dot = lambda x, y: lax.dot_general(x, y, dn, preferred_element_type=F32)
        return dot(ah, bh) + (dot(ah, bl) + dot(al, bh))
    return lax.dot_general(a.astype(F32), b.astype(F32), dn, precision=HI, preferred_element_type=F32)


def _nn_raw(a, b, prec):
    return _dg(a, b, 1, 0, prec)


def _nt_raw(a, b, prec):
    return _dg(a, b, 1, 1, prec)


def _tn_raw(a, b, prec):
    return _dg(a, b, 0, 0, prec)


@functools.partial(jax.custom_vjp, nondiff_argnums=(2,))
def mm_nn(a, b, prec="bf16"):
    return _nn_raw(a, b, prec)


def _mm_nn_f(a, b, prec):
    return _nn_raw(a, b, prec), (a, b)


def _mm_nn_b(prec, res, g):
    a, b = res
    return _nt_raw(g, b, prec).astype(a.dtype), _tn_raw(a, g, prec).astype(b.dtype)


mm_nn.defvjp(_mm_nn_f, _mm_nn_b)


@functools.partial(jax.custom_vjp, nondiff_argnums=(2,))
def mm_nt(a, b, prec="bf16"):
    return _nt_raw(a, b, prec)


def _mm_nt_f(a, b, prec):
    return _nt_raw(a, b, prec), (a, b)


def _mm_nt_b(prec, res, g):
    a, b = res
    return _nn_raw(g, b, prec).astype(a.dtype), _tn_raw(g, a, prec).astype(b.dtype)


mm_nt.defvjp(_mm_nt_f, _mm_nt_b)


@functools.partial(jax.custom_vjp, nondiff_argnums=(2,))
def mm_tn(a, b, prec="bf16"):
    return _tn_raw(a, b, prec)


def _mm_tn_f(a, b, prec):
    return _tn_raw(a, b, prec), (a, b)


def _mm_tn_b(prec, res, g):
    a, b = res
    return _nt_raw(b, g, prec).astype(a.dtype), _nn_raw(a, g, prec).astype(b.dtype)


mm_tn.defvjp(_mm_tn_f, _mm_tn_b)


def _sigmoid(x):
    return 1.0 / (1.0 + jnp.exp(-x))


def _silu(x):
    return x * _sigmoid(x)


def _softplus(x):
    return jnp.maximum(x, 0.0) + jnp.log(1.0 + jnp.exp(-jnp.abs(x)))


def matmul(a, b, *, mode, tm, tn, tk, out_dtype=F32, residual=None, after=None, name):
    if mode == "tn":
        K, M = a.shape
    else:
        M, K = a.shape
    N = b.shape[0] if mode == "nt" else b.shape[1]
    assert M % tm == 0 and N % tn == 0 and K % tk == 0, (a.shape, b.shape, tm, tn, tk)
    nk = K // tk
    a_spec = pl.BlockSpec((tk, tm), lambda i, j, k: (k, i)) if mode == "tn" else pl.BlockSpec((tm, tk), lambda i, j, k: (i, k))
    b_spec = pl.BlockSpec((tn, tk), lambda i, j, k: (j, k)) if mode == "nt" else pl.BlockSpec((tk, tn), lambda i, j, k: (k, j))
    o_spec = pl.BlockSpec((tm, tn), lambda i, j, k: (i, j))
    raw = {"nn": _nn_raw, "nt": _nt_raw, "tn": _tn_raw}[mode]
    has_res = residual is not None

    def body(*refs):
        a_ref, b_ref = refs[:2]
        r_ref = refs[2] if has_res else None
        o_ref, acc_ref = refs[-2:]
        k = pl.program_id(2)
        part = raw(a_ref[...], b_ref[...], "bf16")

        @pl.when(k == 0)
        def _():
            acc_ref[...] = part

        @pl.when(k > 0)
        def _():
            acc_ref[...] += part

        @pl.when(k == nk - 1)
        def _():
            r = acc_ref[...]
            if has_res:
                r = r + r_ref[...].astype(F32)
            o_ref[...] = r.astype(out_dtype)

    in_specs = [a_spec, b_spec] + ([o_spec] if has_res else []) + ([ANY] if after is not None else [])
    args = (a, b) + ((residual,) if has_res else ()) + ((after,) if after is not None else ())
    return pl.pallas_call(
        body, name=name, grid=(M // tm, N // tn, nk), in_specs=in_specs, out_specs=o_spec,
        out_shape=jax.ShapeDtypeStruct((M, N), out_dtype),
        scratch_shapes=[pltpu.VMEM((tm, tn), F32)],
        compiler_params=_cparams(("parallel", "parallel", "arbitrary")),
    )(*args)


def _rms_fn(x, w, eps=1e-6):
    return x * lax.rsqrt(jnp.mean(x * x, axis=-1, keepdims=True) + eps) * w


def rms_fwd(x, w, *, tm, name):
    S, D = x.shape

    def body(x_ref, w_ref, o_ref):
        o_ref[...] = _rms_fn(x_ref[...], w_ref[...]).astype(BF16)

    return pl.pallas_call(
        body, name=name, grid=(S // tm,),
        in_specs=[pl.BlockSpec((tm, D), lambda i: (i, 0)), pl.BlockSpec((1, D), lambda i: (0, 0))],
        out_specs=pl.BlockSpec((tm, D), lambda i: (i, 0)),
        out_shape=jax.ShapeDtypeStruct((S, D), BF16),
        compiler_params=_cparams(("parallel",)),
    )(x, w)


def rms_bwd(x, w, dh, dres, *, tm, name):
    S, D = x.shape

    def body(x_ref, w_ref, dh_ref, dr_ref, dx_ref, dw_ref):
        _, vjp = jax.vjp(_rms_fn, x_ref[...], w_ref[...])
        dx, dw = vjp(dh_ref[...].astype(F32))
        dx_ref[...] = dx + dr_ref[...]

        @pl.when(pl.program_id(0) == 0)
        def _():
            dw_ref[...] = jnp.zeros_like(dw_ref)

        dw_ref[...] += dw

    row = pl.BlockSpec((tm, D), lambda i: (i, 0))
    vec = pl.BlockSpec((1, D), lambda i: (0, 0))
    return pl.pallas_call(
        body, name=name, grid=(S // tm,), in_specs=[row, vec, row, row], out_specs=[row, vec],
        out_shape=[jax.ShapeDtypeStruct((S, D), F32), jax.ShapeDtypeStruct((1, D), F32)],
        compiler_params=_cparams(("arbitrary",)),
    )(x, w, dh, dres)


def _fill_ext(ext_ref, halo, tile, first, H):
    ext_ref[pl.ds(0, H), :] = jnp.where(first, 0.0, halo)
    ext_ref[pl.ds(H, tile.shape[0]), :] = tile


def _conv_taps(ext_ref, w_ref, K, H, tm):
    assert H >= 8 * ((K - 1) // 8 + 1)
    total = None
    for b in range(min(8, K)):
        y = None
        for a in range((K - 1 - b) // 8 + 1):
            term = ext_ref[pl.ds(H - 8 - 8 * a, tm + 8), :] * w_ref[pl.ds(K - 1 - 8 * a - b, 1), :]
            y = term if y is None else y + term
        y = y if b == 0 else pltpu.roll(y, b, 0)
        total = y if total is None else total + y
    return total[8:, :]


def _halo_spec(H, tm, cw, col):
    return pl.BlockSpec((H, cw), lambda *g, _c=col: (jnp.maximum(g[-1] * (tm // H) - 1, 0), _c))


def conv_bwd(dc, srcs, w, du, du_col, *, K, H, tm, cw, glu, name):
    S, C = dc.shape
    nc, nt = C // cw, S // tm
    last_halo = S // H - 1
    n_src = 2 if glu else 1
    bases = [c0 // cw for _, c0 in srcs]

    def body(*refs):
        dc_ref, dcn_ref = refs[0], refs[1]
        src_refs = refs[2:2 + 2 * n_src]
        w_ref = refs[2 + 2 * n_src]
        out_ref, dw_ref, ext_ref, dext_ref = refs[4 + 2 * n_src:]
        i = pl.program_id(1)
        first, last = i == 0, i == nt - 1
        if glu:
            a_ref, ah_ref, b_ref, bh_ref = src_refs
            sg = _sigmoid(b_ref[...])
            _fill_ext(ext_ref, ah_ref[...] * _sigmoid(bh_ref[...]), a_ref[...] * sg, first, H)
        else:
            x_ref, xh_ref = src_refs
            _fill_ext(ext_ref, xh_ref[...], x_ref[...], first, H)
        dc_t = dc_ref[...]
        dext_ref[pl.ds(0, tm), :] = dc_t
        dext_ref[pl.ds(tm, H), :] = jnp.where(last, 0.0, dcn_ref[...])
        N = tm + 8
        dx = None
        for b in range(min(8, K)):
            z = None
            for a_ in range((K - 1 - b) // 8 + 1):
                term = dext_ref[pl.ds(8 * a_, N), :] * w_ref[pl.ds(K - 1 - 8 * a_ - b, 1), :]
                z = term if z is None else z + term
            z = z if b == 0 else pltpu.roll(z, N - b, 0)
            dx = z if dx is None else dx + z
        dx = dx[:tm, :]
        if glu:
            a = a_ref[...]
            out_ref[:, :cw] = (dx * sg).astype(BF16)
            out_ref[:, cw:] = (dx * a * sg * (1.0 - sg)).astype(BF16)
        else:
            out_ref[...] = dx.astype(BF16)

        @pl.when(first)
        def _():
            dw_ref[...] = jnp.zeros_like(dw_ref)

        dpad = jnp.concatenate([jnp.zeros((8, cw), F32), dc_t], axis=0)
        for b in range(min(8, K)):
            shifted = dpad if b == 0 else pltpu.roll(dpad, N - b, 0)
            for a_ in range((K - 1 - b) // 8 + 1):
                k = K - 1 - 8 * a_ - b
                dw_ref[pl.ds(k, 1), :] += jnp.sum(shifted * ext_ref[pl.ds(H - 8 - 8 * a_, N), :], axis=0, keepdims=True)

    tile = lambda base: pl.BlockSpec((tm, cw), lambda j, i, _b=base: (i, _b + j))
    halo = lambda base: pl.BlockSpec((H, cw), lambda j, i, _b=base: (jnp.maximum(i * (tm // H) - 1, 0), _b + j))
    in_specs = [tile(0), pl.BlockSpec((H, cw), lambda j, i: (jnp.minimum((i + 1) * (tm // H), last_halo), j))]
    args = [dc, dc]
    for (arr, _), base in zip(srcs, bases):
        in_specs += [tile(base), halo(base)]
        args += [arr, arr]
    in_specs += [pl.BlockSpec((K, cw), lambda j, i: (0, j)), ANY]
    args += [w, du]
    ow = n_src * cw
    out_specs = [pl.BlockSpec((tm, ow), lambda j, i: (i, du_col // ow + j)), pl.BlockSpec((K, cw), lambda j, i: (0, j))]
    out_shape = [jax.ShapeDtypeStruct(du.shape, du.dtype), jax.ShapeDtypeStruct((K, C), F32)]
    return pl.pallas_call(
        body, name=name, grid=(nc, nt), in_specs=in_specs, out_specs=out_specs, out_shape=out_shape,
        input_output_aliases={len(args) - 1: 0},
        scratch_shapes=[pltpu.VMEM((tm + H, cw), F32), pltpu.VMEM((tm + H, cw), F32)],
        compiler_params=_cparams(("parallel", "arbitrary")),
    )(*args)


def _conf_post(c, gate, ln_w, ln_b, pw):
    mu = jnp.mean(c, axis=-1, keepdims=True)
    cc = c - mu
    var = jnp.mean(cc * cc, axis=-1, keepdims=True)
    hn = cc * lax.rsqrt(var + 1e-5) * ln_w + ln_b
    return mm_nn(_silu(hn), pw) * _silu(gate)


def _conf_specs(tm):
    H = CONF_HALO
    blk = lambda col: pl.BlockSpec((tm, CONV_CH), lambda i, _c=col: (i, _c))
    vec = pl.BlockSpec((1, CONV_CH), lambda i: (0, 0))
    specs = [blk(0), blk(1), blk(2), _halo_spec(H, tm, CONV_CH, 0), _halo_spec(H, tm, CONV_CH, 1),
             pl.BlockSpec((CONV_WIDTH, CONV_CH), lambda i: (0, 0)), vec, vec, vec,
             pl.BlockSpec((CONV_CH, CONV_CH), lambda i: (0, 0))]
    return specs, blk, vec


def _conf_conv(a_ref, b_ref, ah_ref, bh_ref, dww_ref, dwb_ref, ext_ref, tm):
    first = pl.program_id(0) == 0
    _fill_ext(ext_ref, ah_ref[...] * _sigmoid(bh_ref[...]), a_ref[...] * _sigmoid(b_ref[...]), first, CONF_HALO)
    return _conv_taps(ext_ref, dww_ref, CONV_WIDTH, CONF_HALO, tm) + dwb_ref[...]


def conf_fwd(u, dw_w, dw_b, ln_w, ln_b, pw, *, tm, name):
    S = u.shape[0]
    specs, blk, vec = _conf_specs(tm)

    def body(a_ref, b_ref, g_ref, ah_ref, bh_ref, dww_ref, dwb_ref, lnw_ref, lnb_ref, pw_ref, y_ref, c_ref, ext_ref):
        c = _conf_conv(a_ref, b_ref, ah_ref, bh_ref, dww_ref, dwb_ref, ext_ref, tm)
        c_ref[...] = c
        y_ref[...] = _conf_post(c, g_ref[...], lnw_ref[...], lnb_ref[...], pw_ref[...]).astype(BF16)

    return pl.pallas_call(
        body, name=name, grid=(S // tm,), in_specs=specs, out_specs=[blk(0), blk(0)],
        out_shape=[jax.ShapeDtypeStruct((S, CONV_CH), BF16), jax.ShapeDtypeStruct((S, CONV_CH), F32)],
        scratch_shapes=[pltpu.VMEM((tm + CONF_HALO, CONV_CH), F32)],
        compiler_params=_cparams(("parallel",)),
    )(u, u, u, u, u, dw_w, dw_b, ln_w, ln_b, pw)


def conf_bwd_post(u, c, dy, ln_w, ln_b, pw, du, *, tm, name):
    S = u.shape[0]
    blk = lambda col: pl.BlockSpec((tm, CONV_CH), lambda i, _c=col: (i, _c))
    vec = pl.BlockSpec((1, CONV_CH), lambda i: (0, 0))
    mat = pl.BlockSpec((CONV_CH, CONV_CH), lambda i: (0, 0))

    def body(c_ref, g_ref, lnw_ref, lnb_ref, pw_ref, dy_ref, du_in, dc_ref, dg_ref, dlnw_ref, dlnb_ref, dpw_ref, ddwb_ref):
        _, vjp = jax.vjp(_conf_post, c_ref[...], g_ref[...], lnw_ref[...], lnb_ref[...], pw_ref[...])
        dc, dg, dlnw, dlnb, dpw = vjp(dy_ref[...])
        dc_ref[...] = dc
        dg_ref[...] = dg.astype(BF16)

        @pl.when(pl.program_id(0) == 0)
        def _():
            dlnw_ref[...] = jnp.zeros_like(dlnw_ref)
            dlnb_ref[...] = jnp.zeros_like(dlnb_ref)
            dpw_ref[...] = jnp.zeros_like(dpw_ref)

            ddwb_ref[...] = jnp.zeros_like(ddwb_ref)

        dlnw_ref[...] += dlnw
        dlnb_ref[...] += dlnb
        dpw_ref[...] += dpw
        ddwb_ref[...] += jnp.sum(dc, axis=0, keepdims=True)

    return pl.pallas_call(
        body, name=name, grid=(S // tm,), in_specs=[blk(0), blk(C_CG // CONV_CH), vec, vec, mat, blk(0), ANY],
        out_specs=[blk(0), blk(C_CG // CONV_CH), vec, vec, mat, vec],
        out_shape=[jax.ShapeDtypeStruct((S, CONV_CH), F32), jax.ShapeDtypeStruct(du.shape, du.dtype),
                   jax.ShapeDtypeStruct((1, CONV_CH), F32), jax.ShapeDtypeStruct((1, CONV_CH), F32),
                   jax.ShapeDtypeStruct((CONV_CH, CONV_CH), F32), jax.ShapeDtypeStruct((1, CONV_CH), F32)],
        input_output_aliases={6: 1},
        compiler_params=_cparams(("arbitrary",)),
    )(c, u, ln_w, ln_b, pw, dy, du)


def _iota2(shape, dim):
    return lax.broadcasted_iota(jnp.int32, shape, dim)


def _gdn_post(pre_q, pre_k, pre_v, b_in, a_in, a_log, dt_bias):
    tm = pre_q.shape[0]
    q, k, v = _silu(pre_q), _silu(pre_k), _silu(pre_v)
    qs, ks = [], []
    for h in range(GDN_HEADS):
        sl = slice(h * GDN_DK, (h + 1) * GDN_DK)
        qh, kh = q[:, sl], k[:, sl]
        qs.append(qh * lax.rsqrt(jnp.sum(qh * qh, axis=-1, keepdims=True) + 1e-6) * (GDN_DK ** -0.5))
        ks.append(kh * lax.rsqrt(jnp.sum(kh * kh, axis=-1, keepdims=True) + 1e-6))
    beta = _sigmoid(b_in)
    g = -jnp.exp(a_log) * _softplus(a_in + dt_bias)
    nb = tm // GDN_CHUNK
    tril = (_iota2((nb, GDN_CHUNK, GDN_CHUNK), 1) >= _iota2((nb, GDN_CHUNK, GDN_CHUNK), 2)).astype(F32)
    gc = mm_nn(tril, g.reshape(nb, GDN_CHUNK, 128), "f32").reshape(tm, 128)
    return jnp.concatenate(qs, axis=1), jnp.concatenate(ks, axis=1), v, beta, gc


def _gdn_prep_specs(tm):
    H = GDN_HALO
    blk = lambda col: pl.BlockSpec((tm, GDN_W), lambda i, _c=col: (i, _c))
    lane = lambda col: pl.BlockSpec((tm, 128), lambda i, _c=col: (i, _c))
    vec = pl.BlockSpec((1, 128), lambda i: (0, 0))
    q0 = C_GQ // GDN_W
    specs = [blk(q0), blk(q0 + 1), blk(q0 + 2),
             _halo_spec(H, tm, GDN_W, q0), _halo_spec(H, tm, GDN_W, q0 + 1), _halo_spec(H, tm, GDN_W, q0 + 2),
             lane(C_BETA // 128), lane(C_ALPHA // 128),
             pl.BlockSpec((SHORT_CONV, GDN_W), lambda i: (0, 0)), pl.BlockSpec((SHORT_CONV, GDN_W), lambda i: (0, 1)),
             pl.BlockSpec((SHORT_CONV, GDN_W), lambda i: (0, 2)), vec, vec]
    return specs, blk, lane, vec


def _gdn_pre(x_refs, h_refs, w_refs, ext_ref, tm):
    first = pl.program_id(0) == 0
    pres = []
    for x_ref, h_ref, w_ref in zip(x_refs, h_refs, w_refs):
        _fill_ext(ext_ref, h_ref[...], x_ref[...], first, GDN_HALO)
        pres.append(_conv_taps(ext_ref, w_ref, SHORT_CONV, GDN_HALO, tm))
    return pres


def gdn_prep_fwd(u, conv_w, a_log, dt_bias, *, tm, name):
    S = u.shape[0]
    specs, blk, lane, vec = _gdn_prep_specs(tm)

    def body(xq, xk, xv, hq, hk, hv, bi, ai, wq, wk, wv, al, db, q_ref, k_ref, v_ref, beta_ref, gc_ref, ext_ref):
        pres = _gdn_pre((xq, xk, xv), (hq, hk, hv), (wq, wk, wv), ext_ref, tm)
        q, k, v, beta, gc = _gdn_post(*pres, bi[...], ai[...], al[...], db[...])
        q_ref[...] = q
        k_ref[...] = k
        v_ref[...] = v
        beta_ref[...] = beta
        gc_ref[...] = gc

    wide = jax.ShapeDtypeStruct((S, GDN_W), F32)
    narrow = jax.ShapeDtypeStruct((S, 128), F32)
    return pl.pallas_call(
        body, name=name, grid=(S // tm,), in_specs=specs,
        out_specs=[blk(0), blk(0), blk(0), lane(0), lane(0)], out_shape=[wide, wide, wide, narrow, narrow],
        scratch_shapes=[pltpu.VMEM((tm + GDN_HALO, GDN_W), F32)],
        compiler_params=_cparams(("parallel",)),
    )(u, u, u, u, u, u, u, u, conv_w, conv_w, conv_w, a_log, dt_bias)


def gdn_prep_bwd(u, conv_w, a_log, dt_bias, dq, dk, dv, dbeta, dgc, du, *, tm, name):
    S = u.shape[0]
    specs, blk, lane, vec = _gdn_prep_specs(tm)
    tail = IN_PAD - C_BETA

    def body(xq, xk, xv, hq, hk, hv, bi, ai, wq, wk, wv, al, db, dq_ref, dk_ref, dv_ref, dbe_ref, dgc_ref, du_in,
             dpre_ref, du_ref, dal_ref, ddb_ref, ext_ref):
        pres = _gdn_pre((xq, xk, xv), (hq, hk, hv), (wq, wk, wv), ext_ref, tm)
        _, vjp = jax.vjp(_gdn_post, *pres, bi[...], ai[...], al[...], db[...])
        dpq, dpk, dpv, dbi, dai, dal, ddb = vjp((dq_ref[...], dk_ref[...], dv_ref[...], dbe_ref[...], dgc_ref[...]))
        dpre_ref[:, 0:GDN_W] = dpq
        dpre_ref[:, GDN_W:2 * GDN_W] = dpk
        dpre_ref[:, 2 * GDN_W:] = dpv
        du_ref[:, 0:128] = dbi.astype(BF16)
        du_ref[:, 128:256] = dai.astype(BF16)
        du_ref[:, 256:] = jnp.zeros((tm, tail - 256), BF16)

        @pl.when(pl.program_id(0) == 0)
        def _():
            dal_ref[...] = jnp.zeros_like(dal_ref)
            ddb_ref[...] = jnp.zeros_like(ddb_ref)

        dal_ref[...] += dal
        ddb_ref[...] += ddb

    n_in = len(specs) + 6
    return pl.pallas_call(
        body, name=name, grid=(S // tm,), in_specs=specs + [blk(0), blk(0), blk(0), lane(0), lane(0), ANY],
        out_specs=[pl.BlockSpec((tm, 3 * GDN_W), lambda i: (i, 0)), pl.BlockSpec((tm, tail), lambda i: (i, C_BETA // tail)),
                   vec, vec],
        out_shape=[jax.ShapeDtypeStruct((S, 3 * GDN_W), F32), jax.ShapeDtypeStruct(du.shape, du.dtype),
                   jax.ShapeDtypeStruct((1, 128), F32), jax.ShapeDtypeStruct((1, 128), F32)],
        input_output_aliases={n_in - 1: 1},
        scratch_shapes=[pltpu.VMEM((tm + GDN_HALO, GDN_W), F32)],
        compiler_params=_cparams(("arbitrary",)),
    )(u, u, u, u, u, u, u, u, conv_w, conv_w, conv_w, a_log, dt_bias, dq, dk, dv, dbeta, dgc, du)


def _lane_col(blk, h):
    return jnp.sum(jnp.where(_iota2(blk.shape, 1) == h, blk, 0.0), axis=1, keepdims=True)


@jax.custom_vjp
def _tri_inv(low):
    n = low.shape[-1]
    r, c = _iota2(low.shape, low.ndim - 2), _iota2(low.shape, low.ndim - 1)
    eye = (r == c).astype(F32)
    t = eye - jnp.where((r // 2 == c // 2) & (r > c), low, 0.0)
    s = 2
    while s < n:
        off = jnp.where((r // (2 * s) == c // (2 * s)) & (r // s > c // s), low, 0.0)
        prec = "bf16" if s <= 8 else "bf16x3"
        t = t - _nn_raw(t, _nn_raw(off, t, prec), prec)
        s *= 2
    return t


def _tri_inv_f(low):
    t = _tri_inv(low)
    return t, t


def _tri_inv_b(t, dt):
    d = -_nt_raw(_tn_raw(t, dt, "bf16x3"), t, "bf16x3")
    r, c = _iota2(d.shape, d.ndim - 2), _iota2(d.shape, d.ndim - 1)
    return (jnp.where(r > c, d, 0.0),)


_tri_inv.defvjp(_tri_inv_f, _tri_inv_b)


@jax.custom_vjp
def _tri_inv_saved(low, t):
    return t


_tri_inv_saved.defvjp(lambda low, t: (t, t), lambda t, dt: (_tri_inv_b(t, dt)[0], jnp.zeros_like(t)))


def _gdn_group(s0, q, k, v, z, beta_blk, gc_blk, nw, h0, t_saved=None, with_t=False):
    C = GDN_CHUNK
    HP, R, _ = q.shape
    nb = R // C
    B = HP * nb
    q3, k3, v3 = (t.reshape(B, C, GDN_DK) for t in (q, k, v))
    b3 = jnp.stack([_lane_col(beta_blk, h0 + j) for j in range(HP)]).reshape(B, C, 1)
    g3 = jnp.stack([_lane_col(gc_blk, h0 + j) for j in range(HP)]).reshape(B, C, 1)
    r, c = _iota2((B, C, C), 1), _iota2((B, C, C), 2)
    causal, strict = r >= c, r > c
    g_t = gc_blk.T
    rows = [jnp.sum(jnp.where(_iota2((128, R), 0) == h0 + j, g_t, 0.0), axis=0, keepdims=True) for j in range(HP)]
    g_row = jnp.stack([rows[j][:, i * C:(i + 1) * C] for j in range(HP) for i in range(nb)])
    decay = jnp.where(causal, jnp.exp(jnp.where(causal, g3 - g_row, 0.0)), 0.0)
    low = jnp.where(strict, b3 * mm_nt(k3, k3) * decay, 0.0)
    t = _tri_inv(low) if t_saved is None else _tri_inv_saved(low, t_saved)
    eg = jnp.exp(g3)
    four = lambda x: x.reshape((HP, nb) + x.shape[1:])
    w_v = four(mm_nn(t, v3 * b3))
    w_k = four(mm_nn(t, k3 * (b3 * eg)))
    qk = four(jnp.where(causal, mm_nt(q3, k3) * decay, 0.0))
    q_dec = four(q3 * eg)
    g_last = jnp.sum(jnp.where(_iota2((B, C, 1), 1) == C - 1, g3, 0.0), axis=1, keepdims=True)
    k_dec = four(k3 * jnp.exp(g_last - g3))
    e_last = four(jnp.exp(g_last))
    s, outs = s0, []
    for i in range(nb):
        v_new = w_v[:, i] - mm_nn(w_k[:, i], s)
        outs.append(mm_nn(q_dec[:, i], s) + mm_nn(qk[:, i], v_new))
        s = s * e_last[:, i] + mm_tn(k_dec[:, i], v_new)
    o = jnp.concatenate(outs, axis=1)
    y = o * lax.rsqrt(jnp.mean(o * o, axis=-1, keepdims=True) + 1e-6) * nw * _silu(z)
    return (s, y, t) if with_t else (s, y)


def _heads(ref, HP):
    return jnp.stack([ref[:, j * GDN_DK:(j + 1) * GDN_DK] for j in range(HP)])


def gdn_core_fwd(q, k, v, u, beta, gc, nw, *, name):
    S = q.shape[0]
    R = GDN_CHUNK * GDN_GROUP
    G = S // R
    HP = GDN_HEADS_PER_STEP
    W = HP * GDN_DK
    NT = HP * GDN_GROUP
    blk = pl.BlockSpec((R, W), lambda g, h: (g, h))
    lane = pl.BlockSpec((R, 128), lambda g, h: (g, 0))
    st = pl.BlockSpec((1, HP, GDN_DK, GDN_DK), lambda g, h: (g, h, 0, 0))
    inv = pl.BlockSpec((1, NT, GDN_CHUNK, GDN_CHUNK), lambda g, h: (g, h, 0, 0))

    def body(q_ref, k_ref, v_ref, z_ref, be_ref, gc_ref, nw_ref, y_ref, st_ref, t_ref, s_ref):
        g, hs = pl.program_id(0), pl.program_id(1)
        s0 = jnp.where(g == 0, 0.0, s_ref[hs])
        st_ref[0] = s0
        s1, y, t = _gdn_group(s0, _heads(q_ref, HP), _heads(k_ref, HP), _heads(v_ref, HP), _heads(z_ref, HP), be_ref[...],
                              gc_ref[...], nw_ref[...], hs * HP, with_t=True)
        s_ref[hs] = s1
        t_ref[0] = t
        for j in range(HP):
            y_ref[:, j * GDN_DK:(j + 1) * GDN_DK] = y[j].astype(BF16)

    return pl.pallas_call(
        body, name=name, grid=(G, GDN_HEADS // HP),
        in_specs=[blk, blk, blk, pl.BlockSpec((R, W), lambda g, h: (g, C_GZ // W + h)), lane, lane,
                  pl.BlockSpec((1, 128), lambda g, h: (0, 0))],
        out_specs=[blk, st, inv],
        out_shape=[jax.ShapeDtypeStruct((S, GDN_W), BF16), jax.ShapeDtypeStruct((G, GDN_HEADS, GDN_DK, GDN_DK), F32),
                   jax.ShapeDtypeStruct((G, GDN_HEADS * GDN_GROUP, GDN_CHUNK, GDN_CHUNK), F32)],
        scratch_shapes=[pltpu.VMEM((GDN_HEADS // HP, HP, GDN_DK, GDN_DK), F32)],
        compiler_params=_cparams(("arbitrary", "arbitrary")),
    )(q, k, v, u, beta, gc, nw)


def gdn_core_bwd(q, k, v, u, beta, gc, nw, states, tinv, dy, du, *, name):
    S = q.shape[0]
    R = GDN_CHUNK * GDN_GROUP
    G = S // R
    HP = GDN_HEADS_PER_STEP
    W = HP * GDN_DK
    blk = pl.BlockSpec((R, W), lambda g, h: (G - 1 - g, h))
    lane = pl.BlockSpec((R, 128), lambda g, h: (G - 1 - g, 0))
    vec = pl.BlockSpec((1, 128), lambda g, h: (0, 0))

    def body(q_ref, k_ref, v_ref, z_ref, be_ref, gc_ref, nw_ref, st_ref, t_ref, *rest):
        dy_refs = rest[:HP]
        dq_ref, dk_ref, dv_ref, dz_ref, dbe_ref, dgc_ref, dnw_ref, ds_ref = rest[HP + 1:]
        g, hs = pl.program_id(0), pl.program_id(1)

        @pl.when(hs == 0)
        def _():
            dbe_ref[...] = jnp.zeros_like(dbe_ref)
            dgc_ref[...] = jnp.zeros_like(dgc_ref)

        @pl.when((hs == 0) & (g == 0))
        def _():
            dnw_ref[...] = jnp.zeros_like(dnw_ref)

        _, vjp = jax.vjp(functools.partial(_gdn_group, h0=hs * HP, t_saved=t_ref[0]), st_ref[0], _heads(q_ref, HP),
                         _heads(k_ref, HP), _heads(v_ref, HP), _heads(z_ref, HP), be_ref[...], gc_ref[...], nw_ref[...])
        ds_in = jnp.where(g == 0, 0.0, ds_ref[hs])
        dy = jnp.stack([r[...] for r in dy_refs])
        ds0, dq, dk, dv, dz, dbe, dgc, dnw = vjp((ds_in, dy))
        ds_ref[hs] = ds0
        for j in range(HP):
            sl = slice(j * GDN_DK, (j + 1) * GDN_DK)
            dq_ref[:, sl] = dq[j]
            dk_ref[:, sl] = dk[j]
            dv_ref[:, sl] = dv[j]
            dz_ref[:, sl] = dz[j].astype(BF16)
        dbe_ref[...] += dbe
        dgc_ref[...] += dgc
        dnw_ref[...] += dnw

    wide = jax.ShapeDtypeStruct((S, GDN_W), F32)
    narrow = jax.ShapeDtypeStruct((S, 128), F32)
    return pl.pallas_call(
        body, name=name, grid=(G, GDN_HEADS // HP),
        in_specs=[blk, blk, blk, pl.BlockSpec((R, W), lambda g, h: (G - 1 - g, C_GZ // W + h)), lane, lane, vec,
                  pl.BlockSpec((1, HP, GDN_DK, GDN_DK), lambda g, h: (G - 1 - g, h, 0, 0)),
                  pl.BlockSpec((1, HP * GDN_GROUP, GDN_CHUNK, GDN_CHUNK), lambda g, h: (G - 1 - g, h, 0, 0))]
        + [pl.BlockSpec((R, GDN_DK), lambda g, h, _j=j: (G - 1 - g, CONV_CH // GDN_DK + h * HP + _j)) for j in range(HP)]
        + [ANY],
        out_specs=[blk, blk, blk, pl.BlockSpec((R, W), lambda g, h: (G - 1 - g, C_GZ // W + h)), lane, lane, vec],
        out_shape=[wide, wide, wide, jax.ShapeDtypeStruct(du.shape, du.dtype), narrow, narrow,
                   jax.ShapeDtypeStruct((1, 128), F32)],
        input_output_aliases={9 + HP: 3},
        scratch_shapes=[pltpu.VMEM((GDN_HEADS // HP, HP, GDN_DK, GDN_DK), F32)],
        compiler_params=_cparams(("arbitrary", "arbitrary")),
    )(q, k, v, u, beta, gc, nw, states, tinv, *([dy] * HP), du)


def rope_tables(S):
    half = ROPE_DIM // 2
    inv = ROPE_THETA ** (-jnp.arange(half, dtype=F32) / half)
    ang = jnp.arange(S, dtype=F32)[:, None] * inv[None, :]
    cos, sin = jnp.cos(ang), jnp.sin(ang)
    rest = ATT_HD - ROPE_DIM
    c = jnp.concatenate([cos, cos, jnp.ones((S, rest), F32)], axis=1)
    s1 = jnp.concatenate([-sin, jnp.zeros((S, ATT_HD - half), F32)], axis=1)
    s2 = jnp.concatenate([jnp.zeros((S, half), F32), sin, jnp.zeros((S, rest), F32)], axis=1)
    return tuple(jnp.tile(t, (1, 2)) for t in (c, s1, s2))


def _rope(x, c, s1, s2):
    half = ROPE_DIM // 2
    return x * c + pltpu.roll(x, ATT_W - half, 1) * s1 + pltpu.roll(x, half, 1) * s2


def _unrope(dy, c, s1, s2):
    half = ROPE_DIM // 2
    return dy * c + pltpu.roll(dy * s1, half, 1) + pltpu.roll(dy * s2, ATT_W - half, 1)


def att_prep_fwd(u, tables, *, tm, name):
    S = u.shape[0]
    blk = lambda col: pl.BlockSpec((tm, ATT_W), lambda i, _c=col: (i, _c))
    tab = pl.BlockSpec((tm, 128), lambda i: (i, 0))

    def body(q_ref, k_ref, v_ref, c_ref, s1_ref, s2_ref, qo_ref, ko_ref, vo_ref):
        reps = ATT_W // 128
        c, s1, s2 = (jnp.tile(t[...], (1, reps)) for t in (c_ref, s1_ref, s2_ref))
        qo_ref[...] = (_rope(q_ref[...], c, s1, s2) * (ATT_HD ** -0.5)).astype(BF16)
        ko_ref[...] = _rope(k_ref[...], c, s1, s2).astype(BF16)
        vo_ref[...] = v_ref[...].astype(BF16)

    out = jax.ShapeDtypeStruct((S, ATT_W), BF16)
    return pl.pallas_call(
        body, name=name, grid=(S // tm,),
        in_specs=[blk(C_AQ // ATT_W), blk(C_AK // ATT_W), blk(C_AV // ATT_W), tab, tab, tab],
        out_specs=[blk(0)] * 3, out_shape=[out] * 3, compiler_params=_cparams(("parallel",)),
    )(u, u, u, *tables)


def att_prep_bwd(dqs, dks, dvs, tables, du, *, tm, name):
    S = dqs[0].shape[0]
    blk = pl.BlockSpec((tm, ATT_W), lambda i: (i, 0))
    tab = pl.BlockSpec((tm, 128), lambda i: (i, 0))

    def body(*refs):
        dq, dk, dv = (refs[3 * j][...].astype(F32) + refs[3 * j + 1][...].astype(F32) + refs[3 * j + 2][...].astype(F32)
                      for j in range(3))
        c_ref, s1_ref, s2_ref, _, o_ref = refs[9:]
        reps = ATT_W // 128
        c, s1, s2 = (jnp.tile(t[...], (1, reps)) for t in (c_ref, s1_ref, s2_ref))
        o_ref[:, 0:ATT_W] = (_unrope(dq, c, s1, s2) * (ATT_HD ** -0.5)).astype(BF16)
        o_ref[:, ATT_W:2 * ATT_W] = _unrope(dk, c, s1, s2).astype(BF16)
        o_ref[:, 2 * ATT_W:] = dv.astype(BF16)

    return pl.pallas_call(
        body, name=name, grid=(S // tm,), in_specs=[blk] * 9 + [tab] * 3 + [ANY],
        out_specs=pl.BlockSpec((tm, 3 * ATT_W), lambda i: (i, C_AQ // (3 * ATT_W))),
        out_shape=jax.ShapeDtypeStruct(du.shape, du.dtype), input_output_aliases={12: 0},
        compiler_params=_cparams(("parallel",)),
    )(*dqs, *dks, *dvs, *tables, du)


def _band_masks():
    qi, ki = _iota2((ATT_BLOCK, ATT_BLOCK), 0), _iota2((ATT_BLOCK, ATT_BLOCK), 1)
    return qi <= ki, ki <= qi


def _pair_diag(x):
    first = _iota2(x.shape, 1) < ATT_HD
    zero = jnp.zeros_like(x)
    return jnp.concatenate([jnp.where(first, x, zero), jnp.where(first, zero, x)], axis=0)


def att_pattern_fwd(qr, kr, vb, dil, *, name):
    S = qr.shape[0]
    L = S // dil
    nb = L // ATT_BLOCK
    view = lambda t: t.reshape(L, dil * t.shape[1])
    cur = pl.BlockSpec((ATT_BLOCK, ATT_W), lambda r, n: (n, r))
    prev = pl.BlockSpec((ATT_BLOCK, ATT_W), lambda r, n: (jnp.maximum(n - 1, 0), r))

    def body(q_ref, kc_ref, kp_ref, vc_ref, vp_ref, o_ref, l_ref):
        has_prev = pl.program_id(1) > 0
        m_prev, m_cur = _band_masks()
        m_prev = m_prev & has_prev
        first = _iota2((ATT_BLOCK, 128), 1) < ATT_HD
        lane = _iota2((ATT_BLOCK, 128), 1)
        stats = jnp.zeros((ATT_BLOCK, 128), F32)
        pairs = range(ATT_HEADS // 2)
        sls = [slice(p * 128, (p + 1) * 128) for p in pairs]
        sps = [_nt_raw(q_ref[:, sl], _pair_diag(kp_ref[:, sl]), "bf16") for sl in sls]
        scs = [_nt_raw(q_ref[:, sl], _pair_diag(kc_ref[:, sl]), "bf16") for sl in sls]
        probs, inv_dens = [], []
        for p in pairs:
            pps, pcs, dens, lses = [], [], [], []
            for half in range(2):
                hs = slice(half * 128, (half + 1) * 128)
                sp_h, sc_h = jnp.where(m_prev, sps[p][:, hs], NEG_INF), jnp.where(m_cur, scs[p][:, hs], NEG_INF)
                m = jnp.maximum(jnp.max(sp_h, axis=1, keepdims=True), jnp.max(sc_h, axis=1, keepdims=True))
                pp, pc = jnp.exp(sp_h - m), jnp.exp(sc_h - m)
                den = jnp.sum(pp, axis=1, keepdims=True) + jnp.sum(pc, axis=1, keepdims=True)
                pps.append(pp.astype(BF16))
                pcs.append(pc.astype(BF16))
                dens.append(den)
                lses.append(m + jnp.log(den))
            probs.append((jnp.concatenate(pps, axis=1), jnp.concatenate(pcs, axis=1)))
            inv_dens.append(1.0 / jnp.where(first, dens[0], dens[1]))
            stats = jnp.where(lane == 2 * p, lses[0], jnp.where(lane == 2 * p + 1, lses[1], stats))
        outs = [_nn_raw(probs[p][0], _pair_diag(vp_ref[:, sls[p]]), "bf16")
                + _nn_raw(probs[p][1], _pair_diag(vc_ref[:, sls[p]]), "bf16") for p in pairs]
        for p in pairs:
            o_ref[:, sls[p]] = (outs[p] * inv_dens[p]).astype(BF16)
        l_ref[...] = stats

    narrow = pl.BlockSpec((ATT_BLOCK, 128), lambda r, n: (n, r))
    o, l = pl.pallas_call(
        body, name=name, grid=(dil, nb), in_specs=[cur, cur, prev, cur, prev], out_specs=[cur, narrow],
        out_shape=[jax.ShapeDtypeStruct((L, dil * ATT_W), BF16), jax.ShapeDtypeStruct((L, dil * 128), F32)],
        compiler_params=_cparams(("parallel", "arbitrary")),
    )(view(qr), view(kr), view(kr), view(vb), view(vb))
    return o.reshape(S, ATT_W), l.reshape(S, 128)


def _head_spread():
    return (_iota2((128, ATT_W), 1) // ATT_HD == _iota2((128, ATT_W), 0)).astype(F32)


def att_combine_fwd(os_, ls, u, *, tm, name):
    S = u.shape[0]
    blk = lambda col: pl.BlockSpec((tm, ATT_W), lambda i, _c=col: (i, _c))
    lane = pl.BlockSpec((tm, 128), lambda i: (i, 0))

    def body(o1, o2, o3, l1, l2, l3, g_ref, y_ref, o_ref, lse_ref):
        a, b, c = l1[...], l2[...], l3[...]
        m = jnp.maximum(jnp.maximum(a, b), c)
        ea, eb, ec = jnp.exp(a - m), jnp.exp(b - m), jnp.exp(c - m)
        den = ea + eb + ec
        spread = _head_spread()
        wa, wb, wc = (_nn_raw(e / den, spread, "bf16x3") for e in (ea, eb, ec))
        o = wa * o1[...].astype(F32) + wb * o2[...].astype(F32) + wc * o3[...].astype(F32)
        o_ref[...] = o
        lse_ref[...] = m + jnp.log(den)
        y_ref[...] = (o * _silu(g_ref[...])).astype(BF16)

    return pl.pallas_call(
        body, name=name, grid=(S // tm,), in_specs=[blk(0)] * 3 + [lane] * 3 + [blk(C_AG // ATT_W)],
        out_specs=[blk(0), blk(0), lane],
        out_shape=[jax.ShapeDtypeStruct((S, ATT_W), BF16), jax.ShapeDtypeStruct((S, ATT_W), F32),
                   jax.ShapeDtypeStruct((S, 128), F32)],
        compiler_params=_cparams(("parallel",)),
    )(*os_, *ls, u)


def att_combine_bwd(dy, o, u, du, *, tm, name):
    S = u.shape[0]
    cw = 256
    base = (CONV_CH + GDN_W) // cw
    blk = lambda col: pl.BlockSpec((tm, ATT_W), lambda i, _c=col: (i, _c))

    def body(dy0, dy1, dy2, o_ref, g_ref, du_in, do_ref, dg_ref, dl_ref):
        g, d, o = g_ref[...], jnp.concatenate([dy0[...], dy1[...], dy2[...]], axis=1), o_ref[...]
        sg = _sigmoid(g)
        d_o = d * (g * sg)
        do_ref[...] = d_o.astype(BF16)
        dg_ref[...] = (d * o * (sg * (1.0 + g * (1.0 - sg)))).astype(BF16)
        dl_ref[...] = _nt_raw(d_o * o, _head_spread(), "bf16x3")

    return pl.pallas_call(
        body, name=name, grid=(S // tm,),
        in_specs=[pl.BlockSpec((tm, cw), lambda i, _j=j: (i, base + _j)) for j in range(ATT_W // cw)]
        + [blk(0), blk(C_AG // ATT_W), ANY],
        out_specs=[blk(0), blk(C_AG // ATT_W), pl.BlockSpec((tm, 128), lambda i: (i, 0))],
        out_shape=[jax.ShapeDtypeStruct((S, ATT_W), BF16), jax.ShapeDtypeStruct(du.shape, du.dtype),
                   jax.ShapeDtypeStruct((S, 128), F32)],
        input_output_aliases={5: 1},
        compiler_params=_cparams(("parallel",)),
    )(dy, dy, dy, o, u, du)


def att_pattern_bwd(qr, kr, vb, do, delta, lse, dil, *, name):
    S = qr.shape[0]
    L = S // dil
    nb = L // ATT_BLOCK
    view = lambda t: t.reshape(L, dil * t.shape[1])
    cur = pl.BlockSpec((ATT_BLOCK, ATT_W), lambda r, n: (jnp.minimum(n, nb - 1), r))
    prev = pl.BlockSpec((ATT_BLOCK, ATT_W), lambda r, n: (jnp.maximum(n - 1, 0), r))
    narrow = pl.BlockSpec((ATT_BLOCK, 128), lambda r, n: (jnp.minimum(n, nb - 1), r))

    def body(q_ref, kc_ref, kp_ref, vc_ref, vp_ref, do_ref, dl_ref, l_ref, dq_ref, dk_ref, dv_ref, ck_ref, cv_ref):
        n = pl.program_id(1)

        @pl.when(n < nb)
        def _():
            m_prev, m_cur = _band_masks()
            m_prev = m_prev & (n > 0)
            m_prev2, m_cur2 = jnp.concatenate([m_prev, m_prev], axis=1), jnp.concatenate([m_cur, m_cur], axis=1)
            first = _iota2((ATT_BLOCK, 128), 1) < ATT_HD
            wide = (ATT_BLOCK, 128)
            halves = lambda a, b: jnp.concatenate([jnp.broadcast_to(a, wide), jnp.broadcast_to(b, wide)], axis=1)
            fold = lambda t: jnp.where(first, t[:ATT_BLOCK], t[ATT_BLOCK:])
            pairs = range(ATT_HEADS // 2)
            sls = [slice(p * 128, (p + 1) * 128) for p in pairs]
            qs, dos = [q_ref[:, sl] for sl in sls], [do_ref[:, sl] for sl in sls]
            kps, kcs, vps, vcs = ([_pair_diag(r[:, sl]) for sl in sls] for r in (kp_ref, kc_ref, vp_ref, vc_ref))
            s_p = [_nt_raw(qs[p], kps[p], "bf16") for p in pairs]
            s_c = [_nt_raw(qs[p], kcs[p], "bf16") for p in pairs]
            dp_p = [_nt_raw(dos[p], vps[p], "bf16") for p in pairs]
            dp_c = [_nt_raw(dos[p], vcs[p], "bf16") for p in pairs]
            pps, pcs, dsps, dscs = [], [], [], []
            for p in pairs:
                delta = halves(dl_ref[:, 2 * p:2 * p + 1], dl_ref[:, 2 * p + 1:2 * p + 2])
                lse2 = halves(l_ref[:, 2 * p:2 * p + 1], l_ref[:, 2 * p + 1:2 * p + 2])
                pp = jnp.where(m_prev2, jnp.exp(s_p[p] - lse2), 0.0)
                pc = jnp.where(m_cur2, jnp.exp(s_c[p] - lse2), 0.0)
                dsps.append((pp * (dp_p[p] - delta)).astype(BF16))
                dscs.append((pc * (dp_c[p] - delta)).astype(BF16))
                pps.append(pp.astype(BF16))
                pcs.append(pc.astype(BF16))
            dqs = [_nn_raw(dsps[p], kps[p], "bf16") + _nn_raw(dscs[p], kcs[p], "bf16") for p in pairs]
            dk_prev = [fold(_tn_raw(dsps[p], qs[p], "bf16")) for p in pairs]
            dv_prev = [fold(_tn_raw(pps[p], dos[p], "bf16")) for p in pairs]
            dk_cur = [fold(_tn_raw(dscs[p], qs[p], "bf16")) for p in pairs]
            dv_cur = [fold(_tn_raw(pcs[p], dos[p], "bf16")) for p in pairs]
            for p in pairs:
                dq_ref[:, sls[p]] = dqs[p].astype(BF16)

            @pl.when(n > 0)
            def _():
                for p in pairs:
                    dk_ref[:, sls[p]] = (ck_ref[:, sls[p]] + dk_prev[p]).astype(BF16)
                    dv_ref[:, sls[p]] = (cv_ref[:, sls[p]] + dv_prev[p]).astype(BF16)

            for p in pairs:
                ck_ref[:, sls[p]] = dk_cur[p]
                cv_ref[:, sls[p]] = dv_cur[p]

        @pl.when(n == nb)
        def _():
            dk_ref[...] = ck_ref[...].astype(BF16)
            dv_ref[...] = cv_ref[...].astype(BF16)

    out = jax.ShapeDtypeStruct((L, dil * ATT_W), BF16)
    dq, dk, dv = pl.pallas_call(
        body, name=name, grid=(dil, nb + 1), in_specs=[cur, cur, prev, cur, prev, cur, narrow, narrow],
        out_specs=[cur, prev, prev], out_shape=[out, out, out],
        scratch_shapes=[pltpu.VMEM((ATT_BLOCK, ATT_W), F32), pltpu.VMEM((ATT_BLOCK, ATT_W), F32)],
        compiler_params=_cparams(("arbitrary", "arbitrary")),
    )(view(qr), view(kr), view(kr), view(vb), view(vb), view(do), view(delta), view(lse))
    return dq.reshape(S, ATT_W), dk.reshape(S, ATT_W), dv.reshape(S, ATT_W)


def _loss_rows(x, w, tgt):
    err = _rms_fn(x, w) - tgt
    return jnp.sum(0.5 * jnp.mean(err * err, axis=-1, keepdims=True), axis=0, keepdims=True)


def loss_head(x, w, tgt, *, tm, name):
    S, D = x.shape

    def body(x_ref, w_ref, t_ref, l_ref, dx_ref, dw_ref):
        val, vjp = jax.vjp(_loss_rows, x_ref[...], w_ref[...], t_ref[...])
        dx, dw, _ = vjp(jnp.ones((1, 1), F32))
        dx_ref[...] = dx

        @pl.when(pl.program_id(0) == 0)
        def _():
            l_ref[...] = jnp.zeros_like(l_ref)
            dw_ref[...] = jnp.zeros_like(dw_ref)

        l_ref[...] += val
        dw_ref[...] += dw

    row = pl.BlockSpec((tm, D), lambda i: (i, 0))
    vec = pl.BlockSpec((1, D), lambda i: (0, 0))
    one = pl.BlockSpec((1, 1), lambda i: (0, 0))
    return pl.pallas_call(
        body, name=name, grid=(S // tm,), in_specs=[row, vec, row], out_specs=[one, row, vec],
        out_shape=[jax.ShapeDtypeStruct((1, 1), F32), jax.ShapeDtypeStruct((S, D), F32), jax.ShapeDtypeStruct((1, D), F32)],
        compiler_params=_cparams(("arbitrary",)),
    )(x, w, tgt)


def adam(w, g, m, v, *, name):
    shape = w.shape
    C = shape[-1]
    R = w.size // C
    br = R
    while br * C * 4 > (1 << 21) and br % 16 == 0:
        br //= 2
    two = lambda t: t.reshape(R, C)

    def body(w_ref, g_ref, m_ref, v_ref, d_ref, mo_ref, vo_ref):
        gg = g_ref[...]
        m_new = ADAM_B1 * m_ref[...] + (1.0 - ADAM_B1) * gg
        v_new = ADAM_B2 * v_ref[...] + (1.0 - ADAM_B2) * jnp.square(gg)
        m_hat = m_new / (1.0 - ADAM_B1 ** ADAM_STEP)
        v_hat = v_new / (1.0 - ADAM_B2 ** ADAM_STEP)
        d_ref[...] = -ADAM_LR * (m_hat / (jnp.sqrt(v_hat) + ADAM_EPS) + ADAM_WD * w_ref[...])
        mo_ref[...] = m_new
        vo_ref[...] = v_new

    blk = pl.BlockSpec((br, C), lambda i: (i, 0))
    out = jax.ShapeDtypeStruct((R, C), F32)
    d, mo, vo = pl.pallas_call(
        body, name=name, grid=(R // br,), in_specs=[blk] * 4, out_specs=[blk] * 3, out_shape=[out] * 3,
        compiler_params=_cparams(("parallel",)),
    )(two(w), two(g), two(m), two(v))
    return d.reshape(shape), mo.reshape(shape), vo.reshape(shape)


MESH_IDS = pl.DeviceIdType.MESH
ANY = pl.BlockSpec(memory_space=pl.ANY)


def _my_id():
    return 4 * lax.axis_index("x") + 2 * lax.axis_index("y") + lax.axis_index("c")


def _peer(k):
    x, y, c = lax.axis_index("x"), lax.axis_index("y"), lax.axis_index("c")
    flip = lambda v, bit: 1 - v if bit else v
    return (flip(x, k & 4), flip(y, k & 2), flip(c, k & 1))


def all_gather_two_level(arrs, *, name):
    n = len(arrs)

    def body(*refs):
        ins, outs = refs[:n], refs[n:2 * n]
        send, recv, local = refs[2 * n:]
        x, y, c = lax.axis_index("x"), lax.axis_index("y"), lax.axis_index("c")
        me, sibling = (x, y, c), (x, y, 1 - c)
        chips = [(1 - x, y), (x, 1 - y), (1 - x, 1 - y)]

        def slot(a, dev):
            return outs[a].at[4 * dev[0] + 2 * dev[1] + dev[2]]

        def copy(a, k, block, to, src=None):
            return pltpu.make_async_remote_copy(
                src_ref=slot(a, block) if src is None else src, dst_ref=slot(a, block), send_sem=send.at[a, k],
                recv_sem=recv.at[a, k], device_id=to, device_id_type=MESH_IDS)

        mine = [pltpu.make_async_copy(ins[a], slot(a, me), local.at[a]) for a in range(n)]
        first = [copy(a, 1 + j, me, (*chip, c), src=ins[a]) for a in range(n) for j, chip in enumerate(chips)]
        first += [copy(a, 0, me, sibling, src=ins[a]) for a in range(n)]
        for cp in first + mine:
            cp.start()
        passed = []
        for a in range(n):
            for j, chip in enumerate(chips):
                copy(a, 1 + j, (*chip, c), me).wait_recv()
                fwd = copy(a, 4 + j, (*chip, c), sibling)
                fwd.start()
                passed.append(fwd)
        for a in range(n):
            copy(a, 0, sibling, me).wait_recv()
            for j, chip in enumerate(chips):
                copy(a, 4 + j, (*chip, 1 - c), me).wait_recv()
        for cp in first + passed:
            cp.wait_send()
        for cp in mine:
            cp.wait()

    return pl.pallas_call(
        body, name=name, in_specs=[ANY] * n, out_specs=[ANY] * n,
        out_shape=[jax.ShapeDtypeStruct((N_DEV,) + a.shape, a.dtype) for a in arrs],
        scratch_shapes=[pltpu.SemaphoreType.DMA((n, N_DEV - 1)), pltpu.SemaphoreType.DMA((n, N_DEV - 1)),
                        pltpu.SemaphoreType.DMA((n,))],
        compiler_params=pltpu.CompilerParams(has_side_effects=True),
    )(*arrs)


def scatter_exchange(groups, pack, *, name):
    flat = [a for grp in groups for a in grp]
    n = len(flat) + 1
    shapes = [jax.ShapeDtypeStruct((N_DEV, len(grp), grp[0].shape[0] // N_DEV, grp[0].shape[1]), grp[0].dtype) for grp in groups]
    shapes.append(jax.ShapeDtypeStruct((N_DEV,) + pack.shape, pack.dtype))
    index = [(gi, li) for gi, grp in enumerate(groups) for li in range(len(grp))]

    def body(*refs):
        ins, outs = refs[:n], refs[n:n + len(shapes)]
        send, recv, local = refs[n + len(shapes):]
        me = _my_id()
        started = []
        for a in range(n):
            if a < n - 1:
                gi, li = index[a]
                r = ins[a].shape[0] // N_DEV
                src = lambda j, _a=a, _r=r: ins[_a].at[pl.ds(pl.multiple_of(j * _r, 8), _r), :]
                dst = outs[gi].at[me, li]
            else:
                src = lambda j, _a=a: ins[_a]
                dst = outs[-1].at[me]
            lc = pltpu.make_async_copy(src(me), dst, local.at[a])
            lc.start()
            started.append(lc)
            for k in range(1, N_DEV):
                cp = pltpu.make_async_remote_copy(src_ref=src(me ^ k), dst_ref=dst, send_sem=send.at[a, k - 1],
                                                  recv_sem=recv.at[a, k - 1], device_id=_peer(k), device_id_type=MESH_IDS)
                cp.start()
                started.append(cp)
        for cp in started:
            cp.wait()

    return pl.pallas_call(
        body, name=name, in_specs=[ANY] * n, out_specs=[ANY] * len(shapes), out_shape=shapes,
        scratch_shapes=[pltpu.SemaphoreType.DMA((n, N_DEV - 1)), pltpu.SemaphoreType.DMA((n, N_DEV - 1)),
                        pltpu.SemaphoreType.DMA((n,))],
        compiler_params=pltpu.CompilerParams(has_side_effects=True),
    )(*flat, pack)


def slot_sum(x, *, name):
    _, A, R, C = x.shape
    br = R
    while br * C * 4 * N_DEV > (1 << 23) and br % 16 == 0:
        br //= 2

    def body(x_ref, o_ref):
        acc = x_ref[0, 0].astype(F32)
        for s in range(1, N_DEV):
            acc = acc + x_ref[s, 0].astype(F32)
        o_ref[0] = acc

    return pl.pallas_call(
        body, name=name, grid=(A, R // br),
        in_specs=[pl.BlockSpec((N_DEV, 1, br, C), lambda a, i: (0, a, i, 0))],
        out_specs=pl.BlockSpec((1, br, C), lambda a, i: (a, i, 0)),
        out_shape=jax.ShapeDtypeStruct((A, R, C), F32),
        compiler_params=_cparams(("parallel", "parallel")),
    )(x)


HBM_SPEC = pl.BlockSpec(memory_space=pltpu.HBM)
SEM_SPEC = pl.BlockSpec(memory_space=pltpu.SEMAPHORE)
DATAFLOW = pltpu.SideEffectType.DATAFLOW_SIDE_EFFECTING


def _push_copies(src_refs, land_refs, send_sems, recv_sems, by_rows):
    me = _my_id()
    copies = []
    for a, (src, land) in enumerate(zip(src_refs, land_refs)):
        rows = land.shape[1]
        for k in range(1, N_DEV):
            piece = src.at[pl.ds(pl.multiple_of((me ^ k) * rows, 8), rows), :] if by_rows else src
            copies.append(pltpu.make_async_remote_copy(
                src_ref=piece, dst_ref=land.at[me], send_sem=send_sems[a].at[k - 1], recv_sem=recv_sems[a].at[k - 1],
                device_id=_peer(k), device_id_type=MESH_IDS))
    return copies


def push_start(srcs, lands, *, by_rows, name):
    n = len(srcs)

    def body(*refs):
        src_refs, land_refs = refs[:n], refs[n:2 * n]
        send_sems, recv_sems = refs[2 * n:3 * n], refs[3 * n:4 * n]
        token = refs[6 * n]
        for cp in _push_copies(src_refs, land_refs, send_sems, recv_sems, by_rows):
            cp.start()
        token[...] = jnp.zeros_like(token)

    sems = [pltpu.SemaphoreType.DMA((N_DEV - 1,))] * (2 * n)
    bufs = [pltpu.HBM(a.shape, a.dtype) for a in list(srcs) + list(lands)]
    outs = pl.pallas_call(
        body, name=name, out_shape=tuple(sems + bufs + [jax.ShapeDtypeStruct((8, 128), F32)]),
        in_specs=[HBM_SPEC] * (2 * n), out_specs=tuple([SEM_SPEC] * (2 * n) + [HBM_SPEC] * (2 * n) + [pl.BlockSpec(memory_space=pltpu.VMEM)]),
        input_output_aliases={i: 2 * n + i for i in range(2 * n)},
        compiler_params=pltpu.CompilerParams(has_side_effects=DATAFLOW),
    )(*[pltpu.with_memory_space_constraint(a, pltpu.HBM) for a in list(srcs) + list(lands)])
    return outs[:n], outs[n:2 * n], outs[2 * n:3 * n], outs[3 * n:4 * n], outs[4 * n]


def push_wait(send_sems, recv_sems, srcs, lands, after, *, by_rows, name):
    n = len(srcs)

    def body(*refs):
        src_refs, land_refs = refs[:n], refs[n:2 * n]
        send, recv = refs[2 * n:3 * n], refs[3 * n:4 * n]
        for cp in _push_copies(src_refs, land_refs, send, recv, by_rows):
            cp.wait_send()
            cp.wait_recv()

    outs = pl.pallas_call(
        body, name=name, out_shape=tuple(pltpu.HBM(a.shape, a.dtype) for a in list(srcs) + list(lands)),
        in_specs=[HBM_SPEC] * (2 * n) + [SEM_SPEC] * (2 * n) + [ANY], out_specs=tuple([HBM_SPEC] * (2 * n)),
        input_output_aliases={i: i for i in range(2 * n)},
        compiler_params=pltpu.CompilerParams(has_side_effects=DATAFLOW),
    )(*srcs, *lands, *send_sems, *recv_sems, after)
    return outs[n:]


LANDING_BLOCKS = 4


def _landing(src, slots_shape, after=None, *, by_rows, name):
    rows, cols = slots_shape[1], slots_shape[2]
    br = rows // LANDING_BLOCKS
    me = _my_id().astype(jnp.int32).reshape(1)

    def body(me_ref, src_ref, *rest):
        rest[-1][0] = src_ref[...]

    extra = [] if after is None else [after]
    grid_spec = pltpu.PrefetchScalarGridSpec(
        num_scalar_prefetch=1, grid=(LANDING_BLOCKS,),
        in_specs=[pl.BlockSpec((br, cols), lambda i, me_ref: ((me_ref[0] * LANDING_BLOCKS if by_rows else 0) + i, 0))]
        + [ANY] * len(extra),
        out_specs=pl.BlockSpec((1, br, cols), lambda i, me_ref: (me_ref[0], i, 0)))
    return pl.pallas_call(
        body, name=name, grid_spec=grid_spec, out_shape=jax.ShapeDtypeStruct(slots_shape, src.dtype),
    )(me, src, *extra)


def _pack(arrs):
    flat = []
    for a in arrs:
        f = a.reshape(-1).astype(F32)
        flat.append(jnp.pad(f, (0, (-f.size) % 128)))
    f = jnp.concatenate(flat)
    return jnp.pad(f, (0, (-f.size) % 1024)).reshape(-1, 128)


def _unpack(p, shapes):
    f = p.reshape(-1)
    out, off = [], 0
    for s in shapes:
        n = math.prod(s)
        out.append(f[off:off + n].reshape(s))
        off += n + (-n) % 128
    return out


def _to_padded_cols(w):
    z = lambda n: jnp.zeros(w.shape[:-1] + (n,), w.dtype)
    return jnp.concatenate([w[..., 0:4608], w[..., 4620:7692], w[..., 4608:4614], z(122), w[..., 4614:4620], z(378)], axis=-1)


def _from_padded_cols(w):
    return jnp.concatenate([w[..., 0:4608], w[..., C_BETA:C_BETA + 6], w[..., C_ALPHA:C_ALPHA + 6], w[..., 4608:7680]], axis=-1)


def _lane_pad(v):
    return jnp.pad(v, (0, 128 - v.shape[0]))[None, :]


TM_MM, TN_MM, TK_MM = 1024, 1024, 2048
TM_ROW = 512


def layer_fwd(x, p, tabs, l):
    h = rms_fwd(x, p["norm_w"], tm=TM_ROW, name=f"rms_fwd_{l}")
    u = matmul(h, p["w_in"], mode="nn", tm=TM_MM, tn=TN_MM, tk=TK_MM, name=f"in_proj_{l}")
    y_conv, c = conf_fwd(u, p["dw_w"], p["dw_b"], p["ln_w"], p["ln_b"], p["pw"], tm=TM_ROW, name=f"conf_fwd_{l}")
    q, k, v, beta, gc = gdn_prep_fwd(u, p["conv_w"], p["a_log"], p["dt_bias"], tm=TM_ROW, name=f"gdn_prep_fwd_{l}")
    y_gdn, states, tinv = gdn_core_fwd(q, k, v, u, beta, gc, p["gdn_nw"], name=f"gdn_core_fwd_{l}")
    qr, kr, vb = att_prep_fwd(u, tabs, tm=TM_ROW, name=f"att_prep_fwd_{l}")
    os_, ls = [], []
    for _, dil in DIL_PATTERNS:
        o_p, l_p = att_pattern_fwd(qr, kr, vb, dil, name=f"att_fwd_d{dil}_{l}")
        os_.append(o_p)
        ls.append(l_p)
    y_att, o, lse = att_combine_fwd(os_, ls, u, tm=TM_ROW, name=f"att_combine_fwd_{l}")
    y = jnp.concatenate([y_conv, y_gdn, y_att], axis=1)
    if callable(p["w_out"]):
        p["w_out"] = p["w_out"](y)
    x_new = matmul(y, p["w_out"], mode="nn", tm=TM_MM // 2, tn=D_MODEL, tk=TK_MM, residual=x, name=f"out_proj_{l}")
    saved = dict(x=x, h=h, u=u, y=y, c=c, q=q, k=k, v=v, beta=beta, gc=gc, states=states, tinv=tinv, qr=qr, kr=kr, vb=vb, o=o,
                 lse=lse)
    return x_new, saved


def layer_bwd(dx_out, s, p, tabs, l, send_w_out=None, send_w_in=None):
    S = dx_out.shape[0]
    u = s["u"]
    dy = matmul(dx_out, p["w_out"], mode="nt", tm=TM_MM // 2, tn=D_MODEL, tk=TK_MM, name=f"out_proj_dy_{l}")
    g_w_out = matmul(s["y"], dx_out, mode="tn", tm=TM_MM, tn=TN_MM, tk=TK_MM, out_dtype=BF16, name=f"out_proj_dw_{l}")
    ln_b = p["ln_b"] if send_w_out is None else p["ln_b"] + send_w_out(g_w_out)
    du = lax.empty((S, IN_PAD), BF16)
    dc, du, g_ln_w, g_ln_b, g_pw, g_dw_b = conf_bwd_post(u, s["c"], dy, p["ln_w"], ln_b, p["pw"], du,
                                                       tm=TM_ROW, name=f"conf_bwd_post_{l}")
    du, g_dw_w = conv_bwd(dc, [(u, C_CA), (u, C_CB)], p["dw_w"], du, C_CA, K=CONV_WIDTH, H=CONF_HALO, tm=TM_ROW, cw=CONV_CH,
                          glu=True, name=f"conf_bwd_conv_{l}")
    dq, dk, dv, du, dbeta, dgc, g_gdn_nw = gdn_core_bwd(s["q"], s["k"], s["v"], u, s["beta"], s["gc"], p["gdn_nw"],
                                                        s["states"], s["tinv"], dy, du, name=f"gdn_core_bwd_{l}")
    dpre, du, g_a_log, g_dt_bias = gdn_prep_bwd(u, p["conv_w"], p["a_log"], p["dt_bias"], dq, dk, dv, dbeta, dgc, du,
                                                tm=TM_ROW, name=f"gdn_prep_bwd_{l}")
    du, g_conv_w = conv_bwd(dpre, [(u, C_GQ)], p["conv_w"], du, C_GQ, K=SHORT_CONV, H=GDN_HALO, tm=TM_ROW, cw=GDN_W,
                            glu=False, name=f"gdn_bwd_conv_{l}")
    do, du, delta = att_combine_bwd(dy, s["o"], u, du, tm=TM_ROW, name=f"att_combine_bwd_{l}")
    dqs, dks, dvs = [], [], []
    for _, dil in DIL_PATTERNS:
        a, b, c = att_pattern_bwd(s["qr"], s["kr"], s["vb"], do, delta, s["lse"], dil, name=f"att_bwd_d{dil}_{l}")
        dqs.append(a)
        dks.append(b)
        dvs.append(c)
    du = att_prep_bwd(dqs, dks, dvs, tabs, du, tm=TM_ROW, name=f"att_prep_bwd_{l}")
    g_w_in = matmul(s["h"], du, mode="tn", tm=TM_MM, tn=TN_MM, tk=min(2 * TK_MM, S), out_dtype=BF16, name=f"in_proj_dw_{l}")
    sent = None if send_w_in is None else send_w_in(g_w_in)
    dh = matmul(du, p["w_in"], mode="nt", tm=TM_MM, tn=TN_MM, tk=2 * TK_MM, after=sent, name=f"in_proj_dh_{l}")
    dx, g_norm_w = rms_bwd(s["x"], p["norm_w"], dh, dx_out, tm=TM_ROW // 2, name=f"rms_bwd_{l}")
    grads = dict(norm_w=g_norm_w[0], w_in=g_w_in, conv_qkv_w=g_conv_w, a_log=g_a_log[0, :GDN_HEADS], dt_bias=g_dt_bias[0, :GDN_HEADS],
                 gdn_norm_w=g_gdn_nw[0], conf_dw_w=g_dw_w, conf_dw_b=g_dw_b[0], conf_ln_w=g_ln_w[0], conf_ln_b=g_ln_b[0],
                 conf_pw_w=g_pw, w_out=g_w_out)
    return dx, grads


WEIGHTS = ("norm_w", "w_in", "conv_qkv_w", "a_log", "dt_bias", "gdn_norm_w", "conf_dw_w", "conf_dw_b", "conf_ln_w",
           "conf_ln_b", "conf_pw_w", "w_out", "final_norm_w")
SMALL_REPLICATED = ("norm_w", "a_log", "dt_bias", "gdn_norm_w", "conf_dw_b", "conf_ln_w", "conf_ln_b")


def kernel(x, norm_w, w_in, conv_qkv_w, a_log, dt_bias, gdn_norm_w, conf_dw_w, conf_dw_b, conf_ln_w, conf_ln_b, conf_pw_w, w_out, final_norm_w, loss_target, m_norm_w, m_w_in, m_conv_qkv_w, m_a_log, m_dt_bias, m_gdn_norm_w, m_conf_dw_w, m_conf_dw_b, m_conf_ln_w, m_conf_ln_b, m_conf_pw_w, m_w_out, m_final_norm_w, v_norm_w, v_w_in, v_conv_qkv_w, v_a_log, v_dt_bias, v_gdn_norm_w, v_conf_dw_w, v_conf_dw_b, v_conf_ln_w, v_conf_ln_b, v_conf_pw_w, v_w_out, v_final_norm_w):
    w = dict(norm_w=norm_w, w_in=w_in, conv_qkv_w=conv_qkv_w, a_log=a_log, dt_bias=dt_bias, gdn_norm_w=gdn_norm_w,
             conf_dw_w=conf_dw_w, conf_dw_b=conf_dw_b, conf_ln_w=conf_ln_w, conf_ln_b=conf_ln_b, conf_pw_w=conf_pw_w,
             w_out=w_out, final_norm_w=final_norm_w)
    m = dict(zip(WEIGHTS, (m_norm_w, m_w_in, m_conv_qkv_w, m_a_log, m_dt_bias, m_gdn_norm_w, m_conf_dw_w, m_conf_dw_b,
                           m_conf_ln_w, m_conf_ln_b, m_conf_pw_w, m_w_out, m_final_norm_w)))
    v = dict(zip(WEIGHTS, (v_norm_w, v_w_in, v_conv_qkv_w, v_a_log, v_dt_bias, v_gdn_norm_w, v_conf_dw_w, v_conf_dw_b,
                           v_conf_ln_w, v_conf_ln_b, v_conf_pw_w, v_w_out, v_final_norm_w)))
    S = x.shape[1]
    L = norm_w.shape[0]
    me = _my_id()

    small_shapes = [conv_qkv_w.shape, conf_dw_w.shape, conf_pw_w.shape]
    w_in_b, w_out_b = _to_padded_cols(w_in).astype(BF16), w_out.astype(BF16)
    in_slots, out_slots = (N_DEV,) + w_in_b.shape[1:], (N_DEV,) + w_out_b.shape[1:]
    g_in0, g_small = all_gather_two_level([w_in_b[0], _pack([conv_qkv_w, conf_dw_w, conf_pw_w])], name="gather_first")
    gathers, tie = {}, jnp.zeros((1, 1), F32)
    for l in range(L):
        srcs = [w_out_b[0]] if l == 0 else [w_in_b[l], w_out_b[l]]
        slots = [out_slots] if l == 0 else [in_slots, out_slots]
        lands = [_landing(a, sl, g_in0, by_rows=False, name=f"gather_own_{l}_{j}") for j, (a, sl) in enumerate(zip(srcs, slots))]
        *flight, token = push_start(srcs, lands, by_rows=False, name=f"gather_start_{l}")
        gathers[l] = flight
        tie = tie + token[0:1, 0:1]
    parts = [_unpack(g_small[s], small_shapes) for s in range(N_DEV)]
    conv_full = jnp.concatenate([pt[0] for pt in parts], axis=2)
    dw_full = jnp.concatenate([pt[1] for pt in parts], axis=2)
    pw_full = jnp.concatenate([pt[2] for pt in parts], axis=1)
    tabs = rope_tables(S)

    def layer_params(l, full_in, full_out):
        return dict(
            norm_w=norm_w[l][None], w_in=full_in.reshape(D_MODEL, IN_PAD), w_out=full_out,
            conv_w=conv_full[l], a_log=_lane_pad(a_log[l]), dt_bias=_lane_pad(dt_bias[l]), gdn_nw=gdn_norm_w[l][None],
            dw_w=dw_full[l], dw_b=conf_dw_b[l][None], ln_w=conf_ln_w[l][None], ln_b=conf_ln_b[l][None], pw=pw_full[l])

    xs = x[0]
    params, saved = [], []
    for l in range(L):
        if l == 0:
            late_out = lambda after: push_wait(*gathers[0], after, by_rows=False, name="gather_wait_0")[0].reshape(D_MODEL, D_MODEL)
            p = layer_params(0, g_in0, late_out)
            p["norm_w"] = p["norm_w"] + tie
        else:
            full_in, full_out = push_wait(*gathers[l], xs, by_rows=False, name=f"gather_wait_{l}")
            p = layer_params(l, full_in, full_out.reshape(D_MODEL, D_MODEL))
        params.append(p)
        xs, sv = layer_fwd(xs, p, tabs, l)
        saved.append(sv)
    loss_part, dx, g_final = loss_head(xs, final_norm_w[None], loss_target[0], tm=TM_ROW // 2, name="loss_head")

    layer_grads, scatters = [None] * L, {}

    def send(kind, l, grad, slots):
        land = _landing(grad, slots, by_rows=True, name=f"scatter_own_{kind}_{l}")
        *flight, token = push_start([grad], [land], by_rows=True, name=f"scatter_start_{kind}_{l}")
        scatters[kind, l] = flight
        return token[0:1, 0:1]

    for l in reversed(range(L)):
        dx, layer_grads[l] = layer_bwd(dx, saved[l], params[l], tabs, l, functools.partial(send, "out", l, slots=out_slots),
                                       functools.partial(send, "in", l, slots=in_slots))

    stack = lambda name: jnp.stack([layer_grads[l][name] for l in range(L)])
    small = [loss_part] + [stack(n) for n in SMALL_REPLICATED] + [g_final[0], stack("conv_qkv_w"), stack("conf_dw_w")]
    small_shapes = [a.shape for a in small]
    r_pw, r_small = scatter_exchange([[layer_grads[l]["conf_pw_w"] for l in range(L)]], _pack(small), name="scatter_small")
    g = {}
    g["conf_pw_w"] = slot_sum(r_pw, name="sum_pw")
    summed = _unpack(slot_sum(r_small[:, None], name="sum_small")[0], small_shapes)
    loss = summed[0].reshape(())
    for n, a in zip(SMALL_REPLICATED, summed[1:1 + len(SMALL_REPLICATED)]):
        g[n] = a
    g["final_norm_w"] = summed[-3]
    g["conv_qkv_w"] = lax.dynamic_slice_in_dim(summed[-2], me * conv_qkv_w.shape[2], conv_qkv_w.shape[2], axis=2)
    g["conf_dw_w"] = lax.dynamic_slice_in_dim(summed[-1], me * conf_dw_w.shape[2], conf_dw_w.shape[2], axis=2)
    deltas, new_m, new_v = {}, {}, {}
    for n in WEIGHTS:
        if n not in ("w_in", "w_out"):
            deltas[n], new_m[n], new_v[n] = adam(w[n], g[n], m[n], v[n], name=f"adam_{n}")
    sums = {}
    order = [(kind, l) for l in reversed(range(L)) for kind in ("out", "in")]
    done_first = loss_part
    for kind, l in order:
        last = (kind, l) == order[-1]
        after = (done_first + deltas["a_log"][0:1, 0:1]) if last else dx
        land, = push_wait(*scatters[kind, l], after, by_rows=True, name=f"scatter_wait_{kind}_{l}")
        sums[kind, l] = slot_sum(land[:, None], name=f"sum_w_{kind}_{l}")
        if not last:
            done_first = done_first + sums[kind, l][0, 0:1, 0:1]
    g["w_in"] = _from_padded_cols(jnp.concatenate([sums["in", l] for l in range(L)], axis=0))
    g["w_out"] = jnp.concatenate([sums["out", l] for l in range(L)], axis=0)
    for n in ("w_in", "w_out"):
        deltas[n], new_m[n], new_v[n] = adam(w[n], g[n], m[n], v[n], name=f"adam_{n}")
    return (loss, dx[None], *[g[n] for n in WEIGHTS], *[deltas[n] for n in WEIGHTS],
            *[new_m[n] for n in WEIGHTS], *[new_v[n] for n in WEIGHTS])
```

```python
import functools
import math

import jax
import jax.numpy as jnp
from jax import lax
from jax.experimental import pallas as pl
from jax.experimental.pallas import tpu as pltpu

D_MODEL = 2048
DEPTH = 4
N_DEV = 8
GDN_DK = 128
GDN_HEADS = 6
GDN_W = 768
ATT_HD = 64
ATT_HEADS = 12
ATT_W = 768
CONV_CH = 512
CONV_WIDTH = 31
SHORT_CONV = 4
GDN_CHUNK = 64
ROPE_THETA = 500000.0
ROPE_DIM = 16
DIL_PATTERNS = ((128, 1), (512, 4), (2048, 16))
ATT_BLOCK = 128
NEG_INF = -1e30
IN_W = 7692

ADAM_LR = 0.001
ADAM_B1 = 0.9
ADAM_B2 = 0.999
ADAM_EPS = 1e-08
ADAM_WD = 0.01
ADAM_STEP = 10

C_CA, C_CB, C_CG = 0, 512, 1024
C_GQ, C_GK, C_GV, C_GZ = 1536, 2304, 3072, 3840
C_AQ, C_AK, C_AV, C_AG = 4608, 5376, 6144, 6912
C_BETA, C_ALPHA = 7680, 7808
IN_PAD = 8192

VMEM_LIMIT = 56 * 1024 * 1024
CONF_HALO = 32
GDN_HALO = 8
GDN_GROUP = 4
GDN_HEADS_PER_STEP = 6

F32 = jnp.float32
BF16 = jnp.bfloat16
HI = lax.Precision.HIGHEST


def _cparams(sem, vmem=VMEM_LIMIT):
    return pltpu.CompilerParams(dimension_semantics=sem, vmem_limit_bytes=vmem)


def _dg(a, b, ca, cb, prec):
    nb = a.ndim - 2
    batch = tuple(range(nb))
    dn = (((ca + nb,), (cb + nb,)), (batch, batch))
    if prec == "bf16":
        return lax.dot_general(a.astype(BF16), b.astype(BF16), dn, preferred_element_type=F32)
    if prec == "bf16x3":
        ah, bh = a.astype(BF16), b.astype(BF16)
        al, bl = (a - ah.astype(F32)).astype(BF16), (b - bh.astype(F32)).astype(BF16)
        dot = lambda x, y: lax.dot_general(x, y, dn, preferred_element_type=F32)
        return dot(ah, bh) + (dot(ah, bl) + dot(al, bh))
    return lax.dot_general(a.astype(F32), b.astype(F32), dn, precision=HI, preferred_element_type=F32)


def _nn_raw(a, b, prec):
    return _dg(a, b, 1, 0, prec)


def _nt_raw(a, b, prec):
    return _dg(a, b, 1, 1, prec)


def _tn_raw(a, b, prec):
    return _dg(a, b, 0, 0, prec)


@functools.partial(jax.custom_vjp, nondiff_argnums=(2,))
def mm_nn(a, b, prec="bf16"):
    return _nn_raw(a, b, prec)


def _mm_nn_f(a, b, prec):
    return _nn_raw(a, b, prec), (a, b)


def _mm_nn_b(prec, res, g):
    a, b = res
    return _nt_raw(g, b, prec).astype(a.dtype), _tn_raw(a, g, prec).astype(b.dtype)


mm_nn.defvjp(_mm_nn_f, _mm_nn_b)


@functools.partial(jax.custom_vjp, nondiff_argnums=(2,))
def mm_nt(a, b, prec="bf16"):
    return _nt_raw(a, b, prec)


def _mm_nt_f(a, b, prec):
    return _nt_raw(a, b, prec), (a, b)


def _mm_nt_b(prec, res, g):
    a, b = res
    return _nn_raw(g, b, prec).astype(a.dtype), _tn_raw(g, a, prec).astype(b.dtype)


mm_nt.defvjp(_mm_nt_f, _mm_nt_b)


@functools.partial(jax.custom_vjp, nondiff_argnums=(2,))
def mm_tn(a, b, prec="bf16"):
    return _tn_raw(a, b, prec)


def _mm_tn_f(a, b, prec):
    return _tn_raw(a, b, prec), (a, b)


def _mm_tn_b(prec, res, g):
    a, b = res
    return _nt_raw(b, g, prec).astype(a.dtype), _nn_raw(a, g, prec).astype(b.dtype)


mm_tn.defvjp(_mm_tn_f, _mm_tn_b)


def _sigmoid(x):
    return 1.0 / (1.0 + jnp.exp(-x))


def _silu(x):
    return x * _sigmoid(x)


def _softplus(x):
    return jnp.maximum(x, 0.0) + jnp.log(1.0 + jnp.exp(-jnp.abs(x)))


def matmul(a, b, *, mode, tm, tn, tk, out_dtype=F32, residual=None, after=None, name):
    if mode == "tn":
        K, M = a.shape
    else:
        M, K = a.shape
    N = b.shape[0] if mode == "nt" else b.shape[1]
    assert M % tm == 0 and N % tn == 0 and K % tk == 0, (a.shape, b.shape, tm, tn, tk)
    nk = K // tk
    a_spec = pl.BlockSpec((tk, tm), lambda i, j, k: (k, i)) if mode == "tn" else pl.BlockSpec((tm, tk), lambda i, j, k: (i, k))
    b_spec = pl.BlockSpec((tn, tk), lambda i, j, k: (j, k)) if mode == "nt" else pl.BlockSpec((tk, tn), lambda i, j, k: (k, j))
    o_spec = pl.BlockSpec((tm, tn), lambda i, j, k: (i, j))
    raw = {"nn": _nn_raw, "nt": _nt_raw, "tn": _tn_raw}[mode]
    has_res = residual is not None

    def body(*refs):
        a_ref, b_ref = refs[:2]
        r_ref = refs[2] if has_res else None
        o_ref, acc_ref = refs[-2:]
        k = pl.program_id(2)
        part = raw(a_ref[...], b_ref[...], "bf16")

        @pl.when(k == 0)
        def _():
            acc_ref[...] = part

        @pl.when(k > 0)
        def _():
            acc_ref[...] += part

        @pl.when(k == nk - 1)
        def _():
            r = acc_ref[...]
            if has_res:
                r = r + r_ref[...].astype(F32)
            o_ref[...] = r.astype(out_dtype)

    in_specs = [a_spec, b_spec] + ([o_spec] if has_res else []) + ([ANY] if after is not None else [])
    args = (a, b) + ((residual,) if has_res else ()) + ((after,) if after is not None else ())
    return pl.pallas_call(
        body, name=name, grid=(M // tm, N // tn, nk), in_specs=in_specs, out_specs=o_spec,
        out_shape=jax.ShapeDtypeStruct((M, N), out_dtype),
        scratch_shapes=[pltpu.VMEM((tm, tn), F32)],
        compiler_params=_cparams(("parallel", "parallel", "arbitrary")),
    )(*args)


def _rms_fn(x, w, eps=1e-6):
    return x * lax.rsqrt(jnp.mean(x * x, axis=-1, keepdims=True) + eps) * w


def rms_fwd(x, w, *, tm, name):
    S, D = x.shape

    def body(x_ref, w_ref, o_ref):
        o_ref[...] = _rms_fn(x_ref[...], w_ref[...]).astype(BF16)

    return pl.pallas_call(
        body, name=name, grid=(S // tm,),
        in_specs=[pl.BlockSpec((tm, D), lambda i: (i, 0)), pl.BlockSpec((1, D), lambda i: (0, 0))],
        out_specs=pl.BlockSpec((tm, D), lambda i: (i, 0)),
        out_shape=jax.ShapeDtypeStruct((S, D), BF16),
        compiler_params=_cparams(("parallel",)),
    )(x, w)


def rms_bwd(x, w, dh, dres, *, tm, name):
    S, D = x.shape

    def body(x_ref, w_ref, dh_ref, dr_ref, dx_ref, dw_ref):
        _, vjp = jax.vjp(_rms_fn, x_ref[...], w_ref[...])
        dx, dw = vjp(dh_ref[...].astype(F32))
        dx_ref[...] = dx + dr_ref[...]

        @pl.when(pl.program_id(0) == 0)
        def _():
            dw_ref[...] = jnp.zeros_like(dw_ref)

        dw_ref[...] += dw

    row = pl.BlockSpec((tm, D), lambda i: (i, 0))
    vec = pl.BlockSpec((1, D), lambda i: (0, 0))
    return pl.pallas_call(
        body, name=name, grid=(S // tm,), in_specs=[row, vec, row, row], out_specs=[row, vec],
        out_shape=[jax.ShapeDtypeStruct((S, D), F32), jax.ShapeDtypeStruct((1, D), F32)],
        compiler_params=_cparams(("arbitrary",)),
    )(x, w, dh, dres)


def _fill_ext(ext_ref, halo, tile, first, H):
    ext_ref[pl.ds(0, H), :] = jnp.where(first, 0.0, halo)
    ext_ref[pl.ds(H, tile.shape[0]), :] = tile


def _conv_taps(ext_ref, w_ref, K, H, tm):
    assert H >= 8 * ((K - 1) // 8 + 1)
    total = None
    for b in range(min(8, K)):
        y = None
        for a in range((K - 1 - b) // 8 + 1):
            term = ext_ref[pl.ds(H - 8 - 8 * a, tm + 8), :] * w_ref[pl.ds(K - 1 - 8 * a - b, 1), :]
            y = term if y is None else y + term
        y = y if b == 0 else pltpu.roll(y, b, 0)
        total = y if total is None else total + y
    return total[8:, :]


def _halo_spec(H, tm, cw, col):
    return pl.BlockSpec((H, cw), lambda *g, _c=col: (jnp.maximum(g[-1] * (tm // H) - 1, 0), _c))


def conv_bwd(dc, srcs, w, du, du_col, *, K, H, tm, cw, glu, name):
    S, C = dc.shape
    nc, nt = C // cw, S // tm
    last_halo = S // H - 1
    n_src = 2 if glu else 1
    bases = [c0 // cw for _, c0 in srcs]

    def body(*refs):
        dc_ref, dcn_ref = refs[0], refs[1]
        src_refs = refs[2:2 + 2 * n_src]
        w_ref = refs[2 + 2 * n_src]
        out_ref, dw_ref, ext_ref, dext_ref = refs[4 + 2 * n_src:]
        i = pl.program_id(1)
        first, last = i == 0, i == nt - 1
        if glu:
            a_ref, ah_ref, b_ref, bh_ref = src_refs
            sg = _sigmoid(b_ref[...])
            _fill_ext(ext_ref, ah_ref[...] * _sigmoid(bh_ref[...]), a_ref[...] * sg, first, H)
        else:
            x_ref, xh_ref = src_refs
            _fill_ext(ext_ref, xh_ref[...], x_ref[...], first, H)
        dc_t = dc_ref[...]
        dext_ref[pl.ds(0, tm), :] = dc_t
        dext_ref[pl.ds(tm, H), :] = jnp.where(last, 0.0, dcn_ref[...])
        N = tm + 8
        dx = None
        for b in range(min(8, K)):
            z = None
            for a_ in range((K - 1 - b) // 8 + 1):
                term = dext_ref[pl.ds(8 * a_, N), :] * w_ref[pl.ds(K - 1 - 8 * a_ - b, 1), :]
                z = term if z is None else z + term
            z = z if b == 0 else pltpu.roll(z, N - b, 0)
            dx = z if dx is None else dx + z
        dx = dx[:tm, :]
        if glu:
            a = a_ref[...]
            out_ref[:, :cw] = (dx * sg).astype(BF16)
            out_ref[:, cw:] = (dx * a * sg * (1.0 - sg)).astype(BF16)
        else:
            out_ref[...] = dx.astype(BF16)

        @pl.when(first)
        def _():
            dw_ref[...] = jnp.zeros_like(dw_ref)

        dpad = jnp.concatenate([jnp.zeros((8, cw), F32), dc_t], axis=0)
        for b in range(min(8, K)):
            shifted = dpad if b == 0 else pltpu.roll(dpad, N - b, 0)
            for a_ in range((K - 1 - b) // 8 + 1):
                k = K - 1 - 8 * a_ - b
                dw_ref[pl.ds(k, 1), :] += jnp.sum(shifted * ext_ref[pl.ds(H - 8 - 8 * a_, N), :], axis=0, keepdims=True)

    tile = lambda base: pl.BlockSpec((tm, cw), lambda j, i, _b=base: (i, _b + j))
    halo = lambda base: pl.BlockSpec((H, cw), lambda j, i, _b=base: (jnp.maximum(i * (tm // H) - 1, 0), _b + j))
    in_specs = [tile(0), pl.BlockSpec((H, cw), lambda j, i: (jnp.minimum((i + 1) * (tm // H), last_halo), j))]
    args = [dc, dc]
    for (arr, _), base in zip(srcs, bases):
        in_specs += [tile(base), halo(base)]
        args += [arr, arr]
    in_specs += [pl.BlockSpec((K, cw), lambda j, i: (0, j)), ANY]
    args += [w, du]
    ow = n_src * cw
    out_specs = [pl.BlockSpec((tm, ow), lambda j, i: (i, du_col // ow + j)), pl.BlockSpec((K, cw), lambda j, i: (0, j))]
    out_shape = [jax.ShapeDtypeStruct(du.shape, du.dtype), jax.ShapeDtypeStruct((K, C), F32)]
    return pl.pallas_call(
        body, name=name, grid=(nc, nt), in_specs=in_specs, out_specs=out_specs, out_shape=out_shape,
        input_output_aliases={len(args) - 1: 0},
        scratch_shapes=[pltpu.VMEM((tm + H, cw), F32), pltpu.VMEM((tm + H, cw), F32)],
        compiler_params=_cparams(("parallel", "arbitrary")),
    )(*args)


def _conf_post(c, gate, ln_w, ln_b, pw):
    mu = jnp.mean(c, axis=-1, keepdims=True)
    cc = c - mu
    var = jnp.mean(cc * cc, axis=-1, keepdims=True)
    hn = cc * lax.rsqrt(var + 1e-5) * ln_w + ln_b
    return mm_nn(_silu(hn), pw) * _silu(gate)


def _conf_specs(tm):
    H = CONF_HALO
    blk = lambda col: pl.BlockSpec((tm, CONV_CH), lambda i, _c=col: (i, _c))
    vec = pl.BlockSpec((1, CONV_CH), lambda i: (0, 0))
    specs = [blk(0), blk(1), blk(2), _halo_spec(H, tm, CONV_CH, 0), _halo_spec(H, tm, CONV_CH, 1),
             pl.BlockSpec((CONV_WIDTH, CONV_CH), lambda i: (0, 0)), vec, vec, vec,
             pl.BlockSpec((CONV_CH, CONV_CH), lambda i: (0, 0))]
    return specs, blk, vec


def _conf_conv(a_ref, b_ref, ah_ref, bh_ref, dww_ref, dwb_ref, ext_ref, tm):
    first = pl.program_id(0) == 0
    _fill_ext(ext_ref, ah_ref[...] * _sigmoid(bh_ref[...]), a_ref[...] * _sigmoid(b_ref[...]), first, CONF_HALO)
    return _conv_taps(ext_ref, dww_ref, CONV_WIDTH, CONF_HALO, tm) + dwb_ref[...]


def conf_fwd(u, dw_w, dw_b, ln_w, ln_b, pw, *, tm, name):
    S = u.shape[0]
    specs, blk, vec = _conf_specs(tm)

    def body(a_ref, b_ref, g_ref, ah_ref, bh_ref, dww_ref, dwb_ref, lnw_ref, lnb_ref, pw_ref, y_ref, c_ref, ext_ref):
        c = _conf_conv(a_ref, b_ref, ah_ref, bh_ref, dww_ref, dwb_ref, ext_ref, tm)
        c_ref[...] = c
        y_ref[...] = _conf_post(c, g_ref[...], lnw_ref[...], lnb_ref[...], pw_ref[...]).astype(BF16)

    return pl.pallas_call(
        body, name=name, grid=(S // tm,), in_specs=specs, out_specs=[blk(0), blk(0)],
        out_shape=[jax.ShapeDtypeStruct((S, CONV_CH), BF16), jax.ShapeDtypeStruct((S, CONV_CH), F32)],
        scratch_shapes=[pltpu.VMEM((tm + CONF_HALO, CONV_CH), F32)],
        compiler_params=_cparams(("parallel",)),
    )(u, u, u, u, u, dw_w, dw_b, ln_w, ln_b, pw)


def conf_bwd_post(u, c, dy, ln_w, ln_b, pw, du, *, tm, name):
    S = u.shape[0]
    blk = lambda col: pl.BlockSpec((tm, CONV_CH), lambda i, _c=col: (i, _c))
    vec = pl.BlockSpec((1, CONV_CH), lambda i: (0, 0))
    mat = pl.BlockSpec((CONV_CH, CONV_CH), lambda i: (0, 0))

    def body(c_ref, g_ref, lnw_ref, lnb_ref, pw_ref, dy_ref, du_in, dc_ref, dg_ref, dlnw_ref, dlnb_ref, dpw_ref, ddwb_ref):
        _, vjp = jax.vjp(_conf_post, c_ref[...], g_ref[...], lnw_ref[...], lnb_ref[...], pw_ref[...])
        dc, dg, dlnw, dlnb, dpw = vjp(dy_ref[...])
        dc_ref[...] = dc
        dg_ref[...] = dg.astype(BF16)

        @pl.when(pl.program_id(0) == 0)
        def _():
            dlnw_ref[...] = jnp.zeros_like(dlnw_ref)
            dlnb_ref[...] = jnp.zeros_like(dlnb_ref)
            dpw_ref[...] = jnp.zeros_like(dpw_ref)

            ddwb_ref[...] = jnp.zeros_like(ddwb_ref)

        dlnw_ref[...] += dlnw
        dlnb_ref[...] += dlnb
        dpw_ref[...] += dpw
        ddwb_ref[...] += jnp.sum(dc, axis=0, keepdims=True)

    return pl.pallas_call(
        body, name=name, grid=(S // tm,), in_specs=[blk(0), blk(C_CG // CONV_CH), vec, vec, mat, blk(0), ANY],
        out_specs=[blk(0), blk(C_CG // CONV_CH), vec, vec, mat, vec],
        out_shape=[jax.ShapeDtypeStruct((S, CONV_CH), F32), jax.ShapeDtypeStruct(du.shape, du.dtype),
                   jax.ShapeDtypeStruct((1, CONV_CH), F32), jax.ShapeDtypeStruct((1, CONV_CH), F32),
                   jax.ShapeDtypeStruct((CONV_CH, CONV_CH), F32), jax.ShapeDtypeStruct((1, CONV_CH), F32)],
        input_output_aliases={6: 1},
        compiler_params=_cparams(("arbitrary",)),
    )(c, u, ln_w, ln_b, pw, dy, du)


def _iota2(shape, dim):
    return lax.broadcasted_iota(jnp.int32, shape, dim)


def _gdn_post(pre_q, pre_k, pre_v, b_in, a_in, a_log, dt_bias):
    tm = pre_q.shape[0]
    q, k, v = _silu(pre_q), _silu(pre_k), _silu(pre_v)
    qs, ks = [], []
    for h in range(GDN_HEADS):
        sl = slice(h * GDN_DK, (h + 1) * GDN_DK)
        qh, kh = q[:, sl], k[:, sl]
        qs.append(qh * lax.rsqrt(jnp.sum(qh * qh, axis=-1, keepdims=True) + 1e-6) * (GDN_DK ** -0.5))
        ks.append(kh * lax.rsqrt(jnp.sum(kh * kh, axis=-1, keepdims=True) + 1e-6))
    beta = _sigmoid(b_in)
    g = -jnp.exp(a_log) * _softplus(a_in + dt_bias)
    nb = tm // GDN_CHUNK
    tril = (_iota2((nb, GDN_CHUNK, GDN_CHUNK), 1) >= _iota2((nb, GDN_CHUNK, GDN_CHUNK), 2)).astype(F32)
    gc = mm_nn(tril, g.reshape(nb, GDN_CHUNK, 128), "f32").reshape(tm, 128)
    return jnp.concatenate(qs, axis=1), jnp.concatenate(ks, axis=1), v, beta, gc


def _gdn_prep_specs(tm):
    H = GDN_HALO
    blk = lambda col: pl.BlockSpec((tm, GDN_W), lambda i, _c=col: (i, _c))
    lane = lambda col: pl.BlockSpec((tm, 128), lambda i, _c=col: (i, _c))
    vec = pl.BlockSpec((1, 128), lambda i: (0, 0))
    q0 = C_GQ // GDN_W
    specs = [blk(q0), blk(q0 + 1), blk(q0 + 2),
             _halo_spec(H, tm, GDN_W, q0), _halo_spec(H, tm, GDN_W, q0 + 1), _halo_spec(H, tm, GDN_W, q0 + 2),
             lane(C_BETA // 128), lane(C_ALPHA // 128),
             pl.BlockSpec((SHORT_CONV, GDN_W), lambda i: (0, 0)), pl.BlockSpec((SHORT_CONV, GDN_W), lambda i: (0, 1)),
             pl.BlockSpec((SHORT_CONV, GDN_W), lambda i: (0, 2)), vec, vec]
    return specs, blk, lane, vec


def _gdn_pre(x_refs, h_refs, w_refs, ext_ref, tm):
    first = pl.program_id(0) == 0
    pres = []
    for x_ref, h_ref, w_ref in zip(x_refs, h_refs, w_refs):
        _fill_ext(ext_ref, h_ref[...], x_ref[...], first, GDN_HALO)
        pres.append(_conv_taps(ext_ref, w_ref, SHORT_CONV, GDN_HALO, tm))
    return pres


def gdn_prep_fwd(u, conv_w, a_log, dt_bias, *, tm, name):
    S = u.shape[0]
    specs, blk, lane, vec = _gdn_prep_specs(tm)

    def body(xq, xk, xv, hq, hk, hv, bi, ai, wq, wk, wv, al, db, q_ref, k_ref, v_ref, beta_ref, gc_ref, ext_ref):
        pres = _gdn_pre((xq, xk, xv), (hq, hk, hv), (wq, wk, wv), ext_ref, tm)
        q, k, v, beta, gc = _gdn_post(*pres, bi[...], ai[...], al[...], db[...])
        q_ref[...] = q
        k_ref[...] = k
        v_ref[...] = v
        beta_ref[...] = beta
        gc_ref[...] = gc

    wide = jax.ShapeDtypeStruct((S, GDN_W), F32)
    narrow = jax.ShapeDtypeStruct((S, 128), F32)
    return pl.pallas_call(
        body, name=name, grid=(S // tm,), in_specs=specs,
        out_specs=[blk(0), blk(0), blk(0), lane(0), lane(0)], out_shape=[wide, wide, wide, narrow, narrow],
        scratch_shapes=[pltpu.VMEM((tm + GDN_HALO, GDN_W), F32)],
        compiler_params=_cparams(("parallel",)),
    )(u, u, u, u, u, u, u, u, conv_w, conv_w, conv_w, a_log, dt_bias)


def gdn_prep_bwd(u, conv_w, a_log, dt_bias, dq, dk, dv, dbeta, dgc, du, *, tm, name):
    S = u.shape[0]
    specs, blk, lane, vec = _gdn_prep_specs(tm)
    tail = IN_PAD - C_BETA

    def body(xq, xk, xv, hq, hk, hv, bi, ai, wq, wk, wv, al, db, dq_ref, dk_ref, dv_ref, dbe_ref, dgc_ref, du_in,
             dpre_ref, du_ref, dal_ref, ddb_ref, ext_ref):
        pres = _gdn_pre((xq, xk, xv), (hq, hk, hv), (wq, wk, wv), ext_ref, tm)
        _, vjp = jax.vjp(_gdn_post, *pres, bi[...], ai[...], al[...], db[...])
        dpq, dpk, dpv, dbi, dai, dal, ddb = vjp((dq_ref[...], dk_ref[...], dv_ref[...], dbe_ref[...], dgc_ref[...]))
        dpre_ref[:, 0:GDN_W] = dpq
        dpre_ref[:, GDN_W:2 * GDN_W] = dpk
        dpre_ref[:, 2 * GDN_W:] = dpv
        du_ref[:, 0:128] = dbi.astype(BF16)
        du_ref[:, 128:256] = dai.astype(BF16)
        du_ref[:, 256:] = jnp.zeros((tm, tail - 256), BF16)

        @pl.when(pl.program_id(0) == 0)
        def _():
            dal_ref[...] = jnp.zeros_like(dal_ref)
            ddb_ref[...] = jnp.zeros_like(ddb_ref)

        dal_ref[...] += dal
        ddb_ref[...] += ddb

    n_in = len(specs) + 6
    return pl.pallas_call(
        body, name=name, grid=(S // tm,), in_specs=specs + [blk(0), blk(0), blk(0), lane(0), lane(0), ANY],
        out_specs=[pl.BlockSpec((tm, 3 * GDN_W), lambda i: (i, 0)), pl.BlockSpec((tm, tail), lambda i: (i, C_BETA // tail)),
                   vec, vec],
        out_shape=[jax.ShapeDtypeStruct((S, 3 * GDN_W), F32), jax.ShapeDtypeStruct(du.shape, du.dtype),
                   jax.ShapeDtypeStruct((1, 128), F32), jax.ShapeDtypeStruct((1, 128), F32)],
        input_output_aliases={n_in - 1: 1},
        scratch_shapes=[pltpu.VMEM((tm + GDN_HALO, GDN_W), F32)],
        compiler_params=_cparams(("arbitrary",)),
    )(u, u, u, u, u, u, u, u, conv_w, conv_w, conv_w, a_log, dt_bias, dq, dk, dv, dbeta, dgc, du)


def _lane_col(blk, h):
    return jnp.sum(jnp.where(_iota2(blk.shape, 1) == h, blk, 0.0), axis=1, keepdims=True)


@jax.custom_vjp
def _tri_inv(low):
    n = low.shape[-1]
    r, c = _iota2(low.shape, low.ndim - 2), _iota2(low.shape, low.ndim - 1)
    eye = (r == c).astype(F32)
    t = eye - jnp.where((r // 2 == c // 2) & (r > c), low, 0.0)
    s = 2
    while s < n:
        off = jnp.where((r // (2 * s) == c // (2 * s)) & (r // s > c // s), low, 0.0)
        prec = "bf16" if s <= 8 else "bf16x3"
        t = t - _nn_raw(t, _nn_raw(off, t, prec), prec)
        s *= 2
    return t


def _tri_inv_f(low):
    t = _tri_inv(low)
    return t, t


def _tri_inv_b(t, dt):
    d = -_nt_raw(_tn_raw(t, dt, "bf16x3"), t, "bf16x3")
    r, c = _iota2(d.shape, d.ndim - 2), _iota2(d.shape, d.ndim - 1)
    return (jnp.where(r > c, d, 0.0),)


_tri_inv.defvjp(_tri_inv_f, _tri_inv_b)


@jax.custom_vjp
def _tri_inv_saved(low, t):
    return t


_tri_inv_saved.defvjp(lambda low, t: (t, t), lambda t, dt: (_tri_inv_b(t, dt)[0], jnp.zeros_like(t)))


def _gdn_group(s0, q, k, v, z, beta_blk, gc_blk, nw, h0, t_saved=None, with_t=False):
    C = GDN_CHUNK
    HP, R, _ = q.shape
    nb = R // C
    B = HP * nb
    q3, k3, v3 = (t.reshape(B, C, GDN_DK) for t in (q, k, v))
    b3 = jnp.stack([_lane_col(beta_blk, h0 + j) for j in range(HP)]).reshape(B, C, 1)
    g3 = jnp.stack([_lane_col(gc_blk, h0 + j) for j in range(HP)]).reshape(B, C, 1)
    r, c = _iota2((B, C, C), 1), _iota2((B, C, C), 2)
    causal, strict = r >= c, r > c
    g_t = gc_blk.T
    rows = [jnp.sum(jnp.where(_iota2((128, R), 0) == h0 + j, g_t, 0.0), axis=0, keepdims=True) for j in range(HP)]
    g_row = jnp.stack([rows[j][:, i * C:(i + 1) * C] for j in range(HP) for i in range(nb)])
    decay = jnp.where(causal, jnp.exp(jnp.where(causal, g3 - g_row, 0.0)), 0.0)
    low = jnp.where(strict, b3 * mm_nt(k3, k3) * decay, 0.0)
    t = _tri_inv(low) if t_saved is None else _tri_inv_saved(low, t_saved)
    eg = jnp.exp(g3)
    four = lambda x: x.reshape((HP, nb) + x.shape[1:])
    w_v = four(mm_nn(t, v3 * b3))
    w_k = four(mm_nn(t, k3 * (b3 * eg)))
    qk = four(jnp.where(causal, mm_nt(q3, k3) * decay, 0.0))
    q_dec = four(q3 * eg)
    g_last = jnp.sum(jnp.where(_iota2((B, C, 1), 1) == C - 1, g3, 0.0), axis=1, keepdims=True)
    k_dec = four(k3 * jnp.exp(g_last - g3))
    e_last = four(jnp.exp(g_last))
    s, outs = s0, []
    for i in range(nb):
        v_new = w_v[:, i] - mm_nn(w_k[:, i], s)
        outs.append(mm_nn(q_dec[:, i], s) + mm_nn(qk[:, i], v_new))
        s = s * e_last[:, i] + mm_tn(k_dec[:, i], v_new)
    o = jnp.concatenate(outs, axis=1)
    y = o * lax.rsqrt(jnp.mean(o * o, axis=-1, keepdims=True) + 1e-6) * nw * _silu(z)
    return (s, y, t) if with_t else (s, y)


def _heads(ref, HP):
    return jnp.stack([ref[:, j * GDN_DK:(j + 1) * GDN_DK] for j in range(HP)])


def gdn_core_fwd(q, k, v, u, beta, gc, nw, *, name):
    S = q.shape[0]
    R = GDN_CHUNK * GDN_GROUP
    G = S // R
    HP = GDN_HEADS_PER_STEP
    W = HP * GDN_DK
    NT = HP * GDN_GROUP
    blk = pl.BlockSpec((R, W), lambda g, h: (g, h))
    lane = pl.BlockSpec((R, 128), lambda g, h: (g, 0))
    st = pl.BlockSpec((1, HP, GDN_DK, GDN_DK), lambda g, h: (g, h, 0, 0))
    inv = pl.BlockSpec((1, NT, GDN_CHUNK, GDN_CHUNK), lambda g, h: (g, h, 0, 0))

    def body(q_ref, k_ref, v_ref, z_ref, be_ref, gc_ref, nw_ref, y_ref, st_ref, t_ref, s_ref):
        g, hs = pl.program_id(0), pl.program_id(1)
        s0 = jnp.where(g == 0, 0.0, s_ref[hs])
        st_ref[0] = s0
        s1, y, t = _gdn_group(s0, _heads(q_ref, HP), _heads(k_ref, HP), _heads(v_ref, HP), _heads(z_ref, HP), be_ref[...],
                              gc_ref[...], nw_ref[...], hs * HP, with_t=True)
        s_ref[hs] = s1
        t_ref[0] = t
        for j in range(HP):
            y_ref[:, j * GDN_DK:(j + 1) * GDN_DK] = y[j].astype(BF16)

    return pl.pallas_call(
        body, name=name, grid=(G, GDN_HEADS // HP),
        in_specs=[blk, blk, blk, pl.BlockSpec((R, W), lambda g, h: (g, C_GZ // W + h)), lane, lane,
                  pl.BlockSpec((1, 128), lambda g, h: (0, 0))],
        out_specs=[blk, st, inv],
        out_shape=[jax.ShapeDtypeStruct((S, GDN_W), BF16), jax.ShapeDtypeStruct((G, GDN_HEADS, GDN_DK, GDN_DK), F32),
                   jax.ShapeDtypeStruct((G, GDN_HEADS * GDN_GROUP, GDN_CHUNK, GDN_CHUNK), F32)],
        scratch_shapes=[pltpu.VMEM((GDN_HEADS // HP, HP, GDN_DK, GDN_DK), F32)],
        compiler_params=_cparams(("arbitrary", "arbitrary")),
    )(q, k, v, u, beta, gc, nw)


def gdn_core_bwd(q, k, v, u, beta, gc, nw, states, tinv, dy, du, *, name):
    S = q.shape[0]
    R = GDN_CHUNK * GDN_GROUP
    G = S // R
    HP = GDN_HEADS_PER_STEP
    W = HP * GDN_DK
    blk = pl.BlockSpec((R, W), lambda g, h: (G - 1 - g, h))
    lane = pl.BlockSpec((R, 128), lambda g, h: (G - 1 - g, 0))
    vec = pl.BlockSpec((1, 128), lambda g, h: (0, 0))

    def body(q_ref, k_ref, v_ref, z_ref, be_ref, gc_ref, nw_ref, st_ref, t_ref, *rest):
        dy_refs = rest[:HP]
        dq_ref, dk_ref, dv_ref, dz_ref, dbe_ref, dgc_ref, dnw_ref, ds_ref = rest[HP + 1:]
        g, hs = pl.program_id(0), pl.program_id(1)

        @pl.when(hs == 0)
        def _():
            dbe_ref[...] = jnp.zeros_like(dbe_ref)
            dgc_ref[...] = jnp.zeros_like(dgc_ref)

        @pl.when((hs == 0) & (g == 0))
        def _():
            dnw_ref[...] = jnp.zeros_like(dnw_ref)

        _, vjp = jax.vjp(functools.partial(_gdn_group, h0=hs * HP, t_saved=t_ref[0]), st_ref[0], _heads(q_ref, HP),
                         _heads(k_ref, HP), _heads(v_ref, HP), _heads(z_ref, HP), be_ref[...], gc_ref[...], nw_ref[...])
        ds_in = jnp.where(g == 0, 0.0, ds_ref[hs])
        dy = jnp.stack([r[...] for r in dy_refs])
        ds0, dq, dk, dv, dz, dbe, dgc, dnw = vjp((ds_in, dy))
        ds_ref[hs] = ds0
        for j in range(HP):
            sl = slice(j * GDN_DK, (j + 1) * GDN_DK)
            dq_ref[:, sl] = dq[j]
            dk_ref[:, sl] = dk[j]
            dv_ref[:, sl] = dv[j]
            dz_ref[:, sl] = dz[j].astype(BF16)
        dbe_ref[...] += dbe
        dgc_ref[...] += dgc
        dnw_ref[...] += dnw

    wide = jax.ShapeDtypeStruct((S, GDN_W), F32)
    narrow = jax.ShapeDtypeStruct((S, 128), F32)
    return pl.pallas_call(
        body, name=name, grid=(G, GDN_HEADS // HP),
        in_specs=[blk, blk, blk, pl.BlockSpec((R, W), lambda g, h: (G - 1 - g, C_GZ // W + h)), lane, lane, vec,
                  pl.BlockSpec((1, HP, GDN_DK, GDN_DK), lambda g, h: (G - 1 - g, h, 0, 0)),
                  pl.BlockSpec((1, HP * GDN_GROUP, GDN_CHUNK, GDN_CHUNK), lambda g, h: (G - 1 - g, h, 0, 0))]
        + [pl.BlockSpec((R, GDN_DK), lambda g, h, _j=j: (G - 1 - g, CONV_CH // GDN_DK + h * HP + _j)) for j in range(HP)]
        + [ANY],
        out_specs=[blk, blk, blk, pl.BlockSpec((R, W), lambda g, h: (G - 1 - g, C_GZ // W + h)), lane, lane, vec],
        out_shape=[wide, wide, wide, jax.ShapeDtypeStruct(du.shape, du.dtype), narrow, narrow,
                   jax.ShapeDtypeStruct((1, 128), F32)],
        input_output_aliases={9 + HP: 3},
        scratch_shapes=[pltpu.VMEM((GDN_HEADS // HP, HP, GDN_DK, GDN_DK), F32)],
        compiler_params=_cparams(("arbitrary", "arbitrary")),
    )(q, k, v, u, beta, gc, nw, states, tinv, *([dy] * HP), du)


def rope_tables(S):
    half = ROPE_DIM // 2
    inv = ROPE_THETA ** (-jnp.arange(half, dtype=F32) / half)
    ang = jnp.arange(S, dtype=F32)[:, None] * inv[None, :]
    cos, sin = jnp.cos(ang), jnp.sin(ang)
    rest = ATT_HD - ROPE_DIM
    c = jnp.concatenate([cos, cos, jnp.ones((S, rest), F32)], axis=1)
    s1 = jnp.concatenate([-sin, jnp.zeros((S, ATT_HD - half), F32)], axis=1)
    s2 = jnp.concatenate([jnp.zeros((S, half), F32), sin, jnp.zeros((S, rest), F32)], axis=1)
    return tuple(jnp.tile(t, (1, 2)) for t in (c, s1, s2))


def _rope(x, c, s1, s2):
    half = ROPE_DIM // 2
    return x * c + pltpu.roll(x, ATT_W - half, 1) * s1 + pltpu.roll(x, half, 1) * s2


def _unrope(dy, c, s1, s2):
    half = ROPE_DIM // 2
    return dy * c + pltpu.roll(dy * s1, half, 1) + pltpu.roll(dy * s2, ATT_W - half, 1)


def att_prep_fwd(u, tables, *, tm, name):
    S = u.shape[0]
    blk = lambda col: pl.BlockSpec((tm, ATT_W), lambda i, _c=col: (i, _c))
    tab = pl.BlockSpec((tm, 128), lambda i: (i, 0))

    def body(q_ref, k_ref, v_ref, c_ref, s1_ref, s2_ref, qo_ref, ko_ref, vo_ref):
        reps = ATT_W // 128
        c, s1, s2 = (jnp.tile(t[...], (1, reps)) for t in (c_ref, s1_ref, s2_ref))
        qo_ref[...] = (_rope(q_ref[...], c, s1, s2) * (ATT_HD ** -0.5)).astype(BF16)
        ko_ref[...] = _rope(k_ref[...], c, s1, s2).astype(BF16)
        vo_ref[...] = v_ref[...].astype(BF16)

    out = jax.ShapeDtypeStruct((S, ATT_W), BF16)
    return pl.pallas_call(
        body, name=name, grid=(S // tm,),
        in_specs=[blk(C_AQ // ATT_W), blk(C_AK // ATT_W), blk(C_AV // ATT_W), tab, tab, tab],
        out_specs=[blk(0)] * 3, out_shape=[out] * 3, compiler_params=_cparams(("parallel",)),
    )(u, u, u, *tables)


def att_prep_bwd(dqs, dks, dvs, tables, du, *, tm, name):
    S = dqs[0].shape[0]
    blk = pl.BlockSpec((tm, ATT_W), lambda i: (i, 0))
    tab = pl.BlockSpec((tm, 128), lambda i: (i, 0))

    def body(*refs):
        dq, dk, dv = (refs[3 * j][...].astype(F32) + refs[3 * j + 1][...].astype(F32) + refs[3 * j + 2][...].astype(F32)
                      for j in range(3))
        c_ref, s1_ref, s2_ref, _, o_ref = refs[9:]
        reps = ATT_W // 128
        c, s1, s2 = (jnp.tile(t[...], (1, reps)) for t in (c_ref, s1_ref, s2_ref))
        o_ref[:, 0:ATT_W] = (_unrope(dq, c, s1, s2) * (ATT_HD ** -0.5)).astype(BF16)
        o_ref[:, ATT_W:2 * ATT_W] = _unrope(dk, c, s1, s2).astype(BF16)
        o_ref[:, 2 * ATT_W:] = dv.astype(BF16)

    return pl.pallas_call(
        body, name=name, grid=(S // tm,), in_specs=[blk] * 9 + [tab] * 3 + [ANY],
        out_specs=pl.BlockSpec((tm, 3 * ATT_W), lambda i: (i, C_AQ // (3 * ATT_W))),
        out_shape=jax.ShapeDtypeStruct(du.shape, du.dtype), input_output_aliases={12: 0},
        compiler_params=_cparams(("parallel",)),
    )(*dqs, *dks, *dvs, *tables, du)


ATT_LAYOUT = {1: 1, 4: 16, 16: 16}


def _band_masks(split):
    sub = ATT_BLOCK // split
    pos = lambda i: split * (i % sub) + i // sub
    qi, ki = pos(_iota2((ATT_BLOCK, ATT_BLOCK), 0)), pos(_iota2((ATT_BLOCK, ATT_BLOCK), 1))
    return qi <= ki, ki <= qi


def _residue_view(t, dil):
    lay = ATT_LAYOUT[dil]
    return t.reshape(t.shape[0] // lay, lay * t.shape[1])


def _residue_unview(parts, dil, W):
    lay = ATT_LAYOUT[dil]
    if len(parts) == 1:
        return parts[0].reshape(parts[0].shape[0] * lay, W)
    L = parts[0].shape[0]
    return jnp.stack([p.reshape(L, dil, W) for p in parts], axis=1).reshape(L * lay, W)


def _tile_in_specs(width, dil, split, row_index):
    sub = ATT_BLOCK // split
    return [pl.BlockSpec((sub, width), lambda r, n, _j=j: (row_index(n), r + dil * _j)) for j in range(split)]


def _tile_out_specs(width, split, row_index):
    return [pl.BlockSpec((ATT_BLOCK // split, width), lambda r, n: (row_index(n), r))] * split


def _stage_in(subs, buf):
    if len(subs) == 1:
        return subs[0]
    sub = ATT_BLOCK // len(subs)
    for j, ref in enumerate(subs):
        buf[pl.ds(j * sub, sub), :] = ref[...]
    return buf


def _stage_out(buf, subs):
    if len(subs) > 1:
        sub = ATT_BLOCK // len(subs)
        for j, ref in enumerate(subs):
            ref[...] = buf[pl.ds(j * sub, sub), :]


def _pair_diag(x):
    first = _iota2(x.shape, 1) < ATT_HD
    zero = jnp.zeros_like(x)
    return jnp.concatenate([jnp.where(first, x, zero), jnp.where(first, zero, x)], axis=0)


def att_pattern_fwd(qr, kr, vb, dil, *, name):
    S = qr.shape[0]
    lay = ATT_LAYOUT[dil]
    split = lay // dil
    nb = S // dil // ATT_BLOCK
    view = lambda t: _residue_view(t, dil)
    cur = _tile_in_specs(ATT_W, dil, split, lambda n: n)
    prev = _tile_in_specs(ATT_W, dil, split, lambda n: jnp.maximum(n - 1, 0))

    def body(*refs):
        groups, bufs = [refs[i * split:(i + 1) * split] for i in range(7)], refs[7 * split:]
        q_ref, kc_ref, kp_ref, vc_ref, vp_ref = (_stage_in(g, b) for g, b in zip(groups[:5], bufs or [None] * 5))
        o_ref, l_ref = (groups[5][0], groups[6][0]) if split == 1 else bufs[5:7]
        tile_body(q_ref, kc_ref, kp_ref, vc_ref, vp_ref, o_ref, l_ref)
        _stage_out(o_ref, groups[5])
        _stage_out(l_ref, groups[6])

    def tile_body(q_ref, kc_ref, kp_ref, vc_ref, vp_ref, o_ref, l_ref):
        has_prev = pl.program_id(1) > 0
        m_prev, m_cur = _band_masks(split)
        m_prev = m_prev & has_prev
        first = _iota2((ATT_BLOCK, 128), 1) < ATT_HD
        lane = _iota2((ATT_BLOCK, 128), 1)
        stats = jnp.zeros((ATT_BLOCK, 128), F32)
        pairs = range(ATT_HEADS // 2)
        sls = [slice(p * 128, (p + 1) * 128) for p in pairs]
        sps = [_nt_raw(q_ref[:, sl], _pair_diag(kp_ref[:, sl]), "bf16") for sl in sls]
        scs = [_nt_raw(q_ref[:, sl], _pair_diag(kc_ref[:, sl]), "bf16") for sl in sls]
        probs, inv_dens = [], []
        for p in pairs:
            pps, pcs, dens, lses = [], [], [], []
            for half in range(2):
                hs = slice(half * 128, (half + 1) * 128)
                sp_h, sc_h = jnp.where(m_prev, sps[p][:, hs], NEG_INF), jnp.where(m_cur, scs[p][:, hs], NEG_INF)
                m = jnp.maximum(jnp.max(sp_h, axis=1, keepdims=True), jnp.max(sc_h, axis=1, keepdims=True))
                pp, pc = jnp.exp(sp_h - m), jnp.exp(sc_h - m)
                den = jnp.sum(pp, axis=1, keepdims=True) + jnp.sum(pc, axis=1, keepdims=True)
                pps.append(pp.astype(BF16))
                pcs.append(pc.astype(BF16))
                dens.append(den)
                lses.append(m + jnp.log(den))
            probs.append((jnp.concatenate(pps, axis=1), jnp.concatenate(pcs, axis=1)))
            inv_dens.append(1.0 / jnp.where(first, dens[0], dens[1]))
            stats = jnp.where(lane == 2 * p, lses[0], jnp.where(lane == 2 * p + 1, lses[1], stats))
        outs = [_nn_raw(probs[p][0], _pair_diag(vp_ref[:, sls[p]]), "bf16")
                + _nn_raw(probs[p][1], _pair_diag(vc_ref[:, sls[p]]), "bf16") for p in pairs]
        for p in pairs:
            o_ref[:, sls[p]] = (outs[p] * inv_dens[p]).astype(BF16)
        l_ref[...] = stats

    tile = lambda w, dt: pltpu.VMEM((ATT_BLOCK, w), dt)
    stage = [] if split == 1 else [tile(ATT_W, BF16)] * 6 + [tile(128, F32)]
    args = [view(qr), view(kr), view(kr), view(vb), view(vb)]
    res = pl.pallas_call(
        body, name=name, grid=(dil, nb), in_specs=cur + cur + prev + cur + prev,
        out_specs=_tile_out_specs(ATT_W, split, lambda n: n) + _tile_out_specs(128, split, lambda n: n),
        out_shape=[jax.ShapeDtypeStruct((S // lay, dil * ATT_W), BF16)] * split
        + [jax.ShapeDtypeStruct((S // lay, dil * 128), F32)] * split,
        scratch_shapes=stage, compiler_params=_cparams(("parallel", "arbitrary")),
    )(*[a for a in args for _ in range(split)])
    return _residue_unview(res[:split], dil, ATT_W), _residue_unview(res[split:], dil, 128)


def _head_spread():
    return (_iota2((128, ATT_W), 1) // ATT_HD == _iota2((128, ATT_W), 0)).astype(F32)


def att_combine_fwd(os_, ls, u, *, tm, name):
    S = u.shape[0]
    blk = lambda col: pl.BlockSpec((tm, ATT_W), lambda i, _c=col: (i, _c))
    lane = pl.BlockSpec((tm, 128), lambda i: (i, 0))

    def body(o1, o2, o3, l1, l2, l3, g_ref, y_ref, o_ref, lse_ref):
        a, b, c = l1[...], l2[...], l3[...]
        m = jnp.maximum(jnp.maximum(a, b), c)
        ea, eb, ec = jnp.exp(a - m), jnp.exp(b - m), jnp.exp(c - m)
        den = ea + eb + ec
        spread = _head_spread()
        wa, wb, wc = (_nn_raw(e / den, spread, "bf16x3") for e in (ea, eb, ec))
        o = wa * o1[...].astype(F32) + wb * o2[...].astype(F32) + wc * o3[...].astype(F32)
        o_ref[...] = o
        lse_ref[...] = m + jnp.log(den)
        y_ref[...] = (o * _silu(g_ref[...])).astype(BF16)

    return pl.pallas_call(
        body, name=name, grid=(S // tm,), in_specs=[blk(0)] * 3 + [lane] * 3 + [blk(C_AG // ATT_W)],
        out_specs=[blk(0), blk(0), lane],
        out_shape=[jax.ShapeDtypeStruct((S, ATT_W), BF16), jax.ShapeDtypeStruct((S, ATT_W), F32),
                   jax.ShapeDtypeStruct((S, 128), F32)],
        compiler_params=_cparams(("parallel",)),
    )(*os_, *ls, u)


def att_combine_bwd(dy, o, u, du, *, tm, name):
    S = u.shape[0]
    cw = 256
    base = (CONV_CH + GDN_W) // cw
    blk = lambda col: pl.BlockSpec((tm, ATT_W), lambda i, _c=col: (i, _c))

    def body(dy0, dy1, dy2, o_ref, g_ref, du_in, do_ref, dg_ref, dl_ref):
        g, d, o = g_ref[...], jnp.concatenate([dy0[...], dy1[...], dy2[...]], axis=1), o_ref[...]
        sg = _sigmoid(g)
        d_o = d * (g * sg)
        do_ref[...] = d_o.astype(BF16)
        dg_ref[...] = (d * o * (sg * (1.0 + g * (1.0 - sg)))).astype(BF16)
        dl_ref[...] = _nt_raw(d_o * o, _head_spread(), "bf16x3")

    return pl.pallas_call(
        body, name=name, grid=(S // tm,),
        in_specs=[pl.BlockSpec((tm, cw), lambda i, _j=j: (i, base + _j)) for j in range(ATT_W // cw)]
        + [blk(0), blk(C_AG // ATT_W), ANY],
        out_specs=[blk(0), blk(C_AG // ATT_W), pl.BlockSpec((tm, 128), lambda i: (i, 0))],
        out_shape=[jax.ShapeDtypeStruct((S, ATT_W), BF16), jax.ShapeDtypeStruct(du.shape, du.dtype),
                   jax.ShapeDtypeStruct((S, 128), F32)],
        input_output_aliases={5: 1},
        compiler_params=_cparams(("parallel",)),
    )(dy, dy, dy, o, u, du)


def att_pattern_bwd(qr, kr, vb, do, delta, lse, dil, *, name):
    S = qr.shape[0]
    lay = ATT_LAYOUT[dil]
    split = lay // dil
    nb = S // dil // ATT_BLOCK
    view = lambda t: _residue_view(t, dil)
    at_cur, at_prev = (lambda n: jnp.minimum(n, nb - 1)), (lambda n: jnp.maximum(n - 1, 0))
    cur = _tile_in_specs(ATT_W, dil, split, at_cur)
    prev = _tile_in_specs(ATT_W, dil, split, at_prev)
    narrow = _tile_in_specs(128, dil, split, at_cur)

    def body(*refs):
        groups = [refs[i * split:(i + 1) * split] for i in range(11)]
        (ck_ref, cv_ref), bufs = refs[11 * split:11 * split + 2], refs[11 * split + 2:]
        ins = [_stage_in(g, b) for g, b in zip(groups[:8], bufs or [None] * 8)]
        outs = [g[0] for g in groups[8:11]] if split == 1 else bufs[8:11]
        tile_body(*ins, *outs, ck_ref, cv_ref)
        n = pl.program_id(1)

        @pl.when(n < nb)
        def _():
            _stage_out(outs[0], groups[8])

        @pl.when(n > 0)
        def _():
            _stage_out(outs[1], groups[9])
            _stage_out(outs[2], groups[10])

    def tile_body(q_ref, kc_ref, kp_ref, vc_ref, vp_ref, do_ref, dl_ref, l_ref, dq_ref, dk_ref, dv_ref, ck_ref, cv_ref):
        n = pl.program_id(1)

        @pl.when(n < nb)
        def _():
            m_prev, m_cur = _band_masks(split)
            m_prev = m_prev & (n > 0)
            m_prev2, m_cur2 = jnp.concatenate([m_prev, m_prev], axis=1), jnp.concatenate([m_cur, m_cur], axis=1)
            first = _iota2((ATT_BLOCK, 128), 1) < ATT_HD
            wide = (ATT_BLOCK, 128)
            halves = lambda a, b: jnp.concatenate([jnp.broadcast_to(a, wide), jnp.broadcast_to(b, wide)], axis=1)
            fold = lambda t: jnp.where(first, t[:ATT_BLOCK], t[ATT_BLOCK:])
            pairs = range(ATT_HEADS // 2)
            sls = [slice(p * 128, (p + 1) * 128) for p in pairs]
            qs, dos = [q_ref[:, sl] for sl in sls], [do_ref[:, sl] for sl in sls]
            kps, kcs, vps, vcs = ([_pair_diag(r[:, sl]) for sl in sls] for r in (kp_ref, kc_ref, vp_ref, vc_ref))
            s_p = [_nt_raw(qs[p], kps[p], "bf16") for p in pairs]
            s_c = [_nt_raw(qs[p], kcs[p], "bf16") for p in pairs]
            dp_p = [_nt_raw(dos[p], vps[p], "bf16") for p in pairs]
            dp_c = [_nt_raw(dos[p], vcs[p], "bf16") for p in pairs]
            pps, pcs, dsps, dscs = [], [], [], []
            for p in pairs:
                delta = halves(dl_ref[:, 2 * p:2 * p + 1], dl_ref[:, 2 * p + 1:2 * p + 2])
                lse2 = halves(l_ref[:, 2 * p:2 * p + 1], l_ref[:, 2 * p + 1:2 * p + 2])
                pp = jnp.where(m_prev2, jnp.exp(s_p[p] - lse2), 0.0)
                pc = jnp.where(m_cur2, jnp.exp(s_c[p] - lse2), 0.0)
                dsps.append((pp * (dp_p[p] - delta)).astype(BF16))
                dscs.append((pc * (dp_c[p] - delta)).astype(BF16))
                pps.append(pp.astype(BF16))
                pcs.append(pc.astype(BF16))
            dqs = [_nn_raw(dsps[p], kps[p], "bf16") + _nn_raw(dscs[p], kcs[p], "bf16") for p in pairs]
            dk_prev = [fold(_tn_raw(dsps[p], qs[p], "bf16")) for p in pairs]
            dv_prev = [fold(_tn_raw(pps[p], dos[p], "bf16")) for p in pairs]
            dk_cur = [fold(_tn_raw(dscs[p], qs[p], "bf16")) for p in pairs]
            dv_cur = [fold(_tn_raw(pcs[p], dos[p], "bf16")) for p in pairs]
            for p in pairs:
                dq_ref[:, sls[p]] = dqs[p].astype(BF16)

            @pl.when(n > 0)
            def _():
                for p in pairs:
                    dk_ref[:, sls[p]] = (ck_ref[:, sls[p]] + dk_prev[p]).astype(BF16)
                    dv_ref[:, sls[p]] = (cv_ref[:, sls[p]] + dv_prev[p]).astype(BF16)

            for p in pairs:
                ck_ref[:, sls[p]] = dk_cur[p]
                cv_ref[:, sls[p]] = dv_cur[p]

        @pl.when(n == nb)
        def _():
            dk_ref[...] = ck_ref[...].astype(BF16)
            dv_ref[...] = cv_ref[...].astype(BF16)

    out = [jax.ShapeDtypeStruct((S // lay, dil * ATT_W), BF16)] * split
    tile = lambda w, dt: pltpu.VMEM((ATT_BLOCK, w), dt)
    stage = [] if split == 1 else [tile(ATT_W, BF16)] * 6 + [tile(128, F32)] * 2 + [tile(ATT_W, BF16)] * 3
    args = [view(qr), view(kr), view(kr), view(vb), view(vb), view(do), view(delta), view(lse)]
    res = pl.pallas_call(
        body, name=name, grid=(dil, nb + 1), in_specs=cur + cur + prev + cur + prev + cur + narrow + narrow,
        out_specs=_tile_out_specs(ATT_W, split, at_cur) + _tile_out_specs(ATT_W, split, at_prev) * 2,
        out_shape=out * 3, scratch_shapes=[tile(ATT_W, F32), tile(ATT_W, F32)] + stage,
        compiler_params=_cparams(("arbitrary", "arbitrary")),
    )(*[a for a in args for _ in range(split)])
    return tuple(_residue_unview(res[i * split:(i + 1) * split], dil, ATT_W) for i in range(3))


def _loss_rows(x, w, tgt):
    err = _rms_fn(x, w) - tgt
    return jnp.sum(0.5 * jnp.mean(err * err, axis=-1, keepdims=True), axis=0, keepdims=True)


def loss_head(x, w, tgt, *, tm, name):
    S, D = x.shape

    def body(x_ref, w_ref, t_ref, l_ref, dx_ref, dw_ref):
        val, vjp = jax.vjp(_loss_rows, x_ref[...], w_ref[...], t_ref[...])
        dx, dw, _ = vjp(jnp.ones((1, 1), F32))
        dx_ref[...] = dx

        @pl.when(pl.program_id(0) == 0)
        def _():
            l_ref[...] = jnp.zeros_like(l_ref)
            dw_ref[...] = jnp.zeros_like(dw_ref)

        l_ref[...] += val
        dw_ref[...] += dw

    row = pl.BlockSpec((tm, D), lambda i: (i, 0))
    vec = pl.BlockSpec((1, D), lambda i: (0, 0))
    one = pl.BlockSpec((1, 1), lambda i: (0, 0))
    return pl.pallas_call(
        body, name=name, grid=(S // tm,), in_specs=[row, vec, row], out_specs=[one, row, vec],
        out_shape=[jax.ShapeDtypeStruct((1, 1), F32), jax.ShapeDtypeStruct((S, D), F32), jax.ShapeDtypeStruct((1, D), F32)],
        compiler_params=_cparams(("arbitrary",)),
    )(x, w, tgt)


def adam(w, g, m, v, *, name):
    shape = w.shape
    C = shape[-1]
    R = w.size // C
    br = R
    while br * C * 4 > (1 << 21) and br % 16 == 0:
        br //= 2
    two = lambda t: t.reshape(R, C)

    def body(w_ref, g_ref, m_ref, v_ref, d_ref, mo_ref, vo_ref):
        gg = g_ref[...]
        m_new = ADAM_B1 * m_ref[...] + (1.0 - ADAM_B1) * gg
        v_new = ADAM_B2 * v_ref[...] + (1.0 - ADAM_B2) * jnp.square(gg)
        m_hat = m_new / (1.0 - ADAM_B1 ** ADAM_STEP)
        v_hat = v_new / (1.0 - ADAM_B2 ** ADAM_STEP)
        d_ref[...] = -ADAM_LR * (m_hat / (jnp.sqrt(v_hat) + ADAM_EPS) + ADAM_WD * w_ref[...])
        mo_ref[...] = m_new
        vo_ref[...] = v_new

    blk = pl.BlockSpec((br, C), lambda i: (i, 0))
    out = jax.ShapeDtypeStruct((R, C), F32)
    d, mo, vo = pl.pallas_call(
        body, name=name, grid=(R // br,), in_specs=[blk] * 4, out_specs=[blk] * 3, out_shape=[out] * 3,
        compiler_params=_cparams(("parallel",)),
    )(two(w), two(g), two(m), two(v))
    return d.reshape(shape), mo.reshape(shape), vo.reshape(shape)


MESH_IDS = pl.DeviceIdType.MESH
ANY = pl.BlockSpec(memory_space=pl.ANY)


def _my_id():
    return 4 * lax.axis_index("x") + 2 * lax.axis_index("y") + lax.axis_index("c")


def _peer(k):
    x, y, c = lax.axis_index("x"), lax.axis_index("y"), lax.axis_index("c")
    flip = lambda v, bit: 1 - v if bit else v
    return (flip(x, k & 4), flip(y, k & 2), flip(c, k & 1))


def all_gather_two_level(arrs, *, name):
    n = len(arrs)

    def body(*refs):
        ins, outs = refs[:n], refs[n:2 * n]
        send, recv, local = refs[2 * n:]
        x, y, c = lax.axis_index("x"), lax.axis_index("y"), lax.axis_index("c")
        me, sibling = (x, y, c), (x, y, 1 - c)
        chips = [(1 - x, y), (x, 1 - y), (1 - x, 1 - y)]

        def slot(a, dev):
            return outs[a].at[4 * dev[0] + 2 * dev[1] + dev[2]]

        def copy(a, k, block, to, src=None):
            return pltpu.make_async_remote_copy(
                src_ref=slot(a, block) if src is None else src, dst_ref=slot(a, block), send_sem=send.at[a, k],
                recv_sem=recv.at[a, k], device_id=to, device_id_type=MESH_IDS)

        mine = [pltpu.make_async_copy(ins[a], slot(a, me), local.at[a]) for a in range(n)]
        first = [copy(a, 1 + j, me, (*chip, c), src=ins[a]) for a in range(n) for j, chip in enumerate(chips)]
        first += [copy(a, 0, me, sibling, src=ins[a]) for a in range(n)]
        for cp in first + mine:
            cp.start()
        passed = []
        for a in range(n):
            for j, chip in enumerate(chips):
                copy(a, 1 + j, (*chip, c), me).wait_recv()
                fwd = copy(a, 4 + j, (*chip, c), sibling)
                fwd.start()
                passed.append(fwd)
        for a in range(n):
            copy(a, 0, sibling, me).wait_recv()
            for j, chip in enumerate(chips):
                copy(a, 4 + j, (*chip, 1 - c), me).wait_recv()
        for cp in first + passed:
            cp.wait_send()
        for cp in mine:
            cp.wait()

    return pl.pallas_call(
        body, name=name, in_specs=[ANY] * n, out_specs=[ANY] * n,
        out_shape=[jax.ShapeDtypeStruct((N_DEV,) + a.shape, a.dtype) for a in arrs],
        scratch_shapes=[pltpu.SemaphoreType.DMA((n, N_DEV - 1)), pltpu.SemaphoreType.DMA((n, N_DEV - 1)),
                        pltpu.SemaphoreType.DMA((n,))],
        compiler_params=pltpu.CompilerParams(has_side_effects=True),
    )(*arrs)


def scatter_exchange(groups, pack, *, name):
    flat = [a for grp in groups for a in grp]
    n = len(flat) + 1
    shapes = [jax.ShapeDtypeStruct((N_DEV, len(grp), grp[0].shape[0] // N_DEV, grp[0].shape[1]), grp[0].dtype) for grp in groups]
    shapes.append(jax.ShapeDtypeStruct((N_DEV,) + pack.shape, pack.dtype))
    index = [(gi, li) for gi, grp in enumerate(groups) for li in range(len(grp))]

    def body(*refs):
        ins, outs = refs[:n], refs[n:n + len(shapes)]
        send, recv, local = refs[n + len(shapes):]
        me = _my_id()
        started = []
        for a in range(n):
            if a < n - 1:
                gi, li = index[a]
                r = ins[a].shape[0] // N_DEV
                src = lambda j, _a=a, _r=r: ins[_a].at[pl.ds(pl.multiple_of(j * _r, 8), _r), :]
                dst = outs[gi].at[me, li]
            else:
                src = lambda j, _a=a: ins[_a]
                dst = outs[-1].at[me]
            lc = pltpu.make_async_copy(src(me), dst, local.at[a])
            lc.start()
            started.append(lc)
            for k in range(1, N_DEV):
                cp = pltpu.make_async_remote_copy(src_ref=src(me ^ k), dst_ref=dst, send_sem=send.at[a, k - 1],
                                                  recv_sem=recv.at[a, k - 1], device_id=_peer(k), device_id_type=MESH_IDS)
                cp.start()
                started.append(cp)
        for cp in started:
            cp.wait()

    return pl.pallas_call(
        body, name=name, in_specs=[ANY] * n, out_specs=[ANY] * len(shapes), out_shape=shapes,
        scratch_shapes=[pltpu.SemaphoreType.DMA((n, N_DEV - 1)), pltpu.SemaphoreType.DMA((n, N_DEV - 1)),
                        pltpu.SemaphoreType.DMA((n,))],
        compiler_params=pltpu.CompilerParams(has_side_effects=True),
    )(*flat, pack)


def slot_sum(x, *, name):
    _, A, R, C = x.shape
    br = R
    while br * C * 4 * N_DEV > (1 << 23) and br % 16 == 0:
        br //= 2

    def body(x_ref, o_ref):
        acc = x_ref[0, 0].astype(F32)
        for s in range(1, N_DEV):
            acc = acc + x_ref[s, 0].astype(F32)
        o_ref[0] = acc

    return pl.pallas_call(
        body, name=name, grid=(A, R // br),
        in_specs=[pl.BlockSpec((N_DEV, 1, br, C), lambda a, i: (0, a, i, 0))],
        out_specs=pl.BlockSpec((1, br, C), lambda a, i: (a, i, 0)),
        out_shape=jax.ShapeDtypeStruct((A, R, C), F32),
        compiler_params=_cparams(("parallel", "parallel")),
    )(x)


HBM_SPEC = pl.BlockSpec(memory_space=pltpu.HBM)
SEM_SPEC = pl.BlockSpec(memory_space=pltpu.SEMAPHORE)
DATAFLOW = pltpu.SideEffectType.DATAFLOW_SIDE_EFFECTING


def _push_copies(src_refs, land_refs, send_sems, recv_sems, by_rows):
    me = _my_id()
    copies = []
    for a, (src, land) in enumerate(zip(src_refs, land_refs)):
        rows = land.shape[1]
        for k in range(1, N_DEV):
            piece = src.at[pl.ds(pl.multiple_of((me ^ k) * rows, 8), rows), :] if by_rows else src
            copies.append(pltpu.make_async_remote_copy(
                src_ref=piece, dst_ref=land.at[me], send_sem=send_sems[a].at[k - 1], recv_sem=recv_sems[a].at[k - 1],
                device_id=_peer(k), device_id_type=MESH_IDS))
    return copies


def push_start(srcs, lands, *, by_rows, name):
    n = len(srcs)

    def body(*refs):
        src_refs, land_refs = refs[:n], refs[n:2 * n]
        send_sems, recv_sems = refs[2 * n:3 * n], refs[3 * n:4 * n]
        token = refs[6 * n]
        for cp in _push_copies(src_refs, land_refs, send_sems, recv_sems, by_rows):
            cp.start()
        token[...] = jnp.zeros_like(token)

    sems = [pltpu.SemaphoreType.DMA((N_DEV - 1,))] * (2 * n)
    bufs = [pltpu.HBM(a.shape, a.dtype) for a in list(srcs) + list(lands)]
    outs = pl.pallas_call(
        body, name=name, out_shape=tuple(sems + bufs + [jax.ShapeDtypeStruct((8, 128), F32)]),
        in_specs=[HBM_SPEC] * (2 * n), out_specs=tuple([SEM_SPEC] * (2 * n) + [HBM_SPEC] * (2 * n) + [pl.BlockSpec(memory_space=pltpu.VMEM)]),
        input_output_aliases={i: 2 * n + i for i in range(2 * n)},
        compiler_params=pltpu.CompilerParams(has_side_effects=DATAFLOW),
    )(*[pltpu.with_memory_space_constraint(a, pltpu.HBM) for a in list(srcs) + list(lands)])
    return outs[:n], outs[n:2 * n], outs[2 * n:3 * n], outs[3 * n:4 * n], outs[4 * n]


def push_wait(send_sems, recv_sems, srcs, lands, after, *, by_rows, name):
    n = len(srcs)

    def body(*refs):
        src_refs, land_refs = refs[:n], refs[n:2 * n]
        send, recv = refs[2 * n:3 * n], refs[3 * n:4 * n]
        for cp in _push_copies(src_refs, land_refs, send, recv, by_rows):
            cp.wait_send()
            cp.wait_recv()

    outs = pl.pallas_call(
        body, name=name, out_shape=tuple(pltpu.HBM(a.shape, a.dtype) for a in list(srcs) + list(lands)),
        in_specs=[HBM_SPEC] * (2 * n) + [SEM_SPEC] * (2 * n) + [ANY], out_specs=tuple([HBM_SPEC] * (2 * n)),
        input_output_aliases={i: i for i in range(2 * n)},
        compiler_params=pltpu.CompilerParams(has_side_effects=DATAFLOW),
    )(*srcs, *lands, *send_sems, *recv_sems, after)
    return outs[n:]


LANDING_BLOCKS = 4


def _landing(src, slots_shape, after=None, *, by_rows, name):
    rows, cols = slots_shape[1], slots_shape[2]
    br = rows // LANDING_BLOCKS
    me = _my_id().astype(jnp.int32).reshape(1)

    def body(me_ref, src_ref, *rest):
        rest[-1][0] = src_ref[...]

    extra = [] if after is None else [after]
    grid_spec = pltpu.PrefetchScalarGridSpec(
        num_scalar_prefetch=1, grid=(LANDING_BLOCKS,),
        in_specs=[pl.BlockSpec((br, cols), lambda i, me_ref: ((me_ref[0] * LANDING_BLOCKS if by_rows else 0) + i, 0))]
        + [ANY] * len(extra),
        out_specs=pl.BlockSpec((1, br, cols), lambda i, me_ref: (me_ref[0], i, 0)))
    return pl.pallas_call(
        body, name=name, grid_spec=grid_spec, out_shape=jax.ShapeDtypeStruct(slots_shape, src.dtype),
    )(me, src, *extra)


def _pack(arrs):
    flat = []
    for a in arrs:
        f = a.reshape(-1).astype(F32)
        flat.append(jnp.pad(f, (0, (-f.size) % 128)))
    f = jnp.concatenate(flat)
    return jnp.pad(f, (0, (-f.size) % 1024)).reshape(-1, 128)


def _unpack(p, shapes):
    f = p.reshape(-1)
    out, off = [], 0
    for s in shapes:
        n = math.prod(s)
        out.append(f[off:off + n].reshape(s))
        off += n + (-n) % 128
    return out


def _to_padded_cols(w):
    z = lambda n: jnp.zeros(w.shape[:-1] + (n,), w.dtype)
    return jnp.concatenate([w[..., 0:4608], w[..., 4620:7692], w[..., 4608:4614], z(122), w[..., 4614:4620], z(378)], axis=-1)


def _from_padded_cols(w):
    return jnp.concatenate([w[..., 0:4608], w[..., C_BETA:C_BETA + 6], w[..., C_ALPHA:C_ALPHA + 6], w[..., 4608:7680]], axis=-1)


def _lane_pad(v):
    return jnp.pad(v, (0, 128 - v.shape[0]))[None, :]


TM_MM, TN_MM, TK_MM = 1024, 1024, 2048
TM_ROW = 512


def layer_fwd(x, p, tabs, l):
    h = rms_fwd(x, p["norm_w"], tm=TM_ROW, name=f"rms_fwd_{l}")
    u = matmul(h, p["w_in"], mode="nn", tm=TM_MM, tn=TN_MM, tk=TK_MM, name=f"in_proj_{l}")
    y_conv, c = conf_fwd(u, p["dw_w"], p["dw_b"], p["ln_w"], p["ln_b"], p["pw"], tm=TM_ROW, name=f"conf_fwd_{l}")
    q, k, v, beta, gc = gdn_prep_fwd(u, p["conv_w"], p["a_log"], p["dt_bias"], tm=TM_ROW, name=f"gdn_prep_fwd_{l}")
    y_gdn, states, tinv = gdn_core_fwd(q, k, v, u, beta, gc, p["gdn_nw"], name=f"gdn_core_fwd_{l}")
    qr, kr, vb = att_prep_fwd(u, tabs, tm=TM_ROW, name=f"att_prep_fwd_{l}")
    os_, ls = [], []
    for _, dil in DIL_PATTERNS:
        o_p, l_p = att_pattern_fwd(qr, kr, vb, dil, name=f"att_fwd_d{dil}_{l}")
        os_.append(o_p)
        ls.append(l_p)
    y_att, o, lse = att_combine_fwd(os_, ls, u, tm=TM_ROW, name=f"att_combine_fwd_{l}")
    y = jnp.concatenate([y_conv, y_gdn, y_att], axis=1)
    if callable(p["w_out"]):
        p["w_out"] = p["w_out"](y)
    x_new = matmul(y, p["w_out"], mode="nn", tm=TM_MM // 2, tn=D_MODEL, tk=TK_MM, residual=x, name=f"out_proj_{l}")
    saved = dict(x=x, h=h, u=u, y=y, c=c, q=q, k=k, v=v, beta=beta, gc=gc, states=states, tinv=tinv, qr=qr, kr=kr, vb=vb, o=o,
                 lse=lse)
    return x_new, saved


def layer_bwd(dx_out, s, p, tabs, l, send_w_out=None, send_w_in=None):
    S = dx_out.shape[0]
    u = s["u"]
    dy = matmul(dx_out, p["w_out"], mode="nt", tm=TM_MM // 2, tn=D_MODEL, tk=TK_MM, name=f"out_proj_dy_{l}")
    g_w_out = matmul(s["y"], dx_out, mode="tn", tm=TM_MM, tn=TN_MM, tk=TK_MM, out_dtype=BF16, name=f"out_proj_dw_{l}")
    ln_b = p["ln_b"] if send_w_out is None else p["ln_b"] + send_w_out(g_w_out)
    du = lax.empty((S, IN_PAD), BF16)
    dc, du, g_ln_w, g_ln_b, g_pw, g_dw_b = conf_bwd_post(u, s["c"], dy, p["ln_w"], ln_b, p["pw"], du,
                                                       tm=TM_ROW, name=f"conf_bwd_post_{l}")
    du, g_dw_w = conv_bwd(dc, [(u, C_CA), (u, C_CB)], p["dw_w"], du, C_CA, K=CONV_WIDTH, H=CONF_HALO, tm=TM_ROW, cw=CONV_CH,
                          glu=True, name=f"conf_bwd_conv_{l}")
    dq, dk, dv, du, dbeta, dgc, g_gdn_nw = gdn_core_bwd(s["q"], s["k"], s["v"], u, s["beta"], s["gc"], p["gdn_nw"],
                                                        s["states"], s["tinv"], dy, du, name=f"gdn_core_bwd_{l}")
    dpre, du, g_a_log, g_dt_bias = gdn_prep_bwd(u, p["conv_w"], p["a_log"], p["dt_bias"], dq, dk, dv, dbeta, dgc, du,
                                                tm=TM_ROW, name=f"gdn_prep_bwd_{l}")
    du, g_conv_w = conv_bwd(dpre, [(u, C_GQ)], p["conv_w"], du, C_GQ, K=SHORT_CONV, H=GDN_HALO, tm=TM_ROW, cw=GDN_W,
                            glu=False, name=f"gdn_bwd_conv_{l}")
    do, du, delta = att_combine_bwd(dy, s["o"], u, du, tm=TM_ROW, name=f"att_combine_bwd_{l}")
    dqs, dks, dvs = [], [], []
    for _, dil in DIL_PATTERNS:
        a, b, c = att_pattern_bwd(s["qr"], s["kr"], s["vb"], do, delta, s["lse"], dil, name=f"att_bwd_d{dil}_{l}")
        dqs.append(a)
        dks.append(b)
        dvs.append(c)
    du = att_prep_bwd(dqs, dks, dvs, tabs, du, tm=TM_ROW, name=f"att_prep_bwd_{l}")
    g_w_in = matmul(s["h"], du, mode="tn", tm=TM_MM, tn=TN_MM, tk=min(2 * TK_MM, S), out_dtype=BF16, name=f"in_proj_dw_{l}")
    sent = None if send_w_in is None else send_w_in(g_w_in)
    dh = matmul(du, p["w_in"], mode="nt", tm=TM_MM, tn=TN_MM, tk=2 * TK_MM, after=sent, name=f"in_proj_dh_{l}")
    dx, g_norm_w = rms_bwd(s["x"], p["norm_w"], dh, dx_out, tm=TM_ROW // 2, name=f"rms_bwd_{l}")
    grads = dict(norm_w=g_norm_w[0], w_in=g_w_in, conv_qkv_w=g_conv_w, a_log=g_a_log[0, :GDN_HEADS], dt_bias=g_dt_bias[0, :GDN_HEADS],
                 gdn_norm_w=g_gdn_nw[0], conf_dw_w=g_dw_w, conf_dw_b=g_dw_b[0], conf_ln_w=g_ln_w[0], conf_ln_b=g_ln_b[0],
                 conf_pw_w=g_pw, w_out=g_w_out)
    return dx, grads


WEIGHTS = ("norm_w", "w_in", "conv_qkv_w", "a_log", "dt_bias", "gdn_norm_w", "conf_dw_w", "conf_dw_b", "conf_ln_w",
           "conf_ln_b", "conf_pw_w", "w_out", "final_norm_w")
SMALL_REPLICATED = ("norm_w", "a_log", "dt_bias", "gdn_norm_w", "conf_dw_b", "conf_ln_w", "conf_ln_b")


def kernel(x, norm_w, w_in, conv_qkv_w, a_log, dt_bias, gdn_norm_w, conf_dw_w, conf_dw_b, conf_ln_w, conf_ln_b, conf_pw_w, w_out, final_norm_w, loss_target, m_norm_w, m_w_in, m_conv_qkv_w, m_a_log, m_dt_bias, m_gdn_norm_w, m_conf_dw_w, m_conf_dw_b, m_conf_ln_w, m_conf_ln_b, m_conf_pw_w, m_w_out, m_final_norm_w, v_norm_w, v_w_in, v_conv_qkv_w, v_a_log, v_dt_bias, v_gdn_norm_w, v_conf_dw_w, v_conf_dw_b, v_conf_ln_w, v_conf_ln_b, v_conf_pw_w, v_w_out, v_final_norm_w):
    w = dict(norm_w=norm_w, w_in=w_in, conv_qkv_w=conv_qkv_w, a_log=a_log, dt_bias=dt_bias, gdn_norm_w=gdn_norm_w,
             conf_dw_w=conf_dw_w, conf_dw_b=conf_dw_b, conf_ln_w=conf_ln_w, conf_ln_b=conf_ln_b, conf_pw_w=conf_pw_w,
             w_out=w_out, final_norm_w=final_norm_w)
    m = dict(zip(WEIGHTS, (m_norm_w, m_w_in, m_conv_qkv_w, m_a_log, m_dt_bias, m_gdn_norm_w, m_conf_dw_w, m_conf_dw_b,
                           m_conf_ln_w, m_conf_ln_b, m_conf_pw_w, m_w_out, m_final_norm_w)))
    v = dict(zip(WEIGHTS, (v_norm_w, v_w_in, v_conv_qkv_w, v_a_log, v_dt_bias, v_gdn_norm_w, v_conf_dw_w, v_conf_dw_b,
                           v_conf_ln_w, v_conf_ln_b, v_conf_pw_w, v_w_out, v_final_norm_w)))
    S = x.shape[1]
    L = norm_w.shape[0]
    me = _my_id()

    small_shapes = [conv_qkv_w.shape, conf_dw_w.shape, conf_pw_w.shape]
    w_in_b, w_out_b = _to_padded_cols(w_in).astype(BF16), w_out.astype(BF16)
    in_slots, out_slots = (N_DEV,) + w_in_b.shape[1:], (N_DEV,) + w_out_b.shape[1:]
    g_in0, g_small = all_gather_two_level([w_in_b[0], _pack([conv_qkv_w, conf_dw_w, conf_pw_w])], name="gather_first")
    gathers, tie = {}, jnp.zeros((1, 1), F32)
    for l in range(L):
        srcs = [w_out_b[0]] if l == 0 else [w_in_b[l], w_out_b[l]]
        slots = [out_slots] if l == 0 else [in_slots, out_slots]
        lands = [_landing(a, sl, g_in0, by_rows=False, name=f"gather_own_{l}_{j}") for j, (a, sl) in enumerate(zip(srcs, slots))]
        *flight, token = push_start(srcs, lands, by_rows=False, name=f"gather_start_{l}")
        gathers[l] = flight
        tie = tie + token[0:1, 0:1]
    parts = [_unpack(g_small[s], small_shapes) for s in range(N_DEV)]
    conv_full = jnp.concatenate([pt[0] for pt in parts], axis=2)
    dw_full = jnp.concatenate([pt[1] for pt in parts], axis=2)
    pw_full = jnp.concatenate([pt[2] for pt in parts], axis=1)
    tabs = rope_tables(S)

    def layer_params(l, full_in, full_out):
        return dict(
            norm_w=norm_w[l][None], w_in=full_in.reshape(D_MODEL, IN_PAD), w_out=full_out,
            conv_w=conv_full[l], a_log=_lane_pad(a_log[l]), dt_bias=_lane_pad(dt_bias[l]), gdn_nw=gdn_norm_w[l][None],
            dw_w=dw_full[l], dw_b=conf_dw_b[l][None], ln_w=conf_ln_w[l][None], ln_b=conf_ln_b[l][None], pw=pw_full[l])

    xs = x[0]
    params, saved = [], []
    for l in range(L):
        if l == 0:
            late_out = lambda after: push_wait(*gathers[0], after, by_rows=False, name="gather_wait_0")[0].reshape(D_MODEL, D_MODEL)
            p = layer_params(0, g_in0, late_out)
            p["norm_w"] = p["norm_w"] + tie
        else:
            full_in, full_out = push_wait(*gathers[l], xs, by_rows=False, name=f"gather_wait_{l}")
            p = layer_params(l, full_in, full_out.reshape(D_MODEL, D_MODEL))
        params.append(p)
        xs, sv = layer_fwd(xs, p, tabs, l)
        saved.append(sv)
    loss_part, dx, g_final = loss_head(xs, final_norm_w[None], loss_target[0], tm=TM_ROW // 2, name="loss_head")

    layer_grads, scatters = [None] * L, {}

    def send(kind, l, grad, slots):
        land = _landing(grad, slots, by_rows=True, name=f"scatter_own_{kind}_{l}")
        *flight, token = push_start([grad], [land], by_rows=True, name=f"scatter_start_{kind}_{l}")
        scatters[kind, l] = flight
        return token[0:1, 0:1]

    for l in reversed(range(L)):
        dx, layer_grads[l] = layer_bwd(dx, saved[l], params[l], tabs, l, functools.partial(send, "out", l, slots=out_slots),
                                       functools.partial(send, "in", l, slots=in_slots))

    stack = lambda name: jnp.stack([layer_grads[l][name] for l in range(L)])
    small = [loss_part] + [stack(n) for n in SMALL_REPLICATED] + [g_final[0], stack("conv_qkv_w"), stack("conf_dw_w")]
    small_shapes = [a.shape for a in small]
    r_pw, r_small = scatter_exchange([[layer_grads[l]["conf_pw_w"] for l in range(L)]], _pack(small), name="scatter_small")
    g = {}
    g["conf_pw_w"] = slot_sum(r_pw, name="sum_pw")
    summed = _unpack(slot_sum(r_small[:, None], name="sum_small")[0], small_shapes)
    loss = summed[0].reshape(())
    for n, a in zip(SMALL_REPLICATED, summed[1:1 + len(SMALL_REPLICATED)]):
        g[n] = a
    g["final_norm_w"] = summed[-3]
    g["conv_qkv_w"] = lax.dynamic_slice_in_dim(summed[-2], me * conv_qkv_w.shape[2], conv_qkv_w.shape[2], axis=2)
    g["conf_dw_w"] = lax.dynamic_slice_in_dim(summed[-1], me * conf_dw_w.shape[2], conf_dw_w.shape[2], axis=2)
    deltas, new_m, new_v = {}, {}, {}
    for n in WEIGHTS:
        if n not in ("w_in", "w_out"):
            deltas[n], new_m[n], new_v[n] = adam(w[n], g[n], m[n], v[n], name=f"adam_{n}")
    sums = {}
    order = [(kind, l) for l in reversed(range(L)) for kind in ("out", "in")]
    done_first = loss_part
    for kind, l in order:
        last = (kind, l) == order[-1]
        after = (done_first + deltas["a_log"][0:1, 0:1]) if last else dx
        land, = push_wait(*scatters[kind, l], after, by_rows=True, name=f"scatter_wait_{kind}_{l}")
        sums[kind, l] = slot_sum(land[:, None], name=f"sum_w_{kind}_{l}")
        if not last:
            done_first = done_first + sums[kind, l][0, 0:1, 0:1]
    g["w_in"] = _from_padded_cols(jnp.concatenate([sums["in", l] for l in range(L)], axis=0))
    g["w_out"] = jnp.concatenate([sums["out", l] for l in range(L)], axis=0)
    for n in ("w_in", "w_out"):
        deltas[n], new_m[n], new_v[n] = adam(w[n], g[n], m[n], v[n], name=f"adam_{n}")
    return (loss, dx[None], *[g[n] for n in WEIGHTS], *[deltas[n] for n in WEIGHTS],
            *[new_m[n] for n in WEIGHTS], *[new_v[n] for n in WEIGHTS])
```

```python
import functools
import math

import jax
import jax.numpy as jnp
from jax import lax
from jax.experimental import pallas as pl
from jax.experimental.pallas import tpu as pltpu

D_MODEL = 2048
DEPTH = 4
N_DEV = 8
GDN_DK = 128
GDN_HEADS = 6
GDN_W = 768
ATT_HD = 64
ATT_HEADS = 12
ATT_W = 768
CONV_CH = 512
CONV_WIDTH = 31
SHORT_CONV = 4
GDN_CHUNK = 64
ROPE_THETA = 500000.0
ROPE_DIM = 16
DIL_PATTERNS = ((128, 1), (512, 4), (2048, 16))
ATT_BLOCK = 128
NEG_INF = -1e30
IN_W = 7692

ADAM_LR = 0.001
ADAM_B1 = 0.9
ADAM_B2 = 0.999
ADAM_EPS = 1e-08
ADAM_WD = 0.01
ADAM_STEP = 10

C_CA, C_CB, C_CG = 0, 512, 1024
C_GQ, C_GK, C_GV, C_GZ = 1536, 2304, 3072, 3840
C_AQ, C_AK, C_AV, C_AG = 4608, 5376, 6144, 6912
C_BETA, C_ALPHA = 7680, 7808
IN_PAD = 8192

VMEM_LIMIT = 56 * 1024 * 1024
CONF_HALO = 32
GDN_HALO = 8
CONV_ROWS = 128
GDN_GROUP = 4
GDN_HEADS_PER_STEP = 6

F32 = jnp.float32
BF16 = jnp.bfloat16
HI = lax.Precision.HIGHEST


def _cparams(sem, vmem=VMEM_LIMIT):
    return pltpu.CompilerParams(dimension_semantics=sem, vmem_limit_bytes=vmem)


def _dg(a, b, ca, cb, prec):
    nb = a.ndim - 2
    batch = tuple(range(nb))
    dn = (((ca + nb,), (cb + nb,)), (batch, batch))
    if prec == "bf16":
        return lax.dot_general(a.astype(BF16), b.astype(BF16), dn, preferred_element_type=F32)
    if prec == "bf16x3":
        ah, bh = a.astype(BF16), b.astype(BF16)
        al, bl = (a - ah.astype(F32)).astype(BF16), (b - bh.astype(F32)).astype(BF16)
        dot = lambda x, y: lax.dot_general(x, y, dn, preferred_element_type=F32)
        return dot(ah, bh) + (dot(ah, bl) + dot(al, bh))
    return lax.dot_general(a.astype(F32), b.astype(F32), dn, precision=HI, preferred_element_type=F32)


def _nn_raw(a, b, prec):
    return _dg(a, b, 1, 0, prec)


def _nt_raw(a, b, prec):
    return _dg(a, b, 1, 1, prec)


def _tn_raw(a, b, prec):
    return _dg(a, b, 0, 0, prec)


@functools.partial(jax.custom_vjp, nondiff_argnums=(2,))
def mm_nn(a, b, prec="bf16"):
    return _nn_raw(a, b, prec)


def _mm_nn_f(a, b, prec):
    return _nn_raw(a, b, prec), (a, b)


def _mm_nn_b(prec, res, g):
    a, b = res
    return _nt_raw(g, b, prec).astype(a.dtype), _tn_raw(a, g, prec).astype(b.dtype)


mm_nn.defvjp(_mm_nn_f, _mm_nn_b)


@functools.partial(jax.custom_vjp, nondiff_argnums=(2,))
def mm_nt(a, b, prec="bf16"):
    return _nt_raw(a, b, prec)


def _mm_nt_f(a, b, prec):
    return _nt_raw(a, b, prec), (a, b)


def _mm_nt_b(prec, res, g):
    a, b = res
    return _nn_raw(g, b, prec).astype(a.dtype), _tn_raw(g, a, prec).astype(b.dtype)


mm_nt.defvjp(_mm_nt_f, _mm_nt_b)


@functools.partial(jax.custom_vjp, nondiff_argnums=(2,))
def mm_tn(a, b, prec="bf16"):
    return _tn_raw(a, b, prec)


def _mm_tn_f(a, b, prec):
    return _tn_raw(a, b, prec), (a, b)


def _mm_tn_b(prec, res, g):
    a, b = res
    return _nt_raw(b, g, prec).astype(a.dtype), _nn_raw(a, g, prec).astype(b.dtype)


mm_tn.defvjp(_mm_tn_f, _mm_tn_b)


def _sigmoid(x):
    return 1.0 / (1.0 + jnp.exp(-x))


def _silu(x):
    return x * _sigmoid(x)


def _softplus(x):
    return jnp.maximum(x, 0.0) + jnp.log(1.0 + jnp.exp(-jnp.abs(x)))


def matmul(a, b, *, mode, tm, tn, tk, out_dtype=F32, residual=None, after=None, name):
    if mode == "tn":
        K, M = a.shape
    else:
        M, K = a.shape
    N = b.shape[0] if mode == "nt" else b.shape[1]
    assert M % tm == 0 and N % tn == 0 and K % tk == 0, (a.shape, b.shape, tm, tn, tk)
    nk = K // tk
    a_spec = pl.BlockSpec((tk, tm), lambda i, j, k: (k, i)) if mode == "tn" else pl.BlockSpec((tm, tk), lambda i, j, k: (i, k))
    b_spec = pl.BlockSpec((tn, tk), lambda i, j, k: (j, k)) if mode == "nt" else pl.BlockSpec((tk, tn), lambda i, j, k: (k, j))
    o_spec = pl.BlockSpec((tm, tn), lambda i, j, k: (i, j))
    raw = {"nn": _nn_raw, "nt": _nt_raw, "tn": _tn_raw}[mode]
    has_res = residual is not None

    def body(*refs):
        a_ref, b_ref = refs[:2]
        r_ref = refs[2] if has_res else None
        o_ref, acc_ref = refs[-2:]
        k = pl.program_id(2)
        part = raw(a_ref[...], b_ref[...], "bf16")

        @pl.when(k == 0)
        def _():
            acc_ref[...] = part

        @pl.when(k > 0)
        def _():
            acc_ref[...] += part

        @pl.when(k == nk - 1)
        def _():
            r = acc_ref[...]
            if has_res:
                r = r + r_ref[...].astype(F32)
            o_ref[...] = r.astype(out_dtype)

    in_specs = [a_spec, b_spec] + ([o_spec] if has_res else []) + ([ANY] if after is not None else [])
    args = (a, b) + ((residual,) if has_res else ()) + ((after,) if after is not None else ())
    return pl.pallas_call(
        body, name=name, grid=(M // tm, N // tn, nk), in_specs=in_specs, out_specs=o_spec,
        out_shape=jax.ShapeDtypeStruct((M, N), out_dtype),
        scratch_shapes=[pltpu.VMEM((tm, tn), F32)],
        compiler_params=_cparams(("parallel", "parallel", "arbitrary")),
    )(*args)


def _rms_fn(x, w, eps=1e-6):
    return x * lax.rsqrt(jnp.mean(x * x, axis=-1, keepdims=True) + eps) * w


def rms_fwd(x, w, *, tm, name):
    S, D = x.shape

    def body(x_ref, w_ref, o_ref):
        o_ref[...] = _rms_fn(x_ref[...], w_ref[...]).astype(BF16)

    return pl.pallas_call(
        body, name=name, grid=(S // tm,),
        in_specs=[pl.BlockSpec((tm, D), lambda i: (i, 0)), pl.BlockSpec((1, D), lambda i: (0, 0))],
        out_specs=pl.BlockSpec((tm, D), lambda i: (i, 0)),
        out_shape=jax.ShapeDtypeStruct((S, D), BF16),
        compiler_params=_cparams(("parallel",)),
    )(x, w)


def rms_bwd(x, w, dh, dres, *, tm, name):
    S, D = x.shape

    def body(x_ref, w_ref, dh_ref, dr_ref, dx_ref, dw_ref):
        _, vjp = jax.vjp(_rms_fn, x_ref[...], w_ref[...])
        dx, dw = vjp(dh_ref[...].astype(F32))
        dx_ref[...] = dx + dr_ref[...]

        @pl.when(pl.program_id(0) == 0)
        def _():
            dw_ref[...] = jnp.zeros_like(dw_ref)

        dw_ref[...] += dw

    row = pl.BlockSpec((tm, D), lambda i: (i, 0))
    vec = pl.BlockSpec((1, D), lambda i: (0, 0))
    return pl.pallas_call(
        body, name=name, grid=(S // tm,), in_specs=[row, vec, row, row], out_specs=[row, vec],
        out_shape=[jax.ShapeDtypeStruct((S, D), F32), jax.ShapeDtypeStruct((1, D), F32)],
        compiler_params=_cparams(("arbitrary",)),
    )(x, w, dh, dres)


def _fill_ext(ext_ref, halo, tile, first, H):
    ext_ref[pl.ds(0, H), :] = jnp.where(first, 0.0, halo)
    ext_ref[pl.ds(H, tile.shape[0]), :] = tile


def _conv_rows(K, tm):
    return CONV_ROWS if K <= 8 else tm


def _conv_taps(ext_ref, w_ref, K, H, tm):
    assert H >= 8 * ((K - 1) // 8 + 1)
    rc = _conv_rows(K, tm)
    rows_out = []
    for r0 in range(0, tm, rc):
        pieces = []
        for c0 in range(0, ext_ref.shape[1], 128):
            cs = slice(c0, c0 + 128)
            total = None
            for b in range(min(8, K)):
                y = None
                for a in range((K - 1 - b) // 8 + 1):
                    term = ext_ref[pl.ds(r0 + H - 8 - 8 * a, rc + 8), cs] * w_ref[pl.ds(K - 1 - 8 * a - b, 1), cs]
                    y = term if y is None else y + term
                y = y if b == 0 else pltpu.roll(y, b, 0)
                total = y if total is None else total + y
            pieces.append(total[8:, :])
        rows_out.append(jnp.concatenate(pieces, axis=1))
    return jnp.concatenate(rows_out, axis=0)


def _halo_spec(H, tm, cw, col):
    return pl.BlockSpec((H, cw), lambda *g, _c=col: (jnp.maximum(g[-1] * (tm // H) - 1, 0), _c))


def conv_bwd(dc, srcs, w, du, du_col, *, K, H, tm, cw, glu, name):
    S, C = dc.shape
    nc, nt = C // cw, S // tm
    last_halo = S // H - 1
    n_src = 2 if glu else 1
    bases = [c0 // cw for _, c0 in srcs]

    def body(*refs):
        dc_ref, dcn_ref = refs[0], refs[1]
        src_refs = refs[2:2 + 2 * n_src]
        w_ref = refs[2 + 2 * n_src]
        out_ref, dw_ref, ext_ref, dext_ref = refs[4 + 2 * n_src:]
        i = pl.program_id(1)
        first, last = i == 0, i == nt - 1
        if glu:
            a_ref, ah_ref, b_ref, bh_ref = src_refs
            sg = _sigmoid(b_ref[...])
            _fill_ext(ext_ref, ah_ref[...] * _sigmoid(bh_ref[...]), a_ref[...] * sg, first, H)
        else:
            x_ref, xh_ref = src_refs
            _fill_ext(ext_ref, xh_ref[...], x_ref[...], first, H)
        dc_t = dc_ref[...]
        dext_ref[pl.ds(0, tm), :] = dc_t
        dext_ref[pl.ds(tm, H), :] = jnp.where(last, 0.0, dcn_ref[...])
        rc = _conv_rows(K, tm)
        NR = rc + 8
        for r0 in range(0, tm, rc):
            rs = pl.ds(r0, rc)
            for c0 in range(0, cw, 128):
                cs = slice(c0, c0 + 128)
                dx = None
                for b in range(min(8, K)):
                    z = None
                    for a_ in range((K - 1 - b) // 8 + 1):
                        term = dext_ref[pl.ds(r0 + 8 * a_, NR), cs] * w_ref[pl.ds(K - 1 - 8 * a_ - b, 1), cs]
                        z = term if z is None else z + term
                    z = z if b == 0 else pltpu.roll(z, NR - b, 0)
                    dx = z if dx is None else dx + z
                dx = dx[:rc, :]
                if glu:
                    sg_c = sg[r0:r0 + rc, cs]
                    out_ref[rs, cs] = (dx * sg_c).astype(BF16)
                    out_ref[rs, cw + c0:cw + c0 + 128] = (dx * a_ref[rs, cs] * sg_c * (1.0 - sg_c)).astype(BF16)
                else:
                    out_ref[rs, cs] = dx.astype(BF16)

        @pl.when(first)
        def _():
            dw_ref[...] = jnp.zeros_like(dw_ref)

        N = tm + 8
        for c0 in range(0, cw, 128):
            cs = slice(c0, c0 + 128)
            dpad = jnp.concatenate([jnp.zeros((8, 128), F32), dc_ref[:, cs]], axis=0)
            for b in range(min(8, K)):
                shifted = dpad if b == 0 else pltpu.roll(dpad, N - b, 0)
                for a_ in range((K - 1 - b) // 8 + 1):
                    k = K - 1 - 8 * a_ - b
                    dw_ref[pl.ds(k, 1), cs] += jnp.sum(shifted * ext_ref[pl.ds(H - 8 - 8 * a_, N), cs], axis=0, keepdims=True)

    tile = lambda base: pl.BlockSpec((tm, cw), lambda j, i, _b=base: (i, _b + j))
    halo = lambda base: pl.BlockSpec((H, cw), lambda j, i, _b=base: (jnp.maximum(i * (tm // H) - 1, 0), _b + j))
    in_specs = [tile(0), pl.BlockSpec((H, cw), lambda j, i: (jnp.minimum((i + 1) * (tm // H), last_halo), j))]
    args = [dc, dc]
    for (arr, _), base in zip(srcs, bases):
        in_specs += [tile(base), halo(base)]
        args += [arr, arr]
    in_specs += [pl.BlockSpec((K, cw), lambda j, i: (0, j)), ANY]
    args += [w, du]
    ow = n_src * cw
    out_specs = [pl.BlockSpec((tm, ow), lambda j, i: (i, du_col // ow + j)), pl.BlockSpec((K, cw), lambda j, i: (0, j))]
    out_shape = [jax.ShapeDtypeStruct(du.shape, du.dtype), jax.ShapeDtypeStruct((K, C), F32)]
    return pl.pallas_call(
        body, name=name, grid=(nc, nt), in_specs=in_specs, out_specs=out_specs, out_shape=out_shape,
        input_output_aliases={len(args) - 1: 0},
        scratch_shapes=[pltpu.VMEM((tm + H, cw), F32), pltpu.VMEM((tm + H, cw), F32)],
        compiler_params=_cparams(("parallel", "arbitrary")),
    )(*args)


def _conf_post(c, gate, ln_w, ln_b, pw):
    mu = jnp.mean(c, axis=-1, keepdims=True)
    cc = c - mu
    var = jnp.mean(cc * cc, axis=-1, keepdims=True)
    hn = cc * lax.rsqrt(var + 1e-5) * ln_w + ln_b
    return mm_nn(_silu(hn), pw) * _silu(gate)


def _conf_specs(tm):
    H = CONF_HALO
    blk = lambda col: pl.BlockSpec((tm, CONV_CH), lambda i, _c=col: (i, _c))
    vec = pl.BlockSpec((1, CONV_CH), lambda i: (0, 0))
    specs = [blk(0), blk(1), blk(2), _halo_spec(H, tm, CONV_CH, 0), _halo_spec(H, tm, CONV_CH, 1),
             pl.BlockSpec((CONV_WIDTH, CONV_CH), lambda i: (0, 0)), vec, vec, vec,
             pl.BlockSpec((CONV_CH, CONV_CH), lambda i: (0, 0))]
    return specs, blk, vec


def _conf_conv(a_ref, b_ref, ah_ref, bh_ref, dww_ref, dwb_ref, ext_ref, tm):
    first = pl.program_id(0) == 0
    _fill_ext(ext_ref, ah_ref[...] * _sigmoid(bh_ref[...]), a_ref[...] * _sigmoid(b_ref[...]), first, CONF_HALO)
    return _conv_taps(ext_ref, dww_ref, CONV_WIDTH, CONF_HALO, tm) + dwb_ref[...]


def conf_fwd(u, dw_w, dw_b, ln_w, ln_b, pw, *, tm, name):
    S = u.shape[0]
    specs, blk, vec = _conf_specs(tm)

    def body(a_ref, b_ref, g_ref, ah_ref, bh_ref, dww_ref, dwb_ref, lnw_ref, lnb_ref, pw_ref, y_ref, c_ref, ext_ref):
        c = _conf_conv(a_ref, b_ref, ah_ref, bh_ref, dww_ref, dwb_ref, ext_ref, tm)
        c_ref[...] = c
        y_ref[...] = _conf_post(c, g_ref[...], lnw_ref[...], lnb_ref[...], pw_ref[...]).astype(BF16)

    return pl.pallas_call(
        body, name=name, grid=(S // tm,), in_specs=specs, out_specs=[blk(0), blk(0)],
        out_shape=[jax.ShapeDtypeStruct((S, CONV_CH), BF16), jax.ShapeDtypeStruct((S, CONV_CH), F32)],
        scratch_shapes=[pltpu.VMEM((tm + CONF_HALO, CONV_CH), F32)],
        compiler_params=_cparams(("parallel",)),
    )(u, u, u, u, u, dw_w, dw_b, ln_w, ln_b, pw)


def conf_bwd_post(u, c, dy, ln_w, ln_b, pw, du, *, tm, name):
    S = u.shape[0]
    blk = lambda col: pl.BlockSpec((tm, CONV_CH), lambda i, _c=col: (i, _c))
    vec = pl.BlockSpec((1, CONV_CH), lambda i: (0, 0))
    mat = pl.BlockSpec((CONV_CH, CONV_CH), lambda i: (0, 0))

    def body(c_ref, g_ref, lnw_ref, lnb_ref, pw_ref, dy_ref, du_in, dc_ref, dg_ref, dlnw_ref, dlnb_ref, dpw_ref, ddwb_ref):
        _, vjp = jax.vjp(_conf_post, c_ref[...], g_ref[...], lnw_ref[...], lnb_ref[...], pw_ref[...])
        dc, dg, dlnw, dlnb, dpw = vjp(dy_ref[...])
        dc_ref[...] = dc
        dg_ref[...] = dg.astype(BF16)

        @pl.when(pl.program_id(0) == 0)
        def _():
            dlnw_ref[...] = jnp.zeros_like(dlnw_ref)
            dlnb_ref[...] = jnp.zeros_like(dlnb_ref)
            dpw_ref[...] = jnp.zeros_like(dpw_ref)

            ddwb_ref[...] = jnp.zeros_like(ddwb_ref)

        dlnw_ref[...] += dlnw
        dlnb_ref[...] += dlnb
        dpw_ref[...] += dpw
        ddwb_ref[...] += jnp.sum(dc, axis=0, keepdims=True)

    return pl.pallas_call(
        body, name=name, grid=(S // tm,), in_specs=[blk(0), blk(C_CG // CONV_CH), vec, vec, mat, blk(0), ANY],
        out_specs=[blk(0), blk(C_CG // CONV_CH), vec, vec, mat, vec],
        out_shape=[jax.ShapeDtypeStruct((S, CONV_CH), F32), jax.ShapeDtypeStruct(du.shape, du.dtype),
                   jax.ShapeDtypeStruct((1, CONV_CH), F32), jax.ShapeDtypeStruct((1, CONV_CH), F32),
                   jax.ShapeDtypeStruct((CONV_CH, CONV_CH), F32), jax.ShapeDtypeStruct((1, CONV_CH), F32)],
        input_output_aliases={6: 1},
        compiler_params=_cparams(("arbitrary",)),
    )(c, u, ln_w, ln_b, pw, dy, du)


def _iota2(shape, dim):
    return lax.broadcasted_iota(jnp.int32, shape, dim)


def _gdn_post(pre_q, pre_k, pre_v, b_in, a_in, a_log, dt_bias):
    tm = pre_q.shape[0]
    q, k, v = _silu(pre_q), _silu(pre_k), _silu(pre_v)
    qs, ks = [], []
    for h in range(GDN_HEADS):
        sl = slice(h * GDN_DK, (h + 1) * GDN_DK)
        qh, kh = q[:, sl], k[:, sl]
        qs.append(qh * lax.rsqrt(jnp.sum(qh * qh, axis=-1, keepdims=True) + 1e-6) * (GDN_DK ** -0.5))
        ks.append(kh * lax.rsqrt(jnp.sum(kh * kh, axis=-1, keepdims=True) + 1e-6))
    beta = _sigmoid(b_in)
    g = -jnp.exp(a_log) * _softplus(a_in + dt_bias)
    nb = tm // GDN_CHUNK
    tril = (_iota2((nb, GDN_CHUNK, GDN_CHUNK), 1) >= _iota2((nb, GDN_CHUNK, GDN_CHUNK), 2)).astype(F32)
    gc = mm_nn(tril, g.reshape(nb, GDN_CHUNK, 128), "f32").reshape(tm, 128)
    return jnp.concatenate(qs, axis=1), jnp.concatenate(ks, axis=1), v, beta, gc


def _gdn_prep_specs(tm):
    H = GDN_HALO
    blk = lambda col: pl.BlockSpec((tm, GDN_W), lambda i, _c=col: (i, _c))
    lane = lambda col: pl.BlockSpec((tm, 128), lambda i, _c=col: (i, _c))
    vec = pl.BlockSpec((1, 128), lambda i: (0, 0))
    q0 = C_GQ // GDN_W
    specs = [blk(q0), blk(q0 + 1), blk(q0 + 2),
             _halo_spec(H, tm, GDN_W, q0), _halo_spec(H, tm, GDN_W, q0 + 1), _halo_spec(H, tm, GDN_W, q0 + 2),
             lane(C_BETA // 128), lane(C_ALPHA // 128),
             pl.BlockSpec((SHORT_CONV, GDN_W), lambda i: (0, 0)), pl.BlockSpec((SHORT_CONV, GDN_W), lambda i: (0, 1)),
             pl.BlockSpec((SHORT_CONV, GDN_W), lambda i: (0, 2)), vec, vec]
    return specs, blk, lane, vec


def _gdn_pre(x_refs, h_refs, w_refs, ext_ref, tm):
    first = pl.program_id(0) == 0
    pres = []
    for x_ref, h_ref, w_ref in zip(x_refs, h_refs, w_refs):
        _fill_ext(ext_ref, h_ref[...], x_ref[...], first, GDN_HALO)
        pres.append(_conv_taps(ext_ref, w_ref, SHORT_CONV, GDN_HALO, tm))
    return pres


def gdn_prep_fwd(u, conv_w, a_log, dt_bias, *, tm, name):
    S = u.shape[0]
    specs, blk, lane, vec = _gdn_prep_specs(tm)

    def body(xq, xk, xv, hq, hk, hv, bi, ai, wq, wk, wv, al, db, q_ref, k_ref, v_ref, beta_ref, gc_ref, ext_ref):
        pres = _gdn_pre((xq, xk, xv), (hq, hk, hv), (wq, wk, wv), ext_ref, tm)
        q, k, v, beta, gc = _gdn_post(*pres, bi[...], ai[...], al[...], db[...])
        q_ref[...] = q
        k_ref[...] = k
        v_ref[...] = v
        beta_ref[...] = beta
        gc_ref[...] = gc

    wide = jax.ShapeDtypeStruct((S, GDN_W), F32)
    narrow = jax.ShapeDtypeStruct((S, 128), F32)
    return pl.pallas_call(
        body, name=name, grid=(S // tm,), in_specs=specs,
        out_specs=[blk(0), blk(0), blk(0), lane(0), lane(0)], out_shape=[wide, wide, wide, narrow, narrow],
        scratch_shapes=[pltpu.VMEM((tm + GDN_HALO, GDN_W), F32)],
        compiler_params=_cparams(("parallel",)),
    )(u, u, u, u, u, u, u, u, conv_w, conv_w, conv_w, a_log, dt_bias)


def gdn_prep_bwd(u, conv_w, a_log, dt_bias, dq, dk, dv, dbeta, dgc, du, *, tm, name):
    S = u.shape[0]
    specs, blk, lane, vec = _gdn_prep_specs(tm)
    tail = IN_PAD - C_BETA

    def body(xq, xk, xv, hq, hk, hv, bi, ai, wq, wk, wv, al, db, dq_ref, dk_ref, dv_ref, dbe_ref, dgc_ref, du_in,
             dpre_ref, du_ref, dal_ref, ddb_ref, ext_ref):
        pres = _gdn_pre((xq, xk, xv), (hq, hk, hv), (wq, wk, wv), ext_ref, tm)
        _, vjp = jax.vjp(_gdn_post, *pres, bi[...], ai[...], al[...], db[...])
        dpq, dpk, dpv, dbi, dai, dal, ddb = vjp((dq_ref[...], dk_ref[...], dv_ref[...], dbe_ref[...], dgc_ref[...]))
        dpre_ref[:, 0:GDN_W] = dpq
        dpre_ref[:, GDN_W:2 * GDN_W] = dpk
        dpre_ref[:, 2 * GDN_W:] = dpv
        du_ref[:, 0:128] = dbi.astype(BF16)
        du_ref[:, 128:256] = dai.astype(BF16)
        du_ref[:, 256:] = jnp.zeros((tm, tail - 256), BF16)

        @pl.when(pl.program_id(0) == 0)
        def _():
            dal_ref[...] = jnp.zeros_like(dal_ref)
            ddb_ref[...] = jnp.zeros_like(ddb_ref)

        dal_ref[...] += dal
        ddb_ref[...] += ddb

    n_in = len(specs) + 6
    return pl.pallas_call(
        body, name=name, grid=(S // tm,), in_specs=specs + [blk(0), blk(0), blk(0), lane(0), lane(0), ANY],
        out_specs=[pl.BlockSpec((tm, 3 * GDN_W), lambda i: (i, 0)), pl.BlockSpec((tm, tail), lambda i: (i, C_BETA // tail)),
                   vec, vec],
        out_shape=[jax.ShapeDtypeStruct((S, 3 * GDN_W), F32), jax.ShapeDtypeStruct(du.shape, du.dtype),
                   jax.ShapeDtypeStruct((1, 128), F32), jax.ShapeDtypeStruct((1, 128), F32)],
        input_output_aliases={n_in - 1: 1},
        scratch_shapes=[pltpu.VMEM((tm + GDN_HALO, GDN_W), F32)],
        compiler_params=_cparams(("arbitrary",)),
    )(u, u, u, u, u, u, u, u, conv_w, conv_w, conv_w, a_log, dt_bias, dq, dk, dv, dbeta, dgc, du)


def _lane_col(blk, h):
    return jnp.sum(jnp.where(_iota2(blk.shape, 1) == h, blk, 0.0), axis=1, keepdims=True)


@jax.custom_vjp
def _tri_inv(low):
    n = low.shape[-1]
    r, c = _iota2(low.shape, low.ndim - 2), _iota2(low.shape, low.ndim - 1)
    eye = (r == c).astype(F32)
    t = eye - jnp.where((r // 2 == c // 2) & (r > c), low, 0.0)
    s = 2
    while s < n:
        off = jnp.where((r // (2 * s) == c // (2 * s)) & (r // s > c // s), low, 0.0)
        prec = "bf16" if s <= 8 else "bf16x3"
        t = t - _nn_raw(t, _nn_raw(off, t, prec), prec)
        s *= 2
    return t


def _tri_inv_f(low):
    t = _tri_inv(low)
    return t, t


def _tri_inv_b(t, dt):
    d = -_nt_raw(_tn_raw(t, dt, "bf16x3"), t, "bf16x3")
    r, c = _iota2(d.shape, d.ndim - 2), _iota2(d.shape, d.ndim - 1)
    return (jnp.where(r > c, d, 0.0),)


_tri_inv.defvjp(_tri_inv_f, _tri_inv_b)


@jax.custom_vjp
def _tri_inv_saved(low, t):
    return t


_tri_inv_saved.defvjp(lambda low, t: (t, t), lambda t, dt: (_tri_inv_b(t, dt)[0], jnp.zeros_like(t)))


def _gdn_group(s0, q, k, v, z, beta_blk, gc_blk, nw, h0, t_saved=None, with_t=False):
    C = GDN_CHUNK
    HP, R, _ = q.shape
    nb = R // C
    B = HP * nb
    q3, k3, v3 = (t.reshape(B, C, GDN_DK) for t in (q, k, v))
    b3 = jnp.stack([_lane_col(beta_blk, h0 + j) for j in range(HP)]).reshape(B, C, 1)
    g3 = jnp.stack([_lane_col(gc_blk, h0 + j) for j in range(HP)]).reshape(B, C, 1)
    r, c = _iota2((B, C, C), 1), _iota2((B, C, C), 2)
    causal, strict = r >= c, r > c
    g_t = gc_blk.T
    rows = [jnp.sum(jnp.where(_iota2((128, R), 0) == h0 + j, g_t, 0.0), axis=0, keepdims=True) for j in range(HP)]
    g_row = jnp.stack([rows[j][:, i * C:(i + 1) * C] for j in range(HP) for i in range(nb)])
    decay = jnp.where(causal, jnp.exp(jnp.where(causal, g3 - g_row, 0.0)), 0.0)
    low = jnp.where(strict, b3 * mm_nt(k3, k3) * decay, 0.0)
    t = _tri_inv(low) if t_saved is None else _tri_inv_saved(low, t_saved)
    eg = jnp.exp(g3)
    four = lambda x: x.reshape((HP, nb) + x.shape[1:])
    w_v = four(mm_nn(t, v3 * b3))
    w_k = four(mm_nn(t, k3 * (b3 * eg)))
    qk = four(jnp.where(causal, mm_nt(q3, k3) * decay, 0.0))
    q_dec = four(q3 * eg)
    g_last = jnp.sum(jnp.where(_iota2((B, C, 1), 1) == C - 1, g3, 0.0), axis=1, keepdims=True)
    k_dec = four(k3 * jnp.exp(g_last - g3))
    e_last = four(jnp.exp(g_last))
    s, outs = s0, []
    for i in range(nb):
        v_new = w_v[:, i] - mm_nn(w_k[:, i], s)
        outs.append(mm_nn(q_dec[:, i], s) + mm_nn(qk[:, i], v_new))
        s = s * e_last[:, i] + mm_tn(k_dec[:, i], v_new)
    o = jnp.concatenate(outs, axis=1)
    y = o * lax.rsqrt(jnp.mean(o * o, axis=-1, keepdims=True) + 1e-6) * nw * _silu(z)
    return (s, y, t) if with_t else (s, y)


def _heads(ref, HP):
    return jnp.stack([ref[:, j * GDN_DK:(j + 1) * GDN_DK] for j in range(HP)])


def gdn_core_fwd(q, k, v, u, beta, gc, nw, *, name):
    S = q.shape[0]
    R = GDN_CHUNK * GDN_GROUP
    G = S // R
    HP = GDN_HEADS_PER_STEP
    W = HP * GDN_DK
    NT = HP * GDN_GROUP
    blk = pl.BlockSpec((R, W), lambda g, h: (g, h))
    lane = pl.BlockSpec((R, 128), lambda g, h: (g, 0))
    st = pl.BlockSpec((1, HP, GDN_DK, GDN_DK), lambda g, h: (g, h, 0, 0))
    inv = pl.BlockSpec((1, NT, GDN_CHUNK, GDN_CHUNK), lambda g, h: (g, h, 0, 0))

    def body(q_ref, k_ref, v_ref, z_ref, be_ref, gc_ref, nw_ref, y_ref, st_ref, t_ref, s_ref):
        g, hs = pl.program_id(0), pl.program_id(1)
        s0 = jnp.where(g == 0, 0.0, s_ref[hs])
        st_ref[0] = s0
        s1, y, t = _gdn_group(s0, _heads(q_ref, HP), _heads(k_ref, HP), _heads(v_ref, HP), _heads(z_ref, HP), be_ref[...],
                              gc_ref[...], nw_ref[...], hs * HP, with_t=True)
        s_ref[hs] = s1
        t_ref[0] = t
        for j in range(HP):
            y_ref[:, j * GDN_DK:(j + 1) * GDN_DK] = y[j].astype(BF16)

    return pl.pallas_call(
        body, name=name, grid=(G, GDN_HEADS // HP),
        in_specs=[blk, blk, blk, pl.BlockSpec((R, W), lambda g, h: (g, C_GZ // W + h)), lane, lane,
                  pl.BlockSpec((1, 128), lambda g, h: (0, 0))],
        out_specs=[blk, st, inv],
        out_shape=[jax.ShapeDtypeStruct((S, GDN_W), BF16), jax.ShapeDtypeStruct((G, GDN_HEADS, GDN_DK, GDN_DK), F32),
                   jax.ShapeDtypeStruct((G, GDN_HEADS * GDN_GROUP, GDN_CHUNK, GDN_CHUNK), F32)],
        scratch_shapes=[pltpu.VMEM((GDN_HEADS // HP, HP, GDN_DK, GDN_DK), F32)],
        compiler_params=_cparams(("arbitrary", "arbitrary")),
    )(q, k, v, u, beta, gc, nw)


def gdn_core_bwd(q, k, v, u, beta, gc, nw, states, tinv, dy, du, *, name):
    S = q.shape[0]
    R = GDN_CHUNK * GDN_GROUP
    G = S // R
    HP = GDN_HEADS_PER_STEP
    W = HP * GDN_DK
    blk = pl.BlockSpec((R, W), lambda g, h: (G - 1 - g, h))
    lane = pl.BlockSpec((R, 128), lambda g, h: (G - 1 - g, 0))
    vec = pl.BlockSpec((1, 128), lambda g, h: (0, 0))

    def body(q_ref, k_ref, v_ref, z_ref, be_ref, gc_ref, nw_ref, st_ref, t_ref, *rest):
        dy_refs = rest[:HP]
        dq_ref, dk_ref, dv_ref, dz_ref, dbe_ref, dgc_ref, dnw_ref, ds_ref = rest[HP + 1:]
        g, hs = pl.program_id(0), pl.program_id(1)

        @pl.when(hs == 0)
        def _():
            dbe_ref[...] = jnp.zeros_like(dbe_ref)
            dgc_ref[...] = jnp.zeros_like(dgc_ref)

        @pl.when((hs == 0) & (g == 0))
        def _():
            dnw_ref[...] = jnp.zeros_like(dnw_ref)

        _, vjp = jax.vjp(functools.partial(_gdn_group, h0=hs * HP, t_saved=t_ref[0]), st_ref[0], _heads(q_ref, HP),
                         _heads(k_ref, HP), _heads(v_ref, HP), _heads(z_ref, HP), be_ref[...], gc_ref[...], nw_ref[...])
        ds_in = jnp.where(g == 0, 0.0, ds_ref[hs])
        dy = jnp.stack([r[...] for r in dy_refs])
        ds0, dq, dk, dv, dz, dbe, dgc, dnw = vjp((ds_in, dy))
        ds_ref[hs] = ds0
        for j in range(HP):
            sl = slice(j * GDN_DK, (j + 1) * GDN_DK)
            dq_ref[:, sl] = dq[j]
            dk_ref[:, sl] = dk[j]
            dv_ref[:, sl] = dv[j]
            dz_ref[:, sl] = dz[j].astype(BF16)
        dbe_ref[...] += dbe
        dgc_ref[...] += dgc
        dnw_ref[...] += dnw

    wide = jax.ShapeDtypeStruct((S, GDN_W), F32)
    narrow = jax.ShapeDtypeStruct((S, 128), F32)
    return pl.pallas_call(
        body, name=name, grid=(G, GDN_HEADS // HP),
        in_specs=[blk, blk, blk, pl.BlockSpec((R, W), lambda g, h: (G - 1 - g, C_GZ // W + h)), lane, lane, vec,
                  pl.BlockSpec((1, HP, GDN_DK, GDN_DK), lambda g, h: (G - 1 - g, h, 0, 0)),
                  pl.BlockSpec((1, HP * GDN_GROUP, GDN_CHUNK, GDN_CHUNK), lambda g, h: (G - 1 - g, h, 0, 0))]
        + [pl.BlockSpec((R, GDN_DK), lambda g, h, _j=j: (G - 1 - g, CONV_CH // GDN_DK + h * HP + _j)) for j in range(HP)]
        + [ANY],
        out_specs=[blk, blk, blk, pl.BlockSpec((R, W), lambda g, h: (G - 1 - g, C_GZ // W + h)), lane, lane, vec],
        out_shape=[wide, wide, wide, jax.ShapeDtypeStruct(du.shape, du.dtype), narrow, narrow,
                   jax.ShapeDtypeStruct((1, 128), F32)],
        input_output_aliases={9 + HP: 3},
        scratch_shapes=[pltpu.VMEM((GDN_HEADS // HP, HP, GDN_DK, GDN_DK), F32)],
        compiler_params=_cparams(("arbitrary", "arbitrary")),
    )(q, k, v, u, beta, gc, nw, states, tinv, *([dy] * HP), du)


def rope_tables(S):
    half = ROPE_DIM // 2
    inv = ROPE_THETA ** (-jnp.arange(half, dtype=F32) / half)
    ang = jnp.arange(S, dtype=F32)[:, None] * inv[None, :]
    cos, sin = jnp.cos(ang), jnp.sin(ang)
    rest = ATT_HD - ROPE_DIM
    c = jnp.concatenate([cos, cos, jnp.ones((S, rest), F32)], axis=1)
    s1 = jnp.concatenate([-sin, jnp.zeros((S, ATT_HD - half), F32)], axis=1)
    s2 = jnp.concatenate([jnp.zeros((S, half), F32), sin, jnp.zeros((S, rest), F32)], axis=1)
    return tuple(jnp.tile(t, (1, 2)) for t in (c, s1, s2))


def _rope(x, c, s1, s2):
    half = ROPE_DIM // 2
    return x * c + pltpu.roll(x, ATT_W - half, 1) * s1 + pltpu.roll(x, half, 1) * s2


def _unrope(dy, c, s1, s2):
    half = ROPE_DIM // 2
    return dy * c + pltpu.roll(dy * s1, half, 1) + pltpu.roll(dy * s2, ATT_W - half, 1)


def att_prep_fwd(u, tables, *, tm, name):
    S = u.shape[0]
    blk = lambda col: pl.BlockSpec((tm, ATT_W), lambda i, _c=col: (i, _c))
    tab = pl.BlockSpec((tm, 128), lambda i: (i, 0))

    def body(q_ref, k_ref, v_ref, c_ref, s1_ref, s2_ref, qo_ref, ko_ref, vo_ref):
        reps = ATT_W // 128
        c, s1, s2 = (jnp.tile(t[...], (1, reps)) for t in (c_ref, s1_ref, s2_ref))
        qo_ref[...] = (_rope(q_ref[...], c, s1, s2) * (ATT_HD ** -0.5)).astype(BF16)
        ko_ref[...] = _rope(k_ref[...], c, s1, s2).astype(BF16)
        vo_ref[...] = v_ref[...].astype(BF16)

    out = jax.ShapeDtypeStruct((S, ATT_W), BF16)
    return pl.pallas_call(
        body, name=name, grid=(S // tm,),
        in_specs=[blk(C_AQ // ATT_W), blk(C_AK // ATT_W), blk(C_AV // ATT_W), tab, tab, tab],
        out_specs=[blk(0)] * 3, out_shape=[out] * 3, compiler_params=_cparams(("parallel",)),
    )(u, u, u, *tables)


def att_prep_bwd(dqs, dks, dvs, tables, du, *, tm, name):
    S = dqs[0].shape[0]
    blk = pl.BlockSpec((tm, ATT_W), lambda i: (i, 0))
    tab = pl.BlockSpec((tm, 128), lambda i: (i, 0))

    def body(*refs):
        dq, dk, dv = (refs[3 * j][...].astype(F32) + refs[3 * j + 1][...].astype(F32) + refs[3 * j + 2][...].astype(F32)
                      for j in range(3))
        c_ref, s1_ref, s2_ref, _, o_ref = refs[9:]
        reps = ATT_W // 128
        c, s1, s2 = (jnp.tile(t[...], (1, reps)) for t in (c_ref, s1_ref, s2_ref))
        o_ref[:, 0:ATT_W] = (_unrope(dq, c, s1, s2) * (ATT_HD ** -0.5)).astype(BF16)
        o_ref[:, ATT_W:2 * ATT_W] = _unrope(dk, c, s1, s2).astype(BF16)
        o_ref[:, 2 * ATT_W:] = dv.astype(BF16)

    return pl.pallas_call(
        body, name=name, grid=(S // tm,), in_specs=[blk] * 9 + [tab] * 3 + [ANY],
        out_specs=pl.BlockSpec((tm, 3 * ATT_W), lambda i: (i, C_AQ // (3 * ATT_W))),
        out_shape=jax.ShapeDtypeStruct(du.shape, du.dtype), input_output_aliases={12: 0},
        compiler_params=_cparams(("parallel",)),
    )(*dqs, *dks, *dvs, *tables, du)


def _band_masks():
    qi, ki = _iota2((ATT_BLOCK, ATT_BLOCK), 0), _iota2((ATT_BLOCK, ATT_BLOCK), 1)
    return qi <= ki, ki <= qi


def _pair_diag(x):
    first = _iota2(x.shape, 1) < ATT_HD
    zero = jnp.zeros_like(x)
    return jnp.concatenate([jnp.where(first, x, zero), jnp.where(first, zero, x)], axis=0)


def att_pattern_fwd(qr, kr, vb, dil, *, name):
    S = qr.shape[0]
    L = S // dil
    nb = L // ATT_BLOCK
    view = lambda t: t.reshape(L, dil * t.shape[1])
    cur = pl.BlockSpec((ATT_BLOCK, ATT_W), lambda r, n: (n, r))
    prev = pl.BlockSpec((ATT_BLOCK, ATT_W), lambda r, n: (jnp.maximum(n - 1, 0), r))

    def body(q_ref, kc_ref, kp_ref, vc_ref, vp_ref, o_ref, l_ref):
        has_prev = pl.program_id(1) > 0
        m_prev, m_cur = _band_masks()
        m_prev = m_prev & has_prev
        first = _iota2((ATT_BLOCK, 128), 1) < ATT_HD
        lane = _iota2((ATT_BLOCK, 128), 1)
        stats = jnp.zeros((ATT_BLOCK, 128), F32)
        pairs = range(ATT_HEADS // 2)
        sls = [slice(p * 128, (p + 1) * 128) for p in pairs]
        sps = [_nt_raw(q_ref[:, sl], _pair_diag(kp_ref[:, sl]), "bf16") for sl in sls]
        scs = [_nt_raw(q_ref[:, sl], _pair_diag(kc_ref[:, sl]), "bf16") for sl in sls]
        probs, inv_dens = [], []
        for p in pairs:
            pps, pcs, dens, lses = [], [], [], []
            for half in range(2):
                hs = slice(half * 128, (half + 1) * 128)
                sp_h, sc_h = jnp.where(m_prev, sps[p][:, hs], NEG_INF), jnp.where(m_cur, scs[p][:, hs], NEG_INF)
                m = jnp.maximum(jnp.max(sp_h, axis=1, keepdims=True), jnp.max(sc_h, axis=1, keepdims=True))
                pp, pc = jnp.exp(sp_h - m), jnp.exp(sc_h - m)
                den = jnp.sum(pp, axis=1, keepdims=True) + jnp.sum(pc, axis=1, keepdims=True)
                pps.append(pp.astype(BF16))
                pcs.append(pc.astype(BF16))
                dens.append(den)
                lses.append(m + jnp.log(den))
            probs.append((jnp.concatenate(pps, axis=1), jnp.concatenate(pcs, axis=1)))
            inv_dens.append(1.0 / jnp.where(first, dens[0], dens[1]))
            stats = jnp.where(lane == 2 * p, lses[0], jnp.where(lane == 2 * p + 1, lses[1], stats))
        outs = [_nn_raw(probs[p][0], _pair_diag(vp_ref[:, sls[p]]), "bf16")
                + _nn_raw(probs[p][1], _pair_diag(vc_ref[:, sls[p]]), "bf16") for p in pairs]
        for p in pairs:
            o_ref[:, sls[p]] = (outs[p] * inv_dens[p]).astype(BF16)
        l_ref[...] = stats

    narrow = pl.BlockSpec((ATT_BLOCK, 128), lambda r, n: (n, r))
    o, l = pl.pallas_call(
        body, name=name, grid=(dil, nb), in_specs=[cur, cur, prev, cur, prev], out_specs=[cur, narrow],
        out_shape=[jax.ShapeDtypeStruct((L, dil * ATT_W), BF16), jax.ShapeDtypeStruct((L, dil * 128), F32)],
        compiler_params=_cparams(("parallel", "arbitrary")),
    )(view(qr), view(kr), view(kr), view(vb), view(vb))
    return o.reshape(S, ATT_W), l.reshape(S, 128)


def _head_spread():
    return (_iota2((128, ATT_W), 1) // ATT_HD == _iota2((128, ATT_W), 0)).astype(F32)


def att_combine_fwd(os_, ls, u, *, tm, name):
    S = u.shape[0]
    blk = lambda col: pl.BlockSpec((tm, ATT_W), lambda i, _c=col: (i, _c))
    lane = pl.BlockSpec((tm, 128), lambda i: (i, 0))

    def body(o1, o2, o3, l1, l2, l3, g_ref, y_ref, o_ref, lse_ref):
        a, b, c = l1[...], l2[...], l3[...]
        m = jnp.maximum(jnp.maximum(a, b), c)
        ea, eb, ec = jnp.exp(a - m), jnp.exp(b - m), jnp.exp(c - m)
        den = ea + eb + ec
        spread = _head_spread()
        wa, wb, wc = (_nn_raw(e / den, spread, "bf16x3") for e in (ea, eb, ec))
        o = wa * o1[...].astype(F32) + wb * o2[...].astype(F32) + wc * o3[...].astype(F32)
        o_ref[...] = o
        lse_ref[...] = m + jnp.log(den)
        y_ref[...] = (o * _silu(g_ref[...])).astype(BF16)

    return pl.pallas_call(
        body, name=name, grid=(S // tm,), in_specs=[blk(0)] * 3 + [lane] * 3 + [blk(C_AG // ATT_W)],
        out_specs=[blk(0), blk(0), lane],
        out_shape=[jax.ShapeDtypeStruct((S, ATT_W), BF16), jax.ShapeDtypeStruct((S, ATT_W), F32),
                   jax.ShapeDtypeStruct((S, 128), F32)],
        compiler_params=_cparams(("parallel",)),
    )(*os_, *ls, u)


def att_combine_bwd(dy, o, u, du, *, tm, name):
    S = u.shape[0]
    cw = 256
    base = (CONV_CH + GDN_W) // cw
    blk = lambda col: pl.BlockSpec((tm, ATT_W), lambda i, _c=col: (i, _c))

    def body(dy0, dy1, dy2, o_ref, g_ref, du_in, do_ref, dg_ref, dl_ref):
        g, d, o = g_ref[...], jnp.concatenate([dy0[...], dy1[...], dy2[...]], axis=1), o_ref[...]
        sg = _sigmoid(g)
        d_o = d * (g * sg)
        do_ref[...] = d_o.astype(BF16)
        dg_ref[...] = (d * o * (sg * (1.0 + g * (1.0 - sg)))).astype(BF16)
        dl_ref[...] = _nt_raw(d_o * o, _head_spread(), "bf16x3")

    return pl.pallas_call(
        body, name=name, grid=(S // tm,),
        in_specs=[pl.BlockSpec((tm, cw), lambda i, _j=j: (i, base + _j)) for j in range(ATT_W // cw)]
        + [blk(0), blk(C_AG // ATT_W), ANY],
        out_specs=[blk(0), blk(C_AG // ATT_W), pl.BlockSpec((tm, 128), lambda i: (i, 0))],
        out_shape=[jax.ShapeDtypeStruct((S, ATT_W), BF16), jax.ShapeDtypeStruct(du.shape, du.dtype),
                   jax.ShapeDtypeStruct((S, 128), F32)],
        input_output_aliases={5: 1},
        compiler_params=_cparams(("parallel",)),
    )(dy, dy, dy, o, u, du)


def att_pattern_bwd(qr, kr, vb, do, delta, lse, dil, *, name):
    S = qr.shape[0]
    L = S // dil
    nb = L // ATT_BLOCK
    view = lambda t: t.reshape(L, dil * t.shape[1])
    cur = pl.BlockSpec((ATT_BLOCK, ATT_W), lambda r, n: (jnp.minimum(n, nb - 1), r))
    prev = pl.BlockSpec((ATT_BLOCK, ATT_W), lambda r, n: (jnp.maximum(n - 1, 0), r))
    narrow = pl.BlockSpec((ATT_BLOCK, 128), lambda r, n: (jnp.minimum(n, nb - 1), r))

    def body(q_ref, kc_ref, kp_ref, vc_ref, vp_ref, do_ref, dl_ref, l_ref, dq_ref, dk_ref, dv_ref, ck_ref, cv_ref):
        n = pl.program_id(1)

        @pl.when(n < nb)
        def _():
            m_prev, m_cur = _band_masks()
            m_prev = m_prev & (n > 0)
            m_prev2, m_cur2 = jnp.concatenate([m_prev, m_prev], axis=1), jnp.concatenate([m_cur, m_cur], axis=1)
            first = _iota2((ATT_BLOCK, 128), 1) < ATT_HD
            wide = (ATT_BLOCK, 128)
            halves = lambda a, b: jnp.concatenate([jnp.broadcast_to(a, wide), jnp.broadcast_to(b, wide)], axis=1)
            fold = lambda t: jnp.where(first, t[:ATT_BLOCK], t[ATT_BLOCK:])
            pairs = range(ATT_HEADS // 2)
            sls = [slice(p * 128, (p + 1) * 128) for p in pairs]
            qs, dos = [q_ref[:, sl] for sl in sls], [do_ref[:, sl] for sl in sls]
            kps, kcs, vps, vcs = ([_pair_diag(r[:, sl]) for sl in sls] for r in (kp_ref, kc_ref, vp_ref, vc_ref))
            s_p = [_nt_raw(qs[p], kps[p], "bf16") for p in pairs]
            s_c = [_nt_raw(qs[p], kcs[p], "bf16") for p in pairs]
            dp_p = [_nt_raw(dos[p], vps[p], "bf16") for p in pairs]
            dp_c = [_nt_raw(dos[p], vcs[p], "bf16") for p in pairs]
            pps, pcs, dsps, dscs = [], [], [], []
            for p in pairs:
                delta = halves(dl_ref[:, 2 * p:2 * p + 1], dl_ref[:, 2 * p + 1:2 * p + 2])
                lse2 = halves(l_ref[:, 2 * p:2 * p + 1], l_ref[:, 2 * p + 1:2 * p + 2])
                pp = jnp.where(m_prev2, jnp.exp(s_p[p] - lse2), 0.0)
                pc = jnp.where(m_cur2, jnp.exp(s_c[p] - lse2), 0.0)
                dsps.append((pp * (dp_p[p] - delta)).astype(BF16))
                dscs.append((pc * (dp_c[p] - delta)).astype(BF16))
                pps.append(pp.astype(BF16))
                pcs.append(pc.astype(BF16))
            dqs = [_nn_raw(dsps[p], kps[p], "bf16") + _nn_raw(dscs[p], kcs[p], "bf16") for p in pairs]
            dk_prev = [fold(_tn_raw(dsps[p], qs[p], "bf16")) for p in pairs]
            dv_prev = [fold(_tn_raw(pps[p], dos[p], "bf16")) for p in pairs]
            dk_cur = [fold(_tn_raw(dscs[p], qs[p], "bf16")) for p in pairs]
            dv_cur = [fold(_tn_raw(pcs[p], dos[p], "bf16")) for p in pairs]
            for p in pairs:
                dq_ref[:, sls[p]] = dqs[p].astype(BF16)

            @pl.when(n > 0)
            def _():
                for p in pairs:
                    dk_ref[:, sls[p]] = (ck_ref[:, sls[p]] + dk_prev[p]).astype(BF16)
                    dv_ref[:, sls[p]] = (cv_ref[:, sls[p]] + dv_prev[p]).astype(BF16)

            for p in pairs:
                ck_ref[:, sls[p]] = dk_cur[p]
                cv_ref[:, sls[p]] = dv_cur[p]

        @pl.when(n == nb)
        def _():
            dk_ref[...] = ck_ref[...].astype(BF16)
            dv_ref[...] = cv_ref[...].astype(BF16)

    out = jax.ShapeDtypeStruct((L, dil * ATT_W), BF16)
    dq, dk, dv = pl.pallas_call(
        body, name=name, grid=(dil, nb + 1), in_specs=[cur, cur, prev, cur, prev, cur, narrow, narrow],
        out_specs=[cur, prev, prev], out_shape=[out, out, out],
        scratch_shapes=[pltpu.VMEM((ATT_BLOCK, ATT_W), F32), pltpu.VMEM((ATT_BLOCK, ATT_W), F32)],
        compiler_params=_cparams(("arbitrary", "arbitrary")),
    )(view(qr), view(kr), view(kr), view(vb), view(vb), view(do), view(delta), view(lse))
    return dq.reshape(S, ATT_W), dk.reshape(S, ATT_W), dv.reshape(S, ATT_W)


def _loss_rows(x, w, tgt):
    err = _rms_fn(x, w) - tgt
    return jnp.sum(0.5 * jnp.mean(err * err, axis=-1, keepdims=True), axis=0, keepdims=True)


def loss_head(x, w, tgt, *, tm, name):
    S, D = x.shape

    def body(x_ref, w_ref, t_ref, l_ref, dx_ref, dw_ref):
        val, vjp = jax.vjp(_loss_rows, x_ref[...], w_ref[...], t_ref[...])
        dx, dw, _ = vjp(jnp.ones((1, 1), F32))
        dx_ref[...] = dx

        @pl.when(pl.program_id(0) == 0)
        def _():
            l_ref[...] = jnp.zeros_like(l_ref)
            dw_ref[...] = jnp.zeros_like(dw_ref)

        l_ref[...] += val
        dw_ref[...] += dw

    row = pl.BlockSpec((tm, D), lambda i: (i, 0))
    vec = pl.BlockSpec((1, D), lambda i: (0, 0))
    one = pl.BlockSpec((1, 1), lambda i: (0, 0))
    return pl.pallas_call(
        body, name=name, grid=(S // tm,), in_specs=[row, vec, row], out_specs=[one, row, vec],
        out_shape=[jax.ShapeDtypeStruct((1, 1), F32), jax.ShapeDtypeStruct((S, D), F32), jax.ShapeDtypeStruct((1, D), F32)],
        compiler_params=_cparams(("arbitrary",)),
    )(x, w, tgt)


def adam(w, g, m, v, *, name):
    shape = w.shape
    C = shape[-1]
    R = w.size // C
    br = R
    while br * C * 4 > (1 << 21) and br % 16 == 0:
        br //= 2
    two = lambda t: t.reshape(R, C)

    def body(w_ref, g_ref, m_ref, v_ref, d_ref, mo_ref, vo_ref):
        gg = g_ref[...]
        m_new = ADAM_B1 * m_ref[...] + (1.0 - ADAM_B1) * gg
        v_new = ADAM_B2 * v_ref[...] + (1.0 - ADAM_B2) * jnp.square(gg)
        m_hat = m_new / (1.0 - ADAM_B1 ** ADAM_STEP)
        v_hat = v_new / (1.0 - ADAM_B2 ** ADAM_STEP)
        d_ref[...] = -ADAM_LR * (m_hat / (jnp.sqrt(v_hat) + ADAM_EPS) + ADAM_WD * w_ref[...])
        mo_ref[...] = m_new
        vo_ref[...] = v_new

    blk = pl.BlockSpec((br, C), lambda i: (i, 0))
    out = jax.ShapeDtypeStruct((R, C), F32)
    d, mo, vo = pl.pallas_call(
        body, name=name, grid=(R // br,), in_specs=[blk] * 4, out_specs=[blk] * 3, out_shape=[out] * 3,
        compiler_params=_cparams(("parallel",)),
    )(two(w), two(g), two(m), two(v))
    return d.reshape(shape), mo.reshape(shape), vo.reshape(shape)


MESH_IDS = pl.DeviceIdType.MESH
ANY = pl.BlockSpec(memory_space=pl.ANY)


def _my_id():
    return 4 * lax.axis_index("x") + 2 * lax.axis_index("y") + lax.axis_index("c")


def _peer(k):
    x, y, c = lax.axis_index("x"), lax.axis_index("y"), lax.axis_index("c")
    flip = lambda v, bit: 1 - v if bit else v
    return (flip(x, k & 4), flip(y, k & 2), flip(c, k & 1))


def all_gather_two_level(arrs, *, name):
    n = len(arrs)

    def body(*refs):
        ins, outs = refs[:n], refs[n:2 * n]
        send, recv, local = refs[2 * n:]
        x, y, c = lax.axis_index("x"), lax.axis_index("y"), lax.axis_index("c")
        me, sibling = (x, y, c), (x, y, 1 - c)
        chips = [(1 - x, y), (x, 1 - y), (1 - x, 1 - y)]

        def slot(a, dev):
            return outs[a].at[4 * dev[0] + 2 * dev[1] + dev[2]]

        def copy(a, k, block, to, src=None):
            return pltpu.make_async_remote_copy(
                src_ref=slot(a, block) if src is None else src, dst_ref=slot(a, block), send_sem=send.at[a, k],
                recv_sem=recv.at[a, k], device_id=to, device_id_type=MESH_IDS)

        mine = [pltpu.make_async_copy(ins[a], slot(a, me), local.at[a]) for a in range(n)]
        first = [copy(a, 1 + j, me, (*chip, c), src=ins[a]) for a in range(n) for j, chip in enumerate(chips)]
        first += [copy(a, 0, me, sibling, src=ins[a]) for a in range(n)]
        for cp in first + mine:
            cp.start()
        passed = []
        for a in range(n):
            for j, chip in enumerate(chips):
                copy(a, 1 + j, (*chip, c), me).wait_recv()
                fwd = copy(a, 4 + j, (*chip, c), sibling)
                fwd.start()
                passed.append(fwd)
        for a in range(n):
            copy(a, 0, sibling, me).wait_recv()
            for j, chip in enumerate(chips):
                copy(a, 4 + j, (*chip, 1 - c), me).wait_recv()
        for cp in first + passed:
            cp.wait_send()
        for cp in mine:
            cp.wait()

    return pl.pallas_call(
        body, name=name, in_specs=[ANY] * n, out_specs=[ANY] * n,
        out_shape=[jax.ShapeDtypeStruct((N_DEV,) + a.shape, a.dtype) for a in arrs],
        scratch_shapes=[pltpu.SemaphoreType.DMA((n, N_DEV - 1)), pltpu.SemaphoreType.DMA((n, N_DEV - 1)),
                        pltpu.SemaphoreType.DMA((n,))],
        compiler_params=pltpu.CompilerParams(has_side_effects=True),
    )(*arrs)


def scatter_exchange(groups, pack, *, name):
    flat = [a for grp in groups for a in grp]
    n = len(flat) + 1
    shapes = [jax.ShapeDtypeStruct((N_DEV, len(grp), grp[0].shape[0] // N_DEV, grp[0].shape[1]), grp[0].dtype) for grp in groups]
    shapes.append(jax.ShapeDtypeStruct((N_DEV,) + pack.shape, pack.dtype))
    index = [(gi, li) for gi, grp in enumerate(groups) for li in range(len(grp))]

    def body(*refs):
        ins, outs = refs[:n], refs[n:n + len(shapes)]
        send, recv, local = refs[n + len(shapes):]
        me = _my_id()
        started = []
        for a in range(n):
            if a < n - 1:
                gi, li = index[a]
                r = ins[a].shape[0] // N_DEV
                src = lambda j, _a=a, _r=r: ins[_a].at[pl.ds(pl.multiple_of(j * _r, 8), _r), :]
                dst = outs[gi].at[me, li]
            else:
                src = lambda j, _a=a: ins[_a]
                dst = outs[-1].at[me]
            lc = pltpu.make_async_copy(src(me), dst, local.at[a])
            lc.start()
            started.append(lc)
            for k in range(1, N_DEV):
                cp = pltpu.make_async_remote_copy(src_ref=src(me ^ k), dst_ref=dst, send_sem=send.at[a, k - 1],
                                                  recv_sem=recv.at[a, k - 1], device_id=_peer(k), device_id_type=MESH_IDS)
                cp.start()
                started.append(cp)
        for cp in started:
            cp.wait()

    return pl.pallas_call(
        body, name=name, in_specs=[ANY] * n, out_specs=[ANY] * len(shapes), out_shape=shapes,
        scratch_shapes=[pltpu.SemaphoreType.DMA((n, N_DEV - 1)), pltpu.SemaphoreType.DMA((n, N_DEV - 1)),
                        pltpu.SemaphoreType.DMA((n,))],
        compiler_params=pltpu.CompilerParams(has_side_effects=True),
    )(*flat, pack)


def slot_sum(x, *, name):
    _, A, R, C = x.shape
    br = R
    while br * C * 4 * N_DEV > (1 << 23) and br % 16 == 0:
        br //= 2

    def body(x_ref, o_ref):
        acc = x_ref[0, 0].astype(F32)
        for s in range(1, N_DEV):
            acc = acc + x_ref[s, 0].astype(F32)
        o_ref[0] = acc

    return pl.pallas_call(
        body, name=name, grid=(A, R // br),
        in_specs=[pl.BlockSpec((N_DEV, 1, br, C), lambda a, i: (0, a, i, 0))],
        out_specs=pl.BlockSpec((1, br, C), lambda a, i: (a, i, 0)),
        out_shape=jax.ShapeDtypeStruct((A, R, C), F32),
        compiler_params=_cparams(("parallel", "parallel")),
    )(x)


HBM_SPEC = pl.BlockSpec(memory_space=pltpu.HBM)
SEM_SPEC = pl.BlockSpec(memory_space=pltpu.SEMAPHORE)
DATAFLOW = pltpu.SideEffectType.DATAFLOW_SIDE_EFFECTING


def _push_copies(src_refs, land_refs, send_sems, recv_sems, by_rows):
    me = _my_id()
    copies = []
    for a, (src, land) in enumerate(zip(src_refs, land_refs)):
        rows = land.shape[1]
        for k in range(1, N_DEV):
            piece = src.at[pl.ds(pl.multiple_of((me ^ k) * rows, 8), rows), :] if by_rows else src
            copies.append(pltpu.make_async_remote_copy(
                src_ref=piece, dst_ref=land.at[me], send_sem=send_sems[a].at[k - 1], recv_sem=recv_sems[a].at[k - 1],
                device_id=_peer(k), device_id_type=MESH_IDS))
    return copies


def push_start(srcs, lands, *, by_rows, name):
    n = len(srcs)

    def body(*refs):
        src_refs, land_refs = refs[:n], refs[n:2 * n]
        send_sems, recv_sems = refs[2 * n:3 * n], refs[3 * n:4 * n]
        token = refs[6 * n]
        for cp in _push_copies(src_refs, land_refs, send_sems, recv_sems, by_rows):
            cp.start()
        token[...] = jnp.zeros_like(token)

    sems = [pltpu.SemaphoreType.DMA((N_DEV - 1,))] * (2 * n)
    bufs = [pltpu.HBM(a.shape, a.dtype) for a in list(srcs) + list(lands)]
    outs = pl.pallas_call(
        body, name=name, out_shape=tuple(sems + bufs + [jax.ShapeDtypeStruct((8, 128), F32)]),
        in_specs=[HBM_SPEC] * (2 * n), out_specs=tuple([SEM_SPEC] * (2 * n) + [HBM_SPEC] * (2 * n) + [pl.BlockSpec(memory_space=pltpu.VMEM)]),
        input_output_aliases={i: 2 * n + i for i in range(2 * n)},
        compiler_params=pltpu.CompilerParams(has_side_effects=DATAFLOW),
    )(*[pltpu.with_memory_space_constraint(a, pltpu.HBM) for a in list(srcs) + list(lands)])
    return outs[:n], outs[n:2 * n], outs[2 * n:3 * n], outs[3 * n:4 * n], outs[4 * n]


def push_wait(send_sems, recv_sems, srcs, lands, after, *, by_rows, name):
    n = len(srcs)

    def body(*refs):
        src_refs, land_refs = refs[:n], refs[n:2 * n]
        send, recv = refs[2 * n:3 * n], refs[3 * n:4 * n]
        for cp in _push_copies(src_refs, land_refs, send, recv, by_rows):
            cp.wait_send()
            cp.wait_recv()

    outs = pl.pallas_call(
        body, name=name, out_shape=tuple(pltpu.HBM(a.shape, a.dtype) for a in list(srcs) + list(lands)),
        in_specs=[HBM_SPEC] * (2 * n) + [SEM_SPEC] * (2 * n) + [ANY], out_specs=tuple([HBM_SPEC] * (2 * n)),
        input_output_aliases={i: i for i in range(2 * n)},
        compiler_params=pltpu.CompilerParams(has_side_effects=DATAFLOW),
    )(*srcs, *lands, *send_sems, *recv_sems, after)
    return outs[n:]


LANDING_BLOCKS = 4


def _landing(src, slots_shape, after=None, *, by_rows, name):
    rows, cols = slots_shape[1], slots_shape[2]
    br = rows // LANDING_BLOCKS
    me = _my_id().astype(jnp.int32).reshape(1)

    def body(me_ref, src_ref, *rest):
        rest[-1][0] = src_ref[...]

    extra = [] if after is None else [after]
    grid_spec = pltpu.PrefetchScalarGridSpec(
        num_scalar_prefetch=1, grid=(LANDING_BLOCKS,),
        in_specs=[pl.BlockSpec((br, cols), lambda i, me_ref: ((me_ref[0] * LANDING_BLOCKS if by_rows else 0) + i, 0))]
        + [ANY] * len(extra),
        out_specs=pl.BlockSpec((1, br, cols), lambda i, me_ref: (me_ref[0], i, 0)))
    return pl.pallas_call(
        body, name=name, grid_spec=grid_spec, out_shape=jax.ShapeDtypeStruct(slots_shape, src.dtype),
    )(me, src, *extra)


def _pack(arrs):
    flat = []
    for a in arrs:
        f = a.reshape(-1).astype(F32)
        flat.append(jnp.pad(f, (0, (-f.size) % 128)))
    f = jnp.concatenate(flat)
    return jnp.pad(f, (0, (-f.size) % 1024)).reshape(-1, 128)


def _unpack(p, shapes):
    f = p.reshape(-1)
    out, off = [], 0
    for s in shapes:
        n = math.prod(s)
        out.append(f[off:off + n].reshape(s))
        off += n + (-n) % 128
    return out


def _to_padded_cols(w):
    z = lambda n: jnp.zeros(w.shape[:-1] + (n,), w.dtype)
    return jnp.concatenate([w[..., 0:4608], w[..., 4620:7692], w[..., 4608:4614], z(122), w[..., 4614:4620], z(378)], axis=-1)


def _from_padded_cols(w):
    return jnp.concatenate([w[..., 0:4608], w[..., C_BETA:C_BETA + 6], w[..., C_ALPHA:C_ALPHA + 6], w[..., 4608:7680]], axis=-1)


def _lane_pad(v):
    return jnp.pad(v, (0, 128 - v.shape[0]))[None, :]


TM_MM, TN_MM, TK_MM = 1024, 1024, 2048
TM_ROW = 512


def layer_fwd(x, p, tabs, l):
    h = rms_fwd(x, p["norm_w"], tm=TM_ROW, name=f"rms_fwd_{l}")
    u = matmul(h, p["w_in"], mode="nn", tm=TM_MM, tn=TN_MM, tk=TK_MM, name=f"in_proj_{l}")
    y_conv, c = conf_fwd(u, p["dw_w"], p["dw_b"], p["ln_w"], p["ln_b"], p["pw"], tm=TM_ROW, name=f"conf_fwd_{l}")
    q, k, v, beta, gc = gdn_prep_fwd(u, p["conv_w"], p["a_log"], p["dt_bias"], tm=TM_ROW, name=f"gdn_prep_fwd_{l}")
    y_gdn, states, tinv = gdn_core_fwd(q, k, v, u, beta, gc, p["gdn_nw"], name=f"gdn_core_fwd_{l}")
    qr, kr, vb = att_prep_fwd(u, tabs, tm=TM_ROW, name=f"att_prep_fwd_{l}")
    os_, ls = [], []
    for _, dil in DIL_PATTERNS:
        o_p, l_p = att_pattern_fwd(qr, kr, vb, dil, name=f"att_fwd_d{dil}_{l}")
        os_.append(o_p)
        ls.append(l_p)
    y_att, o, lse = att_combine_fwd(os_, ls, u, tm=TM_ROW, name=f"att_combine_fwd_{l}")
    y = jnp.concatenate([y_conv, y_gdn, y_att], axis=1)
    if callable(p["w_out"]):
        p["w_out"] = p["w_out"](y)
    x_new = matmul(y, p["w_out"], mode="nn", tm=TM_MM // 2, tn=D_MODEL, tk=TK_MM, residual=x, name=f"out_proj_{l}")
    saved = dict(x=x, h=h, u=u, y=y, c=c, q=q, k=k, v=v, beta=beta, gc=gc, states=states, tinv=tinv, qr=qr, kr=kr, vb=vb, o=o,
                 lse=lse)
    return x_new, saved


def layer_bwd(dx_out, s, p, tabs, l, send_w_out=None, send_w_in=None):
    S = dx_out.shape[0]
    u = s["u"]
    dy = matmul(dx_out, p["w_out"], mode="nt", tm=TM_MM // 2, tn=D_MODEL, tk=TK_MM, name=f"out_proj_dy_{l}")
    g_w_out = matmul(s["y"], dx_out, mode="tn", tm=TM_MM, tn=TN_MM, tk=TK_MM, out_dtype=BF16, name=f"out_proj_dw_{l}")
    ln_b = p["ln_b"] if send_w_out is None else p["ln_b"] + send_w_out(g_w_out)
    du = lax.empty((S, IN_PAD), BF16)
    dc, du, g_ln_w, g_ln_b, g_pw, g_dw_b = conf_bwd_post(u, s["c"], dy, p["ln_w"], ln_b, p["pw"], du,
                                                       tm=TM_ROW, name=f"conf_bwd_post_{l}")
    du, g_dw_w = conv_bwd(dc, [(u, C_CA), (u, C_CB)], p["dw_w"], du, C_CA, K=CONV_WIDTH, H=CONF_HALO, tm=TM_ROW, cw=CONV_CH,
                          glu=True, name=f"conf_bwd_conv_{l}")
    dq, dk, dv, du, dbeta, dgc, g_gdn_nw = gdn_core_bwd(s["q"], s["k"], s["v"], u, s["beta"], s["gc"], p["gdn_nw"],
                                                        s["states"], s["tinv"], dy, du, name=f"gdn_core_bwd_{l}")
    dpre, du, g_a_log, g_dt_bias = gdn_prep_bwd(u, p["conv_w"], p["a_log"], p["dt_bias"], dq, dk, dv, dbeta, dgc, du,
                                                tm=TM_ROW, name=f"gdn_prep_bwd_{l}")
    du, g_conv_w = conv_bwd(dpre, [(u, C_GQ)], p["conv_w"], du, C_GQ, K=SHORT_CONV, H=GDN_HALO, tm=TM_ROW, cw=GDN_W,
                            glu=False, name=f"gdn_bwd_conv_{l}")
    do, du, delta = att_combine_bwd(dy, s["o"], u, du, tm=TM_ROW, name=f"att_combine_bwd_{l}")
    dqs, dks, dvs = [], [], []
    for _, dil in DIL_PATTERNS:
        a, b, c = att_pattern_bwd(s["qr"], s["kr"], s["vb"], do, delta, s["lse"], dil, name=f"att_bwd_d{dil}_{l}")
        dqs.append(a)
        dks.append(b)
        dvs.append(c)
    du = att_prep_bwd(dqs, dks, dvs, tabs, du, tm=TM_ROW, name=f"att_prep_bwd_{l}")
    g_w_in = matmul(s["h"], du, mode="tn", tm=TM_MM, tn=TN_MM, tk=min(2 * TK_MM, S), out_dtype=BF16, name=f"in_proj_dw_{l}")
    sent = None if send_w_in is None else send_w_in(g_w_in)
    dh = matmul(du, p["w_in"], mode="nt", tm=TM_MM, tn=TN_MM, tk=2 * TK_MM, after=sent, name=f"in_proj_dh_{l}")
    dx, g_norm_w = rms_bwd(s["x"], p["norm_w"], dh, dx_out, tm=TM_ROW // 2, name=f"rms_bwd_{l}")
    grads = dict(norm_w=g_norm_w[0], w_in=g_w_in, conv_qkv_w=g_conv_w, a_log=g_a_log[0, :GDN_HEADS], dt_bias=g_dt_bias[0, :GDN_HEADS],
                 gdn_norm_w=g_gdn_nw[0], conf_dw_w=g_dw_w, conf_dw_b=g_dw_b[0], conf_ln_w=g_ln_w[0], conf_ln_b=g_ln_b[0],
                 conf_pw_w=g_pw, w_out=g_w_out)
    return dx, grads


WEIGHTS = ("norm_w", "w_in", "conv_qkv_w", "a_log", "dt_bias", "gdn_norm_w", "conf_dw_w", "conf_dw_b", "conf_ln_w",
           "conf_ln_b", "conf_pw_w", "w_out", "final_norm_w")
SMALL_REPLICATED = ("norm_w", "a_log", "dt_bias", "gdn_norm_w", "conf_dw_b", "conf_ln_w", "conf_ln_b")


def kernel(x, norm_w, w_in, conv_qkv_w, a_log, dt_bias, gdn_norm_w, conf_dw_w, conf_dw_b, conf_ln_w, conf_ln_b, conf_pw_w, w_out, final_norm_w, loss_target, m_norm_w, m_w_in, m_conv_qkv_w, m_a_log, m_dt_bias, m_gdn_norm_w, m_conf_dw_w, m_conf_dw_b, m_conf_ln_w, m_conf_ln_b, m_conf_pw_w, m_w_out, m_final_norm_w, v_norm_w, v_w_in, v_conv_qkv_w, v_a_log, v_dt_bias, v_gdn_norm_w, v_conf_dw_w, v_conf_dw_b, v_conf_ln_w, v_conf_ln_b, v_conf_pw_w, v_w_out, v_final_norm_w):
    w = dict(norm_w=norm_w, w_in=w_in, conv_qkv_w=conv_qkv_w, a_log=a_log, dt_bias=dt_bias, gdn_norm_w=gdn_norm_w,
             conf_dw_w=conf_dw_w, conf_dw_b=conf_dw_b, conf_ln_w=conf_ln_w, conf_ln_b=conf_ln_b, conf_pw_w=conf_pw_w,
             w_out=w_out, final_norm_w=final_norm_w)
    m = dict(zip(WEIGHTS, (m_norm_w, m_w_in, m_conv_qkv_w, m_a_log, m_dt_bias, m_gdn_norm_w, m_conf_dw_w, m_conf_dw_b,
                           m_conf_ln_w, m_conf_ln_b, m_conf_pw_w, m_w_out, m_final_norm_w)))
    v = dict(zip(WEIGHTS, (v_norm_w, v_w_in, v_conv_qkv_w, v_a_log, v_dt_bias, v_gdn_norm_w, v_conf_dw_w, v_conf_dw_b,
                           v_conf_ln_w, v_conf_ln_b, v_conf_pw_w, v_w_out, v_final_norm_w)))
    S = x.shape[1]
    L = norm_w.shape[0]
    me = _my_id()

    small_shapes = [conv_qkv_w.shape, conf_dw_w.shape, conf_pw_w.shape]
    w_in_b, w_out_b = _to_padded_cols(w_in).astype(BF16), w_out.astype(BF16)
    in_slots, out_slots = (N_DEV,) + w_in_b.shape[1:], (N_DEV,) + w_out_b.shape[1:]
    g_in0, g_small = all_gather_two_level([w_in_b[0], _pack([conv_qkv_w, conf_dw_w, conf_pw_w])], name="gather_first")
    gathers, tie = {}, jnp.zeros((1, 1), F32)
    for l in range(L):
        srcs = [w_out_b[0]] if l == 0 else [w_in_b[l], w_out_b[l]]
        slots = [out_slots] if l == 0 else [in_slots, out_slots]
        lands = [_landing(a, sl, g_in0, by_rows=False, name=f"gather_own_{l}_{j}") for j, (a, sl) in enumerate(zip(srcs, slots))]
        *flight, token = push_start(srcs, lands, by_rows=False, name=f"gather_start_{l}")
        gathers[l] = flight
        tie = tie + token[0:1, 0:1]
    parts = [_unpack(g_small[s], small_shapes) for s in range(N_DEV)]
    conv_full = jnp.concatenate([pt[0] for pt in parts], axis=2)
    dw_full = jnp.concatenate([pt[1] for pt in parts], axis=2)
    pw_full = jnp.concatenate([pt[2] for pt in parts], axis=1)
    tabs = rope_tables(S)

    def layer_params(l, full_in, full_out):
        return dict(
            norm_w=norm_w[l][None], w_in=full_in.reshape(D_MODEL, IN_PAD), w_out=full_out,
            conv_w=conv_full[l], a_log=_lane_pad(a_log[l]), dt_bias=_lane_pad(dt_bias[l]), gdn_nw=gdn_norm_w[l][None],
            dw_w=dw_full[l], dw_b=conf_dw_b[l][None], ln_w=conf_ln_w[l][None], ln_b=conf_ln_b[l][None], pw=pw_full[l])

    xs = x[0]
    params, saved = [], []
    for l in range(L):
        if l == 0:
            late_out = lambda after: push_wait(*gathers[0], after, by_rows=False, name="gather_wait_0")[0].reshape(D_MODEL, D_MODEL)
            p = layer_params(0, g_in0, late_out)
            p["norm_w"] = p["norm_w"] + tie
        else:
            full_in, full_out = push_wait(*gathers[l], xs, by_rows=False, name=f"gather_wait_{l}")
            p = layer_params(l, full_in, full_out.reshape(D_MODEL, D_MODEL))
        params.append(p)
        xs, sv = layer_fwd(xs, p, tabs, l)
        saved.append(sv)
    loss_part, dx, g_final = loss_head(xs, final_norm_w[None], loss_target[0], tm=TM_ROW // 2, name="loss_head")

    layer_grads, scatters = [None] * L, {}

    def send(kind, l, grad, slots):
        land = _landing(grad, slots, by_rows=True, name=f"scatter_own_{kind}_{l}")
        *flight, token = push_start([grad], [land], by_rows=True, name=f"scatter_start_{kind}_{l}")
        scatters[kind, l] = flight
        return token[0:1, 0:1]

    for l in reversed(range(L)):
        dx, layer_grads[l] = layer_bwd(dx, saved[l], params[l], tabs, l, functools.partial(send, "out", l, slots=out_slots),
                                       functools.partial(send, "in", l, slots=in_slots))

    stack = lambda name: jnp.stack([layer_grads[l][name] for l in range(L)])
    small = [loss_part] + [stack(n) for n in SMALL_REPLICATED] + [g_final[0], stack("conv_qkv_w"), stack("conf_dw_w")]
    small_shapes = [a.shape for a in small]
    r_pw, r_small = scatter_exchange([[layer_grads[l]["conf_pw_w"] for l in range(L)]], _pack(small), name="scatter_small")
    g = {}
    g["conf_pw_w"] = slot_sum(r_pw, name="sum_pw")
    summed = _unpack(slot_sum(r_small[:, None], name="sum_small")[0], small_shapes)
    loss = summed[0].reshape(())
    for n, a in zip(SMALL_REPLICATED, summed[1:1 + len(SMALL_REPLICATED)]):
        g[n] = a
    g["final_norm_w"] = summed[-3]
    g["conv_qkv_w"] = lax.dynamic_slice_in_dim(summed[-2], me * conv_qkv_w.shape[2], conv_qkv_w.shape[2], axis=2)
    g["conf_dw_w"] = lax.dynamic_slice_in_dim(summed[-1], me * conf_dw_w.shape[2], conf_dw_w.shape[2], axis=2)
    deltas, new_m, new_v = {}, {}, {}
    for n in WEIGHTS:
        if n not in ("w_in", "w_out"):
            deltas[n], new_m[n], new_v[n] = adam(w[n], g[n], m[n], v[n], name=f"adam_{n}")
    sums = {}
    order = [(kind, l) for l in reversed(range(L)) for kind in ("out", "in")]
    done_first = loss_part
    for kind, l in order:
        last = (kind, l) == order[-1]
        after = (done_first + deltas["a_log"][0:1, 0:1]) if last else dx
        land, = push_wait(*scatters[kind, l], after, by_rows=True, name=f"scatter_wait_{kind}_{l}")
        sums[kind, l] = slot_sum(land[:, None], name=f"sum_w_{kind}_{l}")
        if not last:
            done_first = done_first + sums[kind, l][0, 0:1, 0:1]
    g["w_in"] = _from_padded_cols(jnp.concatenate([sums["in", l] for l in range(L)], axis=0))
    g["w_out"] = jnp.concatenate([sums["out", l] for l in range(L)], axis=0)
    for n in ("w_in", "w_out"):
        deltas[n], new_m[n], new_v[n] = adam(w[n], g[n], m[n], v[n], name=f"adam_{n}")
    return (loss, dx[None], *[g[n] for n in WEIGHTS], *[deltas[n] for n in WEIGHTS],
            *[new_m[n] for n in WEIGHTS], *[new_v[n] for n in WEIGHTS])
```
